```python
import jax, jax.numpy as jnp
from jax import lax
import numpy as np

D_MODEL = 2048
BATCH = 4
SEQ = 4096
DEPTH = 2

HEAD_DIM = 64
HALF_DIM = HEAD_DIM // 2
SCALE = HEAD_DIM ** -0.5
NORM_EPS = 1e-6
ROPE_THETA = 10000.0
NEG_INF = -1e30

RW_HEADS = 8
RW_WIDTH = RW_HEADS * HEAD_DIM
RW_DECAY_LORA = 32
RW_AAA_LORA = 32
RW_GATE_LORA = 96
RW_LNX_EPS = 64e-5
RW_SPLITS = (RW_WIDTH, RW_WIDTH, RW_WIDTH, 2 * RW_DECAY_LORA, 2 * RW_AAA_LORA, RW_GATE_LORA)
RW_COLS = sum(RW_SPLITS)

DIL_PATTERNS = ((128, 1), (512, 4), (2048, 16))
DIL_GROUPS = len(DIL_PATTERNS)
DIL_HEADS_PER_GROUP = 4
DIL_HEADS = DIL_GROUPS * DIL_HEADS_PER_GROUP
DIL_WIDTH = DIL_HEADS * HEAD_DIM
DIL_OUT_WIDTH = DIL_HEADS_PER_GROUP * HEAD_DIM
DIL_BLOCK = 64

GRID_W = 64
NA_HEADS = 12
NA_WIDTH = NA_HEADS * HEAD_DIM
NA_KH = 8
NA_KW = 16
NA_QCB = 16
NA_KCB = 32

N_BRANCH = 3
IN_SPLITS = (RW_COLS, DIL_WIDTH, DIL_WIDTH, DIL_WIDTH, NA_WIDTH, NA_WIDTH, NA_WIDTH, N_BRANCH * D_MODEL)
IN_COLS = sum(IN_SPLITS)

D_FF = 5632
PLE_DIM = 256

kernel_name = 'hybrid_rwkv7_dilated_natten_encoder'


def _split(z, sizes):
    idx = [int(s) for s in np.cumsum(sizes)[:-1]]
    return jnp.split(z, idx, axis=-1)


def rmsnorm(z, g):
    zf = z.astype(jnp.float32)
    zf = zf * lax.rsqrt(jnp.mean(zf * zf, axis=-1, keepdims=True) + NORM_EPS)
    return (zf * g).astype(z.dtype)


def centred_shift(z):
    zp = jnp.pad(z, ((0, 0), (1, 1), (0, 0)))
    return 0.5 * (zp[:, :-2] + zp[:, 2:])


def rope_tables(T):
    inv = ROPE_THETA ** (-jnp.arange(0, HEAD_DIM, 2, dtype=jnp.float32) / HEAD_DIM)
    ang = jnp.arange(T, dtype=jnp.float32)[:, None] * inv[None, :]
    return jnp.cos(ang), jnp.sin(ang)


def apply_rope(z, cos, sin):
    zf = z.astype(jnp.float32)
    z1, z2 = zf[..., :HALF_DIM], zf[..., HALF_DIM:]
    c, s = cos[:, None, :], sin[:, None, :]
    return jnp.concatenate([z1 * c - z2 * s, z2 * c + z1 * s], axis=-1).astype(z.dtype)


def rwkv7_step(S, inp):
    r_t, w_t, k_t, v_t, a_t, b_t = inp
    sa = jnp.einsum('dbhij,dbhj->dbhi', S, a_t)
    S = S * w_t[..., None, :] + sa[..., :, None] * b_t[..., None, :] + v_t[..., :, None] * k_t[..., None, :]
    y = jnp.einsum('dbhij,dbhj->dbhi', S, r_t)
    return S, y


def rwkv7_bidir(cols, mu, w0, w2, a0, a2, g2, k_k, k_a, r_k, lnx_g, lnx_b):
    dt = cols.dtype
    c = cols.astype(jnp.float32)
    c = c + mu * (centred_shift(c) - c)
    r, k, v, wd, ad, gd = _split(c, RW_SPLITS)
    B, T, C = r.shape
    H, N = RW_HEADS, HEAD_DIM
    wd = wd.reshape(B, T, 2, RW_DECAY_LORA)
    ad = ad.reshape(B, T, 2, RW_AAA_LORA)
    w = -jax.nn.softplus(-(w0 + jnp.einsum('btdl,dlc->btdc', jnp.tanh(wd), w2))) - 0.5
    decay = jnp.exp(-jnp.exp(w))
    lr = jax.nn.sigmoid(a0 + jnp.einsum('btdl,dlc->btdc', ad, a2))
    gate = jax.nn.sigmoid(gd) @ g2
    kk = (k * k_k).reshape(B, T, H, N)
    kk = kk / jnp.maximum(jnp.linalg.norm(kk, axis=-1, keepdims=True), 1e-12)
    kd = k[:, :, None] * (1.0 + (lr - 1.0) * k_a)
    heads = lambda z: z.reshape(z.shape[:-1] + (H, N))
    r_h, v_h = heads(r), heads(v)
    kd_h, decay_h, lr_h = heads(kd), heads(decay), heads(lr)
    both = lambda z: jnp.broadcast_to(z[:, :, None], kd_h.shape)
    kk2 = both(kk)

    def to_scan(z):
        z = jnp.stack([z[:, :, 0], z[:, ::-1, 1]], axis=0)
        return jnp.moveaxis(z, 2, 0)

    xs = (to_scan(both(r_h)), to_scan(decay_h), to_scan(kd_h), to_scan(both(v_h)),
          to_scan(-kk2), to_scan(kk2 * lr_h))
    S0 = jnp.zeros((2, B, H, N, N), jnp.float32)
    _, ys = lax.scan(rwkv7_step, S0, xs)
    y = jnp.moveaxis(ys[:, 0], 0, 1) + jnp.moveaxis(ys[::-1, 1], 0, 1)
    mean = jnp.mean(y, axis=-1, keepdims=True)
    var = jnp.mean(jnp.square(y - mean), axis=-1, keepdims=True)
    yn = ((y - mean) * lax.rsqrt(var + RW_LNX_EPS)).reshape(B, T, C) * lnx_g + lnx_b
    bonus = jnp.sum(jnp.sum(r_h[:, :, None] * kd_h * r_k, axis=-1, keepdims=True) * v_h[:, :, None], axis=2)
    return ((yn + bonus.reshape(B, T, C)) * gate).astype(dt)


def dilated_band(q, k, v, half, dil):
    B, T, H, E = q.shape
    n_sub = T // dil
    ns = -(-half // DIL_BLOCK)
    kw = (2 * ns + 1) * DIL_BLOCK
    nblk = -(-n_sub // DIL_BLOCK)
    n_pad = nblk * DIL_BLOCK
    sub = lambda z: jnp.swapaxes(z.reshape(B, n_sub, dil, H, E), 1, 2)
    qs = jnp.pad(sub(q), ((0, 0), (0, 0), (0, n_pad - n_sub), (0, 0), (0, 0)))
    qs = qs.reshape(B, dil, nblk, DIL_BLOCK, H, E)
    padk = ((0, 0), (0, 0), (ns * DIL_BLOCK, n_pad - n_sub + ns * DIL_BLOCK), (0, 0), (0, 0))

    def windows(z):
        zb = jnp.pad(sub(z), padk).reshape(B, dil, nblk + 2 * ns, DIL_BLOCK, H, E)
        return jnp.concatenate([zb[:, :, s:s + nblk] for s in range(2 * ns + 1)], axis=3)

    kwin, vwin = windows(k), windows(v)
    jq = np.arange(nblk)[:, None] * DIL_BLOCK + np.arange(DIL_BLOCK)[None, :]
    jk = np.arange(nblk)[:, None] * DIL_BLOCK - ns * DIL_BLOCK + np.arange(kw)[None, :]
    ok = ((np.abs(jk[:, None, :] - jq[:, :, None]) <= half)
          & (jk[:, None, :] >= 0) & (jk[:, None, :] < n_sub))
    s = jnp.einsum('briqhe,brimhe->brihqm', qs, kwin, preferred_element_type=jnp.float32)
    s = jnp.where(jnp.asarray(ok)[None, None, :, None, :, :], s, NEG_INF)
    lse = jax.nn.logsumexp(s, axis=-1)
    pr = jnp.exp(s - lse[..., None])
    o = jnp.einsum('brihqm,brimhe->briqhe', pr, vwin.astype(jnp.float32))
    o = jnp.swapaxes(o.reshape(B, dil, n_pad, H, E)[:, :, :n_sub], 1, 2).reshape(B, T, H, E)
    lse = jnp.transpose(lse, (0, 1, 2, 4, 3)).reshape(B, dil, n_pad, H)[:, :, :n_sub]
    lse = jnp.swapaxes(lse, 1, 2).reshape(B, T, H)
    return o, lse


def dilated_attention(q, k, v, cos, sin):
    B, T, _ = q.shape
    q = apply_rope(q.reshape(B, T, DIL_HEADS, HEAD_DIM), cos, sin) * SCALE
    k = apply_rope(k.reshape(B, T, DIL_HEADS, HEAD_DIM), cos, sin)
    v = v.reshape(B, T, DIL_HEADS, HEAD_DIM)
    outs, lses = [], []
    for g, (window, dil) in enumerate(DIL_PATTERNS):
        hs = slice(g * DIL_HEADS_PER_GROUP, (g + 1) * DIL_HEADS_PER_GROUP)
        o, l = dilated_band(q[:, :, hs], k[:, :, hs], v[:, :, hs], window // (2 * dil), dil)
        outs.append(o)
        lses.append(l)
    wts = jax.nn.softmax(jnp.stack(lses, axis=0), axis=0)
    o = jnp.einsum('gbth,gbthe->bthe', wts, jnp.stack(outs, axis=0))
    return o.reshape(B, T, DIL_OUT_WIDTH).astype(v.dtype)


def neighbourhood_attention(q, k, v, rel_bias):
    B, T, _ = q.shape
    rows = T // GRID_W
    kh = min(NA_KH, rows)
    n_cb = GRID_W // NA_QCB
    qc = np.arange(GRID_W).reshape(n_cb, NA_QCB)
    kc0 = np.clip(np.arange(n_cb) * NA_QCB - NA_KW // 2, 0, GRID_W - NA_KCB)
    kc = kc0[:, None] + np.arange(NA_KCB)[None, :]
    wc0 = np.clip(qc - NA_KW // 2, 0, GRID_W - NA_KW)
    col_ok = jnp.asarray((kc[:, None, :] >= wc0[:, :, None]) & (kc[:, None, :] < wc0[:, :, None] + NA_KW))
    dx_idx = np.clip(kc[:, None, :] - qc[:, :, None], 1 - NA_KW, NA_KW - 1) + NA_KW - 1
    qg = (q * SCALE).reshape(B, rows, n_cb, NA_QCB, NA_HEADS, HEAD_DIM)
    kg = k.reshape(B, rows, GRID_W, NA_HEADS, HEAD_DIM)
    vg = v.reshape(B, rows, GRID_W, NA_HEADS, HEAD_DIM)

    def row_fn(r):
        r0 = jnp.clip(r - kh // 2, 0, rows - kh)
        k_blk = lax.dynamic_slice_in_dim(kg, r0, kh, axis=1)[:, :, kc]
        v_blk = lax.dynamic_slice_in_dim(vg, r0, kh, axis=1)[:, :, kc]
        q_r = lax.dynamic_index_in_dim(qg, r, axis=1, keepdims=False)
        s = jnp.einsum('bcqhe,bycmhe->bhcqym', q_r, k_blk, preferred_element_type=jnp.float32)
        dy = r0 + jnp.arange(kh) - r
        bias = rel_bias[:, dy + NA_KH - 1][:, :, dx_idx]
        s = s + jnp.transpose(bias, (0, 2, 3, 1, 4))[None].astype(jnp.float32)
        s = jnp.where(col_ok[None, None, :, :, None, :], s, NEG_INF)
        pr = jax.nn.softmax(s, axis=(-2, -1))
        return jnp.einsum('bhcqym,bycmhe->bcqhe', pr, v_blk.astype(jnp.float32))

    out = lax.map(row_fn, jnp.arange(rows))
    return jnp.moveaxis(out, 0, 1).reshape(B, T, NA_WIDTH).astype(v.dtype)


def conv_ffn(h, w_gate, w_up, conv_w, conv_b, w_down):
    gp = jnp.pad(h @ w_gate, ((0, 0), (1, 1), (0, 0)))
    gc = gp[:, :-2] * conv_w[0] + gp[:, 1:-1] * conv_w[1] + gp[:, 2:] * conv_w[2] + conv_b
    return (jax.nn.gelu(gc) * (h @ w_up)) @ w_down


def setup_inputs(seed: int = 0) -> dict:
    key = jax.random.key(seed)
    ks = iter(jax.random.split(key, 32))
    nrm = lambda shape, scale: scale * jax.random.normal(next(ks), shape, jnp.float32)
    L, C = DEPTH, RW_WIDTH
    return {
        'x': nrm((BATCH, SEQ, D_MODEL), 1.0),
        'p': nrm((DEPTH, BATCH, SEQ, PLE_DIM), 1.0),
        'norm_mix': 1.0 + nrm((L, D_MODEL), 0.02),
        'w_in': nrm((L, D_MODEL, IN_COLS), D_MODEL ** -0.5),
        'rw_mu': jax.random.uniform(next(ks), (L, RW_COLS), jnp.float32),
        'rw_w0': jax.random.uniform(next(ks), (L, 2, C), jnp.float32, -5.0, 1.0),
        'rw_w2': nrm((L, 2, RW_DECAY_LORA, C), 0.1),
        'rw_a0': nrm((L, 2, C), 0.5),
        'rw_a2': nrm((L, 2, RW_AAA_LORA, C), 0.5 * RW_AAA_LORA ** -0.5),
        'rw_g2': nrm((L, RW_GATE_LORA, C), RW_GATE_LORA ** -0.5),
        'rw_k_k': 0.85 + nrm((L, C), 0.05),
        'rw_k_a': 1.0 + nrm((L, C), 0.05),
        'rw_r_k': nrm((L, RW_HEADS, HEAD_DIM), 0.1),
        'rw_lnx_g': 1.0 + nrm((L, C), 0.02),
        'rw_lnx_b': nrm((L, C), 0.02),
        'na_bias': nrm((L, NA_HEADS, 2 * NA_KH - 1, 2 * NA_KW - 1), 0.1),
        'w_br_a': nrm((L, RW_WIDTH, D_MODEL), RW_WIDTH ** -0.5),
        'w_br_b': nrm((L, DIL_OUT_WIDTH, D_MODEL), DIL_OUT_WIDTH ** -0.5),
        'w_br_c': nrm((L, NA_WIDTH, D_MODEL), NA_WIDTH ** -0.5),
        'w_out': nrm((L, D_MODEL, D_MODEL), D_MODEL ** -0.5),
        'norm_ffn': 1.0 + nrm((L, D_MODEL), 0.02),
        'w_ffn_gate': nrm((L, D_MODEL, D_FF), D_MODEL ** -0.5),
        'w_ffn_up': nrm((L, D_MODEL, D_FF), D_MODEL ** -0.5),
        'ffn_conv_w': nrm((L, 3, D_FF), 3 ** -0.5),
        'ffn_conv_b': nrm((L, D_FF), 0.02),
        'w_ffn_down': nrm((L, D_FF, D_MODEL), D_FF ** -0.5),
        'norm_ple': 1.0 + nrm((L, D_MODEL), 0.02),
        'w_ple_gate': nrm((L, D_MODEL, D_MODEL), D_MODEL ** -0.5),
        'w_ple': nrm((L, PLE_DIM, D_MODEL), PLE_DIM ** -0.5),
        'norm_final': 1.0 + nrm((D_MODEL,), 0.02),
    }


def reference(x, p, norm_mix, w_in, rw_mu, rw_w0, rw_w2, rw_a0, rw_a2, rw_g2, rw_k_k, rw_k_a,
              rw_r_k, rw_lnx_g, rw_lnx_b, na_bias, w_br_a, w_br_b, w_br_c, w_out, norm_ffn,
              w_ffn_gate, w_ffn_up, ffn_conv_w, ffn_conv_b, w_ffn_down, norm_ple, w_ple_gate,
              w_ple, norm_final):
    T = x.shape[1]
    cos, sin = rope_tables(T)
    for i in range(DEPTH):
        h = rmsnorm(x, norm_mix[i])
        rw_cols, dq, dk, dv, nq, nk, nv, gates = _split(h @ w_in[i], IN_SPLITS)
        ya = rwkv7_bidir(rw_cols, rw_mu[i], rw_w0[i], rw_w2[i], rw_a0[i], rw_a2[i], rw_g2[i],
                         rw_k_k[i], rw_k_a[i], rw_r_k[i], rw_lnx_g[i], rw_lnx_b[i])
        yb = dilated_attention(dq, dk, dv, cos, sin)
        yc = neighbourhood_attention(nq, nk, nv, na_bias[i])
        ga, gb, gc = jnp.split(jax.nn.sigmoid(gates), N_BRANCH, axis=-1)
        merged = ga * (ya @ w_br_a[i]) + gb * (yb @ w_br_b[i]) + gc * (yc @ w_br_c[i])
        x = x + merged @ w_out[i]
        h = rmsnorm(x, norm_ffn[i])
        x = x + conv_ffn(h, w_ffn_gate[i], w_ffn_up[i], ffn_conv_w[i], ffn_conv_b[i], w_ffn_down[i])
        h = rmsnorm(x, norm_ple[i])
        x = x + jax.nn.sigmoid(h @ w_ple_gate[i]) * (p[i] @ w_ple[i])
    return rmsnorm(x, norm_final)
```

```python
import functools

import numpy as np
import jax
import jax.numpy as jnp
from jax import lax
from jax.experimental import pallas as pl
from jax.experimental.pallas import tpu as pltpu

HEAD_DIM = 64
HALF_DIM = HEAD_DIM // 2
SCALE = HEAD_DIM ** -0.5
NORM_EPS = 1e-6
ROPE_THETA = 10000.0
NEG_INF = -1e30

RW_HEADS = 8
RW_WIDTH = RW_HEADS * HEAD_DIM
RW_DECAY_LORA = 32
RW_AAA_LORA = 32
RW_GATE_LORA = 96
RW_LNX_EPS = 64e-5
RW_COLS = 3 * RW_WIDTH + 2 * RW_DECAY_LORA + 2 * RW_AAA_LORA + RW_GATE_LORA
RW_COLS_PAD = 1792

DIL_PATTERNS = ((128, 1), (512, 4), (2048, 16))
DIL_HEADS_PER_GROUP = 4
DIL_GROUP_WIDTH = DIL_HEADS_PER_GROUP * HEAD_DIM
DIL_WIDTH = len(DIL_PATTERNS) * DIL_GROUP_WIDTH

GRID_W = 64
NA_HEADS = 12
NA_WIDTH = NA_HEADS * HEAD_DIM
NA_KH = 8
NA_KW = 16
NA_GROUP_HEADS = 4
NA_GROUP_WIDTH = NA_GROUP_HEADS * HEAD_DIM

N_BRANCH = 3
VMEM_LIMIT = 56 * 1024 * 1024

BF16 = jnp.bfloat16
F32 = jnp.float32
HIGHEST = lax.Precision.HIGHEST


def _params(*sem):
    return pltpu.CompilerParams(dimension_semantics=sem, vmem_limit_bytes=VMEM_LIMIT)


def _rms(x, g):
    ms = jnp.mean(x * x, axis=-1, keepdims=True)
    return x * lax.rsqrt(ms + NORM_EPS) * g


def _sigmoid(x):
    return 1.0 / (1.0 + jnp.exp(-x))


def _dot(a, b):
    return jnp.dot(a, b, preferred_element_type=F32)


def _dot_nt(a, b):
    return lax.dot_general(a, b, (((1,), (1,)), ((), ())), preferred_element_type=F32)


def _dot_hi(a, b):
    return jnp.dot(a, b, preferred_element_type=F32, precision=HIGHEST)


def _norm_matmul_kernel(x_ref, g_ref, w_ref, o_ref, h_ref):
    @pl.when(pl.program_id(1) == 0)
    def _():
        h_ref[...] = _rms(x_ref[...], g_ref[...]).astype(BF16)

    o_ref[...] = _dot(h_ref[...], w_ref[...])


def norm_matmul(x, g, w, bm, bn):
    M, K = x.shape
    N = w.shape[1]
    return pl.pallas_call(
        _norm_matmul_kernel,
        grid=(M // bm, N // bn),
        in_specs=[pl.BlockSpec((bm, K), lambda i, j: (i, 0)),
                  pl.BlockSpec((1, K), lambda i, j: (0, 0)),
                  pl.BlockSpec((K, bn), lambda i, j: (0, j))],
        out_specs=pl.BlockSpec((bm, bn), lambda i, j: (i, j)),
        out_shape=jax.ShapeDtypeStruct((M, N), F32),
        scratch_shapes=[pltpu.VMEM((bm, K), BF16)],
        compiler_params=_params("parallel", "arbitrary"),
        name="norm_in_proj",
    )(x, g, w)


def _softplus(x):
    return jnp.maximum(x, 0.0) + jnp.log1p(jnp.exp(-jnp.abs(x)))


def _rw_prep_kernel(c_ref, cp_ref, cn_ref, mu_ref, w0_ref, w2_ref, a0_ref, a2_ref, g2_ref, kk_ref, ka_ref,
                    seg_ref, r_out, v_out, a_out, dec0_out, dec1_out, kd0_out, kd1_out, b0_out, b1_out,
                    gate_out, buf_ref, *, bt, seq):
    i = pl.program_id(0)
    pos = (i * bt) % seq
    prev_ok = (pos != 0).astype(F32)
    next_ok = (pos + bt != seq).astype(F32)
    buf_ref[0:8, :] = cp_ref[...] * prev_ok
    buf_ref[8:bt + 8, :] = c_ref[...]
    buf_ref[bt + 8:bt + 16, :] = cn_ref[...] * next_ok
    cols = c_ref[...]
    shift = 0.5 * (buf_ref[7:bt + 7, :] + buf_ref[9:bt + 9, :])
    c = cols + mu_ref[...] * (shift - cols)
    C = RW_WIDTH
    r = c[:, 0:C]
    k = c[:, C:2 * C]
    v = c[:, 2 * C:3 * C]
    o = 3 * C
    wd = c[:, o:o + 2 * RW_DECAY_LORA]
    o += 2 * RW_DECAY_LORA
    ad = c[:, o:o + 2 * RW_AAA_LORA]
    o += 2 * RW_AAA_LORA
    gd = c[:, o:o + RW_GATE_LORA]
    seg = seg_ref[...]
    kk = k * kk_ref[...]
    nrm = jnp.sqrt(_dot_hi(kk * kk, seg))
    kk = kk / jnp.maximum(nrm, 1e-12)
    r_out[...] = r
    v_out[...] = v
    a_out[...] = -kk
    gate_out[...] = _dot_hi(_sigmoid(gd), g2_ref[...])
    twd = jnp.tanh(wd)
    for d, (dec_out, kd_out, b_out) in enumerate(((dec0_out, kd0_out, b0_out), (dec1_out, kd1_out, b1_out))):
        lw = _dot_hi(twd[:, d * RW_DECAY_LORA:(d + 1) * RW_DECAY_LORA], w2_ref[d])
        w = -_softplus(-(w0_ref[d:d + 1, :] + lw)) - 0.5
        dec_out[...] = jnp.exp(-jnp.exp(w))
        la = _dot_hi(ad[:, d * RW_AAA_LORA:(d + 1) * RW_AAA_LORA], a2_ref[d])
        lr = _sigmoid(a0_ref[d:d + 1, :] + la)
        kd_out[...] = k * (1.0 + (lr - 1.0) * ka_ref[...])
        b_out[...] = kk * lr


def rw_prep(z, rw_col_block, mu, w0, w2, a0, a2, g2, k_k, k_a, seg, seq, bt):
    M = z.shape[0]
    C = RW_WIDTH
    W = RW_COLS_PAD
    nb8 = M // 8
    full = lambda shape: pl.BlockSpec(shape, lambda i: (0,) * len(shape))
    out = jax.ShapeDtypeStruct((M, C), F32)
    return pl.pallas_call(
        functools.partial(_rw_prep_kernel, bt=bt, seq=seq),
        grid=(M // bt,),
        in_specs=[pl.BlockSpec((bt, W), lambda i: (i, rw_col_block)),
                  pl.BlockSpec((8, W), lambda i: (jnp.maximum(i * (bt // 8) - 1, 0), rw_col_block)),
                  pl.BlockSpec((8, W), lambda i: (jnp.minimum((i + 1) * (bt // 8), nb8 - 1), rw_col_block)),
                  full((1, W)), full((2, C)), full((2, RW_DECAY_LORA, C)), full((2, C)),
                  full((2, RW_AAA_LORA, C)), full((RW_GATE_LORA, C)), full((1, C)), full((1, C)),
                  full((C, C))],
        out_specs=[pl.BlockSpec((bt, C), lambda i: (i, 0))] * 10,
        out_shape=[out] * 10,
        scratch_shapes=[pltpu.VMEM((bt + 16, W), F32)],
        compiler_params=_params("parallel"),
        name="rw_prep",
    )(z, z, z, mu, w0, w2, a0, a2, g2, k_k, k_a, seg)


def _rw_scan_kernel(w_ref, k_ref, a_ref, b_ref, r_ref, v_ref, y_ref, s_ref, *, tb, nv):
    @pl.when(pl.program_id(0) == 0)
    def _():
        s_ref[...] = jnp.zeros_like(s_ref)

    def step(t, carry):
        w = w_ref[t]
        k = k_ref[t]
        a = a_ref[t]
        b = b_ref[t]
        r = r_ref[t]
        vt = v_ref[t]
        rows = []
        for ih in range(nv):
            s = s_ref[ih]
            sa = jnp.sum(s * a, axis=0, keepdims=True)
            s = s * w + sa * b + vt[ih:ih + 1, :] * k
            s_ref[ih] = s
            rows.append(jnp.sum(s * r, axis=0, keepdims=True))
        y_ref[t] = jnp.concatenate(rows, axis=0)
        return carry

    lax.fori_loop(0, tb, step, 0)


def rw_scan(w, k, a, b, r, v, tb):
    T, N, L = w.shape
    nv = v.shape[1]
    kspec = pl.BlockSpec((tb, N, L), lambda i: (i, 0, 0))
    vspec = pl.BlockSpec((tb, nv, L), lambda i: (i, 0, 0))
    return pl.pallas_call(
        functools.partial(_rw_scan_kernel, tb=tb, nv=nv),
        grid=(T // tb,),
        in_specs=[kspec] * 5 + [vspec],
        out_specs=vspec,
        out_shape=jax.ShapeDtypeStruct((T, nv, L), F32),
        scratch_shapes=[pltpu.VMEM((nv, N, L), F32)],
        compiler_params=_params("arbitrary"),
        name="rw_scan",
    )(w, k, a, b, r, v)


def _rw_post_kernel(yf_ref, yb_ref, r_ref, v_ref, kd0_ref, kd1_ref, gate_ref, rk_ref, g_ref, b_ref, seg_ref,
                    o_ref):
    seg = seg_ref[...]
    y = yf_ref[...] + yb_ref[...]
    inv_n = 1.0 / HEAD_DIM
    mean = _dot_hi(y, seg) * inv_n
    d = y - mean
    var = _dot_hi(d * d, seg) * inv_n
    yn = d * lax.rsqrt(var + RW_LNX_EPS) * g_ref[...] + b_ref[...]
    r = r_ref[...]
    rk = rk_ref[...]
    bonus = (_dot_hi(r * kd0_ref[...] * rk, seg) + _dot_hi(r * kd1_ref[...] * rk, seg)) * v_ref[...]
    o_ref[...] = (yn + bonus) * gate_ref[...]


def rw_post(yf, yb, r, v, kd0, kd1, gate, rk, g, b, seg, bt):
    M, C = yf.shape
    tok = pl.BlockSpec((bt, C), lambda i: (i, 0))
    row = pl.BlockSpec((1, C), lambda i: (0, 0))
    return pl.pallas_call(
        _rw_post_kernel,
        grid=(M // bt,),
        in_specs=[tok] * 7 + [row] * 3 + [pl.BlockSpec((C, C), lambda i: (0, 0))],
        out_specs=tok,
        out_shape=jax.ShapeDtypeStruct((M, C), F32),
        compiler_params=_params("parallel"),
        name="rw_post",
    )(yf, yb, r, v, kd0, kd1, gate, rk, g, b, seg)


def _dil_kernel(q_ref, k_ref, v_ref, cos_ref, sin_ref, o_ref, lse_ref, qs_ref, ks_ref, *, n_sub, qb, kwin, half):
    W = DIL_GROUP_WIDTH
    lane = lax.broadcasted_iota(jnp.int32, (1, W), 1)
    first_half = (lane % HEAD_DIM) < HALF_DIM

    def rope(z, c, s):
        swapped = jnp.where(first_half, pltpu.roll(z, W - HALF_DIM, 1), pltpu.roll(z, HALF_DIM, 1))
        return z * c + swapped * s

    rc = min(256, n_sub)

    def rope_chunk(i, carry):
        sl = pl.ds(pl.multiple_of(i * rc, rc), rc)
        c = cos_ref[sl, :]
        s = sin_ref[sl, :]
        qs_ref[sl, :] = (rope(q_ref[0, sl, :], c, s) * SCALE).astype(BF16)
        ks_ref[sl, :] = rope(k_ref[0, sl, :], c, s).astype(BF16)
        return carry

    lax.fori_loop(0, n_sub // rc, rope_chunk, 0)

    def block(i, carry):
        q0 = pl.multiple_of(i * qb, qb)
        k0 = pl.multiple_of(jnp.clip(q0 - (kwin - qb) // 2, 0, n_sub - kwin), 64)
        q = qs_ref[pl.ds(q0, qb), :]
        kk = ks_ref[pl.ds(k0, kwin), :]
        vv = v_ref[0, pl.ds(k0, kwin), :].astype(BF16)
        jq = q0 + lax.broadcasted_iota(jnp.int32, (qb, kwin), 0)
        jk = k0 + lax.broadcasted_iota(jnp.int32, (qb, kwin), 1)
        ok = jnp.abs(jk - jq) <= half
        outs, lses = [], []
        for h in range(DIL_HEADS_PER_GROUP):
            hs = slice(h * HEAD_DIM, (h + 1) * HEAD_DIM)
            s = jnp.where(ok, _dot_nt(q[:, hs], kk[:, hs]), NEG_INF)
            m = jnp.max(s, axis=-1, keepdims=True)
            p = jnp.exp(s - m)
            l = jnp.sum(p, axis=-1, keepdims=True)
            outs.append(_dot(p.astype(BF16), vv[:, hs]) / l)
            lses.append(jnp.broadcast_to(m + jnp.log(l), (qb, HEAD_DIM)))
        o_ref[0, pl.ds(q0, qb), :] = jnp.concatenate(outs, axis=-1)
        lse_ref[0, pl.ds(q0, qb), :] = jnp.concatenate(lses, axis=-1)
        return carry

    lax.fori_loop(0, n_sub // qb, block, 0)


def dilated_group(z3, q_blk, k_blk, v_blk, cos_t, sin_t, window, dil):
    B, T, ncols = z3.shape
    W = DIL_GROUP_WIDTH
    n_sub = T // dil
    half = window // (2 * dil)
    qb = min(128, n_sub)
    kwin = min(qb + 2 * half, n_sub)
    per_row = ncols // W
    zv = z3.reshape(B, n_sub, dil * ncols)
    cv = cos_t.reshape(n_sub, dil * W)
    sv = sin_t.reshape(n_sub, dil * W)
    zspec = lambda blk: pl.BlockSpec((1, n_sub, W), lambda b, r: (b, 0, r * per_row + blk))
    tspec = pl.BlockSpec((n_sub, W), lambda b, r: (0, r))
    ospec = pl.BlockSpec((1, n_sub, W), lambda b, r: (b, 0, r))
    oshape = jax.ShapeDtypeStruct((B, n_sub, dil * W), F32)
    o, lse = pl.pallas_call(
        functools.partial(_dil_kernel, n_sub=n_sub, qb=qb, kwin=kwin, half=half),
        grid=(B, dil),
        in_specs=[zspec(q_blk), zspec(k_blk), zspec(v_blk), tspec, tspec],
        out_specs=[ospec, ospec],
        out_shape=[oshape, oshape],
        scratch_shapes=[pltpu.VMEM((n_sub, W), BF16), pltpu.VMEM((n_sub, W), BF16)],
        compiler_params=_params("parallel", "parallel"),
        name=f"dilated_d{dil}",
    )(zv, zv, zv, cv, sv)
    return o.reshape(B * T, W), lse.reshape(B * T, W)


def _na_kernel(q_ref, k_ref, v_ref, bias_ref, o_ref, *, rows):
    nkeys = NA_KH * GRID_W

    def row(r, carry):
        r0 = jnp.clip(r - NA_KH // 2, 0, rows - NA_KH)
        qrow = pl.ds(pl.multiple_of(r * GRID_W, GRID_W), GRID_W)
        krow = pl.ds(pl.multiple_of(r0 * GRID_W, GRID_W), nkeys)
        q = (q_ref[0, qrow, :] * SCALE).astype(BF16)
        kb = k_ref[0, krow, :].astype(BF16)
        vb = v_ref[0, krow, :].astype(BF16)
        outs = []
        for h in range(NA_GROUP_HEADS):
            hs = slice(h * HEAD_DIM, (h + 1) * HEAD_DIM)
            s = _dot_nt(q[:, hs], kb[:, hs]) + bias_ref[r - r0, h]
            m = jnp.max(s, axis=-1, keepdims=True)
            p = jnp.exp(s - m)
            l = jnp.sum(p, axis=-1, keepdims=True)
            outs.append(_dot(p.astype(BF16), vb[:, hs]) / l)
        o_ref[0, qrow, :] = jnp.concatenate(outs, axis=-1)
        return carry

    lax.fori_loop(0, rows, row, 0)


def na_bias_table(rel_bias):
    qc = np.arange(GRID_W)
    kc = np.arange(GRID_W)
    wc0 = np.clip(qc - NA_KW // 2, 0, GRID_W - NA_KW)
    col_ok = (kc[None, :] >= wc0[:, None]) & (kc[None, :] < wc0[:, None] + NA_KW)
    dx_idx = np.clip(kc[None, :] - qc[:, None], 1 - NA_KW, NA_KW - 1) + NA_KW - 1
    dy_idx = np.arange(NA_KH)[None, :] - np.arange(NA_KH)[:, None] + NA_KH - 1
    tab = rel_bias[:, dy_idx][:, :, :, dx_idx]
    tab = jnp.where(jnp.asarray(col_ok)[None, None, None], tab, NEG_INF)
    tab = jnp.transpose(tab, (1, 0, 3, 2, 4))
    return tab.reshape(NA_KH, rel_bias.shape[0], GRID_W, NA_KH * GRID_W).astype(F32)


def neighbourhood(z3, q_blk0, k_blk0, v_blk0, bias_tab):
    B, T, ncols = z3.shape
    rows = T // GRID_W
    assert rows >= NA_KH
    W = NA_GROUP_WIDTH
    G = NA_HEADS // NA_GROUP_HEADS
    zspec = lambda blk0: pl.BlockSpec((1, T, W), lambda b, g: (b, 0, blk0 + g))
    return pl.pallas_call(
        functools.partial(_na_kernel, rows=rows),
        grid=(B, G),
        in_specs=[zspec(q_blk0), zspec(k_blk0), zspec(v_blk0),
                  pl.BlockSpec((NA_KH, NA_GROUP_HEADS, GRID_W, NA_KH * GRID_W), lambda b, g: (0, g, 0, 0))],
        out_specs=pl.BlockSpec((1, T, W), lambda b, g: (b, 0, g)),
        out_shape=jax.ShapeDtypeStruct((B, T, NA_WIDTH), F32),
        compiler_params=_params("parallel", "parallel"),
        name="neighbourhood",
    )(z3, z3, z3, bias_tab)


def _merge_kernel(ya_ref, o1_ref, o2_ref, o3_ref, l1_ref, l2_ref, l3_ref, yc_ref, ga_ref, gb_ref, gc_ref,
                  wa_ref, wb_ref, wc_ref, out_ref, yb_ref):
    @pl.when(pl.program_id(1) == 0)
    def _():
        l1, l2, l3 = l1_ref[...], l2_ref[...], l3_ref[...]
        m = jnp.maximum(jnp.maximum(l1, l2), l3)
        e1, e2, e3 = jnp.exp(l1 - m), jnp.exp(l2 - m), jnp.exp(l3 - m)
        yb = (e1 * o1_ref[...] + e2 * o2_ref[...] + e3 * o3_ref[...]) / (e1 + e2 + e3)
        yb_ref[...] = yb.astype(BF16)

    pa = _dot(ya_ref[...].astype(BF16), wa_ref[...])
    pb = _dot(yb_ref[...], wb_ref[...])
    pc = _dot(yc_ref[...].astype(BF16), wc_ref[...])
    out = _sigmoid(ga_ref[...]) * pa + _sigmoid(gb_ref[...]) * pb + _sigmoid(gc_ref[...]) * pc
    out_ref[...] = out.astype(BF16)


def merge(ya, dil_outs, yc, z, wa, wb, wc, bm, bn):
    M = ya.shape[0]
    D = wa.shape[1]
    nj = D // bn
    tok = lambda w: pl.BlockSpec((bm, w), lambda i, j: (i, 0))
    gate = lambda g: pl.BlockSpec((bm, bn), lambda i, j: (i, g * nj + j))
    wsp = lambda k: pl.BlockSpec((k, bn), lambda i, j: (0, j))
    (o1, l1), (o2, l2), (o3, l3) = dil_outs
    W = DIL_GROUP_WIDTH
    return pl.pallas_call(
        _merge_kernel,
        grid=(M // bm, nj),
        in_specs=[tok(RW_WIDTH)] + [tok(W)] * 6 + [tok(NA_WIDTH), gate(0), gate(1), gate(2),
                                                  wsp(RW_WIDTH), wsp(W), wsp(NA_WIDTH)],
        out_specs=pl.BlockSpec((bm, bn), lambda i, j: (i, j)),
        out_shape=jax.ShapeDtypeStruct((M, D), BF16),
        scratch_shapes=[pltpu.VMEM((bm, W), BF16)],
        compiler_params=_params("parallel", "arbitrary"),
        name="merge",
    )(ya, o1, o2, o3, l1, l2, l3, yc, z, z, z, wa, wb, wc)


def _matmul_res_kernel(a_ref, w_ref, x_ref, o_ref):
    o_ref[...] = x_ref[...] + _dot(a_ref[...], w_ref[...])


def matmul_res(a, w, x, bm, bn):
    M, K = a.shape
    N = w.shape[1]
    return pl.pallas_call(
        _matmul_res_kernel,
        grid=(M // bm, N // bn),
        in_specs=[pl.BlockSpec((bm, K), lambda i, j: (i, 0)),
                  pl.BlockSpec((K, bn), lambda i, j: (0, j)),
                  pl.BlockSpec((bm, bn), lambda i, j: (i, j))],
        out_specs=pl.BlockSpec((bm, bn), lambda i, j: (i, j)),
        out_shape=jax.ShapeDtypeStruct((M, N), F32),
        compiler_params=_params("parallel", "parallel"),
        name="out_proj",
    )(a, w, x)


def _gelu(x):
    return 0.5 * x * (1.0 + jnp.tanh(np.sqrt(2.0 / np.pi).astype(np.float32) * (x + 0.044715 * (x * x * x))))


def _ffn_kernel(x_ref, xp_ref, xn_ref, g_ref, wg_ref, wu_ref, cw_ref, cb_ref, wd_ref, o_ref,
                h_ref, hh_ref, gs_ref, acc_ref, *, bm, seq):
    i = pl.program_id(0)
    f = pl.program_id(1)

    @pl.when(f == 0)
    def _():
        g = g_ref[...]
        h_ref[...] = _rms(x_ref[...], g).astype(BF16)
        hh_ref[0:8, :] = _rms(xp_ref[...], g).astype(F32)
        hh_ref[8:16, :] = _rms(xn_ref[...], g).astype(F32)
        acc_ref[...] = jnp.zeros_like(acc_ref)

    pos = (i * bm) % seq
    prev_ok = (pos != 0).astype(F32)
    next_ok = (pos + bm != seq).astype(F32)
    wg = wg_ref[...]
    gate = _dot(h_ref[...], wg)
    halo = _dot(hh_ref[...].astype(BF16), wg)
    gs_ref[0:8, :] = halo[0:8] * prev_ok
    gs_ref[8:bm + 8, :] = gate
    gs_ref[bm + 8:bm + 16, :] = halo[8:16] * next_ok
    cw = cw_ref[...]
    gc = gs_ref[7:bm + 7, :] * cw[0:1] + gate * cw[1:2] + gs_ref[9:bm + 9, :] * cw[2:3] + cb_ref[...]
    up = _dot(h_ref[...], wu_ref[...])
    act = (_gelu(gc) * up).astype(BF16)
    acc_ref[...] += _dot(act, wd_ref[...])

    @pl.when(f == pl.num_programs(1) - 1)
    def _():
        o_ref[...] = x_ref[...] + acc_ref[...]


def ffn(x, g, wg, wu, cw, cb, wd, seq, bm, bf):
    M, D = x.shape
    F = wg.shape[1]
    nb8 = M // 8
    return pl.pallas_call(
        functools.partial(_ffn_kernel, bm=bm, seq=seq),
        grid=(M // bm, F // bf),
        in_specs=[pl.BlockSpec((bm, D), lambda i, f: (i, 0)),
                  pl.BlockSpec((8, D), lambda i, f: (jnp.maximum(i * (bm // 8) - 1, 0), 0)),
                  pl.BlockSpec((8, D), lambda i, f: (jnp.minimum((i + 1) * (bm // 8), nb8 - 1), 0)),
                  pl.BlockSpec((1, D), lambda i, f: (0, 0)),
                  pl.BlockSpec((D, bf), lambda i, f: (0, f)),
                  pl.BlockSpec((D, bf), lambda i, f: (0, f)),
                  pl.BlockSpec((3, bf), lambda i, f: (0, f)),
                  pl.BlockSpec((1, bf), lambda i, f: (0, f)),
                  pl.BlockSpec((bf, D), lambda i, f: (f, 0))],
        out_specs=pl.BlockSpec((bm, D), lambda i, f: (i, 0)),
        out_shape=jax.ShapeDtypeStruct((M, D), F32),
        scratch_shapes=[pltpu.VMEM((bm, D), BF16), pltpu.VMEM((16, D), F32),
                        pltpu.VMEM((bm + 16, bf), F32), pltpu.VMEM((bm, D), F32)],
        compiler_params=_params("parallel", "arbitrary"),
        name="conv_ffn",
    )(x, x, x, g, wg, wu, cw, cb, wd)


def _ple_kernel(x_ref, p_ref, g_ref, wg_ref, wp_ref, gf_ref, o_ref, *, final_norm):
    x = x_ref[...]
    h = _rms(x, g_ref[...]).astype(BF16)
    gate = _sigmoid(_dot(h, wg_ref[...]))
    y = x + gate * _dot(p_ref[...].astype(BF16), wp_ref[...])
    if final_norm:
        y = _rms(y, gf_ref[...])
    o_ref[...] = y


def ple(x, p, g, wg, wp, gf, bm, final_norm):
    M, D = x.shape
    P = p.shape[1]
    return pl.pallas_call(
        functools.partial(_ple_kernel, final_norm=final_norm),
        grid=(M // bm,),
        in_specs=[pl.BlockSpec((bm, D), lambda i: (i, 0)),
                  pl.BlockSpec((bm, P), lambda i: (i, 0)),
                  pl.BlockSpec((1, D), lambda i: (0, 0)),
                  pl.BlockSpec((D, D), lambda i: (0, 0)),
                  pl.BlockSpec((P, D), lambda i: (0, 0)),
                  pl.BlockSpec((1, D), lambda i: (0, 0))],
        out_specs=pl.BlockSpec((bm, D), lambda i: (i, 0)),
        out_shape=jax.ShapeDtypeStruct((M, D), F32),
        compiler_params=_params("parallel"),
        name="ple_final" if final_norm else "ple",
    )(x, p, g, wg, wp, gf)


def _to_scan_keys(x0, x1, B, T):
    H, N = RW_HEADS, HEAD_DIM
    z = jnp.stack([x0.reshape(B, T, H, N), x1.reshape(B, T, H, N)[:, ::-1]], axis=0)
    z = jnp.transpose(z, (2, 4, 0, 1, 3)).reshape(T, N, 2 * B * H)
    return jnp.concatenate([z, z], axis=-1)


def _to_scan_values(v, B, T):
    H, N = RW_HEADS, HEAD_DIM
    vh = v.reshape(B, T, H, N)
    z = jnp.stack([vh, vh[:, ::-1]], axis=0)
    z = jnp.transpose(z, (2, 4, 0, 1, 3)).reshape(T, N // 2, 2, 2 * B * H)
    return z.reshape(T, N // 2, 4 * B * H)


def _from_scan(y, B, T):
    H, N = RW_HEADS, HEAD_DIM
    z = y.reshape(T, N, 2, B, H)
    z = jnp.transpose(z, (2, 3, 0, 4, 1))
    yf = z[0].reshape(B * T, H * N)
    yb = z[1][:, ::-1].reshape(B * T, H * N)
    return yf, yb


def _rope_tables(T):
    inv = ROPE_THETA ** (-jnp.arange(0, HEAD_DIM, 2, dtype=jnp.float32) / HEAD_DIM)
    ang = jnp.arange(T, dtype=jnp.float32)[:, None] * inv[None, :]
    cos, sin = jnp.cos(ang), jnp.sin(ang)
    cos_t = jnp.tile(jnp.concatenate([cos, cos], axis=-1), (1, DIL_HEADS_PER_GROUP))
    sin_t = jnp.tile(jnp.concatenate([-sin, sin], axis=-1), (1, DIL_HEADS_PER_GROUP))
    return cos_t, sin_t


def _pick(n, prefs):
    for c in prefs:
        if n % c == 0:
            return c
    return n


def kernel(x, p, norm_mix, w_in, rw_mu, rw_w0, rw_w2, rw_a0, rw_a2, rw_g2, rw_k_k, rw_k_a, rw_r_k, rw_lnx_g,
           rw_lnx_b, na_bias, w_br_a, w_br_b, w_br_c, w_out, norm_ffn, w_ffn_gate, w_ffn_up, ffn_conv_w,
           ffn_conv_b, w_ffn_down, norm_ple, w_ple_gate, w_ple, norm_final):
    B, T, D = x.shape
    depth = w_in.shape[0]
    M = B * T
    F = w_ffn_gate.shape[-1]
    C = RW_WIDTH
    W = DIL_GROUP_WIDTH
    assert (N_BRANCH * D) % W == 0 and w_in.shape[-1] == RW_COLS + 2 * 3 * DIL_WIDTH + N_BRANCH * D

    gates_w = N_BRANCH * D
    attn_w = 3 * DIL_WIDTH + 3 * NA_WIDTH
    ncols = gates_w + attn_w + RW_COLS_PAD
    dil_blk0 = gates_w // W
    na_blk0 = dil_blk0 + 3 * DIL_WIDTH // W
    assert (gates_w + attn_w) % RW_COLS_PAD == 0
    rw_blk = (gates_w + attn_w) // RW_COLS_PAD

    cos_t, sin_t = _rope_tables(T)
    seg = jnp.asarray(np.kron(np.eye(RW_HEADS), np.ones((HEAD_DIM, HEAD_DIM))), F32)
    xf = x.reshape(M, D)
    bm_in = _pick(M, (512, 256, 128))
    bn_in = _pick(ncols, (1792, 896, 256, 128))
    bt_rw = _pick(T, (256, 128))
    tb = _pick(T, (64, 32))

    for i in range(depth):
        w = w_in[i]
        w_pad = jnp.concatenate(
            [w[:, RW_COLS + attn_w:], w[:, RW_COLS:RW_COLS + attn_w], w[:, :RW_COLS],
             jnp.zeros((D, RW_COLS_PAD - RW_COLS), w.dtype)], axis=1).astype(BF16)
        z = norm_matmul(xf, norm_mix[i][None], w_pad, bm_in, bn_in)
        z3 = z.reshape(B, T, ncols)

        mu = jnp.pad(rw_mu[i], (0, RW_COLS_PAD - RW_COLS))[None]
        r, v, a, dec0, dec1, kd0, kd1, b0, b1, gate = rw_prep(
            z, rw_blk, mu, rw_w0[i], rw_w2[i], rw_a0[i], rw_a2[i], rw_g2[i], rw_k_k[i][None], rw_k_a[i][None],
            seg, T, bt_rw)
        y = rw_scan(_to_scan_keys(dec0, dec1, B, T), _to_scan_keys(kd0, kd1, B, T), _to_scan_keys(a, a, B, T),
                    _to_scan_keys(b0, b1, B, T), _to_scan_keys(r, r, B, T), _to_scan_values(v, B, T), tb)
        yf, yb = _from_scan(y, B, T)
        ya = rw_post(yf, yb, r, v, kd0, kd1, gate, rw_r_k[i].reshape(1, C), rw_lnx_g[i][None], rw_lnx_b[i][None],
                     seg, bt_rw)

        dil_outs = []
        for g, (window, dil) in enumerate(DIL_PATTERNS):
            dil_outs.append(dilated_group(z3, dil_blk0 + g, dil_blk0 + 3 + g, dil_blk0 + 6 + g, cos_t, sin_t,
                                          window, dil))

        yc = neighbourhood(z3, na_blk0, na_blk0 + 3, na_blk0 + 6, na_bias_table(na_bias[i])).reshape(M, NA_WIDTH)

        merged = merge(ya, dil_outs, yc, z, w_br_a[i].astype(BF16), w_br_b[i].astype(BF16),
                       w_br_c[i].astype(BF16), _pick(M, (512, 256, 128)), _pick(D, (512, 256, 128)))
        xf = matmul_res(merged, w_out[i].astype(BF16), xf, _pick(M, (1024, 512, 256, 128)),
                        _pick(D, (1024, 512, 256, 128)))

        xf = ffn(xf, norm_ffn[i][None], w_ffn_gate[i].astype(BF16), w_ffn_up[i].astype(BF16), ffn_conv_w[i],
                 ffn_conv_b[i][None], w_ffn_down[i].astype(BF16), T, _pick(T, (512, 256, 128)),
                 _pick(F, (512, 256, 128)))

        xf = ple(xf, p[i].reshape(M, -1), norm_ple[i][None], w_ple_gate[i].astype(BF16), w_ple[i].astype(BF16),
                 norm_final[None], _pick(M, (512, 256, 128)), final_norm=(i == depth - 1))
    return xf.reshape(B, T, D)
```

```python
import functools

import numpy as np
import jax
import jax.numpy as jnp
from jax import lax
from jax.experimental import pallas as pl
from jax.experimental.pallas import tpu as pltpu

HEAD_DIM = 64
HALF_DIM = HEAD_DIM // 2
SCALE = HEAD_DIM ** -0.5
NORM_EPS = 1e-6
ROPE_THETA = 10000.0
NEG_INF = -1e30

RW_HEADS = 8
RW_WIDTH = RW_HEADS * HEAD_DIM
RW_DECAY_LORA = 32
RW_AAA_LORA = 32
RW_GATE_LORA = 96
RW_LNX_EPS = 64e-5
RW_COLS = 3 * RW_WIDTH + 2 * RW_DECAY_LORA + 2 * RW_AAA_LORA + RW_GATE_LORA
RW_COLS_PAD = 1792

DIL_PATTERNS = ((128, 1), (512, 4), (2048, 16))
DIL_HEADS_PER_GROUP = 4
DIL_GROUP_WIDTH = DIL_HEADS_PER_GROUP * HEAD_DIM
DIL_WIDTH = len(DIL_PATTERNS) * DIL_GROUP_WIDTH
DIL_STEP_WIDTH = 2 * HEAD_DIM

GRID_W = 64
NA_HEADS = 12
NA_WIDTH = NA_HEADS * HEAD_DIM
NA_KH = 8
NA_KW = 16
NA_GROUP_HEADS = 4
NA_GROUP_WIDTH = NA_GROUP_HEADS * HEAD_DIM

N_BRANCH = 3
VMEM_LIMIT = 56 * 1024 * 1024

BF16 = jnp.bfloat16
F32 = jnp.float32
HIGHEST = lax.Precision.HIGHEST


def _params(*sem):
    return pltpu.CompilerParams(dimension_semantics=sem, vmem_limit_bytes=VMEM_LIMIT)


def _rms(x, g):
    ms = jnp.mean(x * x, axis=-1, keepdims=True)
    return x * lax.rsqrt(ms + NORM_EPS) * g


def _sigmoid(x):
    return 1.0 / (1.0 + jnp.exp(-x))


def _dot(a, b):
    return jnp.dot(a, b, preferred_element_type=F32)


def _dot_nt(a, b):
    return lax.dot_general(a, b, (((1,), (1,)), ((), ())), preferred_element_type=F32)


def _dot_hi(a, b):
    return jnp.dot(a, b, preferred_element_type=F32, precision=HIGHEST)


def _norm_matmul_kernel(x_ref, g_ref, w_ref, o_ref, h_ref):
    @pl.when(pl.program_id(1) == 0)
    def _():
        h_ref[...] = _rms(x_ref[...], g_ref[...]).astype(BF16)

    o_ref[...] = _dot(h_ref[...], w_ref[...])


def norm_matmul(x, g, w, bm, bn):
    M, K = x.shape
    N = w.shape[1]
    return pl.pallas_call(
        _norm_matmul_kernel,
        grid=(M // bm, N // bn),
        in_specs=[pl.BlockSpec((bm, K), lambda i, j: (i, 0)),
                  pl.BlockSpec((1, K), lambda i, j: (0, 0)),
                  pl.BlockSpec((K, bn), lambda i, j: (0, j))],
        out_specs=pl.BlockSpec((bm, bn), lambda i, j: (i, j)),
        out_shape=jax.ShapeDtypeStruct((M, N), F32),
        scratch_shapes=[pltpu.VMEM((bm, K), BF16)],
        compiler_params=_params("parallel", "arbitrary"),
        name="norm_in_proj",
    )(x, g, w)


def _softplus(x):
    return jnp.maximum(x, 0.0) + jnp.log1p(jnp.exp(-jnp.abs(x)))


def _rw_prep_kernel(c_ref, cp_ref, cn_ref, mu_ref, w0_ref, w2_ref, a0_ref, a2_ref, g2_ref, kk_ref, ka_ref,
                    seg_ref, r_out, v_out, a_out, dec0_out, dec1_out, kd0_out, kd1_out, b0_out, b1_out,
                    gate_out, buf_ref, *, bt, seq):
    i = pl.program_id(0)
    pos = (i * bt) % seq
    prev_ok = (pos != 0).astype(F32)
    next_ok = (pos + bt != seq).astype(F32)
    buf_ref[0:8, :] = cp_ref[...] * prev_ok
    buf_ref[8:bt + 8, :] = c_ref[...]
    buf_ref[bt + 8:bt + 16, :] = cn_ref[...] * next_ok
    cols = c_ref[...]
    shift = 0.5 * (buf_ref[7:bt + 7, :] + buf_ref[9:bt + 9, :])
    c = cols + mu_ref[...] * (shift - cols)
    C = RW_WIDTH
    r = c[:, 0:C]
    k = c[:, C:2 * C]
    v = c[:, 2 * C:3 * C]
    o = 3 * C
    wd = c[:, o:o + 2 * RW_DECAY_LORA]
    o += 2 * RW_DECAY_LORA
    ad = c[:, o:o + 2 * RW_AAA_LORA]
    o += 2 * RW_AAA_LORA
    gd = c[:, o:o + RW_GATE_LORA]
    seg = seg_ref[...]
    kk = k * kk_ref[...]
    nrm = jnp.sqrt(_dot_hi(kk * kk, seg))
    kk = kk / jnp.maximum(nrm, 1e-12)
    r_out[...] = r
    v_out[...] = v
    a_out[...] = -kk
    gate_out[...] = _dot_hi(_sigmoid(gd), g2_ref[...])
    twd = jnp.tanh(wd)
    for d, (dec_out, kd_out, b_out) in enumerate(((dec0_out, kd0_out, b0_out), (dec1_out, kd1_out, b1_out))):
        lw = _dot_hi(twd[:, d * RW_DECAY_LORA:(d + 1) * RW_DECAY_LORA], w2_ref[d])
        w = -_softplus(-(w0_ref[d:d + 1, :] + lw)) - 0.5
        dec_out[...] = jnp.exp(-jnp.exp(w))
        la = _dot_hi(ad[:, d * RW_AAA_LORA:(d + 1) * RW_AAA_LORA], a2_ref[d])
        lr = _sigmoid(a0_ref[d:d + 1, :] + la)
        kd_out[...] = k * (1.0 + (lr - 1.0) * ka_ref[...])
        b_out[...] = kk * lr


def rw_prep(z, rw_col_block, mu, w0, w2, a0, a2, g2, k_k, k_a, seg, seq, bt):
    M = z.shape[0]
    C = RW_WIDTH
    W = RW_COLS_PAD
    nb8 = M // 8
    full = lambda shape: pl.BlockSpec(shape, lambda i: (0,) * len(shape))
    out = jax.ShapeDtypeStruct((M, C), F32)
    return pl.pallas_call(
        functools.partial(_rw_prep_kernel, bt=bt, seq=seq),
        grid=(M // bt,),
        in_specs=[pl.BlockSpec((bt, W), lambda i: (i, rw_col_block)),
                  pl.BlockSpec((8, W), lambda i: (jnp.maximum(i * (bt // 8) - 1, 0), rw_col_block)),
                  pl.BlockSpec((8, W), lambda i: (jnp.minimum((i + 1) * (bt // 8), nb8 - 1), rw_col_block)),
                  full((1, W)), full((2, C)), full((2, RW_DECAY_LORA, C)), full((2, C)),
                  full((2, RW_AAA_LORA, C)), full((RW_GATE_LORA, C)), full((1, C)), full((1, C)),
                  full((C, C))],
        out_specs=[pl.BlockSpec((bt, C), lambda i: (i, 0))] * 10,
        out_shape=[out] * 10,
        scratch_shapes=[pltpu.VMEM((bt + 16, W), F32)],
        compiler_params=_params("parallel"),
        name="rw_prep",
    )(z, z, z, mu, w0, w2, a0, a2, g2, k_k, k_a, seg)


def _rw_scan_kernel(w_ref, k_ref, a_ref, b_ref, r_ref, v_ref, y_ref, s_ref, *, tb, nv):
    @pl.when(pl.program_id(0) == 0)
    def _():
        s_ref[...] = jnp.zeros_like(s_ref)

    def step(t, carry):
        w = w_ref[t]
        k = k_ref[t]
        a = a_ref[t]
        b = b_ref[t]
        r = r_ref[t]
        vt = v_ref[t]
        rows = []
        for ih in range(nv):
            s = s_ref[ih]
            sa = jnp.sum(s * a, axis=0, keepdims=True)
            s = s * w + sa * b + vt[ih:ih + 1, :] * k
            s_ref[ih] = s
            rows.append(jnp.sum(s * r, axis=0, keepdims=True))
        y_ref[t] = jnp.concatenate(rows, axis=0)
        return carry

    lax.fori_loop(0, tb, step, 0)


def rw_scan(w, k, a, b, r, v, tb):
    T, N, L = w.shape
    nv = v.shape[1]
    kspec = pl.BlockSpec((tb, N, L), lambda i: (i, 0, 0))
    vspec = pl.BlockSpec((tb, nv, L), lambda i: (i, 0, 0))
    return pl.pallas_call(
        functools.partial(_rw_scan_kernel, tb=tb, nv=nv),
        grid=(T // tb,),
        in_specs=[kspec] * 5 + [vspec],
        out_specs=vspec,
        out_shape=jax.ShapeDtypeStruct((T, nv, L), F32),
        scratch_shapes=[pltpu.VMEM((nv, N, L), F32)],
        compiler_params=_params("arbitrary"),
        name="rw_scan",
    )(w, k, a, b, r, v)


def _rw_post_kernel(yf_ref, yb_ref, r_ref, v_ref, kd0_ref, kd1_ref, gate_ref, rk_ref, g_ref, b_ref, seg_ref,
                    o_ref):
    seg = seg_ref[...]
    y = yf_ref[...] + yb_ref[...]
    inv_n = 1.0 / HEAD_DIM
    mean = _dot_hi(y, seg) * inv_n
    d = y - mean
    var = _dot_hi(d * d, seg) * inv_n
    yn = d * lax.rsqrt(var + RW_LNX_EPS) * g_ref[...] + b_ref[...]
    r = r_ref[...]
    rk = rk_ref[...]
    bonus = (_dot_hi(r * kd0_ref[...] * rk, seg) + _dot_hi(r * kd1_ref[...] * rk, seg)) * v_ref[...]
    o_ref[...] = (yn + bonus) * gate_ref[...]


def rw_post(yf, yb, r, v, kd0, kd1, gate, rk, g, b, seg, bt):
    M, C = yf.shape
    tok = pl.BlockSpec((bt, C), lambda i: (i, 0))
    row = pl.BlockSpec((1, C), lambda i: (0, 0))
    return pl.pallas_call(
        _rw_post_kernel,
        grid=(M // bt,),
        in_specs=[tok] * 7 + [row] * 3 + [pl.BlockSpec((C, C), lambda i: (0, 0))],
        out_specs=tok,
        out_shape=jax.ShapeDtypeStruct((M, C), F32),
        compiler_params=_params("parallel"),
        name="rw_post",
    )(yf, yb, r, v, kd0, kd1, gate, rk, g, b, seg)


def _dil_kernel(q_ref, k_ref, v_ref, cos_ref, sin_ref, o_ref, lse_ref, qs_ref, ks_ref, vs_ref, *, n_sub, dil, qb,
                kwin, half):
    W = DIL_STEP_WIDTH
    lane = lax.broadcasted_iota(jnp.int32, (1, W), 1)
    first_half = (lane % HEAD_DIM) < HALF_DIM

    def rope(z, c, s):
        swapped = jnp.where(first_half, pltpu.roll(z, W - HALF_DIM, 1), pltpu.roll(z, HALF_DIM, 1))
        return z * c + swapped * s

    rc = min(256, n_sub)

    def subsequence(r, carry):
        def rope_chunk(i, carry):
            rows = pl.ds(r + i * (rc * dil), rc, stride=dil)
            sl = pl.ds(pl.multiple_of(i * rc, rc), rc)
            c = cos_ref[rows, :]
            s = sin_ref[rows, :]
            qs_ref[sl, :] = (rope(q_ref[0, rows, :], c, s) * SCALE).astype(BF16)
            ks_ref[sl, :] = rope(k_ref[0, rows, :], c, s).astype(BF16)
            vs_ref[sl, :] = v_ref[0, rows, :].astype(BF16)
            return carry

        lax.fori_loop(0, n_sub // rc, rope_chunk, 0)

        def block(i, carry):
            q0 = pl.multiple_of(i * qb, qb)
            k0 = pl.multiple_of(jnp.clip(q0 - (kwin - qb) // 2, 0, n_sub - kwin), 64)
            q = qs_ref[pl.ds(q0, qb), :]
            kk = ks_ref[pl.ds(k0, kwin), :]
            vv = vs_ref[pl.ds(k0, kwin), :]
            jq = q0 + lax.broadcasted_iota(jnp.int32, (qb, kwin), 0)
            jk = k0 + lax.broadcasted_iota(jnp.int32, (qb, kwin), 1)
            ok = jnp.abs(jk - jq) <= half
            outs, lses = [], []
            for h in range(W // HEAD_DIM):
                hs = slice(h * HEAD_DIM, (h + 1) * HEAD_DIM)
                s = jnp.where(ok, _dot_nt(q[:, hs], kk[:, hs]), NEG_INF)
                m = jnp.max(s, axis=-1, keepdims=True)
                p = jnp.exp(s - m)
                l = jnp.sum(p, axis=-1, keepdims=True)
                outs.append(_dot(p.astype(BF16), vv[:, hs]) / l)
                lses.append(jnp.broadcast_to(m + jnp.log(l), (qb, HEAD_DIM)))
            rows = pl.ds(r + q0 * dil, qb, stride=dil)
            o_ref[0, rows, :] = jnp.concatenate(outs, axis=-1)
            lse_ref[0, rows, :] = jnp.concatenate(lses, axis=-1)
            return carry

        lax.fori_loop(0, n_sub // qb, block, 0)
        return carry

    lax.fori_loop(0, dil, subsequence, 0)


def dilated_group(z3, q_blk, k_blk, v_blk, cos_t, sin_t, window, dil):
    B, T, ncols = z3.shape
    W = DIL_STEP_WIDTH
    steps = DIL_GROUP_WIDTH // W
    n_sub = T // dil
    half = window // (2 * dil)
    qb = min(128, n_sub)
    kwin = min(qb + 2 * half, n_sub)
    zspec = lambda blk: pl.BlockSpec((1, T, W), lambda b, s: (b, 0, blk * steps + s))
    tspec = pl.BlockSpec((T, W), lambda b, s: (0, 0))
    ospec = pl.BlockSpec((1, T, W), lambda b, s: (b, 0, s))
    oshape = jax.ShapeDtypeStruct((B, T, DIL_GROUP_WIDTH), F32)
    o, lse = pl.pallas_call(
        functools.partial(_dil_kernel, n_sub=n_sub, dil=dil, qb=qb, kwin=kwin, half=half),
        grid=(B, steps),
        in_specs=[zspec(q_blk), zspec(k_blk), zspec(v_blk), tspec, tspec],
        out_specs=[ospec, ospec],
        out_shape=[oshape, oshape],
        scratch_shapes=[pltpu.VMEM((n_sub, W), BF16)] * 3,
        compiler_params=_params("parallel", "parallel"),
        name=f"dilated_d{dil}",
    )(z3, z3, z3, cos_t, sin_t)
    return o.reshape(B * T, DIL_GROUP_WIDTH), lse.reshape(B * T, DIL_GROUP_WIDTH)


def _na_kernel(q_ref, k_ref, v_ref, bias_ref, o_ref, *, rows):
    nkeys = NA_KH * GRID_W

    def row(r, carry):
        r0 = jnp.clip(r - NA_KH // 2, 0, rows - NA_KH)
        qrow = pl.ds(pl.multiple_of(r * GRID_W, GRID_W), GRID_W)
        krow = pl.ds(pl.multiple_of(r0 * GRID_W, GRID_W), nkeys)
        q = (q_ref[0, qrow, :] * SCALE).astype(BF16)
        kb = k_ref[0, krow, :].astype(BF16)
        vb = v_ref[0, krow, :].astype(BF16)
        outs = []
        for h in range(NA_GROUP_HEADS):
            hs = slice(h * HEAD_DIM, (h + 1) * HEAD_DIM)
            s = _dot_nt(q[:, hs], kb[:, hs]) + bias_ref[r - r0, h]
            m = jnp.max(s, axis=-1, keepdims=True)
            p = jnp.exp(s - m)
            l = jnp.sum(p, axis=-1, keepdims=True)
            outs.append(_dot(p.astype(BF16), vb[:, hs]) / l)
        o_ref[0, qrow, :] = jnp.concatenate(outs, axis=-1)
        return carry

    lax.fori_loop(0, rows, row, 0)


def na_bias_table(rel_bias):
    qc = np.arange(GRID_W)
    kc = np.arange(GRID_W)
    wc0 = np.clip(qc - NA_KW // 2, 0, GRID_W - NA_KW)
    col_ok = (kc[None, :] >= wc0[:, None]) & (kc[None, :] < wc0[:, None] + NA_KW)
    dx_idx = np.clip(kc[None, :] - qc[:, None], 1 - NA_KW, NA_KW - 1) + NA_KW - 1
    dy_idx = np.arange(NA_KH)[None, :] - np.arange(NA_KH)[:, None] + NA_KH - 1
    tab = rel_bias[:, dy_idx][:, :, :, dx_idx]
    tab = jnp.where(jnp.asarray(col_ok)[None, None, None], tab, NEG_INF)
    tab = jnp.transpose(tab, (1, 0, 3, 2, 4))
    return tab.reshape(NA_KH, rel_bias.shape[0], GRID_W, NA_KH * GRID_W).astype(F32)


def neighbourhood(z3, q_blk0, k_blk0, v_blk0, bias_tab):
    B, T, ncols = z3.shape
    rows = T // GRID_W
    assert rows >= NA_KH
    W = NA_GROUP_WIDTH
    G = NA_HEADS // NA_GROUP_HEADS
    zspec = lambda blk0: pl.BlockSpec((1, T, W), lambda b, g: (b, 0, blk0 + g))
    return pl.pallas_call(
        functools.partial(_na_kernel, rows=rows),
        grid=(B, G),
        in_specs=[zspec(q_blk0), zspec(k_blk0), zspec(v_blk0),
                  pl.BlockSpec((NA_KH, NA_GROUP_HEADS, GRID_W, NA_KH * GRID_W), lambda b, g: (0, g, 0, 0))],
        out_specs=pl.BlockSpec((1, T, W), lambda b, g: (b, 0, g)),
        out_shape=jax.ShapeDtypeStruct((B, T, NA_WIDTH), F32),
        compiler_params=_params("parallel", "parallel"),
        name="neighbourhood",
    )(z3, z3, z3, bias_tab)


def _merge_kernel(ya_ref, o1_ref, o2_ref, o3_ref, l1_ref, l2_ref, l3_ref, yc_ref, ga_ref, gb_ref, gc_ref,
                  wa_ref, wb_ref, wc_ref, out_ref, yb_ref):
    @pl.when(pl.program_id(1) == 0)
    def _():
        l1, l2, l3 = l1_ref[...], l2_ref[...], l3_ref[...]
        m = jnp.maximum(jnp.maximum(l1, l2), l3)
        e1, e2, e3 = jnp.exp(l1 - m), jnp.exp(l2 - m), jnp.exp(l3 - m)
        yb = (e1 * o1_ref[...] + e2 * o2_ref[...] + e3 * o3_ref[...]) / (e1 + e2 + e3)
        yb_ref[...] = yb.astype(BF16)

    pa = _dot(ya_ref[...].astype(BF16), wa_ref[...])
    pb = _dot(yb_ref[...], wb_ref[...])
    pc = _dot(yc_ref[...].astype(BF16), wc_ref[...])
    out = _sigmoid(ga_ref[...]) * pa + _sigmoid(gb_ref[...]) * pb + _sigmoid(gc_ref[...]) * pc
    out_ref[...] = out.astype(BF16)


def merge(ya, dil_outs, yc, z, wa, wb, wc, bm, bn):
    M = ya.shape[0]
    D = wa.shape[1]
    nj = D // bn
    tok = lambda w: pl.BlockSpec((bm, w), lambda i, j: (i, 0))
    gate = lambda g: pl.BlockSpec((bm, bn), lambda i, j: (i, g * nj + j))
    wsp = lambda k: pl.BlockSpec((k, bn), lambda i, j: (0, j))
    (o1, l1), (o2, l2), (o3, l3) = dil_outs
    W = DIL_GROUP_WIDTH
    return pl.pallas_call(
        _merge_kernel,
        grid=(M // bm, nj),
        in_specs=[tok(RW_WIDTH)] + [tok(W)] * 6 + [tok(NA_WIDTH), gate(0), gate(1), gate(2),
                                                  wsp(RW_WIDTH), wsp(W), wsp(NA_WIDTH)],
        out_specs=pl.BlockSpec((bm, bn), lambda i, j: (i, j)),
        out_shape=jax.ShapeDtypeStruct((M, D), BF16),
        scratch_shapes=[pltpu.VMEM((bm, W), BF16)],
        compiler_params=_params("parallel", "arbitrary"),
        name="merge",
    )(ya, o1, o2, o3, l1, l2, l3, yc, z, z, z, wa, wb, wc)


def _matmul_res_kernel(a_ref, w_ref, x_ref, o_ref):
    o_ref[...] = x_ref[...] + _dot(a_ref[...], w_ref[...])


def matmul_res(a, w, x, bm, bn):
    M, K = a.shape
    N = w.shape[1]
    return pl.pallas_call(
        _matmul_res_kernel,
        grid=(M // bm, N // bn),
        in_specs=[pl.BlockSpec((bm, K), lambda i, j: (i, 0)),
                  pl.BlockSpec((K, bn), lambda i, j: (0, j)),
                  pl.BlockSpec((bm, bn), lambda i, j: (i, j))],
        out_specs=pl.BlockSpec((bm, bn), lambda i, j: (i, j)),
        out_shape=jax.ShapeDtypeStruct((M, N), F32),
        compiler_params=_params("parallel", "parallel"),
        name="out_proj",
    )(a, w, x)


def _gelu(x):
    return 0.5 * x * (1.0 + jnp.tanh(np.sqrt(2.0 / np.pi).astype(np.float32) * (x + 0.044715 * (x * x * x))))


def _ffn_kernel(x_ref, xp_ref, xn_ref, g_ref, wg_ref, wu_ref, cw_ref, cb_ref, wd_ref, o_ref,
                h_ref, hh_ref, gs_ref, acc_ref, *, bm, seq):
    i = pl.program_id(0)
    f = pl.program_id(1)

    @pl.when(f == 0)
    def _():
        g = g_ref[...]
        h_ref[...] = _rms(x_ref[...], g).astype(BF16)
        hh_ref[0:8, :] = _rms(xp_ref[...], g).astype(F32)
        hh_ref[8:16, :] = _rms(xn_ref[...], g).astype(F32)
        acc_ref[...] = jnp.zeros_like(acc_ref)

    pos = (i * bm) % seq
    prev_ok = (pos != 0).astype(F32)
    next_ok = (pos + bm != seq).astype(F32)
    wg = wg_ref[...]
    gate = _dot(h_ref[...], wg)
    halo = _dot(hh_ref[...].astype(BF16), wg)
    gs_ref[0:8, :] = halo[0:8] * prev_ok
    gs_ref[8:bm + 8, :] = gate
    gs_ref[bm + 8:bm + 16, :] = halo[8:16] * next_ok
    cw = cw_ref[...]
    gc = gs_ref[7:bm + 7, :] * cw[0:1] + gate * cw[1:2] + gs_ref[9:bm + 9, :] * cw[2:3] + cb_ref[...]
    up = _dot(h_ref[...], wu_ref[...])
    act = (_gelu(gc) * up).astype(BF16)
    acc_ref[...] += _dot(act, wd_ref[...])

    @pl.when(f == pl.num_programs(1) - 1)
    def _():
        o_ref[...] = x_ref[...] + acc_ref[...]


def ffn(x, g, wg, wu, cw, cb, wd, seq, bm, bf):
    M, D = x.shape
    F = wg.shape[1]
    nb8 = M // 8
    return pl.pallas_call(
        functools.partial(_ffn_kernel, bm=bm, seq=seq),
        grid=(M // bm, F // bf),
        in_specs=[pl.BlockSpec((bm, D), lambda i, f: (i, 0)),
                  pl.BlockSpec((8, D), lambda i, f: (jnp.maximum(i * (bm // 8) - 1, 0), 0)),
                  pl.BlockSpec((8, D), lambda i, f: (jnp.minimum((i + 1) * (bm // 8), nb8 - 1), 0)),
                  pl.BlockSpec((1, D), lambda i, f: (0, 0)),
                  pl.BlockSpec((D, bf), lambda i, f: (0, f)),
                  pl.BlockSpec((D, bf), lambda i, f: (0, f)),
                  pl.BlockSpec((3, bf), lambda i, f: (0, f)),
                  pl.BlockSpec((1, bf), lambda i, f: (0, f)),
                  pl.BlockSpec((bf, D), lambda i, f: (f, 0))],
        out_specs=pl.BlockSpec((bm, D), lambda i, f: (i, 0)),
        out_shape=jax.ShapeDtypeStruct((M, D), F32),
        scratch_shapes=[pltpu.VMEM((bm, D), BF16), pltpu.VMEM((16, D), F32),
                        pltpu.VMEM((bm + 16, bf), F32), pltpu.VMEM((bm, D), F32)],
        compiler_params=_params("parallel", "arbitrary"),
        name="conv_ffn",
    )(x, x, x, g, wg, wu, cw, cb, wd)


def _ple_kernel(x_ref, p_ref, g_ref, wg_ref, wp_ref, gf_ref, o_ref, *, final_norm):
    x = x_ref[...]
    h = _rms(x, g_ref[...]).astype(BF16)
    gate = _sigmoid(_dot(h, wg_ref[...]))
    y = x + gate * _dot(p_ref[...].astype(BF16), wp_ref[...])
    if final_norm:
        y = _rms(y, gf_ref[...])
    o_ref[...] = y


def ple(x, p, g, wg, wp, gf, bm, final_norm):
    M, D = x.shape
    P = p.shape[1]
    return pl.pallas_call(
        functools.partial(_ple_kernel, final_norm=final_norm),
        grid=(M // bm,),
        in_specs=[pl.BlockSpec((bm, D), lambda i: (i, 0)),
                  pl.BlockSpec((bm, P), lambda i: (i, 0)),
                  pl.BlockSpec((1, D), lambda i: (0, 0)),
                  pl.BlockSpec((D, D), lambda i: (0, 0)),
                  pl.BlockSpec((P, D), lambda i: (0, 0)),
                  pl.BlockSpec((1, D), lambda i: (0, 0))],
        out_specs=pl.BlockSpec((bm, D), lambda i: (i, 0)),
        out_shape=jax.ShapeDtypeStruct((M, D), F32),
        compiler_params=_params("parallel"),
        name="ple_final" if final_norm else "ple",
    )(x, p, g, wg, wp, gf)


def _to_scan_kernel(x0_ref, x1_ref, rev_ref, o_ref, a_ref, *, nb, tb, dup):
    N = HEAD_DIM
    lanes = 2 * 2 * nb * RW_HEADS
    rev = rev_ref[...]
    for d, ref in enumerate((x0_ref, x1_ref)):
        for b in range(nb):
            x = ref[b]
            if d == 1:
                x = _dot_hi(rev, x)
            xt = x.T
            p0 = (d * nb + b) * RW_HEADS
            if dup:
                for h in range(RW_HEADS):
                    blk = xt[h * N:(h + 1) * N]
                    a_ref[(p0 + h) * 2 * N:(p0 + h) * 2 * N + N, :] = blk
                    a_ref[(p0 + h) * 2 * N + N:(p0 + h + 1) * 2 * N, :] = blk
            else:
                a_ref[p0 * N:(p0 + RW_HEADS) * N, :] = xt
    n_rows = N if dup else N // 2
    for j in range(n_rows):
        rows = a_ref[pl.ds(j, lanes, stride=n_rows), :]
        o_ref[pl.ds(j, tb, stride=n_rows), :] = rows.T


def to_scan(x0, x1, rev, B, T, tb, dup):
    C = RW_WIDTH
    N = HEAD_DIM
    nblk = T // tb
    lanes = 4 * B * RW_HEADS
    n_rows = N if dup else N // 2
    out = pl.pallas_call(
        functools.partial(_to_scan_kernel, nb=B, tb=tb, dup=dup),
        grid=(nblk,),
        in_specs=[pl.BlockSpec((B, tb, C), lambda i: (0, i, 0)),
                  pl.BlockSpec((B, tb, C), lambda i: (0, nblk - 1 - i, 0)),
                  pl.BlockSpec((tb, tb), lambda i: (0, 0))],
        out_specs=pl.BlockSpec((tb * n_rows, lanes), lambda i: (i, 0)),
        out_shape=jax.ShapeDtypeStruct((T * n_rows, lanes), F32),
        scratch_shapes=[pltpu.VMEM((2 * B * RW_HEADS * (2 * N if dup else N), tb), F32)],
        compiler_params=_params("parallel"),
        name="to_scan_keys" if dup else "to_scan_values",
    )(x0.reshape(B, T, C), x1.reshape(B, T, C), rev)
    return out.reshape(T, n_rows, lanes)


def _from_scan_kernel(y_ref, rev_ref, yf_ref, yb_ref, a_ref, *, nb, tb):
    N = HEAD_DIM
    C = RW_WIDTH
    lanes = 2 * 2 * nb * RW_HEADS
    for ih in range(N // 2):
        rows = y_ref[pl.ds(ih, tb, stride=N // 2), :]
        a_ref[pl.ds(ih, lanes, stride=N // 2), :] = rows.T
    rev = rev_ref[...]
    for d, ref in enumerate((yf_ref, yb_ref)):
        for b in range(nb):
            y = a_ref[(d * nb + b) * C:(d * nb + b + 1) * C, :].T
            if d == 1:
                y = _dot_hi(rev, y)
            ref[b] = y


def from_scan(y, rev, B, T, tb):
    C = RW_WIDTH
    N = HEAD_DIM
    nblk = T // tb
    lanes = y.shape[-1]
    out = jax.ShapeDtypeStruct((B, T, C), F32)
    yf, yb = pl.pallas_call(
        functools.partial(_from_scan_kernel, nb=B, tb=tb),
        grid=(nblk,),
        in_specs=[pl.BlockSpec((tb * (N // 2), lanes), lambda i: (i, 0)),
                  pl.BlockSpec((tb, tb), lambda i: (0, 0))],
        out_specs=[pl.BlockSpec((B, tb, C), lambda i: (0, i, 0)),
                   pl.BlockSpec((B, tb, C), lambda i: (0, nblk - 1 - i, 0))],
        out_shape=[out, out],
        scratch_shapes=[pltpu.VMEM((2 * B * RW_HEADS * N, tb), F32)],
        compiler_params=_params("parallel"),
        name="from_scan",
    )(y.reshape(T * (N // 2), lanes), rev)
    return yf.reshape(B * T, C), yb.reshape(B * T, C)


def _rope_tables(T):
    inv = ROPE_THETA ** (-jnp.arange(0, HEAD_DIM, 2, dtype=jnp.float32) / HEAD_DIM)
    ang = jnp.arange(T, dtype=jnp.float32)[:, None] * inv[None, :]
    cos, sin = jnp.cos(ang), jnp.sin(ang)
    cos_t = jnp.tile(jnp.concatenate([cos, cos], axis=-1), (1, DIL_STEP_WIDTH // HEAD_DIM))
    sin_t = jnp.tile(jnp.concatenate([-sin, sin], axis=-1), (1, DIL_STEP_WIDTH // HEAD_DIM))
    return cos_t, sin_t


def _pick(n, prefs):
    for c in prefs:
        if n % c == 0:
            return c
    return n


def kernel(x, p, norm_mix, w_in, rw_mu, rw_w0, rw_w2, rw_a0, rw_a2, rw_g2, rw_k_k, rw_k_a, rw_r_k, rw_lnx_g,
           rw_lnx_b, na_bias, w_br_a, w_br_b, w_br_c, w_out, norm_ffn, w_ffn_gate, w_ffn_up, ffn_conv_w,
           ffn_conv_b, w_ffn_down, norm_ple, w_ple_gate, w_ple, norm_final):
    B, T, D = x.shape
    depth = w_in.shape[0]
    M = B * T
    F = w_ffn_gate.shape[-1]
    C = RW_WIDTH
    W = DIL_GROUP_WIDTH
    assert (N_BRANCH * D) % W == 0 and w_in.shape[-1] == RW_COLS + 2 * 3 * DIL_WIDTH + N_BRANCH * D

    gates_w = N_BRANCH * D
    attn_w = 3 * DIL_WIDTH + 3 * NA_WIDTH
    ncols = gates_w + attn_w + RW_COLS_PAD
    dil_blk0 = gates_w // W
    na_blk0 = dil_blk0 + 3 * DIL_WIDTH // W
    assert (gates_w + attn_w) % RW_COLS_PAD == 0
    rw_blk = (gates_w + attn_w) // RW_COLS_PAD

    cos_t, sin_t = _rope_tables(T)
    seg = jnp.asarray(np.kron(np.eye(RW_HEADS), np.ones((HEAD_DIM, HEAD_DIM))), F32)
    xf = x.reshape(M, D)
    bm_in = _pick(M, (512, 256, 128))
    bn_in = _pick(ncols, (1792, 896, 256, 128))
    bt_rw = _pick(T, (256, 128))
    tb = _pick(T, (64, 32))
    tt = _pick(T, (128,))
    rev = jnp.asarray(np.eye(tt)[::-1], F32)

    for i in range(depth):
        w = w_in[i]
        w_pad = jnp.concatenate(
            [w[:, RW_COLS + attn_w:], w[:, RW_COLS:RW_COLS + attn_w], w[:, :RW_COLS],
             jnp.zeros((D, RW_COLS_PAD - RW_COLS), w.dtype)], axis=1).astype(BF16)
        z = norm_matmul(xf, norm_mix[i][None], w_pad, bm_in, bn_in)
        z3 = z.reshape(B, T, ncols)

        mu = jnp.pad(rw_mu[i], (0, RW_COLS_PAD - RW_COLS))[None]
        r, v, a, dec0, dec1, kd0, kd1, b0, b1, gate = rw_prep(
            z, rw_blk, mu, rw_w0[i], rw_w2[i], rw_a0[i], rw_a2[i], rw_g2[i], rw_k_k[i][None], rw_k_a[i][None],
            seg, T, bt_rw)
        keys = lambda x0, x1: to_scan(x0, x1, rev, B, T, tt, dup=True)
        y = rw_scan(keys(dec0, dec1), keys(kd0, kd1), keys(a, a), keys(b0, b1), keys(r, r),
                    to_scan(v, v, rev, B, T, tt, dup=False), tb)
        yf, yb = from_scan(y, rev, B, T, tt)
        ya = rw_post(yf, yb, r, v, kd0, kd1, gate, rw_r_k[i].reshape(1, C), rw_lnx_g[i][None], rw_lnx_b[i][None],
                     seg, bt_rw)

        dil_outs = []
        for g, (window, dil) in enumerate(DIL_PATTERNS):
            dil_outs.append(dilated_group(z3, dil_blk0 + g, dil_blk0 + 3 + g, dil_blk0 + 6 + g, cos_t, sin_t,
                                          window, dil))

        yc = neighbourhood(z3, na_blk0, na_blk0 + 3, na_blk0 + 6, na_bias_table(na_bias[i])).reshape(M, NA_WIDTH)

        merged = merge(ya, dil_outs, yc, z, w_br_a[i].astype(BF16), w_br_b[i].astype(BF16),
                       w_br_c[i].astype(BF16), _pick(M, (512, 256, 128)), _pick(D, (512, 256, 128)))
        xf = matmul_res(merged, w_out[i].astype(BF16), xf, _pick(M, (1024, 512, 256, 128)),
                        _pick(D, (1024, 512, 256, 128)))

        xf = ffn(xf, norm_ffn[i][None], w_ffn_gate[i].astype(BF16), w_ffn_up[i].astype(BF16), ffn_conv_w[i],
                 ffn_conv_b[i][None], w_ffn_down[i].astype(BF16), T, _pick(T, (512, 256, 128)),
                 _pick(F, (512, 256, 128)))

        xf = ple(xf, p[i].reshape(M, -1), norm_ple[i][None], w_ple_gate[i].astype(BF16), w_ple[i].astype(BF16),
                 norm_final[None], _pick(M, (512, 256, 128)), final_norm=(i == depth - 1))
    return xf.reshape(B, T, D)
```

```python
import functools

import numpy as np
import jax
import jax.numpy as jnp
from jax import lax
from jax.experimental import pallas as pl
from jax.experimental.pallas import tpu as pltpu

HEAD_DIM = 64
HALF_DIM = HEAD_DIM // 2
SCALE = HEAD_DIM ** -0.5
NORM_EPS = 1e-6
ROPE_THETA = 10000.0
NEG_INF = -1e30

RW_HEADS = 8
RW_WIDTH = RW_HEADS * HEAD_DIM
RW_DECAY_LORA = 32
RW_AAA_LORA = 32
RW_GATE_LORA = 96
RW_LNX_EPS = 64e-5
RW_COLS = 3 * RW_WIDTH + 2 * RW_DECAY_LORA + 2 * RW_AAA_LORA + RW_GATE_LORA
RW_COLS_PAD = 1792
KEY_PITCH = HEAD_DIM + 4
SCAN_UNROLL = 32

DIL_PATTERNS = ((128, 1), (512, 4), (2048, 16))
DIL_HEADS_PER_GROUP = 4
DIL_GROUP_WIDTH = DIL_HEADS_PER_GROUP * HEAD_DIM
DIL_WIDTH = len(DIL_PATTERNS) * DIL_GROUP_WIDTH
DIL_STEP_WIDTH = 2 * HEAD_DIM

GRID_W = 64
NA_HEADS = 12
NA_WIDTH = NA_HEADS * HEAD_DIM
NA_KH = 8
NA_KW = 16
NA_GROUP_HEADS = 4
NA_GROUP_WIDTH = NA_GROUP_HEADS * HEAD_DIM

N_BRANCH = 3
VMEM_LIMIT = 56 * 1024 * 1024

BF16 = jnp.bfloat16
F32 = jnp.float32
HIGHEST = lax.Precision.HIGHEST


def _params(*sem):
    return pltpu.CompilerParams(dimension_semantics=sem, vmem_limit_bytes=VMEM_LIMIT)


def _rms(x, g):
    ms = jnp.mean(x * x, axis=-1, keepdims=True)
    return x * lax.rsqrt(ms + NORM_EPS) * g


def _sigmoid(x):
    return 1.0 / (1.0 + jnp.exp(-x))


def _dot(a, b):
    return jnp.dot(a, b, preferred_element_type=F32)


def _dot_nt(a, b):
    return lax.dot_general(a, b, (((1,), (1,)), ((), ())), preferred_element_type=F32)


def _dot_hi(a, b):
    return jnp.dot(a, b, preferred_element_type=F32, precision=HIGHEST)


def _norm_matmul_kernel(x_ref, g_ref, w_ref, o_ref, h_ref):
    @pl.when(pl.program_id(1) == 0)
    def _():
        h_ref[...] = _rms(x_ref[...], g_ref[...]).astype(BF16)

    o_ref[...] = _dot(h_ref[...], w_ref[...])


def norm_matmul(x, g, w, bm, bn):
    M, K = x.shape
    N = w.shape[1]
    return pl.pallas_call(
        _norm_matmul_kernel,
        grid=(M // bm, N // bn),
        in_specs=[pl.BlockSpec((bm, K), lambda i, j: (i, 0)),
                  pl.BlockSpec((1, K), lambda i, j: (0, 0)),
                  pl.BlockSpec((K, bn), lambda i, j: (0, j))],
        out_specs=pl.BlockSpec((bm, bn), lambda i, j: (i, j)),
        out_shape=jax.ShapeDtypeStruct((M, N), F32),
        scratch_shapes=[pltpu.VMEM((bm, K), BF16)],
        compiler_params=_params("parallel", "arbitrary"),
        name="norm_in_proj",
    )(x, g, w)


def _softplus(x):
    return jnp.maximum(x, 0.0) + jnp.log1p(jnp.exp(-jnp.abs(x)))


def _rw_prep_kernel(c_ref, cp_ref, cn_ref, mu_ref, w0_ref, w2_ref, a0_ref, a2_ref, g2_ref, kk_ref, ka_ref,
                    seg_ref, r_out, v_out, a_out, dec0_out, dec1_out, kd0_out, kd1_out, b0_out, b1_out,
                    gate_out, buf_ref, *, bt, seq):
    i = pl.program_id(0)
    pos = (i * bt) % seq
    prev_ok = (pos != 0).astype(F32)
    next_ok = (pos + bt != seq).astype(F32)
    buf_ref[0:8, :] = cp_ref[...] * prev_ok
    buf_ref[8:bt + 8, :] = c_ref[...]
    buf_ref[bt + 8:bt + 16, :] = cn_ref[...] * next_ok
    cols = c_ref[...]
    shift = 0.5 * (buf_ref[7:bt + 7, :] + buf_ref[9:bt + 9, :])
    c = cols + mu_ref[...] * (shift - cols)
    C = RW_WIDTH
    r = c[:, 0:C]
    k = c[:, C:2 * C]
    v = c[:, 2 * C:3 * C]
    o = 3 * C
    wd = c[:, o:o + 2 * RW_DECAY_LORA]
    o += 2 * RW_DECAY_LORA
    ad = c[:, o:o + 2 * RW_AAA_LORA]
    o += 2 * RW_AAA_LORA
    gd = c[:, o:o + RW_GATE_LORA]
    seg = seg_ref[...]
    kk = k * kk_ref[...]
    nrm = jnp.sqrt(_dot_hi(kk * kk, seg))
    kk = kk / jnp.maximum(nrm, 1e-12)
    r_out[...] = r
    v_out[...] = v
    a_out[...] = -kk
    gate_out[...] = _dot_hi(_sigmoid(gd), g2_ref[...])
    twd = jnp.tanh(wd)
    for d, (dec_out, kd_out, b_out) in enumerate(((dec0_out, kd0_out, b0_out), (dec1_out, kd1_out, b1_out))):
        lw = _dot_hi(twd[:, d * RW_DECAY_LORA:(d + 1) * RW_DECAY_LORA], w2_ref[d])
        w = -_softplus(-(w0_ref[d:d + 1, :] + lw)) - 0.5
        dec_out[...] = jnp.exp(-jnp.exp(w))
        la = _dot_hi(ad[:, d * RW_AAA_LORA:(d + 1) * RW_AAA_LORA], a2_ref[d])
        lr = _sigmoid(a0_ref[d:d + 1, :] + la)
        kd_out[...] = k * (1.0 + (lr - 1.0) * ka_ref[...])
        b_out[...] = kk * lr


def rw_prep(z, rw_col_block, mu, w0, w2, a0, a2, g2, k_k, k_a, seg, seq, bt):
    M = z.shape[0]
    C = RW_WIDTH
    W = RW_COLS_PAD
    nb8 = M // 8
    full = lambda shape: pl.BlockSpec(shape, lambda i: (0,) * len(shape))
    out = jax.ShapeDtypeStruct((M, C), F32)
    return pl.pallas_call(
        functools.partial(_rw_prep_kernel, bt=bt, seq=seq),
        grid=(M // bt,),
        in_specs=[pl.BlockSpec((bt, W), lambda i: (i, rw_col_block)),
                  pl.BlockSpec((8, W), lambda i: (jnp.maximum(i * (bt // 8) - 1, 0), rw_col_block)),
                  pl.BlockSpec((8, W), lambda i: (jnp.minimum((i + 1) * (bt // 8), nb8 - 1), rw_col_block)),
                  full((1, W)), full((2, C)), full((2, RW_DECAY_LORA, C)), full((2, C)),
                  full((2, RW_AAA_LORA, C)), full((RW_GATE_LORA, C)), full((1, C)), full((1, C)),
                  full((C, C))],
        out_specs=[pl.BlockSpec((bt, C), lambda i: (i, 0))] * 10,
        out_shape=[out] * 10,
        scratch_shapes=[pltpu.VMEM((bt + 16, W), F32)],
        compiler_params=_params("parallel"),
        name="rw_prep",
    )(z, z, z, mu, w0, w2, a0, a2, g2, k_k, k_a, seg)


def _rw_scan_kernel(w_ref, k_ref, a_ref, b_ref, r_ref, v_ref, y_ref, s_ref, *, tb, pitch):
    N = HEAD_DIM

    @pl.when(pl.program_id(0) == 0)
    def _():
        s_ref[...] = jnp.zeros_like(s_ref)

    def step(t, carry):
        base = t * pitch
        row = lambda ref, j: ref[pl.ds(base + j, 1), :]
        vt = v_ref[t]
        zero = jnp.zeros_like(vt)

        def dot_a(c, acc):
            acc0, acc1 = acc
            for u in range(SCAN_UNROLL):
                j = c * SCAN_UNROLL + u
                term = s_ref[j] * row(a_ref, j)
                if u % 2 == 0:
                    acc0 = acc0 + term
                else:
                    acc1 = acc1 + term
            return acc0, acc1

        sa0, sa1 = lax.fori_loop(0, N // SCAN_UNROLL, dot_a, (zero, zero))
        sa = sa0 + sa1

        def update(c, acc):
            acc0, acc1 = acc
            for u in range(SCAN_UNROLL):
                j = c * SCAN_UNROLL + u
                s = s_ref[j] * row(w_ref, j) + sa * row(b_ref, j) + vt * row(k_ref, j)
                s_ref[j] = s
                term = s * row(r_ref, j)
                if u % 2 == 0:
                    acc0 = acc0 + term
                else:
                    acc1 = acc1 + term
            return acc0, acc1

        y0, y1 = lax.fori_loop(0, N // SCAN_UNROLL, update, (zero, zero))
        y_ref[t] = y0 + y1
        return carry

    lax.fori_loop(0, tb, step, 0)


def rw_scan(w, k, a, b, r, v, tb, pitch):
    T, nv, L = v.shape
    kspec = pl.BlockSpec((tb * pitch, L), lambda i: (i, 0))
    vspec = pl.BlockSpec((tb, nv, L), lambda i: (i, 0, 0))
    return pl.pallas_call(
        functools.partial(_rw_scan_kernel, tb=tb, pitch=pitch),
        grid=(T // tb,),
        in_specs=[kspec] * 5 + [vspec],
        out_specs=vspec,
        out_shape=jax.ShapeDtypeStruct((T, nv, L), F32),
        scratch_shapes=[pltpu.VMEM((HEAD_DIM, nv, L), F32)],
        compiler_params=_params("arbitrary"),
        name="rw_scan",
    )(w, k, a, b, r, v)


def _rw_post_kernel(yf_ref, yb_ref, r_ref, v_ref, kd0_ref, kd1_ref, gate_ref, rk_ref, g_ref, b_ref, seg_ref,
                    o_ref):
    seg = seg_ref[...]
    y = yf_ref[...] + yb_ref[...]
    inv_n = 1.0 / HEAD_DIM
    mean = _dot_hi(y, seg) * inv_n
    d = y - mean
    var = _dot_hi(d * d, seg) * inv_n
    yn = d * lax.rsqrt(var + RW_LNX_EPS) * g_ref[...] + b_ref[...]
    r = r_ref[...]
    rk = rk_ref[...]
    bonus = (_dot_hi(r * kd0_ref[...] * rk, seg) + _dot_hi(r * kd1_ref[...] * rk, seg)) * v_ref[...]
    o_ref[...] = (yn + bonus) * gate_ref[...]


def rw_post(yf, yb, r, v, kd0, kd1, gate, rk, g, b, seg, bt):
    M, C = yf.shape
    tok = pl.BlockSpec((bt, C), lambda i: (i, 0))
    row = pl.BlockSpec((1, C), lambda i: (0, 0))
    return pl.pallas_call(
        _rw_post_kernel,
        grid=(M // bt,),
        in_specs=[tok] * 7 + [row] * 3 + [pl.BlockSpec((C, C), lambda i: (0, 0))],
        out_specs=tok,
        out_shape=jax.ShapeDtypeStruct((M, C), F32),
        compiler_params=_params("parallel"),
        name="rw_post",
    )(yf, yb, r, v, kd0, kd1, gate, rk, g, b, seg)


def _dil_kernel(q_ref, k_ref, v_ref, cos_ref, sin_ref, o_ref, lse_ref, qs_ref, ks_ref, vs_ref, *, n_sub, dil, qb,
                kwin, half):
    W = DIL_STEP_WIDTH
    lane = lax.broadcasted_iota(jnp.int32, (1, W), 1)
    first_half = (lane % HEAD_DIM) < HALF_DIM

    def rope(z, c, s):
        swapped = jnp.where(first_half, pltpu.roll(z, W - HALF_DIM, 1), pltpu.roll(z, HALF_DIM, 1))
        return z * c + swapped * s

    rc = min(256, n_sub)

    def subsequence(r, carry):
        def rope_chunk(i, carry):
            rows = pl.ds(r + i * (rc * dil), rc, stride=dil)
            sl = pl.ds(pl.multiple_of(i * rc, rc), rc)
            c = cos_ref[rows, :]
            s = sin_ref[rows, :]
            qs_ref[sl, :] = (rope(q_ref[0, rows, :], c, s) * SCALE).astype(BF16)
            ks_ref[sl, :] = rope(k_ref[0, rows, :], c, s).astype(BF16)
            vs_ref[sl, :] = v_ref[0, rows, :].astype(BF16)
            return carry

        lax.fori_loop(0, n_sub // rc, rope_chunk, 0)

        def block(i, carry):
            q0 = pl.multiple_of(i * qb, qb)
            k0 = pl.multiple_of(jnp.clip(q0 - (kwin - qb) // 2, 0, n_sub - kwin), 64)
            q = qs_ref[pl.ds(q0, qb), :]
            kk = ks_ref[pl.ds(k0, kwin), :]
            vv = vs_ref[pl.ds(k0, kwin), :]
            jq = q0 + lax.broadcasted_iota(jnp.int32, (qb, kwin), 0)
            jk = k0 + lax.broadcasted_iota(jnp.int32, (qb, kwin), 1)
            ok = jnp.abs(jk - jq) <= half
            outs, lses = [], []
            for h in range(W // HEAD_DIM):
                hs = slice(h * HEAD_DIM, (h + 1) * HEAD_DIM)
                s = jnp.where(ok, _dot_nt(q[:, hs], kk[:, hs]), NEG_INF)
                m = jnp.max(s, axis=-1, keepdims=True)
                p = jnp.exp(s - m)
                l = jnp.sum(p, axis=-1, keepdims=True)
                outs.append(_dot(p.astype(BF16), vv[:, hs]) / l)
                lses.append(jnp.broadcast_to(m + jnp.log(l), (qb, HEAD_DIM)))
            rows = pl.ds(r + q0 * dil, qb, stride=dil)
            o_ref[0, rows, :] = jnp.concatenate(outs, axis=-1)
            lse_ref[0, rows, :] = jnp.concatenate(lses, axis=-1)
            return carry

        lax.fori_loop(0, n_sub // qb, block, 0)
        return carry

    lax.fori_loop(0, dil, subsequence, 0)


def dilated_group(z3, q_blk, k_blk, v_blk, cos_t, sin_t, window, dil):
    B, T, ncols = z3.shape
    W = DIL_STEP_WIDTH
    steps = DIL_GROUP_WIDTH // W
    n_sub = T // dil
    half = window // (2 * dil)
    qb = min(128, n_sub)
    kwin = min(qb + 2 * half, n_sub)
    zspec = lambda blk: pl.BlockSpec((1, T, W), lambda b, s: (b, 0, blk * steps + s))
    tspec = pl.BlockSpec((T, W), lambda b, s: (0, 0))
    ospec = pl.BlockSpec((1, T, W), lambda b, s: (b, 0, s))
    oshape = jax.ShapeDtypeStruct((B, T, DIL_GROUP_WIDTH), F32)
    o, lse = pl.pallas_call(
        functools.partial(_dil_kernel, n_sub=n_sub, dil=dil, qb=qb, kwin=kwin, half=half),
        grid=(B, steps),
        in_specs=[zspec(q_blk), zspec(k_blk), zspec(v_blk), tspec, tspec],
        out_specs=[ospec, ospec],
        out_shape=[oshape, oshape],
        scratch_shapes=[pltpu.VMEM((n_sub, W), BF16)] * 3,
        compiler_params=_params("parallel", "parallel"),
        name=f"dilated_d{dil}",
    )(z3, z3, z3, cos_t, sin_t)
    return o.reshape(B * T, DIL_GROUP_WIDTH), lse.reshape(B * T, DIL_GROUP_WIDTH)


def _na_kernel(q_ref, k_ref, v_ref, bias_ref, o_ref, *, rows):
    nkeys = NA_KH * GRID_W

    def row(r, carry):
        r0 = jnp.clip(r - NA_KH // 2, 0, rows - NA_KH)
        qrow = pl.ds(pl.multiple_of(r * GRID_W, GRID_W), GRID_W)
        krow = pl.ds(pl.multiple_of(r0 * GRID_W, GRID_W), nkeys)
        q = (q_ref[0, qrow, :] * SCALE).astype(BF16)
        kb = k_ref[0, krow, :].astype(BF16)
        vb = v_ref[0, krow, :].astype(BF16)
        outs = []
        for h in range(NA_GROUP_HEADS):
            hs = slice(h * HEAD_DIM, (h + 1) * HEAD_DIM)
            s = _dot_nt(q[:, hs], kb[:, hs]) + bias_ref[r - r0, h]
            m = jnp.max(s, axis=-1, keepdims=True)
            p = jnp.exp(s - m)
            l = jnp.sum(p, axis=-1, keepdims=True)
            outs.append(_dot(p.astype(BF16), vb[:, hs]) / l)
        o_ref[0, qrow, :] = jnp.concatenate(outs, axis=-1)
        return carry

    lax.fori_loop(0, rows, row, 0)


def na_bias_table(rel_bias):
    qc = np.arange(GRID_W)
    kc = np.arange(GRID_W)
    wc0 = np.clip(qc - NA_KW // 2, 0, GRID_W - NA_KW)
    col_ok = (kc[None, :] >= wc0[:, None]) & (kc[None, :] < wc0[:, None] + NA_KW)
    dx_idx = np.clip(kc[None, :] - qc[:, None], 1 - NA_KW, NA_KW - 1) + NA_KW - 1
    dy_idx = np.arange(NA_KH)[None, :] - np.arange(NA_KH)[:, None] + NA_KH - 1
    tab = rel_bias[:, dy_idx][:, :, :, dx_idx]
    tab = jnp.where(jnp.asarray(col_ok)[None, None, None], tab, NEG_INF)
    tab = jnp.transpose(tab, (1, 0, 3, 2, 4))
    return tab.reshape(NA_KH, rel_bias.shape[0], GRID_W, NA_KH * GRID_W).astype(F32)


def neighbourhood(z3, q_blk0, k_blk0, v_blk0, bias_tab):
    B, T, ncols = z3.shape
    rows = T // GRID_W
    assert rows >= NA_KH
    W = NA_GROUP_WIDTH
    G = NA_HEADS // NA_GROUP_HEADS
    zspec = lambda blk0: pl.BlockSpec((1, T, W), lambda b, g: (b, 0, blk0 + g))
    return pl.pallas_call(
        functools.partial(_na_kernel, rows=rows),
        grid=(B, G),
        in_specs=[zspec(q_blk0), zspec(k_blk0), zspec(v_blk0),
                  pl.BlockSpec((NA_KH, NA_GROUP_HEADS, GRID_W, NA_KH * GRID_W), lambda b, g: (0, g, 0, 0))],
        out_specs=pl.BlockSpec((1, T, W), lambda b, g: (b, 0, g)),
        out_shape=jax.ShapeDtypeStruct((B, T, NA_WIDTH), F32),
        compiler_params=_params("parallel", "parallel"),
        name="neighbourhood",
    )(z3, z3, z3, bias_tab)


def _merge_kernel(ya_ref, o1_ref, o2_ref, o3_ref, l1_ref, l2_ref, l3_ref, yc_ref, ga_ref, gb_ref, gc_ref,
                  wa_ref, wb_ref, wc_ref, out_ref, yb_ref):
    @pl.when(pl.program_id(1) == 0)
    def _():
        l1, l2, l3 = l1_ref[...], l2_ref[...], l3_ref[...]
        m = jnp.maximum(jnp.maximum(l1, l2), l3)
        e1, e2, e3 = jnp.exp(l1 - m), jnp.exp(l2 - m), jnp.exp(l3 - m)
        yb = (e1 * o1_ref[...] + e2 * o2_ref[...] + e3 * o3_ref[...]) / (e1 + e2 + e3)
        yb_ref[...] = yb.astype(BF16)

    pa = _dot(ya_ref[...].astype(BF16), wa_ref[...])
    pb = _dot(yb_ref[...], wb_ref[...])
    pc = _dot(yc_ref[...].astype(BF16), wc_ref[...])
    out = _sigmoid(ga_ref[...]) * pa + _sigmoid(gb_ref[...]) * pb + _sigmoid(gc_ref[...]) * pc
    out_ref[...] = out.astype(BF16)


def merge(ya, dil_outs, yc, z, wa, wb, wc, bm, bn):
    M = ya.shape[0]
    D = wa.shape[1]
    nj = D // bn
    tok = lambda w: pl.BlockSpec((bm, w), lambda i, j: (i, 0))
    gate = lambda g: pl.BlockSpec((bm, bn), lambda i, j: (i, g * nj + j))
    wsp = lambda k: pl.BlockSpec((k, bn), lambda i, j: (0, j))
    (o1, l1), (o2, l2), (o3, l3) = dil_outs
    W = DIL_GROUP_WIDTH
    return pl.pallas_call(
        _merge_kernel,
        grid=(M // bm, nj),
        in_specs=[tok(RW_WIDTH)] + [tok(W)] * 6 + [tok(NA_WIDTH), gate(0), gate(1), gate(2),
                                                  wsp(RW_WIDTH), wsp(W), wsp(NA_WIDTH)],
        out_specs=pl.BlockSpec((bm, bn), lambda i, j: (i, j)),
        out_shape=jax.ShapeDtypeStruct((M, D), BF16),
        scratch_shapes=[pltpu.VMEM((bm, W), BF16)],
        compiler_params=_params("parallel", "arbitrary"),
        name="merge",
    )(ya, o1, o2, o3, l1, l2, l3, yc, z, z, z, wa, wb, wc)


def _matmul_res_kernel(a_ref, w_ref, x_ref, o_ref):
    o_ref[...] = x_ref[...] + _dot(a_ref[...], w_ref[...])


def matmul_res(a, w, x, bm, bn):
    M, K = a.shape
    N = w.shape[1]
    return pl.pallas_call(
        _matmul_res_kernel,
        grid=(M // bm, N // bn),
        in_specs=[pl.BlockSpec((bm, K), lambda i, j: (i, 0)),
                  pl.BlockSpec((K, bn), lambda i, j: (0, j)),
                  pl.BlockSpec((bm, bn), lambda i, j: (i, j))],
        out_specs=pl.BlockSpec((bm, bn), lambda i, j: (i, j)),
        out_shape=jax.ShapeDtypeStruct((M, N), F32),
        compiler_params=_params("parallel", "parallel"),
        name="out_proj",
    )(a, w, x)


def _gelu(x):
    return 0.5 * x * (1.0 + jnp.tanh(np.sqrt(2.0 / np.pi).astype(np.float32) * (x + 0.044715 * (x * x * x))))


def _ffn_kernel(x_ref, xp_ref, xn_ref, g_ref, wg_ref, wu_ref, cw_ref, cb_ref, wd_ref, o_ref,
                h_ref, hh_ref, gs_ref, acc_ref, *, bm, seq):
    i = pl.program_id(0)
    f = pl.program_id(1)

    @pl.when(f == 0)
    def _():
        g = g_ref[...]
        h_ref[...] = _rms(x_ref[...], g).astype(BF16)
        hh_ref[0:8, :] = _rms(xp_ref[...], g).astype(F32)
        hh_ref[8:16, :] = _rms(xn_ref[...], g).astype(F32)
        acc_ref[...] = jnp.zeros_like(acc_ref)

    pos = (i * bm) % seq
    prev_ok = (pos != 0).astype(F32)
    next_ok = (pos + bm != seq).astype(F32)
    wg = wg_ref[...]
    gate = _dot(h_ref[...], wg)
    halo = _dot(hh_ref[...].astype(BF16), wg)
    gs_ref[0:8, :] = halo[0:8] * prev_ok
    gs_ref[8:bm + 8, :] = gate
    gs_ref[bm + 8:bm + 16, :] = halo[8:16] * next_ok
    cw = cw_ref[...]
    gc = gs_ref[7:bm + 7, :] * cw[0:1] + gate * cw[1:2] + gs_ref[9:bm + 9, :] * cw[2:3] + cb_ref[...]
    up = _dot(h_ref[...], wu_ref[...])
    act = (_gelu(gc) * up).astype(BF16)
    acc_ref[...] += _dot(act, wd_ref[...])

    @pl.when(f == pl.num_programs(1) - 1)
    def _():
        o_ref[...] = x_ref[...] + acc_ref[...]


def ffn(x, g, wg, wu, cw, cb, wd, seq, bm, bf):
    M, D = x.shape
    F = wg.shape[1]
    nb8 = M // 8
    return pl.pallas_call(
        functools.partial(_ffn_kernel, bm=bm, seq=seq),
        grid=(M // bm, F // bf),
        in_specs=[pl.BlockSpec((bm, D), lambda i, f: (i, 0)),
                  pl.BlockSpec((8, D), lambda i, f: (jnp.maximum(i * (bm // 8) - 1, 0), 0)),
                  pl.BlockSpec((8, D), lambda i, f: (jnp.minimum((i + 1) * (bm // 8), nb8 - 1), 0)),
                  pl.BlockSpec((1, D), lambda i, f: (0, 0)),
                  pl.BlockSpec((D, bf), lambda i, f: (0, f)),
                  pl.BlockSpec((D, bf), lambda i, f: (0, f)),
                  pl.BlockSpec((3, bf), lambda i, f: (0, f)),
                  pl.BlockSpec((1, bf), lambda i, f: (0, f)),
                  pl.BlockSpec((bf, D), lambda i, f: (f, 0))],
        out_specs=pl.BlockSpec((bm, D), lambda i, f: (i, 0)),
        out_shape=jax.ShapeDtypeStruct((M, D), F32),
        scratch_shapes=[pltpu.VMEM((bm, D), BF16), pltpu.VMEM((16, D), F32),
                        pltpu.VMEM((bm + 16, bf), F32), pltpu.VMEM((bm, D), F32)],
        compiler_params=_params("parallel", "arbitrary"),
        name="conv_ffn",
    )(x, x, x, g, wg, wu, cw, cb, wd)


def _ple_kernel(x_ref, p_ref, g_ref, wg_ref, wp_ref, gf_ref, o_ref, *, final_norm):
    x = x_ref[...]
    h = _rms(x, g_ref[...]).astype(BF16)
    gate = _sigmoid(_dot(h, wg_ref[...]))
    y = x + gate * _dot(p_ref[...].astype(BF16), wp_ref[...])
    if final_norm:
        y = _rms(y, gf_ref[...])
    o_ref[...] = y


def ple(x, p, g, wg, wp, gf, bm, final_norm):
    M, D = x.shape
    P = p.shape[1]
    return pl.pallas_call(
        functools.partial(_ple_kernel, final_norm=final_norm),
        grid=(M // bm,),
        in_specs=[pl.BlockSpec((bm, D), lambda i: (i, 0)),
                  pl.BlockSpec((bm, P), lambda i: (i, 0)),
                  pl.BlockSpec((1, D), lambda i: (0, 0)),
                  pl.BlockSpec((D, D), lambda i: (0, 0)),
                  pl.BlockSpec((P, D), lambda i: (0, 0)),
                  pl.BlockSpec((1, D), lambda i: (0, 0))],
        out_specs=pl.BlockSpec((bm, D), lambda i: (i, 0)),
        out_shape=jax.ShapeDtypeStruct((M, D), F32),
        compiler_params=_params("parallel"),
        name="ple_final" if final_norm else "ple",
    )(x, p, g, wg, wp, gf)


def _to_scan_kernel(x0_ref, x1_ref, rev_ref, o_ref, a_ref, *, nb, tb, dup, pitch):
    N = HEAD_DIM
    lanes = 2 * 2 * nb * RW_HEADS
    rev = rev_ref[...]
    for d, ref in enumerate((x0_ref, x1_ref)):
        for b in range(nb):
            x = ref[b]
            if d == 1:
                x = _dot_hi(rev, x)
            xt = x.T
            p0 = (d * nb + b) * RW_HEADS
            if dup:
                for h in range(RW_HEADS):
                    blk = xt[h * N:(h + 1) * N]
                    a_ref[(p0 + h) * 2 * pitch:(p0 + h) * 2 * pitch + N, :] = blk
                    a_ref[(p0 + h) * 2 * pitch + pitch:(p0 + h) * 2 * pitch + pitch + N, :] = blk
            else:
                a_ref[p0 * N:(p0 + RW_HEADS) * N, :] = xt
    n_rows = N if dup else N // 2
    for j in range(n_rows):
        rows = a_ref[pl.ds(j, lanes, stride=pitch), :]
        o_ref[pl.ds(j, tb, stride=pitch), :] = rows.T
    for j in range(n_rows, pitch):
        o_ref[pl.ds(j, tb, stride=pitch), :] = jnp.zeros((tb, lanes), F32)


def to_scan(x0, x1, rev, B, T, tb, dup, pitch):
    C = RW_WIDTH
    N = HEAD_DIM
    nblk = T // tb
    lanes = 4 * B * RW_HEADS
    return pl.pallas_call(
        functools.partial(_to_scan_kernel, nb=B, tb=tb, dup=dup, pitch=pitch),
        grid=(nblk,),
        in_specs=[pl.BlockSpec((B, tb, C), lambda i: (0, i, 0)),
                  pl.BlockSpec((B, tb, C), lambda i: (0, nblk - 1 - i, 0)),
                  pl.BlockSpec((tb, tb), lambda i: (0, 0))],
        out_specs=pl.BlockSpec((tb * pitch, lanes), lambda i: (i, 0)),
        out_shape=jax.ShapeDtypeStruct((T * pitch, lanes), F32),
        scratch_shapes=[pltpu.VMEM((2 * B * RW_HEADS * (2 * pitch if dup else N), tb), F32)],
        compiler_params=_params("parallel"),
        name="to_scan_keys" if dup else "to_scan_values",
    )(x0.reshape(B, T, C), x1.reshape(B, T, C), rev)


def _from_scan_kernel(y_ref, rev_ref, yf_ref, yb_ref, a_ref, *, nb, tb):
    N = HEAD_DIM
    C = RW_WIDTH
    lanes = 2 * 2 * nb * RW_HEADS
    for ih in range(N // 2):
        rows = y_ref[pl.ds(ih, tb, stride=N // 2), :]
        a_ref[pl.ds(ih, lanes, stride=N // 2), :] = rows.T
    rev = rev_ref[...]
    for d, ref in enumerate((yf_ref, yb_ref)):
        for b in range(nb):
            y = a_ref[(d * nb + b) * C:(d * nb + b + 1) * C, :].T
            if d == 1:
                y = _dot_hi(rev, y)
            ref[b] = y


def from_scan(y, rev, B, T, tb):
    C = RW_WIDTH
    N = HEAD_DIM
    nblk = T // tb
    lanes = y.shape[-1]
    out = jax.ShapeDtypeStruct((B, T, C), F32)
    yf, yb = pl.pallas_call(
        functools.partial(_from_scan_kernel, nb=B, tb=tb),
        grid=(nblk,),
        in_specs=[pl.BlockSpec((tb * (N // 2), lanes), lambda i: (i, 0)),
                  pl.BlockSpec((tb, tb), lambda i: (0, 0))],
        out_specs=[pl.BlockSpec((B, tb, C), lambda i: (0, i, 0)),
                   pl.BlockSpec((B, tb, C), lambda i: (0, nblk - 1 - i, 0))],
        out_shape=[out, out],
        scratch_shapes=[pltpu.VMEM((2 * B * RW_HEADS * N, tb), F32)],
        compiler_params=_params("parallel"),
        name="from_scan",
    )(y.reshape(T * (N // 2), lanes), rev)
    return yf.reshape(B * T, C), yb.reshape(B * T, C)


def _rope_tables(T):
    inv = ROPE_THETA ** (-jnp.arange(0, HEAD_DIM, 2, dtype=jnp.float32) / HEAD_DIM)
    ang = jnp.arange(T, dtype=jnp.float32)[:, None] * inv[None, :]
    cos, sin = jnp.cos(ang), jnp.sin(ang)
    cos_t = jnp.tile(jnp.concatenate([cos, cos], axis=-1), (1, DIL_STEP_WIDTH // HEAD_DIM))
    sin_t = jnp.tile(jnp.concatenate([-sin, sin], axis=-1), (1, DIL_STEP_WIDTH // HEAD_DIM))
    return cos_t, sin_t


def _pick(n, prefs):
    for c in prefs:
        if n % c == 0:
            return c
    return n


def kernel(x, p, norm_mix, w_in, rw_mu, rw_w0, rw_w2, rw_a0, rw_a2, rw_g2, rw_k_k, rw_k_a, rw_r_k, rw_lnx_g,
           rw_lnx_b, na_bias, w_br_a, w_br_b, w_br_c, w_out, norm_ffn, w_ffn_gate, w_ffn_up, ffn_conv_w,
           ffn_conv_b, w_ffn_down, norm_ple, w_ple_gate, w_ple, norm_final):
    B, T, D = x.shape
    depth = w_in.shape[0]
    M = B * T
    F = w_ffn_gate.shape[-1]
    C = RW_WIDTH
    W = DIL_GROUP_WIDTH
    assert (N_BRANCH * D) % W == 0 and w_in.shape[-1] == RW_COLS + 2 * 3 * DIL_WIDTH + N_BRANCH * D

    gates_w = N_BRANCH * D
    attn_w = 3 * DIL_WIDTH + 3 * NA_WIDTH
    ncols = gates_w + attn_w + RW_COLS_PAD
    dil_blk0 = gates_w // W
    na_blk0 = dil_blk0 + 3 * DIL_WIDTH // W
    assert (gates_w + attn_w) % RW_COLS_PAD == 0
    rw_blk = (gates_w + attn_w) // RW_COLS_PAD

    cos_t, sin_t = _rope_tables(T)
    seg = jnp.asarray(np.kron(np.eye(RW_HEADS), np.ones((HEAD_DIM, HEAD_DIM))), F32)
    xf = x.reshape(M, D)
    bm_in = _pick(M, (512, 256, 128))
    bn_in = _pick(ncols, (1792, 896, 256, 128))
    bt_rw = _pick(T, (256, 128))
    tb = _pick(T, (64, 32))
    tt = _pick(T, (128,))
    rev = jnp.asarray(np.eye(tt)[::-1], F32)

    for i in range(depth):
        w = w_in[i]
        w_pad = jnp.concatenate(
            [w[:, RW_COLS + attn_w:], w[:, RW_COLS:RW_COLS + attn_w], w[:, :RW_COLS],
             jnp.zeros((D, RW_COLS_PAD - RW_COLS), w.dtype)], axis=1).astype(BF16)
        z = norm_matmul(xf, norm_mix[i][None], w_pad, bm_in, bn_in)
        z3 = z.reshape(B, T, ncols)

        mu = jnp.pad(rw_mu[i], (0, RW_COLS_PAD - RW_COLS))[None]
        r, v, a, dec0, dec1, kd0, kd1, b0, b1, gate = rw_prep(
            z, rw_blk, mu, rw_w0[i], rw_w2[i], rw_a0[i], rw_a2[i], rw_g2[i], rw_k_k[i][None], rw_k_a[i][None],
            seg, T, bt_rw)
        keys = lambda x0, x1: to_scan(x0, x1, rev, B, T, tt, dup=True, pitch=KEY_PITCH)
        vals = to_scan(v, v, rev, B, T, tt, dup=False, pitch=HEAD_DIM // 2).reshape(T, HEAD_DIM // 2, -1)
        y = rw_scan(keys(dec0, dec1), keys(kd0, kd1), keys(a, a), keys(b0, b1), keys(r, r), vals, tb, KEY_PITCH)
        yf, yb = from_scan(y, rev, B, T, tt)
        ya = rw_post(yf, yb, r, v, kd0, kd1, gate, rw_r_k[i].reshape(1, C), rw_lnx_g[i][None], rw_lnx_b[i][None],
                     seg, bt_rw)

        dil_outs = []
        for g, (window, dil) in enumerate(DIL_PATTERNS):
            dil_outs.append(dilated_group(z3, dil_blk0 + g, dil_blk0 + 3 + g, dil_blk0 + 6 + g, cos_t, sin_t,
                                          window, dil))

        yc = neighbourhood(z3, na_blk0, na_blk0 + 3, na_blk0 + 6, na_bias_table(na_bias[i])).reshape(M, NA_WIDTH)

        merged = merge(ya, dil_outs, yc, z, w_br_a[i].astype(BF16), w_br_b[i].astype(BF16),
                       w_br_c[i].astype(BF16), _pick(M, (512, 256, 128)), _pick(D, (512, 256, 128)))
        xf = matmul_res(merged, w_out[i].astype(BF16), xf, _pick(M, (1024, 512, 256, 128)),
                        _pick(D, (1024, 512, 256, 128)))

        xf = ffn(xf, norm_ffn[i][None], w_ffn_gate[i].astype(BF16), w_ffn_up[i].astype(BF16), ffn_conv_w[i],
                 ffn_conv_b[i][None], w_ffn_down[i].astype(BF16), T, _pick(T, (512, 256, 128)),
                 _pick(F, (512, 256, 128)))

        xf = ple(xf, p[i].reshape(M, -1), norm_ple[i][None], w_ple_gate[i].astype(BF16), w_ple[i].astype(BF16),
                 norm_final[None], _pick(M, (512, 256, 128)), final_norm=(i == depth - 1))
    return xf.reshape(B, T, D)
```

```python
import functools

import numpy as np
import jax
import jax.numpy as jnp
from jax import lax
from jax.experimental import pallas as pl
from jax.experimental.pallas import tpu as pltpu

HEAD_DIM = 64
HALF_DIM = HEAD_DIM // 2
SCALE = HEAD_DIM ** -0.5
NORM_EPS = 1e-6
ROPE_THETA = 10000.0
NEG_INF = -1e30

RW_HEADS = 8
RW_WIDTH = RW_HEADS * HEAD_DIM
RW_DECAY_LORA = 32
RW_AAA_LORA = 32
RW_GATE_LORA = 96
RW_LNX_EPS = 64e-5
RW_COLS = 3 * RW_WIDTH + 2 * RW_DECAY_LORA + 2 * RW_AAA_LORA + RW_GATE_LORA
RW_COLS_PAD = 1792
KEY_PITCH = HEAD_DIM + 4
SCAN_UNROLL = 32

DIL_PATTERNS = ((128, 1), (512, 4), (2048, 16))
DIL_HEADS_PER_GROUP = 4
DIL_GROUP_WIDTH = DIL_HEADS_PER_GROUP * HEAD_DIM
DIL_WIDTH = len(DIL_PATTERNS) * DIL_GROUP_WIDTH
DIL_STEP_WIDTH = 2 * HEAD_DIM

GRID_W = 64
NA_HEADS = 12
NA_WIDTH = NA_HEADS * HEAD_DIM
NA_KH = 8
NA_KW = 16
NA_GROUP_HEADS = 4
NA_GROUP_WIDTH = NA_GROUP_HEADS * HEAD_DIM
NA_ROW_UNROLL = 4

N_BRANCH = 3
VMEM_LIMIT = 56 * 1024 * 1024

BF16 = jnp.bfloat16
F32 = jnp.float32
HIGHEST = lax.Precision.HIGHEST


def _params(*sem):
    return pltpu.CompilerParams(dimension_semantics=sem, vmem_limit_bytes=VMEM_LIMIT)


def _rms(x, g):
    ms = jnp.mean(x * x, axis=-1, keepdims=True)
    return x * lax.rsqrt(ms + NORM_EPS) * g


def _sigmoid(x):
    return 1.0 / (1.0 + jnp.exp(-x))


def _dot(a, b):
    return jnp.dot(a, b, preferred_element_type=F32)


def _dot_nt(a, b):
    return lax.dot_general(a, b, (((1,), (1,)), ((), ())), preferred_element_type=F32)


def _dot_hi(a, b):
    return jnp.dot(a, b, preferred_element_type=F32, precision=HIGHEST)


def _norm_matmul_kernel(x_ref, g_ref, w_ref, o_ref, h_ref):
    @pl.when(pl.program_id(1) == 0)
    def _():
        h_ref[...] = _rms(x_ref[...], g_ref[...]).astype(BF16)

    o_ref[...] = _dot(h_ref[...], w_ref[...])


def norm_matmul(x, g, w, bm, bn):
    M, K = x.shape
    N = w.shape[1]
    return pl.pallas_call(
        _norm_matmul_kernel,
        grid=(M // bm, N // bn),
        in_specs=[pl.BlockSpec((bm, K), lambda i, j: (i, 0)),
                  pl.BlockSpec((1, K), lambda i, j: (0, 0)),
                  pl.BlockSpec((K, bn), lambda i, j: (0, j))],
        out_specs=pl.BlockSpec((bm, bn), lambda i, j: (i, j)),
        out_shape=jax.ShapeDtypeStruct((M, N), F32),
        scratch_shapes=[pltpu.VMEM((bm, K), BF16)],
        compiler_params=_params("parallel", "arbitrary"),
        name="norm_in_proj",
    )(x, g, w)


def _softplus(x):
    return jnp.maximum(x, 0.0) + jnp.log1p(jnp.exp(-jnp.abs(x)))


def _rw_prep_kernel(c_ref, cp_ref, cn_ref, mu_ref, w0_ref, w2_ref, a0_ref, a2_ref, g2_ref, kk_ref, ka_ref,
                    seg_ref, r_out, v_out, a_out, dec0_out, dec1_out, kd0_out, kd1_out, b0_out, b1_out,
                    gate_out, buf_ref, *, bt, seq):
    i = pl.program_id(0)
    pos = (i * bt) % seq
    prev_ok = (pos != 0).astype(F32)
    next_ok = (pos + bt != seq).astype(F32)
    buf_ref[0:8, :] = cp_ref[...] * prev_ok
    buf_ref[8:bt + 8, :] = c_ref[...]
    buf_ref[bt + 8:bt + 16, :] = cn_ref[...] * next_ok
    cols = c_ref[...]
    shift = 0.5 * (buf_ref[7:bt + 7, :] + buf_ref[9:bt + 9, :])
    c = cols + mu_ref[...] * (shift - cols)
    C = RW_WIDTH
    r = c[:, 0:C]
    k = c[:, C:2 * C]
    v = c[:, 2 * C:3 * C]
    o = 3 * C
    wd = c[:, o:o + 2 * RW_DECAY_LORA]
    o += 2 * RW_DECAY_LORA
    ad = c[:, o:o + 2 * RW_AAA_LORA]
    o += 2 * RW_AAA_LORA
    gd = c[:, o:o + RW_GATE_LORA]
    seg = seg_ref[...]
    kk = k * kk_ref[...]
    nrm = jnp.sqrt(_dot_hi(kk * kk, seg))
    kk = kk / jnp.maximum(nrm, 1e-12)
    r_out[...] = r
    v_out[...] = v
    a_out[...] = -kk
    gate_out[...] = _dot_hi(_sigmoid(gd), g2_ref[...])
    twd = jnp.tanh(wd)
    for d, (dec_out, kd_out, b_out) in enumerate(((dec0_out, kd0_out, b0_out), (dec1_out, kd1_out, b1_out))):
        lw = _dot_hi(twd[:, d * RW_DECAY_LORA:(d + 1) * RW_DECAY_LORA], w2_ref[d])
        w = -_softplus(-(w0_ref[d:d + 1, :] + lw)) - 0.5
        dec_out[...] = jnp.exp(-jnp.exp(w))
        la = _dot_hi(ad[:, d * RW_AAA_LORA:(d + 1) * RW_AAA_LORA], a2_ref[d])
        lr = _sigmoid(a0_ref[d:d + 1, :] + la)
        kd_out[...] = k * (1.0 + (lr - 1.0) * ka_ref[...])
        b_out[...] = kk * lr


def rw_prep(z, rw_col_block, mu, w0, w2, a0, a2, g2, k_k, k_a, seg, seq, bt):
    M = z.shape[0]
    C = RW_WIDTH
    W = RW_COLS_PAD
    nb8 = M // 8
    full = lambda shape: pl.BlockSpec(shape, lambda i: (0,) * len(shape))
    out = jax.ShapeDtypeStruct((M, C), F32)
    return pl.pallas_call(
        functools.partial(_rw_prep_kernel, bt=bt, seq=seq),
        grid=(M // bt,),
        in_specs=[pl.BlockSpec((bt, W), lambda i: (i, rw_col_block)),
                  pl.BlockSpec((8, W), lambda i: (jnp.maximum(i * (bt // 8) - 1, 0), rw_col_block)),
                  pl.BlockSpec((8, W), lambda i: (jnp.minimum((i + 1) * (bt // 8), nb8 - 1), rw_col_block)),
                  full((1, W)), full((2, C)), full((2, RW_DECAY_LORA, C)), full((2, C)),
                  full((2, RW_AAA_LORA, C)), full((RW_GATE_LORA, C)), full((1, C)), full((1, C)),
                  full((C, C))],
        out_specs=[pl.BlockSpec((bt, C), lambda i: (i, 0))] * 10,
        out_shape=[out] * 10,
        scratch_shapes=[pltpu.VMEM((bt + 16, W), F32)],
        compiler_params=_params("parallel"),
        name="rw_prep",
    )(z, z, z, mu, w0, w2, a0, a2, g2, k_k, k_a, seg)


def _rw_scan_kernel(w_ref, k_ref, a_ref, b_ref, r_ref, v_ref, y_ref, s_ref, *, tb, pitch):
    N = HEAD_DIM

    @pl.when(pl.program_id(0) == 0)
    def _():
        s_ref[...] = jnp.zeros_like(s_ref)

    def step(t, carry):
        base = t * pitch
        row = lambda ref, j: ref[pl.ds(base + j, 1), :]
        vt = v_ref[t]
        zero = jnp.zeros_like(vt)

        def dot_a(c, acc):
            acc0, acc1 = acc
            for u in range(SCAN_UNROLL):
                j = c * SCAN_UNROLL + u
                term = s_ref[j] * row(a_ref, j)
                if u % 2 == 0:
                    acc0 = acc0 + term
                else:
                    acc1 = acc1 + term
            return acc0, acc1

        sa0, sa1 = lax.fori_loop(0, N // SCAN_UNROLL, dot_a, (zero, zero))
        sa = sa0 + sa1

        def update(c, acc):
            acc0, acc1 = acc
            for u in range(SCAN_UNROLL):
                j = c * SCAN_UNROLL + u
                s = s_ref[j] * row(w_ref, j) + sa * row(b_ref, j) + vt * row(k_ref, j)
                s_ref[j] = s
                term = s * row(r_ref, j)
                if u % 2 == 0:
                    acc0 = acc0 + term
                else:
                    acc1 = acc1 + term
            return acc0, acc1

        y0, y1 = lax.fori_loop(0, N // SCAN_UNROLL, update, (zero, zero))
        y_ref[t] = y0 + y1
        return carry

    lax.fori_loop(0, tb, step, 0)


def rw_scan(w, k, a, b, r, v, tb, pitch):
    T, nv, L = v.shape
    kspec = pl.BlockSpec((tb * pitch, L), lambda i: (i, 0))
    vspec = pl.BlockSpec((tb, nv, L), lambda i: (i, 0, 0))
    return pl.pallas_call(
        functools.partial(_rw_scan_kernel, tb=tb, pitch=pitch),
        grid=(T // tb,),
        in_specs=[kspec] * 5 + [vspec],
        out_specs=vspec,
        out_shape=jax.ShapeDtypeStruct((T, nv, L), F32),
        scratch_shapes=[pltpu.VMEM((HEAD_DIM, nv, L), F32)],
        compiler_params=_params("arbitrary"),
        name="rw_scan",
    )(w, k, a, b, r, v)


def _rw_post_kernel(yf_ref, yb_ref, r_ref, v_ref, kd0_ref, kd1_ref, gate_ref, rk_ref, g_ref, b_ref, seg_ref,
                    o_ref):
    seg = seg_ref[...]
    y = yf_ref[...] + yb_ref[...]
    inv_n = 1.0 / HEAD_DIM
    mean = _dot_hi(y, seg) * inv_n
    d = y - mean
    var = _dot_hi(d * d, seg) * inv_n
    yn = d * lax.rsqrt(var + RW_LNX_EPS) * g_ref[...] + b_ref[...]
    r = r_ref[...]
    rk = rk_ref[...]
    bonus = (_dot_hi(r * kd0_ref[...] * rk, seg) + _dot_hi(r * kd1_ref[...] * rk, seg)) * v_ref[...]
    o_ref[...] = (yn + bonus) * gate_ref[...]


def rw_post(yf, yb, r, v, kd0, kd1, gate, rk, g, b, seg, bt):
    M, C = yf.shape
    tok = pl.BlockSpec((bt, C), lambda i: (i, 0))
    row = pl.BlockSpec((1, C), lambda i: (0, 0))
    return pl.pallas_call(
        _rw_post_kernel,
        grid=(M // bt,),
        in_specs=[tok] * 7 + [row] * 3 + [pl.BlockSpec((C, C), lambda i: (0, 0))],
        out_specs=tok,
        out_shape=jax.ShapeDtypeStruct((M, C), F32),
        compiler_params=_params("parallel"),
        name="rw_post",
    )(yf, yb, r, v, kd0, kd1, gate, rk, g, b, seg)


def _dil_kernel(q_ref, k_ref, v_ref, cos_ref, sin_ref, o_ref, lse_ref, qs_ref, ks_ref, vs_ref, *, n_sub, dil, qb,
                kwin, half):
    W = DIL_STEP_WIDTH
    lane = lax.broadcasted_iota(jnp.int32, (1, W), 1)
    first_half = (lane % HEAD_DIM) < HALF_DIM

    def rope(z, c, s):
        swapped = jnp.where(first_half, pltpu.roll(z, W - HALF_DIM, 1), pltpu.roll(z, HALF_DIM, 1))
        return z * c + swapped * s

    rc = min(256, n_sub)

    def subsequence(r, carry):
        def rope_chunk(i, carry):
            rows = pl.ds(r + i * (rc * dil), rc, stride=dil)
            sl = pl.ds(pl.multiple_of(i * rc, rc), rc)
            c = cos_ref[rows, :]
            s = sin_ref[rows, :]
            qs_ref[sl, :] = (rope(q_ref[0, rows, :], c, s) * SCALE).astype(BF16)
            ks_ref[sl, :] = rope(k_ref[0, rows, :], c, s).astype(BF16)
            vs_ref[sl, :] = v_ref[0, rows, :].astype(BF16)
            return carry

        lax.fori_loop(0, n_sub // rc, rope_chunk, 0)

        nblk = n_sub // qb
        unroll = 2 if nblk % 2 == 0 else 1
        heads = W // HEAD_DIM

        def block_group(ig, carry):
            chains = []
            for u in range(unroll):
                q0 = pl.multiple_of((ig * unroll + u) * qb, qb)
                k0 = pl.multiple_of(jnp.clip(q0 - (kwin - qb) // 2, 0, n_sub - kwin), 64)
                q = qs_ref[pl.ds(q0, qb), :]
                kk = ks_ref[pl.ds(k0, kwin), :]
                jq = q0 + lax.broadcasted_iota(jnp.int32, (qb, kwin), 0)
                jk = k0 + lax.broadcasted_iota(jnp.int32, (qb, kwin), 1)
                ok = jnp.abs(jk - jq) <= half
                for h in range(heads):
                    hs = slice(h * HEAD_DIM, (h + 1) * HEAD_DIM)
                    chains.append((q0, k0, hs, jnp.where(ok, _dot_nt(q[:, hs], kk[:, hs]), NEG_INF)))
            probs = []
            for q0, k0, hs, s in chains:
                m = jnp.max(s, axis=-1, keepdims=True)
                p = jnp.exp(s - m)
                l = jnp.sum(p, axis=-1, keepdims=True)
                probs.append((q0, k0, hs, p.astype(BF16), l, m + jnp.log(l)))
            outs, lses = [], []
            for q0, k0, hs, p, l, lse in probs:
                outs.append(_dot(p, vs_ref[pl.ds(k0, kwin), hs]) / l)
                lses.append(jnp.broadcast_to(lse, (qb, HEAD_DIM)))
                if len(outs) == heads:
                    rows = pl.ds(r + q0 * dil, qb, stride=dil)
                    o_ref[0, rows, :] = jnp.concatenate(outs, axis=-1)
                    lse_ref[0, rows, :] = jnp.concatenate(lses, axis=-1)
                    outs, lses = [], []
            return carry

        lax.fori_loop(0, nblk // unroll, block_group, 0)
        return carry

    lax.fori_loop(0, dil, subsequence, 0)


def dilated_group(z3, q_blk, k_blk, v_blk, cos_t, sin_t, window, dil):
    B, T, ncols = z3.shape
    W = DIL_STEP_WIDTH
    steps = DIL_GROUP_WIDTH // W
    n_sub = T // dil
    half = window // (2 * dil)
    qb = min(128, n_sub)
    kwin = min(qb + 2 * half, n_sub)
    zspec = lambda blk: pl.BlockSpec((1, T, W), lambda b, s: (b, 0, blk * steps + s))
    tspec = pl.BlockSpec((T, W), lambda b, s: (0, 0))
    ospec = pl.BlockSpec((1, T, W), lambda b, s: (b, 0, s))
    oshape = jax.ShapeDtypeStruct((B, T, DIL_GROUP_WIDTH), F32)
    o, lse = pl.pallas_call(
        functools.partial(_dil_kernel, n_sub=n_sub, dil=dil, qb=qb, kwin=kwin, half=half),
        grid=(B, steps),
        in_specs=[zspec(q_blk), zspec(k_blk), zspec(v_blk), tspec, tspec],
        out_specs=[ospec, ospec],
        out_shape=[oshape, oshape],
        scratch_shapes=[pltpu.VMEM((n_sub, W), BF16)] * 3,
        compiler_params=_params("parallel", "parallel"),
        name=f"dilated_d{dil}",
    )(z3, z3, z3, cos_t, sin_t)
    return o.reshape(B * T, DIL_GROUP_WIDTH), lse.reshape(B * T, DIL_GROUP_WIDTH)


def _na_kernel(q_ref, k_ref, v_ref, bias_ref, o_ref, qs_ref, ks_ref, vs_ref, *, rows):
    nkeys = NA_KH * GRID_W
    pc = NA_KH * GRID_W

    def prep(c, carry):
        sl = pl.ds(pl.multiple_of(c * pc, pc), pc)
        q = q_ref[0, sl, :] * SCALE
        k = k_ref[0, sl, :]
        v = v_ref[0, sl, :]
        for h in range(NA_GROUP_HEADS):
            hs = slice(h * HEAD_DIM, (h + 1) * HEAD_DIM)
            qs_ref[h, sl, :] = q[:, hs].astype(BF16)
            ks_ref[h, sl, :] = k[:, hs].astype(BF16)
            vs_ref[h, sl, :] = v[:, hs].astype(BF16)
        return carry

    lax.fori_loop(0, rows * GRID_W // pc, prep, 0)

    def row_group(rg, carry):
        chains = []
        for u in range(NA_ROW_UNROLL):
            r = rg * NA_ROW_UNROLL + u
            r0 = jnp.clip(r - NA_KH // 2, 0, rows - NA_KH)
            qrow = pl.ds(pl.multiple_of(r * GRID_W, GRID_W), GRID_W)
            krow = pl.ds(pl.multiple_of(r0 * GRID_W, GRID_W), nkeys)
            for h in range(NA_GROUP_HEADS):
                s = _dot_nt(qs_ref[h, qrow, :], ks_ref[h, krow, :]) + bias_ref[r - r0, h]
                chains.append((u, h, qrow, krow, s))
        probs = []
        for u, h, qrow, krow, s in chains:
            m = jnp.max(s, axis=-1, keepdims=True)
            p = jnp.exp(s - m)
            l = jnp.sum(p, axis=-1, keepdims=True)
            probs.append((u, h, qrow, krow, p.astype(BF16), l))
        outs = {}
        for u, h, qrow, krow, p, l in probs:
            outs.setdefault(u, []).append(_dot(p, vs_ref[h, krow, :]) / l)
            if h == NA_GROUP_HEADS - 1:
                o_ref[0, qrow, :] = jnp.concatenate(outs[u], axis=-1)
        return carry

    lax.fori_loop(0, rows // NA_ROW_UNROLL, row_group, 0)


def na_bias_table(rel_bias):
    qc = np.arange(GRID_W)
    kc = np.arange(GRID_W)
    wc0 = np.clip(qc - NA_KW // 2, 0, GRID_W - NA_KW)
    col_ok = (kc[None, :] >= wc0[:, None]) & (kc[None, :] < wc0[:, None] + NA_KW)
    dx_idx = np.clip(kc[None, :] - qc[:, None], 1 - NA_KW, NA_KW - 1) + NA_KW - 1
    dy_idx = np.arange(NA_KH)[None, :] - np.arange(NA_KH)[:, None] + NA_KH - 1
    tab = rel_bias[:, dy_idx][:, :, :, dx_idx]
    tab = jnp.where(jnp.asarray(col_ok)[None, None, None], tab, NEG_INF)
    tab = jnp.transpose(tab, (1, 0, 3, 2, 4))
    return tab.reshape(NA_KH, rel_bias.shape[0], GRID_W, NA_KH * GRID_W).astype(F32)


def neighbourhood(z3, q_blk0, k_blk0, v_blk0, bias_tab):
    B, T, ncols = z3.shape
    rows = T // GRID_W
    assert rows >= NA_KH
    W = NA_GROUP_WIDTH
    G = NA_HEADS // NA_GROUP_HEADS
    zspec = lambda blk0: pl.BlockSpec((1, T, W), lambda b, g: (b, 0, blk0 + g))
    return pl.pallas_call(
        functools.partial(_na_kernel, rows=rows),
        grid=(B, G),
        in_specs=[zspec(q_blk0), zspec(k_blk0), zspec(v_blk0),
                  pl.BlockSpec((NA_KH, NA_GROUP_HEADS, GRID_W, NA_KH * GRID_W), lambda b, g: (0, g, 0, 0))],
        out_specs=pl.BlockSpec((1, T, W), lambda b, g: (b, 0, g)),
        out_shape=jax.ShapeDtypeStruct((B, T, NA_WIDTH), F32),
        scratch_shapes=[pltpu.VMEM((NA_GROUP_HEADS, T, HEAD_DIM), BF16)] * 3,
        compiler_params=_params("parallel", "parallel"),
        name="neighbourhood",
    )(z3, z3, z3, bias_tab)


def _merge_kernel(ya_ref, o1_ref, o2_ref, o3_ref, l1_ref, l2_ref, l3_ref, yc_ref, ga_ref, gb_ref, gc_ref,
                  wa_ref, wb_ref, wc_ref, out_ref, yb_ref):
    @pl.when(pl.program_id(1) == 0)
    def _():
        l1, l2, l3 = l1_ref[...], l2_ref[...], l3_ref[...]
        m = jnp.maximum(jnp.maximum(l1, l2), l3)
        e1, e2, e3 = jnp.exp(l1 - m), jnp.exp(l2 - m), jnp.exp(l3 - m)
        yb = (e1 * o1_ref[...] + e2 * o2_ref[...] + e3 * o3_ref[...]) / (e1 + e2 + e3)
        yb_ref[...] = yb.astype(BF16)

    pa = _dot(ya_ref[...].astype(BF16), wa_ref[...])
    pb = _dot(yb_ref[...], wb_ref[...])
    pc = _dot(yc_ref[...].astype(BF16), wc_ref[...])
    out = _sigmoid(ga_ref[...]) * pa + _sigmoid(gb_ref[...]) * pb + _sigmoid(gc_ref[...]) * pc
    out_ref[...] = out.astype(BF16)


def merge(ya, dil_outs, yc, z, wa, wb, wc, bm, bn):
    M = ya.shape[0]
    D = wa.shape[1]
    nj = D // bn
    tok = lambda w: pl.BlockSpec((bm, w), lambda i, j: (i, 0))
    gate = lambda g: pl.BlockSpec((bm, bn), lambda i, j: (i, g * nj + j))
    wsp = lambda k: pl.BlockSpec((k, bn), lambda i, j: (0, j))
    (o1, l1), (o2, l2), (o3, l3) = dil_outs
    W = DIL_GROUP_WIDTH
    return pl.pallas_call(
        _merge_kernel,
        grid=(M // bm, nj),
        in_specs=[tok(RW_WIDTH)] + [tok(W)] * 6 + [tok(NA_WIDTH), gate(0), gate(1), gate(2),
                                                  wsp(RW_WIDTH), wsp(W), wsp(NA_WIDTH)],
        out_specs=pl.BlockSpec((bm, bn), lambda i, j: (i, j)),
        out_shape=jax.ShapeDtypeStruct((M, D), BF16),
        scratch_shapes=[pltpu.VMEM((bm, W), BF16)],
        compiler_params=_params("parallel", "arbitrary"),
        name="merge",
    )(ya, o1, o2, o3, l1, l2, l3, yc, z, z, z, wa, wb, wc)


def _matmul_res_kernel(a_ref, w_ref, x_ref, o_ref):
    o_ref[...] = x_ref[...] + _dot(a_ref[...], w_ref[...])


def matmul_res(a, w, x, bm, bn):
    M, K = a.shape
    N = w.shape[1]
    return pl.pallas_call(
        _matmul_res_kernel,
        grid=(M // bm, N // bn),
        in_specs=[pl.BlockSpec((bm, K), lambda i, j: (i, 0)),
                  pl.BlockSpec((K, bn), lambda i, j: (0, j)),
                  pl.BlockSpec((bm, bn), lambda i, j: (i, j))],
        out_specs=pl.BlockSpec((bm, bn), lambda i, j: (i, j)),
        out_shape=jax.ShapeDtypeStruct((M, N), F32),
        compiler_params=_params("parallel", "parallel"),
        name="out_proj",
    )(a, w, x)


def _gelu(x):
    return 0.5 * x * (1.0 + jnp.tanh(np.sqrt(2.0 / np.pi).astype(np.float32) * (x + 0.044715 * (x * x * x))))


def _ffn_kernel(x_ref, xp_ref, xn_ref, g_ref, wg_ref, wu_ref, cw_ref, cb_ref, wd_ref, o_ref,
                h_ref, hh_ref, gs_ref, acc_ref, *, bm, seq):
    i = pl.program_id(0)
    f = pl.program_id(1)

    @pl.when(f == 0)
    def _():
        g = g_ref[...]
        h_ref[...] = _rms(x_ref[...], g).astype(BF16)
        hh_ref[0:8, :] = _rms(xp_ref[...], g).astype(F32)
        hh_ref[8:16, :] = _rms(xn_ref[...], g).astype(F32)
        acc_ref[...] = jnp.zeros_like(acc_ref)

    pos = (i * bm) % seq
    prev_ok = (pos != 0).astype(F32)
    next_ok = (pos + bm != seq).astype(F32)
    wg = wg_ref[...]
    gate = _dot(h_ref[...], wg)
    halo = _dot(hh_ref[...].astype(BF16), wg)
    gs_ref[0:8, :] = halo[0:8] * prev_ok
    gs_ref[8:bm + 8, :] = gate
    gs_ref[bm + 8:bm + 16, :] = halo[8:16] * next_ok
    cw = cw_ref[...]
    gc = gs_ref[7:bm + 7, :] * cw[0:1] + gate * cw[1:2] + gs_ref[9:bm + 9, :] * cw[2:3] + cb_ref[...]
    up = _dot(h_ref[...], wu_ref[...])
    act = (_gelu(gc) * up).astype(BF16)
    acc_ref[...] += _dot(act, wd_ref[...])

    @pl.when(f == pl.num_programs(1) - 1)
    def _():
        o_ref[...] = x_ref[...] + acc_ref[...]


def ffn(x, g, wg, wu, cw, cb, wd, seq, bm, bf):
    M, D = x.shape
    F = wg.shape[1]
    nb8 = M // 8
    return pl.pallas_call(
        functools.partial(_ffn_kernel, bm=bm, seq=seq),
        grid=(M // bm, F // bf),
        in_specs=[pl.BlockSpec((bm, D), lambda i, f: (i, 0)),
                  pl.BlockSpec((8, D), lambda i, f: (jnp.maximum(i * (bm // 8) - 1, 0), 0)),
                  pl.BlockSpec((8, D), lambda i, f: (jnp.minimum((i + 1) * (bm // 8), nb8 - 1), 0)),
                  pl.BlockSpec((1, D), lambda i, f: (0, 0)),
                  pl.BlockSpec((D, bf), lambda i, f: (0, f)),
                  pl.BlockSpec((D, bf), lambda i, f: (0, f)),
                  pl.BlockSpec((3, bf), lambda i, f: (0, f)),
                  pl.BlockSpec((1, bf), lambda i, f: (0, f)),
                  pl.BlockSpec((bf, D), lambda i, f: (f, 0))],
        out_specs=pl.BlockSpec((bm, D), lambda i, f: (i, 0)),
        out_shape=jax.ShapeDtypeStruct((M, D), F32),
        scratch_shapes=[pltpu.VMEM((bm, D), BF16), pltpu.VMEM((16, D), F32),
                        pltpu.VMEM((bm + 16, bf), F32), pltpu.VMEM((bm, D), F32)],
        compiler_params=_params("parallel", "arbitrary"),
        name="conv_ffn",
    )(x, x, x, g, wg, wu, cw, cb, wd)


def _ple_kernel(x_ref, p_ref, g_ref, wg_ref, wp_ref, gf_ref, o_ref, *, final_norm):
    x = x_ref[...]
    h = _rms(x, g_ref[...]).astype(BF16)
    gate = _sigmoid(_dot(h, wg_ref[...]))
    y = x + gate * _dot(p_ref[...].astype(BF16), wp_ref[...])
    if final_norm:
        y = _rms(y, gf_ref[...])
    o_ref[...] = y


def ple(x, p, g, wg, wp, gf, bm, final_norm):
    M, D = x.shape
    P = p.shape[1]
    return pl.pallas_call(
        functools.partial(_ple_kernel, final_norm=final_norm),
        grid=(M // bm,),
        in_specs=[pl.BlockSpec((bm, D), lambda i: (i, 0)),
                  pl.BlockSpec((bm, P), lambda i: (i, 0)),
                  pl.BlockSpec((1, D), lambda i: (0, 0)),
                  pl.BlockSpec((D, D), lambda i: (0, 0)),
                  pl.BlockSpec((P, D), lambda i: (0, 0)),
                  pl.BlockSpec((1, D), lambda i: (0, 0))],
        out_specs=pl.BlockSpec((bm, D), lambda i: (i, 0)),
        out_shape=jax.ShapeDtypeStruct((M, D), F32),
        compiler_params=_params("parallel"),
        name="ple_final" if final_norm else "ple",
    )(x, p, g, wg, wp, gf)


def _to_scan_kernel(x0_ref, x1_ref, rev_ref, o_ref, a_ref, *, nb, tb, dup, pitch):
    N = HEAD_DIM
    lanes = 2 * 2 * nb * RW_HEADS
    rev = rev_ref[...]
    for d, ref in enumerate((x0_ref, x1_ref)):
        for b in range(nb):
            x = ref[b]
            if d == 1:
                x = _dot_hi(rev, x)
            xt = x.T
            p0 = (d * nb + b) * RW_HEADS
            if dup:
                for h in range(RW_HEADS):
                    blk = xt[h * N:(h + 1) * N]
                    a_ref[(p0 + h) * 2 * pitch:(p0 + h) * 2 * pitch + N, :] = blk
                    a_ref[(p0 + h) * 2 * pitch + pitch:(p0 + h) * 2 * pitch + pitch + N, :] = blk
            else:
                a_ref[p0 * N:(p0 + RW_HEADS) * N, :] = xt
    n_rows = N if dup else N // 2
    for j in range(n_rows):
        rows = a_ref[pl.ds(j, lanes, stride=pitch), :]
        o_ref[pl.ds(j, tb, stride=pitch), :] = rows.T
    for j in range(n_rows, pitch):
        o_ref[pl.ds(j, tb, stride=pitch), :] = jnp.zeros((tb, lanes), F32)


def to_scan(x0, x1, rev, B, T, tb, dup, pitch):
    C = RW_WIDTH
    N = HEAD_DIM
    nblk = T // tb
    lanes = 4 * B * RW_HEADS
    return pl.pallas_call(
        functools.partial(_to_scan_kernel, nb=B, tb=tb, dup=dup, pitch=pitch),
        grid=(nblk,),
        in_specs=[pl.BlockSpec((B, tb, C), lambda i: (0, i, 0)),
                  pl.BlockSpec((B, tb, C), lambda i: (0, nblk - 1 - i, 0)),
                  pl.BlockSpec((tb, tb), lambda i: (0, 0))],
        out_specs=pl.BlockSpec((tb * pitch, lanes), lambda i: (i, 0)),
        out_shape=jax.ShapeDtypeStruct((T * pitch, lanes), F32),
        scratch_shapes=[pltpu.VMEM((2 * B * RW_HEADS * (2 * pitch if dup else N), tb), F32)],
        compiler_params=_params("parallel"),
        name="to_scan_keys" if dup else "to_scan_values",
    )(x0.reshape(B, T, C), x1.reshape(B, T, C), rev)


def _from_scan_kernel(y_ref, rev_ref, yf_ref, yb_ref, a_ref, *, nb, tb):
    N = HEAD_DIM
    C = RW_WIDTH
    lanes = 2 * 2 * nb * RW_HEADS
    for ih in range(N // 2):
        rows = y_ref[pl.ds(ih, tb, stride=N // 2), :]
        a_ref[pl.ds(ih, lanes, stride=N // 2), :] = rows.T
    rev = rev_ref[...]
    for d, ref in enumerate((yf_ref, yb_ref)):
        for b in range(nb):
            y = a_ref[(d * nb + b) * C:(d * nb + b + 1) * C, :].T
            if d == 1:
                y = _dot_hi(rev, y)
            ref[b] = y


def from_scan(y, rev, B, T, tb):
    C = RW_WIDTH
    N = HEAD_DIM
    nblk = T // tb
    lanes = y.shape[-1]
    out = jax.ShapeDtypeStruct((B, T, C), F32)
    yf, yb = pl.pallas_call(
        functools.partial(_from_scan_kernel, nb=B, tb=tb),
        grid=(nblk,),
        in_specs=[pl.BlockSpec((tb * (N // 2), lanes), lambda i: (i, 0)),
                  pl.BlockSpec((tb, tb), lambda i: (0, 0))],
        out_specs=[pl.BlockSpec((B, tb, C), lambda i: (0, i, 0)),
                   pl.BlockSpec((B, tb, C), lambda i: (0, nblk - 1 - i, 0))],
        out_shape=[out, out],
        scratch_shapes=[pltpu.VMEM((2 * B * RW_HEADS * N, tb), F32)],
        compiler_params=_params("parallel"),
        name="from_scan",
    )(y.reshape(T * (N // 2), lanes), rev)
    return yf.reshape(B * T, C), yb.reshape(B * T, C)


def _rope_tables(T):
    inv = ROPE_THETA ** (-jnp.arange(0, HEAD_DIM, 2, dtype=jnp.float32) / HEAD_DIM)
    ang = jnp.arange(T, dtype=jnp.float32)[:, None] * inv[None, :]
    cos, sin = jnp.cos(ang), jnp.sin(ang)
    cos_t = jnp.tile(jnp.concatenate([cos, cos], axis=-1), (1, DIL_STEP_WIDTH // HEAD_DIM))
    sin_t = jnp.tile(jnp.concatenate([-sin, sin], axis=-1), (1, DIL_STEP_WIDTH // HEAD_DIM))
    return cos_t, sin_t


def _pick(n, prefs):
    for c in prefs:
        if n % c == 0:
            return c
    return n


def kernel(x, p, norm_mix, w_in, rw_mu, rw_w0, rw_w2, rw_a0, rw_a2, rw_g2, rw_k_k, rw_k_a, rw_r_k, rw_lnx_g,
           rw_lnx_b, na_bias, w_br_a, w_br_b, w_br_c, w_out, norm_ffn, w_ffn_gate, w_ffn_up, ffn_conv_w,
           ffn_conv_b, w_ffn_down, norm_ple, w_ple_gate, w_ple, norm_final):
    B, T, D = x.shape
    depth = w_in.shape[0]
    M = B * T
    F = w_ffn_gate.shape[-1]
    C = RW_WIDTH
    W = DIL_GROUP_WIDTH
    assert (N_BRANCH * D) % W == 0 and w_in.shape[-1] == RW_COLS + 2 * 3 * DIL_WIDTH + N_BRANCH * D

    gates_w = N_BRANCH * D
    attn_w = 3 * DIL_WIDTH + 3 * NA_WIDTH
    ncols = gates_w + attn_w + RW_COLS_PAD
    dil_blk0 = gates_w // W
    na_blk0 = dil_blk0 + 3 * DIL_WIDTH // W
    assert (gates_w + attn_w) % RW_COLS_PAD == 0
    rw_blk = (gates_w + attn_w) // RW_COLS_PAD

    cos_t, sin_t = _rope_tables(T)
    seg = jnp.asarray(np.kron(np.eye(RW_HEADS), np.ones((HEAD_DIM, HEAD_DIM))), F32)
    xf = x.reshape(M, D)
    bm_in = _pick(M, (512, 256, 128))
    bn_in = _pick(ncols, (1792, 896, 256, 128))
    bt_rw = _pick(T, (256, 128))
    tb = _pick(T, (64, 32))
    tt = _pick(T, (128,))
    rev = jnp.asarray(np.eye(tt)[::-1], F32)

    for i in range(depth):
        w = w_in[i]
        w_pad = jnp.concatenate(
            [w[:, RW_COLS + attn_w:], w[:, RW_COLS:RW_COLS + attn_w], w[:, :RW_COLS],
             jnp.zeros((D, RW_COLS_PAD - RW_COLS), w.dtype)], axis=1).astype(BF16)
        z = norm_matmul(xf, norm_mix[i][None], w_pad, bm_in, bn_in)
        z3 = z.reshape(B, T, ncols)

        mu = jnp.pad(rw_mu[i], (0, RW_COLS_PAD - RW_COLS))[None]
        r, v, a, dec0, dec1, kd0, kd1, b0, b1, gate = rw_prep(
            z, rw_blk, mu, rw_w0[i], rw_w2[i], rw_a0[i], rw_a2[i], rw_g2[i], rw_k_k[i][None], rw_k_a[i][None],
            seg, T, bt_rw)
        keys = lambda x0, x1: to_scan(x0, x1, rev, B, T, tt, dup=True, pitch=KEY_PITCH)
        vals = to_scan(v, v, rev, B, T, tt, dup=False, pitch=HEAD_DIM // 2).reshape(T, HEAD_DIM // 2, -1)
        y = rw_scan(keys(dec0, dec1), keys(kd0, kd1), keys(a, a), keys(b0, b1), keys(r, r), vals, tb, KEY_PITCH)
        yf, yb = from_scan(y, rev, B, T, tt)
        ya = rw_post(yf, yb, r, v, kd0, kd1, gate, rw_r_k[i].reshape(1, C), rw_lnx_g[i][None], rw_lnx_b[i][None],
                     seg, bt_rw)

        dil_outs = []
        for g, (window, dil) in enumerate(DIL_PATTERNS):
            dil_outs.append(dilated_group(z3, dil_blk0 + g, dil_blk0 + 3 + g, dil_blk0 + 6 + g, cos_t, sin_t,
                                          window, dil))

        yc = neighbourhood(z3, na_blk0, na_blk0 + 3, na_blk0 + 6, na_bias_table(na_bias[i])).reshape(M, NA_WIDTH)

        merged = merge(ya, dil_outs, yc, z, w_br_a[i].astype(BF16), w_br_b[i].astype(BF16),
                       w_br_c[i].astype(BF16), _pick(M, (512, 256, 128)), _pick(D, (512, 256, 128)))
        xf = matmul_res(merged, w_out[i].astype(BF16), xf, _pick(M, (1024, 512, 256, 128)),
                        _pick(D, (1024, 512, 256, 128)))

        xf = ffn(xf, norm_ffn[i][None], w_ffn_gate[i].astype(BF16), w_ffn_up[i].astype(BF16), ffn_conv_w[i],
                 ffn_conv_b[i][None], w_ffn_down[i].astype(BF16), T, _pick(T, (512, 256, 128)),
                 _pick(F, (512, 256, 128)))

        xf = ple(xf, p[i].reshape(M, -1), norm_ple[i][None], w_ple_gate[i].astype(BF16), w_ple[i].astype(BF16),
                 norm_final[None], _pick(M, (512, 256, 128)), final_norm=(i == depth - 1))
    return xf.reshape(B, T, D)
```

```python
import functools

import numpy as np
import jax
import jax.numpy as jnp
from jax import lax
from jax.experimental import pallas as pl
from jax.experimental.pallas import tpu as pltpu

HEAD_DIM = 64
HALF_DIM = HEAD_DIM // 2
SCALE = HEAD_DIM ** -0.5
NORM_EPS = 1e-6
ROPE_THETA = 10000.0
NEG_INF = -1e30

RW_HEADS = 8
RW_WIDTH = RW_HEADS * HEAD_DIM
RW_DECAY_LORA = 32
RW_AAA_LORA = 32
RW_GATE_LORA = 96
RW_LNX_EPS = 64e-5
RW_COLS = 3 * RW_WIDTH + 2 * RW_DECAY_LORA + 2 * RW_AAA_LORA + RW_GATE_LORA
RW_COLS_PAD = 1792
KEY_PITCH = HEAD_DIM + 4
SCAN_UNROLL = 32

DIL_PATTERNS = ((128, 1), (512, 4), (2048, 16))
DIL_HEADS_PER_GROUP = 4
DIL_GROUP_WIDTH = DIL_HEADS_PER_GROUP * HEAD_DIM
DIL_WIDTH = len(DIL_PATTERNS) * DIL_GROUP_WIDTH
DIL_STEP_WIDTH = 2 * HEAD_DIM

GRID_W = 64
NA_HEADS = 12
NA_WIDTH = NA_HEADS * HEAD_DIM
NA_KH = 8
NA_KW = 16
NA_GROUP_HEADS = 4
NA_GROUP_WIDTH = NA_GROUP_HEADS * HEAD_DIM
NA_ROW_UNROLL = 4

N_BRANCH = 3
VMEM_LIMIT = 56 * 1024 * 1024

BF16 = jnp.bfloat16
F32 = jnp.float32
HIGHEST = lax.Precision.HIGHEST


def _params(*sem):
    return pltpu.CompilerParams(dimension_semantics=sem, vmem_limit_bytes=VMEM_LIMIT)


def _rms(x, g):
    ms = jnp.mean(x * x, axis=-1, keepdims=True)
    return x * lax.rsqrt(ms + NORM_EPS) * g


def _sigmoid(x):
    return 1.0 / (1.0 + jnp.exp(-x))


def _dot(a, b):
    return jnp.dot(a, b, preferred_element_type=F32)


def _dot_nt(a, b):
    return lax.dot_general(a, b, (((1,), (1,)), ((), ())), preferred_element_type=F32)


def _dot_hi(a, b):
    return jnp.dot(a, b, preferred_element_type=F32, precision=HIGHEST)


def _seg_sum(x, seg):
    hi = x.astype(BF16)
    rest = x - hi.astype(F32)
    mid = rest.astype(BF16)
    lo = (rest - mid.astype(F32)).astype(BF16)
    return _dot(hi, seg) + _dot(mid, seg) + _dot(lo, seg)


def _norm_matmul_kernel(x_ref, g_ref, w_ref, o_ref, h_ref):
    @pl.when(pl.program_id(1) == 0)
    def _():
        h_ref[...] = _rms(x_ref[...], g_ref[...]).astype(BF16)

    o_ref[...] = _dot(h_ref[...], w_ref[...]).astype(o_ref.dtype)


def norm_matmul(x, g, w, bm, bn, out_dtype, name):
    M, K = x.shape
    N = w.shape[1]
    return pl.pallas_call(
        _norm_matmul_kernel,
        grid=(M // bm, N // bn),
        in_specs=[pl.BlockSpec((bm, K), lambda i, j: (i, 0)),
                  pl.BlockSpec((1, K), lambda i, j: (0, 0)),
                  pl.BlockSpec((K, bn), lambda i, j: (0, j))],
        out_specs=pl.BlockSpec((bm, bn), lambda i, j: (i, j)),
        out_shape=jax.ShapeDtypeStruct((M, N), out_dtype),
        scratch_shapes=[pltpu.VMEM((bm, K), BF16)],
        compiler_params=_params("parallel", "arbitrary"),
        name=name,
    )(x, g, w)


def _softplus(x):
    return jnp.maximum(x, 0.0) + jnp.log1p(jnp.exp(-jnp.abs(x)))


def _rw_prep_kernel(c_ref, cp_ref, cn_ref, mu_ref, w0_ref, w2_ref, a0_ref, a2_ref, g2_ref, kk_ref, ka_ref,
                    seg_ref, r_out, v_out, a_out, dec0_out, dec1_out, kd0_out, kd1_out, b0_out, b1_out,
                    gate_out, buf_ref, *, bt, seq):
    i = pl.program_id(0)
    pos = (i * bt) % seq
    prev_ok = (pos != 0).astype(F32)
    next_ok = (pos + bt != seq).astype(F32)
    buf_ref[0:8, :] = cp_ref[...] * prev_ok
    buf_ref[8:bt + 8, :] = c_ref[...]
    buf_ref[bt + 8:bt + 16, :] = cn_ref[...] * next_ok
    cols = c_ref[...]
    shift = 0.5 * (buf_ref[7:bt + 7, :] + buf_ref[9:bt + 9, :])
    c = cols + mu_ref[...] * (shift - cols)
    C = RW_WIDTH
    r = c[:, 0:C]
    k = c[:, C:2 * C]
    v = c[:, 2 * C:3 * C]
    o = 3 * C
    wd = c[:, o:o + 2 * RW_DECAY_LORA]
    o += 2 * RW_DECAY_LORA
    ad = c[:, o:o + 2 * RW_AAA_LORA]
    o += 2 * RW_AAA_LORA
    gd = c[:, o:o + RW_GATE_LORA]
    seg = seg_ref[...]
    kk = k * kk_ref[...]
    nrm = jnp.sqrt(_seg_sum(kk * kk, seg))
    kk = kk / jnp.maximum(nrm, 1e-12)
    r_out[...] = r
    v_out[...] = v
    a_out[...] = -kk
    gate_out[...] = _dot_hi(_sigmoid(gd), g2_ref[...])
    twd = jnp.tanh(wd)
    for d, (dec_out, kd_out, b_out) in enumerate(((dec0_out, kd0_out, b0_out), (dec1_out, kd1_out, b1_out))):
        lw = _dot_hi(twd[:, d * RW_DECAY_LORA:(d + 1) * RW_DECAY_LORA], w2_ref[d])
        w = -_softplus(-(w0_ref[d:d + 1, :] + lw)) - 0.5
        dec_out[...] = jnp.exp(-jnp.exp(w))
        la = _dot_hi(ad[:, d * RW_AAA_LORA:(d + 1) * RW_AAA_LORA], a2_ref[d])
        lr = _sigmoid(a0_ref[d:d + 1, :] + la)
        kd_out[...] = k * (1.0 + (lr - 1.0) * ka_ref[...])
        b_out[...] = kk * lr


def rw_prep(z, rw_col_block, mu, w0, w2, a0, a2, g2, k_k, k_a, seg, seq, bt):
    M = z.shape[0]
    C = RW_WIDTH
    W = RW_COLS_PAD
    nb8 = M // 8
    full = lambda shape: pl.BlockSpec(shape, lambda i: (0,) * len(shape))
    out = jax.ShapeDtypeStruct((M, C), F32)
    return pl.pallas_call(
        functools.partial(_rw_prep_kernel, bt=bt, seq=seq),
        grid=(M // bt,),
        in_specs=[pl.BlockSpec((bt, W), lambda i: (i, rw_col_block)),
                  pl.BlockSpec((8, W), lambda i: (jnp.maximum(i * (bt // 8) - 1, 0), rw_col_block)),
                  pl.BlockSpec((8, W), lambda i: (jnp.minimum((i + 1) * (bt // 8), nb8 - 1), rw_col_block)),
                  full((1, W)), full((2, C)), full((2, RW_DECAY_LORA, C)), full((2, C)),
                  full((2, RW_AAA_LORA, C)), full((RW_GATE_LORA, C)), full((1, C)), full((1, C)),
                  full((C, C))],
        out_specs=[pl.BlockSpec((bt, C), lambda i: (i, 0))] * 10,
        out_shape=[out] * 10,
        scratch_shapes=[pltpu.VMEM((bt + 16, W), F32)],
        compiler_params=_params("parallel"),
        name="rw_prep",
    )(z, z, z, mu, w0, w2, a0, a2, g2, k_k, k_a, seg)


def _rw_scan_kernel(w_ref, k_ref, a_ref, b_ref, r_ref, v_ref, y_ref, s_ref, *, tb, pitch):
    N = HEAD_DIM

    @pl.when(pl.program_id(0) == 0)
    def _():
        s_ref[...] = jnp.zeros_like(s_ref)

    def step(t, carry):
        base = t * pitch
        row = lambda ref, j: ref[pl.ds(base + j, 1), :]
        vt = v_ref[t]
        zero = jnp.zeros_like(vt)

        def dot_a(c, acc):
            acc0, acc1 = acc
            for u in range(SCAN_UNROLL):
                j = c * SCAN_UNROLL + u
                term = s_ref[j] * row(a_ref, j)
                if u % 2 == 0:
                    acc0 = acc0 + term
                else:
                    acc1 = acc1 + term
            return acc0, acc1

        sa0, sa1 = lax.fori_loop(0, N // SCAN_UNROLL, dot_a, (zero, zero))
        sa = sa0 + sa1

        def update(c, acc):
            acc0, acc1 = acc
            for u in range(SCAN_UNROLL):
                j = c * SCAN_UNROLL + u
                s = s_ref[j] * row(w_ref, j) + sa * row(b_ref, j) + vt * row(k_ref, j)
                s_ref[j] = s
                term = s * row(r_ref, j)
                if u % 2 == 0:
                    acc0 = acc0 + term
                else:
                    acc1 = acc1 + term
            return acc0, acc1

        y0, y1 = lax.fori_loop(0, N // SCAN_UNROLL, update, (zero, zero))
        y_ref[t] = y0 + y1
        return carry

    lax.fori_loop(0, tb, step, 0)


def rw_scan(w, k, a, b, r, v, tb, pitch):
    T, nv, L = v.shape
    kspec = pl.BlockSpec((tb * pitch, L), lambda i: (i, 0))
    vspec = pl.BlockSpec((tb, nv, L), lambda i: (i, 0, 0))
    return pl.pallas_call(
        functools.partial(_rw_scan_kernel, tb=tb, pitch=pitch),
        grid=(T // tb,),
        in_specs=[kspec] * 5 + [vspec],
        out_specs=vspec,
        out_shape=jax.ShapeDtypeStruct((T, nv, L), F32),
        scratch_shapes=[pltpu.VMEM((HEAD_DIM, nv, L), F32)],
        compiler_params=_params("arbitrary"),
        name="rw_scan",
    )(w, k, a, b, r, v)


def _rw_post_kernel(yf_ref, yb_ref, r_ref, v_ref, kd0_ref, kd1_ref, gate_ref, rk_ref, g_ref, b_ref, seg_ref,
                    o_ref):
    seg = seg_ref[...]
    y = yf_ref[...] + yb_ref[...]
    inv_n = 1.0 / HEAD_DIM
    mean = _seg_sum(y, seg) * inv_n
    d = y - mean
    var = _seg_sum(d * d, seg) * inv_n
    yn = d * lax.rsqrt(var + RW_LNX_EPS) * g_ref[...] + b_ref[...]
    r = r_ref[...]
    rk = rk_ref[...]
    bonus = _seg_sum(r * kd0_ref[...] * rk + r * kd1_ref[...] * rk, seg) * v_ref[...]
    o_ref[...] = (yn + bonus) * gate_ref[...]


def rw_post(yf, yb, r, v, kd0, kd1, gate, rk, g, b, seg, bt):
    M, C = yf.shape
    tok = pl.BlockSpec((bt, C), lambda i: (i, 0))
    row = pl.BlockSpec((1, C), lambda i: (0, 0))
    return pl.pallas_call(
        _rw_post_kernel,
        grid=(M // bt,),
        in_specs=[tok] * 7 + [row] * 3 + [pl.BlockSpec((C, C), lambda i: (0, 0))],
        out_specs=tok,
        out_shape=jax.ShapeDtypeStruct((M, C), F32),
        compiler_params=_params("parallel"),
        name="rw_post",
    )(yf, yb, r, v, kd0, kd1, gate, rk, g, b, seg)


def _dil_kernel(q_ref, k_ref, v_ref, cos_ref, sin_ref, o_ref, lse_ref, qs_ref, ks_ref, vs_ref, *, n_sub, dil, qb,
                kwin, half):
    W = DIL_STEP_WIDTH
    lane = lax.broadcasted_iota(jnp.int32, (1, W), 1)
    first_half = (lane % HEAD_DIM) < HALF_DIM

    def rope(z, c, s):
        swapped = jnp.where(first_half, pltpu.roll(z, W - HALF_DIM, 1), pltpu.roll(z, HALF_DIM, 1))
        return z * c + swapped * s

    rc = min(256, n_sub)

    def subsequence(r, carry):
        def rope_chunk(i, carry):
            rows = pl.ds(r + i * (rc * dil), rc, stride=dil)
            sl = pl.ds(pl.multiple_of(i * rc, rc), rc)
            c = cos_ref[rows, :]
            s = sin_ref[rows, :]
            qs_ref[sl, :] = (rope(q_ref[0, rows, :], c, s) * SCALE).astype(BF16)
            ks_ref[sl, :] = rope(k_ref[0, rows, :], c, s).astype(BF16)
            vs_ref[sl, :] = v_ref[0, rows, :].astype(BF16)
            return carry

        lax.fori_loop(0, n_sub // rc, rope_chunk, 0)

        nblk = n_sub // qb
        unroll = 2 if nblk % 2 == 0 else 1
        heads = W // HEAD_DIM

        def block_group(ig, carry):
            chains = []
            for u in range(unroll):
                q0 = pl.multiple_of((ig * unroll + u) * qb, qb)
                k0 = pl.multiple_of(jnp.clip(q0 - (kwin - qb) // 2, 0, n_sub - kwin), 64)
                q = qs_ref[pl.ds(q0, qb), :]
                kk = ks_ref[pl.ds(k0, kwin), :]
                jq = q0 + lax.broadcasted_iota(jnp.int32, (qb, kwin), 0)
                jk = k0 + lax.broadcasted_iota(jnp.int32, (qb, kwin), 1)
                ok = jnp.abs(jk - jq) <= half
                for h in range(heads):
                    hs = slice(h * HEAD_DIM, (h + 1) * HEAD_DIM)
                    chains.append((q0, k0, hs, jnp.where(ok, _dot_nt(q[:, hs], kk[:, hs]), NEG_INF)))
            probs = []
            for q0, k0, hs, s in chains:
                m = jnp.max(s, axis=-1, keepdims=True)
                p = jnp.exp(s - m)
                l = jnp.sum(p, axis=-1, keepdims=True)
                probs.append((q0, k0, hs, p.astype(BF16), l, m + jnp.log(l)))
            outs, lses = [], []
            for q0, k0, hs, p, l, lse in probs:
                outs.append(_dot(p, vs_ref[pl.ds(k0, kwin), hs]) / l)
                lses.append(jnp.broadcast_to(lse, (qb, HEAD_DIM)))
                if len(outs) == heads:
                    rows = pl.ds(r + q0 * dil, qb, stride=dil)
                    o_ref[0, rows, :] = jnp.concatenate(outs, axis=-1)
                    lse_ref[0, rows, :] = jnp.concatenate(lses, axis=-1)
                    outs, lses = [], []
            return carry

        lax.fori_loop(0, nblk // unroll, block_group, 0)
        return carry

    lax.fori_loop(0, dil, subsequence, 0)


def dilated_group(z3, q_blk, k_blk, v_blk, cos_t, sin_t, window, dil):
    B, T, ncols = z3.shape
    W = DIL_STEP_WIDTH
    steps = DIL_GROUP_WIDTH // W
    n_sub = T // dil
    half = window // (2 * dil)
    qb = min(128, n_sub)
    kwin = min(qb + 2 * half, n_sub)
    zspec = lambda blk: pl.BlockSpec((1, T, W), lambda b, s: (b, 0, blk * steps + s))
    tspec = pl.BlockSpec((T, W), lambda b, s: (0, 0))
    ospec = pl.BlockSpec((1, T, W), lambda b, s: (b, 0, s))
    oshape = jax.ShapeDtypeStruct((B, T, DIL_GROUP_WIDTH), F32)
    o, lse = pl.pallas_call(
        functools.partial(_dil_kernel, n_sub=n_sub, dil=dil, qb=qb, kwin=kwin, half=half),
        grid=(B, steps),
        in_specs=[zspec(q_blk), zspec(k_blk), zspec(v_blk), tspec, tspec],
        out_specs=[ospec, ospec],
        out_shape=[oshape, oshape],
        scratch_shapes=[pltpu.VMEM((n_sub, W), BF16)] * 3,
        compiler_params=_params("parallel", "parallel"),
        name=f"dilated_d{dil}",
    )(z3, z3, z3, cos_t, sin_t)
    return o.reshape(B * T, DIL_GROUP_WIDTH), lse.reshape(B * T, DIL_GROUP_WIDTH)


def _na_kernel(q_ref, k_ref, v_ref, bias_ref, o_ref, qs_ref, ks_ref, vs_ref, *, rows):
    nkeys = NA_KH * GRID_W
    pc = NA_KH * GRID_W

    def prep(c, carry):
        sl = pl.ds(pl.multiple_of(c * pc, pc), pc)
        q = q_ref[0, sl, :] * SCALE
        k = k_ref[0, sl, :]
        v = v_ref[0, sl, :]
        for h in range(NA_GROUP_HEADS):
            hs = slice(h * HEAD_DIM, (h + 1) * HEAD_DIM)
            qs_ref[h, sl, :] = q[:, hs].astype(BF16)
            ks_ref[h, sl, :] = k[:, hs].astype(BF16)
            vs_ref[h, sl, :] = v[:, hs].astype(BF16)
        return carry

    lax.fori_loop(0, rows * GRID_W // pc, prep, 0)

    def row_group(rg, carry):
        chains = []
        for u in range(NA_ROW_UNROLL):
            r = rg * NA_ROW_UNROLL + u
            r0 = jnp.clip(r - NA_KH // 2, 0, rows - NA_KH)
            qrow = pl.ds(pl.multiple_of(r * GRID_W, GRID_W), GRID_W)
            krow = pl.ds(pl.multiple_of(r0 * GRID_W, GRID_W), nkeys)
            for h in range(NA_GROUP_HEADS):
                s = _dot_nt(qs_ref[h, qrow, :], ks_ref[h, krow, :]) + bias_ref[r - r0, h]
                chains.append((u, h, qrow, krow, s))
        probs = []
        for u, h, qrow, krow, s in chains:
            m = jnp.max(s, axis=-1, keepdims=True)
            p = jnp.exp(s - m)
            l = jnp.sum(p, axis=-1, keepdims=True)
            probs.append((u, h, qrow, krow, p.astype(BF16), l))
        outs = {}
        for u, h, qrow, krow, p, l in probs:
            outs.setdefault(u, []).append(_dot(p, vs_ref[h, krow, :]) / l)
            if h == NA_GROUP_HEADS - 1:
                o_ref[0, qrow, :] = jnp.concatenate(outs[u], axis=-1)
        return carry

    lax.fori_loop(0, rows // NA_ROW_UNROLL, row_group, 0)


def na_bias_table(rel_bias):
    qc = np.arange(GRID_W)
    kc = np.arange(GRID_W)
    wc0 = np.clip(qc - NA_KW // 2, 0, GRID_W - NA_KW)
    col_ok = (kc[None, :] >= wc0[:, None]) & (kc[None, :] < wc0[:, None] + NA_KW)
    dx_idx = np.clip(kc[None, :] - qc[:, None], 1 - NA_KW, NA_KW - 1) + NA_KW - 1
    dy_idx = np.arange(NA_KH)[None, :] - np.arange(NA_KH)[:, None] + NA_KH - 1
    tab = rel_bias[:, dy_idx][:, :, :, dx_idx]
    tab = jnp.where(jnp.asarray(col_ok)[None, None, None], tab, NEG_INF)
    tab = jnp.transpose(tab, (1, 0, 3, 2, 4))
    return tab.reshape(NA_KH, rel_bias.shape[0], GRID_W, NA_KH * GRID_W).astype(F32)


def neighbourhood(z3, q_blk0, k_blk0, v_blk0, bias_tab):
    B, T, ncols = z3.shape
    rows = T // GRID_W
    assert rows >= NA_KH
    W = NA_GROUP_WIDTH
    G = NA_HEADS // NA_GROUP_HEADS
    zspec = lambda blk0: pl.BlockSpec((1, T, W), lambda b, g: (b, 0, blk0 + g))
    return pl.pallas_call(
        functools.partial(_na_kernel, rows=rows),
        grid=(B, G),
        in_specs=[zspec(q_blk0), zspec(k_blk0), zspec(v_blk0),
                  pl.BlockSpec((NA_KH, NA_GROUP_HEADS, GRID_W, NA_KH * GRID_W), lambda b, g: (0, g, 0, 0))],
        out_specs=pl.BlockSpec((1, T, W), lambda b, g: (b, 0, g)),
        out_shape=jax.ShapeDtypeStruct((B, T, NA_WIDTH), F32),
        scratch_shapes=[pltpu.VMEM((NA_GROUP_HEADS, T, HEAD_DIM), BF16)] * 3,
        compiler_params=_params("parallel", "parallel"),
        name="neighbourhood",
    )(z3, z3, z3, bias_tab)


def _merge_kernel(ya_ref, o1_ref, o2_ref, o3_ref, l1_ref, l2_ref, l3_ref, yc_ref, ga_ref, gb_ref, gc_ref,
                  wa_ref, wb_ref, wc_ref, out_ref, yb_ref):
    @pl.when(pl.program_id(1) == 0)
    def _():
        l1, l2, l3 = l1_ref[...], l2_ref[...], l3_ref[...]
        m = jnp.maximum(jnp.maximum(l1, l2), l3)
        e1, e2, e3 = jnp.exp(l1 - m), jnp.exp(l2 - m), jnp.exp(l3 - m)
        yb = (e1 * o1_ref[...] + e2 * o2_ref[...] + e3 * o3_ref[...]) / (e1 + e2 + e3)
        yb_ref[...] = yb.astype(BF16)

    pa = _dot(ya_ref[...].astype(BF16), wa_ref[...])
    pb = _dot(yb_ref[...], wb_ref[...])
    pc = _dot(yc_ref[...].astype(BF16), wc_ref[...])
    gate = lambda ref: _sigmoid(ref[...].astype(F32))
    out = gate(ga_ref) * pa + gate(gb_ref) * pb + gate(gc_ref) * pc
    out_ref[...] = out.astype(BF16)


def merge(ya, dil_outs, yc, z, wa, wb, wc, bm, bn):
    M = ya.shape[0]
    D = wa.shape[1]
    nj = D // bn
    tok = lambda w: pl.BlockSpec((bm, w), lambda i, j: (i, 0))
    gate = lambda g: pl.BlockSpec((bm, bn), lambda i, j: (i, g * nj + j))
    wsp = lambda k: pl.BlockSpec((k, bn), lambda i, j: (0, j))
    (o1, l1), (o2, l2), (o3, l3) = dil_outs
    W = DIL_GROUP_WIDTH
    return pl.pallas_call(
        _merge_kernel,
        grid=(M // bm, nj),
        in_specs=[tok(RW_WIDTH)] + [tok(W)] * 6 + [tok(NA_WIDTH), gate(0), gate(1), gate(2),
                                                  wsp(RW_WIDTH), wsp(W), wsp(NA_WIDTH)],
        out_specs=pl.BlockSpec((bm, bn), lambda i, j: (i, j)),
        out_shape=jax.ShapeDtypeStruct((M, D), BF16),
        scratch_shapes=[pltpu.VMEM((bm, W), BF16)],
        compiler_params=_params("parallel", "arbitrary"),
        name="merge",
    )(ya, o1, o2, o3, l1, l2, l3, yc, z, z, z, wa, wb, wc)


def _matmul_res_kernel(a_ref, w_ref, x_ref, o_ref):
    o_ref[...] = x_ref[...] + _dot(a_ref[...], w_ref[...])


def matmul_res(a, w, x, bm, bn):
    M, K = a.shape
    N = w.shape[1]
    return pl.pallas_call(
        _matmul_res_kernel,
        grid=(M // bm, N // bn),
        in_specs=[pl.BlockSpec((bm, K), lambda i, j: (i, 0)),
                  pl.BlockSpec((K, bn), lambda i, j: (0, j)),
                  pl.BlockSpec((bm, bn), lambda i, j: (i, j))],
        out_specs=pl.BlockSpec((bm, bn), lambda i, j: (i, j)),
        out_shape=jax.ShapeDtypeStruct((M, N), F32),
        compiler_params=_params("parallel", "parallel"),
        name="out_proj",
    )(a, w, x)


def _gelu(x):
    return 0.5 * x * (1.0 + jnp.tanh(np.sqrt(2.0 / np.pi).astype(np.float32) * (x + 0.044715 * (x * x * x))))


def _ffn_kernel(x_ref, xp_ref, xn_ref, g_ref, wg_ref, wu_ref, cw_ref, cb_ref, wd_ref, o_ref,
                h_ref, hh_ref, gs_ref, acc_ref, *, bm, seq):
    i = pl.program_id(0)
    f = pl.program_id(1)

    @pl.when(f == 0)
    def _():
        g = g_ref[...]
        h_ref[...] = _rms(x_ref[...], g).astype(BF16)
        hh_ref[0:8, :] = _rms(xp_ref[...], g).astype(F32)
        hh_ref[8:16, :] = _rms(xn_ref[...], g).astype(F32)
        acc_ref[...] = jnp.zeros_like(acc_ref)

    pos = (i * bm) % seq
    prev_ok = (pos != 0).astype(F32)
    next_ok = (pos + bm != seq).astype(F32)
    wg = wg_ref[...]
    gate = _dot(h_ref[...], wg)
    halo = _dot(hh_ref[...].astype(BF16), wg)
    gs_ref[0:8, :] = halo[0:8] * prev_ok
    gs_ref[8:bm + 8, :] = gate
    gs_ref[bm + 8:bm + 16, :] = halo[8:16] * next_ok
    cw = cw_ref[...]
    gc = gs_ref[7:bm + 7, :] * cw[0:1] + gate * cw[1:2] + gs_ref[9:bm + 9, :] * cw[2:3] + cb_ref[...]
    up = _dot(h_ref[...], wu_ref[...])
    act = (_gelu(gc) * up).astype(BF16)
    acc_ref[...] += _dot(act, wd_ref[...])

    @pl.when(f == pl.num_programs(1) - 1)
    def _():
        o_ref[...] = x_ref[...] + acc_ref[...]


def ffn(x, g, wg, wu, cw, cb, wd, seq, bm, bf):
    M, D = x.shape
    F = wg.shape[1]
    nb8 = M // 8
    return pl.pallas_call(
        functools.partial(_ffn_kernel, bm=bm, seq=seq),
        grid=(M // bm, F // bf),
        in_specs=[pl.BlockSpec((bm, D), lambda i, f: (i, 0)),
                  pl.BlockSpec((8, D), lambda i, f: (jnp.maximum(i * (bm // 8) - 1, 0), 0)),
                  pl.BlockSpec((8, D), lambda i, f: (jnp.minimum((i + 1) * (bm // 8), nb8 - 1), 0)),
                  pl.BlockSpec((1, D), lambda i, f: (0, 0)),
                  pl.BlockSpec((D, bf), lambda i, f: (0, f)),
                  pl.BlockSpec((D, bf), lambda i, f: (0, f)),
                  pl.BlockSpec((3, bf), lambda i, f: (0, f)),
                  pl.BlockSpec((1, bf), lambda i, f: (0, f)),
                  pl.BlockSpec((bf, D), lambda i, f: (f, 0))],
        out_specs=pl.BlockSpec((bm, D), lambda i, f: (i, 0)),
        out_shape=jax.ShapeDtypeStruct((M, D), F32),
        scratch_shapes=[pltpu.VMEM((bm, D), BF16), pltpu.VMEM((16, D), F32),
                        pltpu.VMEM((bm + 16, bf), F32), pltpu.VMEM((bm, D), F32)],
        compiler_params=_params("parallel", "arbitrary"),
        name="conv_ffn",
    )(x, x, x, g, wg, wu, cw, cb, wd)


def _ple_kernel(x_ref, p_ref, g_ref, wg_ref, wp_ref, gf_ref, o_ref, *, final_norm):
    x = x_ref[...]
    h = _rms(x, g_ref[...]).astype(BF16)
    gate = _sigmoid(_dot(h, wg_ref[...]))
    y = x + gate * _dot(p_ref[...].astype(BF16), wp_ref[...])
    if final_norm:
        y = _rms(y, gf_ref[...])
    o_ref[...] = y


def ple(x, p, layer, g, wg, wp, gf, bm, final_norm):
    M, D = x.shape
    P = p.shape[-1]
    return pl.pallas_call(
        functools.partial(_ple_kernel, final_norm=final_norm),
        grid=(M // bm,),
        in_specs=[pl.BlockSpec((bm, D), lambda i: (i, 0)),
                  pl.BlockSpec((None, bm, P), lambda i: (layer, i, 0)),
                  pl.BlockSpec((1, D), lambda i: (0, 0)),
                  pl.BlockSpec((D, D), lambda i: (0, 0)),
                  pl.BlockSpec((P, D), lambda i: (0, 0)),
                  pl.BlockSpec((1, D), lambda i: (0, 0))],
        out_specs=pl.BlockSpec((bm, D), lambda i: (i, 0)),
        out_shape=jax.ShapeDtypeStruct((M, D), F32),
        compiler_params=_params("parallel"),
        name="ple_final" if final_norm else "ple",
    )(x, p, g, wg, wp, gf)


def _to_scan_kernel(x0_ref, x1_ref, rev_ref, o_ref, a_ref, *, nb, tb, dup, pitch):
    N = HEAD_DIM
    lanes = 2 * 2 * nb * RW_HEADS
    rev = rev_ref[...]
    for d, ref in enumerate((x0_ref, x1_ref)):
        for b in range(nb):
            x = ref[b]
            if d == 1:
                x = _dot_hi(rev, x)
            xt = x.T
            p0 = (d * nb + b) * RW_HEADS
            if dup:
                for h in range(RW_HEADS):
                    blk = xt[h * N:(h + 1) * N]
                    a_ref[(p0 + h) * 2 * pitch:(p0 + h) * 2 * pitch + N, :] = blk
                    a_ref[(p0 + h) * 2 * pitch + pitch:(p0 + h) * 2 * pitch + pitch + N, :] = blk
            else:
                a_ref[p0 * N:(p0 + RW_HEADS) * N, :] = xt
    n_rows = N if dup else N // 2
    for j in range(n_rows):
        rows = a_ref[pl.ds(j, lanes, stride=pitch), :]
        o_ref[pl.ds(j, tb, stride=pitch), :] = rows.T
    for j in range(n_rows, pitch):
        o_ref[pl.ds(j, tb, stride=pitch), :] = jnp.zeros((tb, lanes), F32)


def to_scan(x0, x1, rev, B, T, tb, dup, pitch):
    C = RW_WIDTH
    N = HEAD_DIM
    nblk = T // tb
    lanes = 4 * B * RW_HEADS
    return pl.pallas_call(
        functools.partial(_to_scan_kernel, nb=B, tb=tb, dup=dup, pitch=pitch),
        grid=(nblk,),
        in_specs=[pl.BlockSpec((B, tb, C), lambda i: (0, i, 0)),
                  pl.BlockSpec((B, tb, C), lambda i: (0, nblk - 1 - i, 0)),
                  pl.BlockSpec((tb, tb), lambda i: (0, 0))],
        out_specs=pl.BlockSpec((tb * pitch, lanes), lambda i: (i, 0)),
        out_shape=jax.ShapeDtypeStruct((T * pitch, lanes), F32),
        scratch_shapes=[pltpu.VMEM((2 * B * RW_HEADS * (2 * pitch if dup else N), tb), F32)],
        compiler_params=_params("parallel"),
        name="to_scan_keys" if dup else "to_scan_values",
    )(x0.reshape(B, T, C), x1.reshape(B, T, C), rev)


def _from_scan_kernel(y_ref, rev_ref, yf_ref, yb_ref, a_ref, *, nb, tb):
    N = HEAD_DIM
    C = RW_WIDTH
    lanes = 2 * 2 * nb * RW_HEADS
    for ih in range(N // 2):
        rows = y_ref[pl.ds(ih, tb, stride=N // 2), :]
        a_ref[pl.ds(ih, lanes, stride=N // 2), :] = rows.T
    rev = rev_ref[...]
    for d, ref in enumerate((yf_ref, yb_ref)):
        for b in range(nb):
            y = a_ref[(d * nb + b) * C:(d * nb + b + 1) * C, :].T
            if d == 1:
                y = _dot_hi(rev, y)
            ref[b] = y


def from_scan(y, rev, B, T, tb):
    C = RW_WIDTH
    N = HEAD_DIM
    nblk = T // tb
    lanes = y.shape[-1]
    out = jax.ShapeDtypeStruct((B, T, C), F32)
    yf, yb = pl.pallas_call(
        functools.partial(_from_scan_kernel, nb=B, tb=tb),
        grid=(nblk,),
        in_specs=[pl.BlockSpec((tb * (N // 2), lanes), lambda i: (i, 0)),
                  pl.BlockSpec((tb, tb), lambda i: (0, 0))],
        out_specs=[pl.BlockSpec((B, tb, C), lambda i: (0, i, 0)),
                   pl.BlockSpec((B, tb, C), lambda i: (0, nblk - 1 - i, 0))],
        out_shape=[out, out],
        scratch_shapes=[pltpu.VMEM((2 * B * RW_HEADS * N, tb), F32)],
        compiler_params=_params("parallel"),
        name="from_scan",
    )(y.reshape(T * (N // 2), lanes), rev)
    return yf.reshape(B * T, C), yb.reshape(B * T, C)


def _rope_tables(T):
    inv = ROPE_THETA ** (-jnp.arange(0, HEAD_DIM, 2, dtype=jnp.float32) / HEAD_DIM)
    ang = jnp.arange(T, dtype=jnp.float32)[:, None] * inv[None, :]
    cos, sin = jnp.cos(ang), jnp.sin(ang)
    cos_t = jnp.tile(jnp.concatenate([cos, cos], axis=-1), (1, DIL_STEP_WIDTH // HEAD_DIM))
    sin_t = jnp.tile(jnp.concatenate([-sin, sin], axis=-1), (1, DIL_STEP_WIDTH // HEAD_DIM))
    return cos_t, sin_t


def _pick(n, prefs):
    for c in prefs:
        if n % c == 0:
            return c
    return n


def kernel(x, p, norm_mix, w_in, rw_mu, rw_w0, rw_w2, rw_a0, rw_a2, rw_g2, rw_k_k, rw_k_a, rw_r_k, rw_lnx_g,
           rw_lnx_b, na_bias, w_br_a, w_br_b, w_br_c, w_out, norm_ffn, w_ffn_gate, w_ffn_up, ffn_conv_w,
           ffn_conv_b, w_ffn_down, norm_ple, w_ple_gate, w_ple, norm_final):
    B, T, D = x.shape
    depth = w_in.shape[0]
    M = B * T
    F = w_ffn_gate.shape[-1]
    C = RW_WIDTH
    W = DIL_GROUP_WIDTH
    assert (N_BRANCH * D) % W == 0 and w_in.shape[-1] == RW_COLS + 2 * 3 * DIL_WIDTH + N_BRANCH * D

    gates_w = N_BRANCH * D
    dil_w = 3 * DIL_WIDTH
    na_w = 3 * NA_WIDTH
    nb_cols = gates_w + na_w
    nf_cols = RW_COLS_PAD + dil_w
    na_blk0 = gates_w // W
    dil_blk0 = RW_COLS_PAD // W
    assert RW_COLS_PAD % W == 0

    cos_t, sin_t = _rope_tables(T)
    seg = jnp.asarray(np.kron(np.eye(RW_HEADS), np.ones((HEAD_DIM, HEAD_DIM))), BF16)
    xf = x.reshape(M, D)
    p3 = p.reshape(depth, M, p.shape[-1])
    bm_in = _pick(M, (1024, 512, 256, 128))
    bt_rw = _pick(T, (256, 128))
    tb = _pick(T, (64, 32))
    tt = _pick(T, (128,))
    rev = jnp.asarray(np.eye(tt)[::-1], F32)

    for i in range(depth):
        w = w_in[i]
        o_dil, o_na, o_g = RW_COLS, RW_COLS + dil_w, RW_COLS + dil_w + na_w
        w_b = jnp.concatenate([w[:, o_g:], w[:, o_na:o_g]], axis=1).astype(BF16)
        w_f = jnp.concatenate([w[:, :RW_COLS], jnp.zeros((D, RW_COLS_PAD - RW_COLS), w.dtype), w[:, o_dil:o_na]],
                              axis=1).astype(BF16)
        g_mix = norm_mix[i][None]
        zb = norm_matmul(xf, g_mix, w_b, bm_in, _pick(nb_cols, (1408, 256, 128)), BF16, "in_proj_bf16")
        zf = norm_matmul(xf, g_mix, w_f, bm_in, _pick(nf_cols, (1024, 256, 128)), F32, "in_proj_f32")
        zb3 = zb.reshape(B, T, nb_cols)
        zf3 = zf.reshape(B, T, nf_cols)

        mu = jnp.pad(rw_mu[i], (0, RW_COLS_PAD - RW_COLS))[None]
        r, v, a, dec0, dec1, kd0, kd1, b0, b1, gate = rw_prep(
            zf, 0, mu, rw_w0[i], rw_w2[i], rw_a0[i], rw_a2[i], rw_g2[i], rw_k_k[i][None], rw_k_a[i][None],
            seg, T, bt_rw)
        keys = lambda x0, x1: to_scan(x0, x1, rev, B, T, tt, dup=True, pitch=KEY_PITCH)
        vals = to_scan(v, v, rev, B, T, tt, dup=False, pitch=HEAD_DIM // 2).reshape(T, HEAD_DIM // 2, -1)
        y = rw_scan(keys(dec0, dec1), keys(kd0, kd1), keys(a, a), keys(b0, b1), keys(r, r), vals, tb, KEY_PITCH)
        yf, yb = from_scan(y, rev, B, T, tt)
        ya = rw_post(yf, yb, r, v, kd0, kd1, gate, rw_r_k[i].reshape(1, C), rw_lnx_g[i][None], rw_lnx_b[i][None],
                     seg, bt_rw)

        dil_outs = []
        for g, (window, dil) in enumerate(DIL_PATTERNS):
            dil_outs.append(dilated_group(zf3, dil_blk0 + g, dil_blk0 + 3 + g, dil_blk0 + 6 + g, cos_t, sin_t,
                                          window, dil))

        yc = neighbourhood(zb3, na_blk0, na_blk0 + 3, na_blk0 + 6, na_bias_table(na_bias[i])).reshape(M, NA_WIDTH)

        merged = merge(ya, dil_outs, yc, zb, w_br_a[i].astype(BF16), w_br_b[i].astype(BF16),
                       w_br_c[i].astype(BF16), _pick(M, (512, 256, 128)), _pick(D, (512, 256, 128)))
        xf = matmul_res(merged, w_out[i].astype(BF16), xf, _pick(M, (1024, 512, 256, 128)),
                        _pick(D, (1024, 512, 256, 128)))

        xf = ffn(xf, norm_ffn[i][None], w_ffn_gate[i].astype(BF16), w_ffn_up[i].astype(BF16), ffn_conv_w[i],
                 ffn_conv_b[i][None], w_ffn_down[i].astype(BF16), T, _pick(T, (512, 256, 128)),
                 _pick(F, (512, 256, 128)))

        xf = ple(xf, p3, i, norm_ple[i][None], w_ple_gate[i].astype(BF16), w_ple[i].astype(BF16),
                 norm_final[None], _pick(M, (512, 256, 128)), final_norm=(i == depth - 1))
    return xf.reshape(B, T, D)
```

```python
import functools

import numpy as np
import jax
import jax.numpy as jnp
from jax import lax
from jax.experimental import pallas as pl
from jax.experimental.pallas import tpu as pltpu

HEAD_DIM = 64
HALF_DIM = HEAD_DIM // 2
SCALE = HEAD_DIM ** -0.5
NORM_EPS = 1e-6
ROPE_THETA = 10000.0
NEG_INF = -1e30

RW_HEADS = 8
RW_WIDTH = RW_HEADS * HEAD_DIM
RW_DECAY_LORA = 32
RW_AAA_LORA = 32
RW_GATE_LORA = 96
RW_LNX_EPS = 64e-5
RW_COLS = 3 * RW_WIDTH + 2 * RW_DECAY_LORA + 2 * RW_AAA_LORA + RW_GATE_LORA
RW_COLS_PAD = 1792
KEY_PITCH = HEAD_DIM + 4
SCAN_UNROLL = 32

DIL_PATTERNS = ((128, 1), (512, 4), (2048, 16))
DIL_HEADS_PER_GROUP = 4
DIL_GROUP_WIDTH = DIL_HEADS_PER_GROUP * HEAD_DIM
DIL_WIDTH = len(DIL_PATTERNS) * DIL_GROUP_WIDTH
DIL_STEP_WIDTH = 2 * HEAD_DIM

GRID_W = 64
NA_HEADS = 12
NA_WIDTH = NA_HEADS * HEAD_DIM
NA_KH = 8
NA_KW = 16
NA_GROUP_HEADS = 4
NA_GROUP_WIDTH = NA_GROUP_HEADS * HEAD_DIM
NA_ROW_UNROLL = 4

N_BRANCH = 3
VMEM_LIMIT = 56 * 1024 * 1024

BF16 = jnp.bfloat16
F32 = jnp.float32
HIGHEST = lax.Precision.HIGHEST


def _params(*sem):
    return pltpu.CompilerParams(dimension_semantics=sem, vmem_limit_bytes=VMEM_LIMIT)


def _rms(x, g):
    ms = jnp.mean(x * x, axis=-1, keepdims=True)
    return x * lax.rsqrt(ms + NORM_EPS) * g


def _sigmoid(x):
    return 0.5 * jnp.tanh(0.5 * x) + 0.5


def _dot(a, b):
    return jnp.dot(a, b, preferred_element_type=F32)


def _dot_nt(a, b):
    return lax.dot_general(a, b, (((1,), (1,)), ((), ())), preferred_element_type=F32)


def _dot_hi(a, b):
    return jnp.dot(a, b, preferred_element_type=F32, precision=HIGHEST)


def _seg_sum(x, seg):
    hi = x.astype(BF16)
    rest = x - hi.astype(F32)
    mid = rest.astype(BF16)
    lo = (rest - mid.astype(F32)).astype(BF16)
    return _dot(hi, seg) + _dot(mid, seg) + _dot(lo, seg)


def _norm_matmul_kernel(x_ref, g_ref, w_ref, o_ref, h_ref):
    @pl.when(pl.program_id(1) == 0)
    def _():
        h_ref[...] = _rms(x_ref[...], g_ref[...]).astype(BF16)

    o_ref[...] = _dot(h_ref[...], w_ref[...]).astype(o_ref.dtype)


def norm_matmul(x, g, w, bm, bn, out_dtype, name):
    M, K = x.shape
    N = w.shape[1]
    return pl.pallas_call(
        _norm_matmul_kernel,
        grid=(M // bm, N // bn),
        in_specs=[pl.BlockSpec((bm, K), lambda i, j: (i, 0)),
                  pl.BlockSpec((1, K), lambda i, j: (0, 0)),
                  pl.BlockSpec((K, bn), lambda i, j: (0, j))],
        out_specs=pl.BlockSpec((bm, bn), lambda i, j: (i, j)),
        out_shape=jax.ShapeDtypeStruct((M, N), out_dtype),
        scratch_shapes=[pltpu.VMEM((bm, K), BF16)],
        compiler_params=_params("parallel", "arbitrary"),
        name=name,
    )(x, g, w)


def _softplus(x):
    return jnp.maximum(x, 0.0) + jnp.log1p(jnp.exp(-jnp.abs(x)))


def _rw_prep_kernel(c_ref, cp_ref, cn_ref, mu_ref, w0_ref, w2_ref, a0_ref, a2_ref, g2_ref, kk_ref, ka_ref,
                    seg_ref, r_out, v_out, a_out, dec0_out, dec1_out, kd0_out, kd1_out, b0_out, b1_out,
                    gate_out, buf_ref, *, bt, seq):
    i = pl.program_id(0)
    pos = (i * bt) % seq
    prev_ok = (pos != 0).astype(F32)
    next_ok = (pos + bt != seq).astype(F32)
    buf_ref[0:8, :] = cp_ref[...] * prev_ok
    buf_ref[8:bt + 8, :] = c_ref[...]
    buf_ref[bt + 8:bt + 16, :] = cn_ref[...] * next_ok
    cols = c_ref[...]
    shift = 0.5 * (buf_ref[7:bt + 7, :] + buf_ref[9:bt + 9, :])
    c = cols + mu_ref[...] * (shift - cols)
    C = RW_WIDTH
    r = c[:, 0:C]
    k = c[:, C:2 * C]
    v = c[:, 2 * C:3 * C]
    o = 3 * C
    wd = c[:, o:o + 2 * RW_DECAY_LORA]
    o += 2 * RW_DECAY_LORA
    ad = c[:, o:o + 2 * RW_AAA_LORA]
    o += 2 * RW_AAA_LORA
    gd = c[:, o:o + RW_GATE_LORA]
    seg = seg_ref[...]
    kk = k * kk_ref[...]
    nrm = jnp.sqrt(_seg_sum(kk * kk, seg))
    kk = kk / jnp.maximum(nrm, 1e-12)
    r_out[...] = r
    v_out[...] = v
    a_out[...] = -kk
    gate_out[...] = _dot_hi(_sigmoid(gd), g2_ref[...])
    twd = jnp.tanh(wd)
    for d, (dec_out, kd_out, b_out) in enumerate(((dec0_out, kd0_out, b0_out), (dec1_out, kd1_out, b1_out))):
        lw = _dot_hi(twd[:, d * RW_DECAY_LORA:(d + 1) * RW_DECAY_LORA], w2_ref[d])
        w = -_softplus(-(w0_ref[d:d + 1, :] + lw)) - 0.5
        dec_out[...] = jnp.exp(-jnp.exp(w))
        la = _dot_hi(ad[:, d * RW_AAA_LORA:(d + 1) * RW_AAA_LORA], a2_ref[d])
        lr = _sigmoid(a0_ref[d:d + 1, :] + la)
        kd_out[...] = k * (1.0 + (lr - 1.0) * ka_ref[...])
        b_out[...] = kk * lr


def rw_prep(z, rw_col_block, mu, w0, w2, a0, a2, g2, k_k, k_a, seg, seq, bt):
    M = z.shape[0]
    C = RW_WIDTH
    W = RW_COLS_PAD
    nb8 = M // 8
    full = lambda shape: pl.BlockSpec(shape, lambda i: (0,) * len(shape))
    out = jax.ShapeDtypeStruct((M, C), F32)
    return pl.pallas_call(
        functools.partial(_rw_prep_kernel, bt=bt, seq=seq),
        grid=(M // bt,),
        in_specs=[pl.BlockSpec((bt, W), lambda i: (i, rw_col_block)),
                  pl.BlockSpec((8, W), lambda i: (jnp.maximum(i * (bt // 8) - 1, 0), rw_col_block)),
                  pl.BlockSpec((8, W), lambda i: (jnp.minimum((i + 1) * (bt // 8), nb8 - 1), rw_col_block)),
                  full((1, W)), full((2, C)), full((2, RW_DECAY_LORA, C)), full((2, C)),
                  full((2, RW_AAA_LORA, C)), full((RW_GATE_LORA, C)), full((1, C)), full((1, C)),
                  full((C, C))],
        out_specs=[pl.BlockSpec((bt, C), lambda i: (i, 0))] * 10,
        out_shape=[out] * 10,
        scratch_shapes=[pltpu.VMEM((bt + 16, W), F32)],
        compiler_params=_params("parallel"),
        name="rw_prep",
    )(z, z, z, mu, w0, w2, a0, a2, g2, k_k, k_a, seg)


def _rw_scan_kernel(w_ref, k_ref, a_ref, b_ref, r_ref, v_ref, y_ref, s_ref, *, tb, pitch):
    N = HEAD_DIM

    @pl.when(pl.program_id(0) == 0)
    def _():
        s_ref[...] = jnp.zeros_like(s_ref)

    def step(t, carry):
        base = t * pitch
        row = lambda ref, j: ref[pl.ds(base + j, 1), :]
        vt = v_ref[t]
        zero = jnp.zeros_like(vt)

        def dot_a(c, acc):
            acc0, acc1 = acc
            for u in range(SCAN_UNROLL):
                j = c * SCAN_UNROLL + u
                term = s_ref[j] * row(a_ref, j)
                if u % 2 == 0:
                    acc0 = acc0 + term
                else:
                    acc1 = acc1 + term
            return acc0, acc1

        sa0, sa1 = lax.fori_loop(0, N // SCAN_UNROLL, dot_a, (zero, zero))
        sa = sa0 + sa1

        def update(c, acc):
            acc0, acc1 = acc
            for u in range(SCAN_UNROLL):
                j = c * SCAN_UNROLL + u
                s = s_ref[j] * row(w_ref, j) + sa * row(b_ref, j) + vt * row(k_ref, j)
                s_ref[j] = s
                term = s * row(r_ref, j)
                if u % 2 == 0:
                    acc0 = acc0 + term
                else:
                    acc1 = acc1 + term
            return acc0, acc1

        y0, y1 = lax.fori_loop(0, N // SCAN_UNROLL, update, (zero, zero))
        y_ref[t] = y0 + y1
        return carry

    lax.fori_loop(0, tb, step, 0)


def rw_scan(w, k, a, b, r, v, tb, pitch):
    T, nv, L = v.shape
    kspec = pl.BlockSpec((tb * pitch, L), lambda i: (i, 0))
    vspec = pl.BlockSpec((tb, nv, L), lambda i: (i, 0, 0))
    return pl.pallas_call(
        functools.partial(_rw_scan_kernel, tb=tb, pitch=pitch),
        grid=(T // tb,),
        in_specs=[kspec] * 5 + [vspec],
        out_specs=vspec,
        out_shape=jax.ShapeDtypeStruct((T, nv, L), F32),
        scratch_shapes=[pltpu.VMEM((HEAD_DIM, nv, L), F32)],
        compiler_params=_params("arbitrary"),
        name="rw_scan",
    )(w, k, a, b, r, v)


def _rw_post_kernel(yf_ref, yb_ref, r_ref, v_ref, kd0_ref, kd1_ref, gate_ref, rk_ref, g_ref, b_ref, seg_ref,
                    o_ref):
    seg = seg_ref[...]
    y = yf_ref[...] + yb_ref[...]
    inv_n = 1.0 / HEAD_DIM
    mean = _seg_sum(y, seg) * inv_n
    d = y - mean
    var = _seg_sum(d * d, seg) * inv_n
    yn = d * lax.rsqrt(var + RW_LNX_EPS) * g_ref[...] + b_ref[...]
    r = r_ref[...]
    rk = rk_ref[...]
    bonus = _seg_sum(r * kd0_ref[...] * rk + r * kd1_ref[...] * rk, seg) * v_ref[...]
    o_ref[...] = (yn + bonus) * gate_ref[...]


def rw_post(yf, yb, r, v, kd0, kd1, gate, rk, g, b, seg, bt):
    M, C = yf.shape
    tok = pl.BlockSpec((bt, C), lambda i: (i, 0))
    row = pl.BlockSpec((1, C), lambda i: (0, 0))
    return pl.pallas_call(
        _rw_post_kernel,
        grid=(M // bt,),
        in_specs=[tok] * 7 + [row] * 3 + [pl.BlockSpec((C, C), lambda i: (0, 0))],
        out_specs=tok,
        out_shape=jax.ShapeDtypeStruct((M, C), F32),
        compiler_params=_params("parallel"),
        name="rw_post",
    )(yf, yb, r, v, kd0, kd1, gate, rk, g, b, seg)


def _dil_kernel(q_ref, k_ref, v_ref, cos_ref, sin_ref, o_ref, lse_ref, qs_ref, ks_ref, vs_ref, *, n_sub, dil, qb,
                kwin, half):
    W = DIL_STEP_WIDTH
    lane = lax.broadcasted_iota(jnp.int32, (1, W), 1)
    first_half = (lane % HEAD_DIM) < HALF_DIM

    def rope(z, c, s):
        swapped = jnp.where(first_half, pltpu.roll(z, W - HALF_DIM, 1), pltpu.roll(z, HALF_DIM, 1))
        return z * c + swapped * s

    rc = min(256, n_sub)

    def subsequence(r, carry):
        def rope_chunk(i, carry):
            rows = pl.ds(r + i * (rc * dil), rc, stride=dil)
            sl = pl.ds(pl.multiple_of(i * rc, rc), rc)
            c = cos_ref[rows, :]
            s = sin_ref[rows, :]
            qs_ref[sl, :] = (rope(q_ref[0, rows, :], c, s) * SCALE).astype(BF16)
            ks_ref[sl, :] = rope(k_ref[0, rows, :], c, s).astype(BF16)
            vs_ref[sl, :] = v_ref[0, rows, :].astype(BF16)
            return carry

        lax.fori_loop(0, n_sub // rc, rope_chunk, 0)

        nblk = n_sub // qb
        unroll = 2 if nblk % 2 == 0 else 1
        heads = W // HEAD_DIM

        def block_group(ig, carry):
            chains = []
            for u in range(unroll):
                q0 = pl.multiple_of((ig * unroll + u) * qb, qb)
                k0 = pl.multiple_of(jnp.clip(q0 - (kwin - qb) // 2, 0, n_sub - kwin), 64)
                q = qs_ref[pl.ds(q0, qb), :]
                kk = ks_ref[pl.ds(k0, kwin), :]
                jq = q0 + lax.broadcasted_iota(jnp.int32, (qb, kwin), 0)
                jk = k0 + lax.broadcasted_iota(jnp.int32, (qb, kwin), 1)
                ok = jnp.abs(jk - jq) <= half
                for h in range(heads):
                    hs = slice(h * HEAD_DIM, (h + 1) * HEAD_DIM)
                    chains.append((q0, k0, hs, jnp.where(ok, _dot_nt(q[:, hs], kk[:, hs]), NEG_INF)))
            probs = []
            for q0, k0, hs, s in chains:
                m = jnp.max(s, axis=-1, keepdims=True)
                p = jnp.exp(s - m)
                l = jnp.sum(p, axis=-1, keepdims=True)
                probs.append((q0, k0, hs, p.astype(BF16), l, m + jnp.log(l)))
            outs, lses = [], []
            for q0, k0, hs, p, l, lse in probs:
                outs.append(_dot(p, vs_ref[pl.ds(k0, kwin), hs]) / l)
                lses.append(jnp.broadcast_to(lse, (qb, HEAD_DIM)))
                if len(outs) == heads:
                    rows = pl.ds(r + q0 * dil, qb, stride=dil)
                    o_ref[0, rows, :] = jnp.concatenate(outs, axis=-1)
                    lse_ref[0, rows, :] = jnp.concatenate(lses, axis=-1)
                    outs, lses = [], []
            return carry

        lax.fori_loop(0, nblk // unroll, block_group, 0)
        return carry

    lax.fori_loop(0, dil, subsequence, 0)


def dilated_group(z3, q_blk, k_blk, v_blk, cos_t, sin_t, window, dil):
    B, T, ncols = z3.shape
    W = DIL_STEP_WIDTH
    steps = DIL_GROUP_WIDTH // W
    n_sub = T // dil
    half = window // (2 * dil)
    qb = min(128, n_sub)
    kwin = min(qb + 2 * half, n_sub)
    zspec = lambda blk: pl.BlockSpec((1, T, W), lambda b, s: (b, 0, blk * steps + s))
    tspec = pl.BlockSpec((T, W), lambda b, s: (0, 0))
    ospec = pl.BlockSpec((1, T, W), lambda b, s: (b, 0, s))
    oshape = jax.ShapeDtypeStruct((B, T, DIL_GROUP_WIDTH), F32)
    o, lse = pl.pallas_call(
        functools.partial(_dil_kernel, n_sub=n_sub, dil=dil, qb=qb, kwin=kwin, half=half),
        grid=(B, steps),
        in_specs=[zspec(q_blk), zspec(k_blk), zspec(v_blk), tspec, tspec],
        out_specs=[ospec, ospec],
        out_shape=[oshape, oshape],
        scratch_shapes=[pltpu.VMEM((n_sub, W), BF16)] * 3,
        compiler_params=_params("parallel", "parallel"),
        name=f"dilated_d{dil}",
    )(z3, z3, z3, cos_t, sin_t)
    return o.reshape(B * T, DIL_GROUP_WIDTH), lse.reshape(B * T, DIL_GROUP_WIDTH)


def _na_kernel(q_ref, k_ref, v_ref, tab_ref, o_ref, qs_ref, ks_ref, vs_ref, bias_ref, *, rows):
    nkeys = NA_KH * GRID_W
    pc = NA_KH * GRID_W

    for off in range(NA_KH):
        for h in range(NA_GROUP_HEADS):
            for ky in range(NA_KH):
                bias_ref[off, h, :, ky * GRID_W:(ky + 1) * GRID_W] = tab_ref[h, ky - off + NA_KH - 1]

    def prep(c, carry):
        sl = pl.ds(pl.multiple_of(c * pc, pc), pc)
        q = q_ref[0, sl, :] * SCALE
        k = k_ref[0, sl, :]
        v = v_ref[0, sl, :]
        for h in range(NA_GROUP_HEADS):
            hs = slice(h * HEAD_DIM, (h + 1) * HEAD_DIM)
            qs_ref[h, sl, :] = q[:, hs].astype(BF16)
            ks_ref[h, sl, :] = k[:, hs].astype(BF16)
            vs_ref[h, sl, :] = v[:, hs].astype(BF16)
        return carry

    lax.fori_loop(0, rows * GRID_W // pc, prep, 0)

    def row_group(rg, carry):
        chains = []
        for u in range(NA_ROW_UNROLL):
            r = rg * NA_ROW_UNROLL + u
            r0 = jnp.clip(r - NA_KH // 2, 0, rows - NA_KH)
            qrow = pl.ds(pl.multiple_of(r * GRID_W, GRID_W), GRID_W)
            krow = pl.ds(pl.multiple_of(r0 * GRID_W, GRID_W), nkeys)
            for h in range(NA_GROUP_HEADS):
                s = _dot_nt(qs_ref[h, qrow, :], ks_ref[h, krow, :]) + bias_ref[r - r0, h]
                chains.append((u, h, qrow, krow, s))
        probs = []
        for u, h, qrow, krow, s in chains:
            m = jnp.max(s, axis=-1, keepdims=True)
            p = jnp.exp(s - m)
            l = jnp.sum(p, axis=-1, keepdims=True)
            probs.append((u, h, qrow, krow, p.astype(BF16), l))
        outs = {}
        for u, h, qrow, krow, p, l in probs:
            outs.setdefault(u, []).append(_dot(p, vs_ref[h, krow, :]) / l)
            if h == NA_GROUP_HEADS - 1:
                o_ref[0, qrow, :] = jnp.concatenate(outs[u], axis=-1)
        return carry

    lax.fori_loop(0, rows // NA_ROW_UNROLL, row_group, 0)


def na_bias_table(rel_bias):
    qc = np.arange(GRID_W)
    kc = np.arange(GRID_W)
    wc0 = np.clip(qc - NA_KW // 2, 0, GRID_W - NA_KW)
    col_ok = (kc[None, :] >= wc0[:, None]) & (kc[None, :] < wc0[:, None] + NA_KW)
    dx_idx = np.clip(kc[None, :] - qc[:, None], 1 - NA_KW, NA_KW - 1) + NA_KW - 1
    tab = rel_bias[:, :, dx_idx]
    return jnp.where(jnp.asarray(col_ok)[None, None], tab, NEG_INF).astype(F32)


def neighbourhood(z3, q_blk0, k_blk0, v_blk0, bias_tab):
    B, T, ncols = z3.shape
    rows = T // GRID_W
    assert rows >= NA_KH
    W = NA_GROUP_WIDTH
    G = NA_HEADS // NA_GROUP_HEADS
    zspec = lambda blk0: pl.BlockSpec((1, T, W), lambda b, g: (b, 0, blk0 + g))
    return pl.pallas_call(
        functools.partial(_na_kernel, rows=rows),
        grid=(B, G),
        in_specs=[zspec(q_blk0), zspec(k_blk0), zspec(v_blk0),
                  pl.BlockSpec((NA_GROUP_HEADS, 2 * NA_KH - 1, GRID_W, GRID_W), lambda b, g: (g, 0, 0, 0))],
        out_specs=pl.BlockSpec((1, T, W), lambda b, g: (b, 0, g)),
        out_shape=jax.ShapeDtypeStruct((B, T, NA_WIDTH), F32),
        scratch_shapes=[pltpu.VMEM((NA_GROUP_HEADS, T, HEAD_DIM), BF16)] * 3
                       + [pltpu.VMEM((NA_KH, NA_GROUP_HEADS, GRID_W, NA_KH * GRID_W), F32)],
        compiler_params=_params("parallel", "parallel"),
        name="neighbourhood",
    )(z3, z3, z3, bias_tab)


def _merge_kernel(ya_ref, o1_ref, o2_ref, o3_ref, l1_ref, l2_ref, l3_ref, yc_ref, ga_ref, gb_ref, gc_ref,
                  wa_ref, wb_ref, wc_ref, out_ref, yb_ref):
    @pl.when(pl.program_id(1) == 0)
    def _():
        l1, l2, l3 = l1_ref[...], l2_ref[...], l3_ref[...]
        m = jnp.maximum(jnp.maximum(l1, l2), l3)
        e1, e2, e3 = jnp.exp(l1 - m), jnp.exp(l2 - m), jnp.exp(l3 - m)
        yb = (e1 * o1_ref[...] + e2 * o2_ref[...] + e3 * o3_ref[...]) / (e1 + e2 + e3)
        yb_ref[...] = yb.astype(BF16)

    pa = _dot(ya_ref[...].astype(BF16), wa_ref[...])
    pb = _dot(yb_ref[...], wb_ref[...])
    pc = _dot(yc_ref[...].astype(BF16), wc_ref[...])
    gate = lambda ref: _sigmoid(ref[...].astype(F32))
    out = gate(ga_ref) * pa + gate(gb_ref) * pb + gate(gc_ref) * pc
    out_ref[...] = out.astype(BF16)


def merge(ya, dil_outs, yc, z, wa, wb, wc, bm, bn):
    M = ya.shape[0]
    D = wa.shape[1]
    nj = D // bn
    tok = lambda w: pl.BlockSpec((bm, w), lambda i, j: (i, 0))
    gate = lambda g: pl.BlockSpec((bm, bn), lambda i, j: (i, g * nj + j))
    wsp = lambda k: pl.BlockSpec((k, bn), lambda i, j: (0, j))
    (o1, l1), (o2, l2), (o3, l3) = dil_outs
    W = DIL_GROUP_WIDTH
    return pl.pallas_call(
        _merge_kernel,
        grid=(M // bm, nj),
        in_specs=[tok(RW_WIDTH)] + [tok(W)] * 6 + [tok(NA_WIDTH), gate(0), gate(1), gate(2),
                                                  wsp(RW_WIDTH), wsp(W), wsp(NA_WIDTH)],
        out_specs=pl.BlockSpec((bm, bn), lambda i, j: (i, j)),
        out_shape=jax.ShapeDtypeStruct((M, D), BF16),
        scratch_shapes=[pltpu.VMEM((bm, W), BF16)],
        compiler_params=_params("parallel", "arbitrary"),
        name="merge",
    )(ya, o1, o2, o3, l1, l2, l3, yc, z, z, z, wa, wb, wc)


def _matmul_res_kernel(a_ref, w_ref, x_ref, o_ref):
    o_ref[...] = x_ref[...] + _dot(a_ref[...], w_ref[...])


def matmul_res(a, w, x, bm, bn):
    M, K = a.shape
    N = w.shape[1]
    return pl.pallas_call(
        _matmul_res_kernel,
        grid=(M // bm, N // bn),
        in_specs=[pl.BlockSpec((bm, K), lambda i, j: (i, 0)),
                  pl.BlockSpec((K, bn), lambda i, j: (0, j)),
                  pl.BlockSpec((bm, bn), lambda i, j: (i, j))],
        out_specs=pl.BlockSpec((bm, bn), lambda i, j: (i, j)),
        out_shape=jax.ShapeDtypeStruct((M, N), F32),
        compiler_params=_params("parallel", "parallel"),
        name="out_proj",
    )(a, w, x)


def _gelu(x):
    return 0.5 * x * (1.0 + jnp.tanh(np.sqrt(2.0 / np.pi).astype(np.float32) * (x + 0.044715 * (x * x * x))))


FFN_HALO = 16
FFN_SPLIT = 2


def _ffn_kernel(x_ref, xp_ref, xn_ref, g_ref, wg_ref, wu_ref, cw_ref, cb_ref, wd_ref, o_ref,
                h_ref, gs_ref, acc_ref, *, bm, bf, seq):
    i = pl.program_id(0)
    f = pl.program_id(1)
    HL = FFN_HALO

    @pl.when(f == 0)
    def _():
        g = g_ref[...]
        h_ref[0:HL, :] = _rms(xp_ref[...], g).astype(BF16)
        h_ref[HL:bm + HL, :] = _rms(x_ref[...], g).astype(BF16)
        h_ref[bm + HL:bm + 2 * HL, :] = _rms(xn_ref[...], g).astype(BF16)
        acc_ref[...] = jnp.zeros_like(acc_ref)

    pos = (i * bm) % seq
    prev_ok = (pos != 0).astype(F32)
    next_ok = (pos + bm != seq).astype(F32)
    hw = bf // FFN_SPLIT
    h_ext = h_ref[...]
    h = h_ref[HL:bm + HL, :]
    cw = cw_ref[...]
    cb = cb_ref[...]
    gates, ups = [], []
    for s in range(FFN_SPLIT):
        cs = slice(s * hw, (s + 1) * hw)
        gates.append(_dot(h_ext, wg_ref[:, cs]))
        ups.append(_dot(h, wu_ref[:, cs]))
    acts = []
    for s in range(FFN_SPLIT):
        cs = slice(s * hw, (s + 1) * hw)
        gs_ref[s] = gates[s]
        gs_ref[s, HL - 8:HL, :] = gs_ref[s, HL - 8:HL, :] * prev_ok
        gs_ref[s, bm + HL:bm + HL + 8, :] = gs_ref[s, bm + HL:bm + HL + 8, :] * next_ok
        gc = (gs_ref[s, HL - 1:bm + HL - 1, :] * cw[0:1, cs] + gs_ref[s, HL:bm + HL, :] * cw[1:2, cs]
              + gs_ref[s, HL + 1:bm + HL + 1, :] * cw[2:3, cs] + cb[:, cs])
        acts.append((_gelu(gc) * ups[s]).astype(BF16))
    upd = _dot(acts[0], wd_ref[0:hw, :])
    for s in range(1, FFN_SPLIT):
        upd = upd + _dot(acts[s], wd_ref[s * hw:(s + 1) * hw, :])
    acc_ref[...] += upd

    @pl.when(f == pl.num_programs(1) - 1)
    def _():
        o_ref[...] = x_ref[...] + acc_ref[...]


def ffn(x, g, wg, wu, cw, cb, wd, seq, bm, bf):
    M, D = x.shape
    F = wg.shape[1]
    HL = FFN_HALO
    nbh = M // HL
    return pl.pallas_call(
        functools.partial(_ffn_kernel, bm=bm, bf=bf, seq=seq),
        grid=(M // bm, F // bf),
        in_specs=[pl.BlockSpec((bm, D), lambda i, f: (i, 0)),
                  pl.BlockSpec((HL, D), lambda i, f: (jnp.maximum(i * (bm // HL) - 1, 0), 0)),
                  pl.BlockSpec((HL, D), lambda i, f: (jnp.minimum((i + 1) * (bm // HL), nbh - 1), 0)),
                  pl.BlockSpec((1, D), lambda i, f: (0, 0)),
                  pl.BlockSpec((D, bf), lambda i, f: (0, f)),
                  pl.BlockSpec((D, bf), lambda i, f: (0, f)),
                  pl.BlockSpec((3, bf), lambda i, f: (0, f)),
                  pl.BlockSpec((1, bf), lambda i, f: (0, f)),
                  pl.BlockSpec((bf, D), lambda i, f: (f, 0))],
        out_specs=pl.BlockSpec((bm, D), lambda i, f: (i, 0)),
        out_shape=jax.ShapeDtypeStruct((M, D), F32),
        scratch_shapes=[pltpu.VMEM((bm + 2 * HL, D), BF16),
                        pltpu.VMEM((FFN_SPLIT, bm + 2 * HL, bf // FFN_SPLIT), F32),
                        pltpu.VMEM((bm, D), F32)],
        compiler_params=_params("parallel", "arbitrary"),
        name="conv_ffn",
    )(x, x, x, g, wg, wu, cw, cb, wd)


def _ple_kernel(x_ref, p_ref, g_ref, wg_ref, wp_ref, gf_ref, o_ref, *, final_norm):
    x = x_ref[...]
    h = _rms(x, g_ref[...]).astype(BF16)
    gate = _sigmoid(_dot(h, wg_ref[...]))
    y = x + gate * _dot(p_ref[...].astype(BF16), wp_ref[...])
    if final_norm:
        y = _rms(y, gf_ref[...])
    o_ref[...] = y


def ple(x, p, layer, g, wg, wp, gf, bm, final_norm):
    M, D = x.shape
    P = p.shape[-1]
    return pl.pallas_call(
        functools.partial(_ple_kernel, final_norm=final_norm),
        grid=(M // bm,),
        in_specs=[pl.BlockSpec((bm, D), lambda i: (i, 0)),
                  pl.BlockSpec((None, bm, P), lambda i: (layer, i, 0)),
                  pl.BlockSpec((1, D), lambda i: (0, 0)),
                  pl.BlockSpec((D, D), lambda i: (0, 0)),
                  pl.BlockSpec((P, D), lambda i: (0, 0)),
                  pl.BlockSpec((1, D), lambda i: (0, 0))],
        out_specs=pl.BlockSpec((bm, D), lambda i: (i, 0)),
        out_shape=jax.ShapeDtypeStruct((M, D), F32),
        compiler_params=_params("parallel"),
        name="ple_final" if final_norm else "ple",
    )(x, p, g, wg, wp, gf)


def _to_scan_kernel(x0_ref, x1_ref, rev_ref, o_ref, a_ref, *, nb, tb, dup, pitch):
    N = HEAD_DIM
    lanes = 2 * 2 * nb * RW_HEADS
    rev = rev_ref[...]
    for d, ref in enumerate((x0_ref, x1_ref)):
        for b in range(nb):
            x = ref[b]
            if d == 1:
                x = _dot_hi(rev, x)
            xt = x.T
            p0 = (d * nb + b) * RW_HEADS
            if dup:
                for h in range(RW_HEADS):
                    blk = xt[h * N:(h + 1) * N]
                    a_ref[(p0 + h) * 2 * pitch:(p0 + h) * 2 * pitch + N, :] = blk
                    a_ref[(p0 + h) * 2 * pitch + pitch:(p0 + h) * 2 * pitch + pitch + N, :] = blk
            else:
                a_ref[p0 * N:(p0 + RW_HEADS) * N, :] = xt
    n_rows = N if dup else N // 2
    for j in range(n_rows):
        rows = a_ref[pl.ds(j, lanes, stride=pitch), :]
        o_ref[pl.ds(j, tb, stride=pitch), :] = rows.T
    for j in range(n_rows, pitch):
        o_ref[pl.ds(j, tb, stride=pitch), :] = jnp.zeros((tb, lanes), F32)


def to_scan(x0, x1, rev, B, T, tb, dup, pitch):
    C = RW_WIDTH
    N = HEAD_DIM
    nblk = T // tb
    lanes = 4 * B * RW_HEADS
    return pl.pallas_call(
        functools.partial(_to_scan_kernel, nb=B, tb=tb, dup=dup, pitch=pitch),
        grid=(nblk,),
        in_specs=[pl.BlockSpec((B, tb, C), lambda i: (0, i, 0)),
                  pl.BlockSpec((B, tb, C), lambda i: (0, nblk - 1 - i, 0)),
                  pl.BlockSpec((tb, tb), lambda i: (0, 0))],
        out_specs=pl.BlockSpec((tb * pitch, lanes), lambda i: (i, 0)),
        out_shape=jax.ShapeDtypeStruct((T * pitch, lanes), F32),
        scratch_shapes=[pltpu.VMEM((2 * B * RW_HEADS * (2 * pitch if dup else N), tb), F32)],
        compiler_params=_params("parallel"),
        name="to_scan_keys" if dup else "to_scan_values",
    )(x0.reshape(B, T, C), x1.reshape(B, T, C), rev)


def _from_scan_kernel(y_ref, rev_ref, yf_ref, yb_ref, a_ref, *, nb, tb):
    N = HEAD_DIM
    C = RW_WIDTH
    lanes = 2 * 2 * nb * RW_HEADS
    for ih in range(N // 2):
        rows = y_ref[pl.ds(ih, tb, stride=N // 2), :]
        a_ref[pl.ds(ih, lanes, stride=N // 2), :] = rows.T
    rev = rev_ref[...]
    for d, ref in enumerate((yf_ref, yb_ref)):
        for b in range(nb):
            y = a_ref[(d * nb + b) * C:(d * nb + b + 1) * C, :].T
            if d == 1:
                y = _dot_hi(rev, y)
            ref[b] = y


def from_scan(y, rev, B, T, tb):
    C = RW_WIDTH
    N = HEAD_DIM
    nblk = T // tb
    lanes = y.shape[-1]
    out = jax.ShapeDtypeStruct((B, T, C), F32)
    yf, yb = pl.pallas_call(
        functools.partial(_from_scan_kernel, nb=B, tb=tb),
        grid=(nblk,),
        in_specs=[pl.BlockSpec((tb * (N // 2), lanes), lambda i: (i, 0)),
                  pl.BlockSpec((tb, tb), lambda i: (0, 0))],
        out_specs=[pl.BlockSpec((B, tb, C), lambda i: (0, i, 0)),
                   pl.BlockSpec((B, tb, C), lambda i: (0, nblk - 1 - i, 0))],
        out_shape=[out, out],
        scratch_shapes=[pltpu.VMEM((2 * B * RW_HEADS * N, tb), F32)],
        compiler_params=_params("parallel"),
        name="from_scan",
    )(y.reshape(T * (N // 2), lanes), rev)
    return yf.reshape(B * T, C), yb.reshape(B * T, C)


def _rope_tables(T):
    inv = ROPE_THETA ** (-jnp.arange(0, HEAD_DIM, 2, dtype=jnp.float32) / HEAD_DIM)
    ang = jnp.arange(T, dtype=jnp.float32)[:, None] * inv[None, :]
    cos, sin = jnp.cos(ang), jnp.sin(ang)
    cos_t = jnp.tile(jnp.concatenate([cos, cos], axis=-1), (1, DIL_STEP_WIDTH // HEAD_DIM))
    sin_t = jnp.tile(jnp.concatenate([-sin, sin], axis=-1), (1, DIL_STEP_WIDTH // HEAD_DIM))
    return cos_t, sin_t


def _pick(n, prefs):
    for c in prefs:
        if n % c == 0:
            return c
    return n


def kernel(x, p, norm_mix, w_in, rw_mu, rw_w0, rw_w2, rw_a0, rw_a2, rw_g2, rw_k_k, rw_k_a, rw_r_k, rw_lnx_g,
           rw_lnx_b, na_bias, w_br_a, w_br_b, w_br_c, w_out, norm_ffn, w_ffn_gate, w_ffn_up, ffn_conv_w,
           ffn_conv_b, w_ffn_down, norm_ple, w_ple_gate, w_ple, norm_final):
    B, T, D = x.shape
    depth = w_in.shape[0]
    M = B * T
    F = w_ffn_gate.shape[-1]
    C = RW_WIDTH
    W = DIL_GROUP_WIDTH
    assert (N_BRANCH * D) % W == 0 and w_in.shape[-1] == RW_COLS + 2 * 3 * DIL_WIDTH + N_BRANCH * D

    gates_w = N_BRANCH * D
    dil_w = 3 * DIL_WIDTH
    na_w = 3 * NA_WIDTH
    nb_cols = gates_w + na_w
    nf_cols = RW_COLS_PAD + dil_w
    na_blk0 = gates_w // W
    dil_blk0 = RW_COLS_PAD // W
    assert RW_COLS_PAD % W == 0

    cos_t, sin_t = _rope_tables(T)
    seg = jnp.asarray(np.kron(np.eye(RW_HEADS), np.ones((HEAD_DIM, HEAD_DIM))), BF16)
    xf = x.reshape(M, D)
    p3 = p.reshape(depth, M, p.shape[-1])
    bm_in = _pick(M, (1024, 512, 256, 128))
    bt_rw = _pick(T, (256, 128))
    tb = _pick(T, (64, 32))
    tt = _pick(T, (128,))
    rev = jnp.asarray(np.eye(tt)[::-1], F32)

    for i in range(depth):
        w = w_in[i]
        o_dil, o_na, o_g = RW_COLS, RW_COLS + dil_w, RW_COLS + dil_w + na_w
        w_b = jnp.concatenate([w[:, o_g:], w[:, o_na:o_g]], axis=1).astype(BF16)
        w_f = jnp.concatenate([w[:, :RW_COLS], jnp.zeros((D, RW_COLS_PAD - RW_COLS), w.dtype), w[:, o_dil:o_na]],
                              axis=1).astype(BF16)
        g_mix = norm_mix[i][None]
        zb = norm_matmul(xf, g_mix, w_b, bm_in, _pick(nb_cols, (768, 256, 128)), BF16, "in_proj_bf16")
        zf = norm_matmul(xf, g_mix, w_f, bm_in, _pick(nf_cols, (1024, 256, 128)), F32, "in_proj_f32")
        zb3 = zb.reshape(B, T, nb_cols)
        zf3 = zf.reshape(B, T, nf_cols)

        mu = jnp.pad(rw_mu[i], (0, RW_COLS_PAD - RW_COLS))[None]
        r, v, a, dec0, dec1, kd0, kd1, b0, b1, gate = rw_prep(
            zf, 0, mu, rw_w0[i], rw_w2[i], rw_a0[i], rw_a2[i], rw_g2[i], rw_k_k[i][None], rw_k_a[i][None],
            seg, T, bt_rw)
        keys = lambda x0, x1: to_scan(x0, x1, rev, B, T, tt, dup=True, pitch=KEY_PITCH)
        vals = to_scan(v, v, rev, B, T, tt, dup=False, pitch=HEAD_DIM // 2).reshape(T, HEAD_DIM // 2, -1)
        y = rw_scan(keys(dec0, dec1), keys(kd0, kd1), keys(a, a), keys(b0, b1), keys(r, r), vals, tb, KEY_PITCH)
        yf, yb = from_scan(y, rev, B, T, tt)
        ya = rw_post(yf, yb, r, v, kd0, kd1, gate, rw_r_k[i].reshape(1, C), rw_lnx_g[i][None], rw_lnx_b[i][None],
                     seg, bt_rw)

        dil_outs = []
        for g, (window, dil) in enumerate(DIL_PATTERNS):
            dil_outs.append(dilated_group(zf3, dil_blk0 + g, dil_blk0 + 3 + g, dil_blk0 + 6 + g, cos_t, sin_t,
                                          window, dil))

        yc = neighbourhood(zb3, na_blk0, na_blk0 + 3, na_blk0 + 6, na_bias_table(na_bias[i])).reshape(M, NA_WIDTH)

        merged = merge(ya, dil_outs, yc, zb, w_br_a[i].astype(BF16), w_br_b[i].astype(BF16),
                       w_br_c[i].astype(BF16), _pick(M, (512, 256, 128)), _pick(D, (512, 256, 128)))
        xf = matmul_res(merged, w_out[i].astype(BF16), xf, _pick(M, (1024, 512, 256, 128)),
                        _pick(D, (1024, 512, 256, 128)))

        xf = ffn(xf, norm_ffn[i][None], w_ffn_gate[i].astype(BF16), w_ffn_up[i].astype(BF16), ffn_conv_w[i],
                 ffn_conv_b[i][None], w_ffn_down[i].astype(BF16), T, _pick(T, (512, 256, 128)),
                 _pick(F, (512, 256, 128)))

        xf = ple(xf, p3, i, norm_ple[i][None], w_ple_gate[i].astype(BF16), w_ple[i].astype(BF16),
                 norm_final[None], _pick(M, (512, 256, 128)), final_norm=(i == depth - 1))
    return xf.reshape(B, T, D)
```

```python
import functools

import numpy as np
import jax
import jax.numpy as jnp
from jax import lax
from jax.experimental import pallas as pl
from jax.experimental.pallas import tpu as pltpu

HEAD_DIM = 64
HALF_DIM = HEAD_DIM // 2
SCALE = HEAD_DIM ** -0.5
NORM_EPS = 1e-6
ROPE_THETA = 10000.0
NEG_INF = -1e30

RW_HEADS = 8
RW_WIDTH = RW_HEADS * HEAD_DIM
RW_DECAY_LORA = 32
RW_AAA_LORA = 32
RW_GATE_LORA = 96
RW_LNX_EPS = 64e-5
RW_COLS = 3 * RW_WIDTH + 2 * RW_DECAY_LORA + 2 * RW_AAA_LORA + RW_GATE_LORA
RW_COLS_PAD = 1792
KEY_PITCH = HEAD_DIM + 4
VAL_PITCH = HEAD_DIM // 2 + 8
SCAN_UNROLL = 32

DIL_PATTERNS = ((128, 1), (512, 4), (2048, 16))
DIL_HEADS_PER_GROUP = 4
DIL_GROUP_WIDTH = DIL_HEADS_PER_GROUP * HEAD_DIM
DIL_WIDTH = len(DIL_PATTERNS) * DIL_GROUP_WIDTH
DIL_STEP_WIDTH = 2 * HEAD_DIM

GRID_W = 64
NA_HEADS = 12
NA_WIDTH = NA_HEADS * HEAD_DIM
NA_KH = 8
NA_KW = 16
NA_GROUP_HEADS = 4
NA_GROUP_WIDTH = NA_GROUP_HEADS * HEAD_DIM
NA_ROW_UNROLL = 4

N_BRANCH = 3
VMEM_LIMIT = 56 * 1024 * 1024

BF16 = jnp.bfloat16
F32 = jnp.float32
HIGHEST = lax.Precision.HIGHEST


def _params(*sem):
    return pltpu.CompilerParams(dimension_semantics=sem, vmem_limit_bytes=VMEM_LIMIT)


def _rms(x, g):
    ms = jnp.mean(x * x, axis=-1, keepdims=True)
    return x * lax.rsqrt(ms + NORM_EPS) * g


def _sigmoid(x):
    return 0.5 * jnp.tanh(0.5 * x) + 0.5


def _dot(a, b):
    return jnp.dot(a, b, preferred_element_type=F32)


def _dot_nt(a, b):
    return lax.dot_general(a, b, (((1,), (1,)), ((), ())), preferred_element_type=F32)


def _dot_hi(a, b):
    return jnp.dot(a, b, preferred_element_type=F32, precision=HIGHEST)


def _split_hi_lo(w):
    hi = w.astype(BF16)
    return jnp.stack([hi, (w - hi.astype(F32)).astype(BF16)])


def _dot_split(a, w_hi, w_lo):
    a_hi = a.astype(BF16)
    a_lo = (a - a_hi.astype(F32)).astype(BF16)
    return _dot(a_hi, w_hi) + _dot(a_hi, w_lo) + _dot(a_lo, w_hi)


def _seg_sum(x, seg):
    hi = x.astype(BF16)
    rest = x - hi.astype(F32)
    mid = rest.astype(BF16)
    lo = (rest - mid.astype(F32)).astype(BF16)
    return _dot(hi, seg) + _dot(mid, seg) + _dot(lo, seg)


def _norm_matmul_kernel(x_ref, g_ref, w_ref, o_ref, h_ref):
    @pl.when(pl.program_id(1) == 0)
    def _():
        h_ref[...] = _rms(x_ref[...], g_ref[...]).astype(BF16)

    o_ref[...] = _dot(h_ref[...], w_ref[...]).astype(o_ref.dtype)


def norm_matmul(x, g, w, bm, bn, out_dtype, name):
    M, K = x.shape
    N = w.shape[1]
    return pl.pallas_call(
        _norm_matmul_kernel,
        grid=(M // bm, N // bn),
        in_specs=[pl.BlockSpec((bm, K), lambda i, j: (i, 0)),
                  pl.BlockSpec((1, K), lambda i, j: (0, 0)),
                  pl.BlockSpec((K, bn), lambda i, j: (0, j))],
        out_specs=pl.BlockSpec((bm, bn), lambda i, j: (i, j)),
        out_shape=jax.ShapeDtypeStruct((M, N), out_dtype),
        scratch_shapes=[pltpu.VMEM((bm, K), BF16)],
        compiler_params=_params("parallel", "arbitrary"),
        name=name,
    )(x, g, w)


def _softplus(x):
    return jnp.maximum(x, 0.0) + jnp.log1p(jnp.exp(-jnp.abs(x)))


def _rw_prep_kernel(c_ref, cp_ref, cn_ref, mu_ref, w0_ref, w2_ref, a0_ref, a2_ref, g2_ref, kk_ref, ka_ref,
                    seg_ref, r_out, v_out, a_out, dec0_out, dec1_out, kd0_out, kd1_out, b0_out, b1_out,
                    gate_out, buf_ref, *, bt, seq):
    i = pl.program_id(0)
    pos = (i * bt) % seq
    prev_ok = (pos != 0).astype(F32)
    next_ok = (pos + bt != seq).astype(F32)
    buf_ref[0:8, :] = cp_ref[...] * prev_ok
    buf_ref[8:bt + 8, :] = c_ref[...]
    buf_ref[bt + 8:bt + 16, :] = cn_ref[...] * next_ok
    cols = c_ref[...]
    shift = 0.5 * (buf_ref[7:bt + 7, :] + buf_ref[9:bt + 9, :])
    c = cols + mu_ref[...] * (shift - cols)
    C = RW_WIDTH
    r = c[:, 0:C]
    k = c[:, C:2 * C]
    v = c[:, 2 * C:3 * C]
    o = 3 * C
    wd = c[:, o:o + 2 * RW_DECAY_LORA]
    o += 2 * RW_DECAY_LORA
    ad = c[:, o:o + 2 * RW_AAA_LORA]
    o += 2 * RW_AAA_LORA
    gd = c[:, o:o + RW_GATE_LORA]
    seg = seg_ref[...]
    kk = k * kk_ref[...]
    nrm = jnp.sqrt(_seg_sum(kk * kk, seg))
    kk = kk / jnp.maximum(nrm, 1e-12)
    r_out[...] = r
    v_out[...] = v
    a_out[...] = -kk
    gate_out[...] = _dot_split(_sigmoid(gd), g2_ref[0], g2_ref[1])
    twd = jnp.tanh(wd)
    for d, (dec_out, kd_out, b_out) in enumerate(((dec0_out, kd0_out, b0_out), (dec1_out, kd1_out, b1_out))):
        lw = _dot_split(twd[:, d * RW_DECAY_LORA:(d + 1) * RW_DECAY_LORA], w2_ref[0, d], w2_ref[1, d])
        w = -_softplus(-(w0_ref[d:d + 1, :] + lw)) - 0.5
        dec_out[...] = jnp.exp(-jnp.exp(w))
        la = _dot_split(ad[:, d * RW_AAA_LORA:(d + 1) * RW_AAA_LORA], a2_ref[0, d], a2_ref[1, d])
        lr = _sigmoid(a0_ref[d:d + 1, :] + la)
        kd_out[...] = k * (1.0 + (lr - 1.0) * ka_ref[...])
        b_out[...] = kk * lr


def rw_prep(z, rw_col_block, mu, w0, w2, a0, a2, g2, k_k, k_a, seg, seq, bt):
    M = z.shape[0]
    C = RW_WIDTH
    W = RW_COLS_PAD
    nb8 = M // 8
    full = lambda shape: pl.BlockSpec(shape, lambda i: (0,) * len(shape))
    out = jax.ShapeDtypeStruct((M, C), F32)
    return pl.pallas_call(
        functools.partial(_rw_prep_kernel, bt=bt, seq=seq),
        grid=(M // bt,),
        in_specs=[pl.BlockSpec((bt, W), lambda i: (i, rw_col_block)),
                  pl.BlockSpec((8, W), lambda i: (jnp.maximum(i * (bt // 8) - 1, 0), rw_col_block)),
                  pl.BlockSpec((8, W), lambda i: (jnp.minimum((i + 1) * (bt // 8), nb8 - 1), rw_col_block)),
                  full((1, W)), full((2, C)), full((2, 2, RW_DECAY_LORA, C)), full((2, C)),
                  full((2, 2, RW_AAA_LORA, C)), full((2, RW_GATE_LORA, C)), full((1, C)), full((1, C)),
                  full((C, C))],
        out_specs=[pl.BlockSpec((bt, C), lambda i: (i, 0))] * 10,
        out_shape=[out] * 10,
        scratch_shapes=[pltpu.VMEM((bt + 16, W), F32)],
        compiler_params=_params("parallel"),
        name="rw_prep",
    )(z, z, z, mu, w0, _split_hi_lo(w2), a0, _split_hi_lo(a2), _split_hi_lo(g2), k_k, k_a, seg)


def _rw_scan_kernel(w_ref, k_ref, a_ref, b_ref, r_ref, v_ref, y_ref, s_ref, *, tb, pitch, vpitch):
    N = HEAD_DIM
    NV = N // 2

    @pl.when(pl.program_id(0) == 0)
    def _():
        s_ref[...] = jnp.zeros_like(s_ref)

    def step(t, carry):
        base = t * pitch
        vbase = pl.multiple_of(t * vpitch, 8)
        row = lambda ref, j: ref[pl.ds(base + j, 1), :]
        vt = v_ref[pl.ds(vbase, NV), :]
        zero = jnp.zeros_like(vt)

        def dot_a(c, acc):
            acc0, acc1 = acc
            for u in range(SCAN_UNROLL):
                j = c * SCAN_UNROLL + u
                term = s_ref[j] * row(a_ref, j)
                if u % 2 == 0:
                    acc0 = acc0 + term
                else:
                    acc1 = acc1 + term
            return acc0, acc1

        sa0, sa1 = lax.fori_loop(0, N // SCAN_UNROLL, dot_a, (zero, zero))
        sa = sa0 + sa1

        def update(c, acc):
            acc0, acc1 = acc
            for u in range(SCAN_UNROLL):
                j = c * SCAN_UNROLL + u
                s = s_ref[j] * row(w_ref, j) + sa * row(b_ref, j) + vt * row(k_ref, j)
                s_ref[j] = s
                term = s * row(r_ref, j)
                if u % 2 == 0:
                    acc0 = acc0 + term
                else:
                    acc1 = acc1 + term
            return acc0, acc1

        y0, y1 = lax.fori_loop(0, N // SCAN_UNROLL, update, (zero, zero))
        y_ref[pl.ds(vbase, NV), :] = y0 + y1
        y_ref[pl.ds(vbase + NV, vpitch - NV), :] = jnp.zeros((vpitch - NV, zero.shape[1]), F32)
        return carry

    lax.fori_loop(0, tb, step, 0)


def rw_scan(w, k, a, b, r, v, T, tb, pitch, vpitch):
    L = v.shape[-1]
    kspec = pl.BlockSpec((tb * pitch, L), lambda i: (i, 0))
    vspec = pl.BlockSpec((tb * vpitch, L), lambda i: (i, 0))
    return pl.pallas_call(
        functools.partial(_rw_scan_kernel, tb=tb, pitch=pitch, vpitch=vpitch),
        grid=(T // tb,),
        in_specs=[kspec] * 5 + [vspec],
        out_specs=vspec,
        out_shape=jax.ShapeDtypeStruct((T * vpitch, L), F32),
        scratch_shapes=[pltpu.VMEM((HEAD_DIM, HEAD_DIM // 2, L), F32)],
        compiler_params=_params("arbitrary"),
        name="rw_scan",
    )(w, k, a, b, r, v)


def _rw_post_kernel(yf_ref, yb_ref, r_ref, v_ref, kd0_ref, kd1_ref, gate_ref, rk_ref, g_ref, b_ref, seg_ref,
                    o_ref):
    seg = seg_ref[...]
    y = yf_ref[...] + yb_ref[...]
    inv_n = 1.0 / HEAD_DIM
    mean = _seg_sum(y, seg) * inv_n
    d = y - mean
    var = _seg_sum(d * d, seg) * inv_n
    yn = d * lax.rsqrt(var + RW_LNX_EPS) * g_ref[...] + b_ref[...]
    r = r_ref[...]
    rk = rk_ref[...]
    bonus = _seg_sum(r * kd0_ref[...] * rk + r * kd1_ref[...] * rk, seg) * v_ref[...]
    o_ref[...] = (yn + bonus) * gate_ref[...]


def rw_post(yf, yb, r, v, kd0, kd1, gate, rk, g, b, seg, bt):
    M, C = yf.shape
    tok = pl.BlockSpec((bt, C), lambda i: (i, 0))
    row = pl.BlockSpec((1, C), lambda i: (0, 0))
    return pl.pallas_call(
        _rw_post_kernel,
        grid=(M // bt,),
        in_specs=[tok] * 7 + [row] * 3 + [pl.BlockSpec((C, C), lambda i: (0, 0))],
        out_specs=tok,
        out_shape=jax.ShapeDtypeStruct((M, C), F32),
        compiler_params=_params("parallel"),
        name="rw_post",
    )(yf, yb, r, v, kd0, kd1, gate, rk, g, b, seg)


def _dil_kernel(q_ref, k_ref, v_ref, cos_ref, sin_ref, o_ref, lse_ref, qs_ref, ks_ref, vs_ref, *, n_sub, dil, qb,
                kwin, half):
    W = DIL_STEP_WIDTH
    lane = lax.broadcasted_iota(jnp.int32, (1, W), 1)
    first_half = (lane % HEAD_DIM) < HALF_DIM

    def rope(z, c, s):
        swapped = jnp.where(first_half, pltpu.roll(z, W - HALF_DIM, 1), pltpu.roll(z, HALF_DIM, 1))
        return z * c + swapped * s

    rc = min(256, n_sub)

    def subsequence(r, carry):
        def rope_chunk(i, carry):
            rows = pl.ds(r + i * (rc * dil), rc, stride=dil)
            sl = pl.ds(pl.multiple_of(i * rc, rc), rc)
            c = cos_ref[rows, :]
            s = sin_ref[rows, :]
            qs_ref[sl, :] = (rope(q_ref[0, rows, :], c, s) * SCALE).astype(BF16)
            ks_ref[sl, :] = rope(k_ref[0, rows, :], c, s).astype(BF16)
            vs_ref[sl, :] = v_ref[0, rows, :].astype(BF16)
            return carry

        lax.fori_loop(0, n_sub // rc, rope_chunk, 0)

        nblk = n_sub // qb
        unroll = next(u for u in (4, 2, 1) if nblk % u == 0)
        heads = W // HEAD_DIM

        def block_group(ig, carry):
            chains = []
            for u in range(unroll):
                q0 = pl.multiple_of((ig * unroll + u) * qb, qb)
                k0 = pl.multiple_of(jnp.clip(q0 - (kwin - qb) // 2, 0, n_sub - kwin), 64)
                q = qs_ref[pl.ds(q0, qb), :]
                kk = ks_ref[pl.ds(k0, kwin), :]
                jq = q0 + lax.broadcasted_iota(jnp.int32, (qb, kwin), 0)
                jk = k0 + lax.broadcasted_iota(jnp.int32, (qb, kwin), 1)
                ok = jnp.abs(jk - jq) <= half
                for h in range(heads):
                    hs = slice(h * HEAD_DIM, (h + 1) * HEAD_DIM)
                    chains.append((q0, k0, hs, jnp.where(ok, _dot_nt(q[:, hs], kk[:, hs]), NEG_INF)))
            probs = []
            for q0, k0, hs, s in chains:
                m = jnp.max(s, axis=-1, keepdims=True)
                p = jnp.exp(s - m)
                l = jnp.sum(p, axis=-1, keepdims=True)
                probs.append((q0, k0, hs, p.astype(BF16), l, m + jnp.log(l)))
            outs, lses = [], []
            for q0, k0, hs, p, l, lse in probs:
                outs.append(_dot(p, vs_ref[pl.ds(k0, kwin), hs]) / l)
                lses.append(jnp.broadcast_to(lse, (qb, HEAD_DIM)))
                if len(outs) == heads:
                    rows = pl.ds(r + q0 * dil, qb, stride=dil)
                    o_ref[0, rows, :] = jnp.concatenate(outs, axis=-1)
                    lse_ref[0, rows, :] = jnp.concatenate(lses, axis=-1)
                    outs, lses = [], []
            return carry

        lax.fori_loop(0, nblk // unroll, block_group, 0)
        return carry

    lax.fori_loop(0, dil, subsequence, 0)


def dilated_group(z3, q_blk, k_blk, v_blk, cos_t, sin_t, window, dil):
    B, T, ncols = z3.shape
    W = DIL_STEP_WIDTH
    steps = DIL_GROUP_WIDTH // W
    n_sub = T // dil
    half = window // (2 * dil)
    qb = min(128, n_sub)
    kwin = min(qb + 2 * half, n_sub)
    zspec = lambda blk: pl.BlockSpec((1, T, W), lambda b, s: (b, 0, blk * steps + s))
    tspec = pl.BlockSpec((T, W), lambda b, s: (0, 0))
    ospec = pl.BlockSpec((1, T, W), lambda b, s: (b, 0, s))
    oshape = jax.ShapeDtypeStruct((B, T, DIL_GROUP_WIDTH), F32)
    o, lse = pl.pallas_call(
        functools.partial(_dil_kernel, n_sub=n_sub, dil=dil, qb=qb, kwin=kwin, half=half),
        grid=(B, steps),
        in_specs=[zspec(q_blk), zspec(k_blk), zspec(v_blk), tspec, tspec],
        out_specs=[ospec, ospec],
        out_shape=[oshape, oshape],
        scratch_shapes=[pltpu.VMEM((n_sub, W), BF16)] * 3,
        compiler_params=_params("parallel", "parallel"),
        name=f"dilated_d{dil}",
    )(z3, z3, z3, cos_t, sin_t)
    return o.reshape(B * T, DIL_GROUP_WIDTH), lse.reshape(B * T, DIL_GROUP_WIDTH)


def _na_kernel(q_ref, k_ref, v_ref, tab_ref, o_ref, qs_ref, ks_ref, vs_ref, bias_ref, *, rows):
    nkeys = NA_KH * GRID_W
    pc = NA_KH * GRID_W

    for off in range(NA_KH):
        for h in range(NA_GROUP_HEADS):
            for ky in range(NA_KH):
                bias_ref[off, h, :, ky * GRID_W:(ky + 1) * GRID_W] = tab_ref[h, ky - off + NA_KH - 1]

    def prep(c, carry):
        sl = pl.ds(pl.multiple_of(c * pc, pc), pc)
        q = q_ref[0, sl, :] * SCALE
        k = k_ref[0, sl, :]
        v = v_ref[0, sl, :]
        for h in range(NA_GROUP_HEADS):
            hs = slice(h * HEAD_DIM, (h + 1) * HEAD_DIM)
            qs_ref[h, sl, :] = q[:, hs].astype(BF16)
            ks_ref[h, sl, :] = k[:, hs].astype(BF16)
            vs_ref[h, sl, :] = v[:, hs].astype(BF16)
        return carry

    lax.fori_loop(0, rows * GRID_W // pc, prep, 0)

    def row_group(rg, carry):
        chains = []
        for u in range(NA_ROW_UNROLL):
            r = rg * NA_ROW_UNROLL + u
            r0 = jnp.clip(r - NA_KH // 2, 0, rows - NA_KH)
            qrow = pl.ds(pl.multiple_of(r * GRID_W, GRID_W), GRID_W)
            krow = pl.ds(pl.multiple_of(r0 * GRID_W, GRID_W), nkeys)
            for h in range(NA_GROUP_HEADS):
                s = _dot_nt(qs_ref[h, qrow, :], ks_ref[h, krow, :]) + bias_ref[r - r0, h]
                chains.append((u, h, qrow, krow, s))
        probs = []
        for u, h, qrow, krow, s in chains:
            m = jnp.max(s, axis=-1, keepdims=True)
            p = jnp.exp(s - m)
            l = jnp.sum(p, axis=-1, keepdims=True)
            probs.append((u, h, qrow, krow, p.astype(BF16), l))
        outs = {}
        for u, h, qrow, krow, p, l in probs:
            outs.setdefault(u, []).append(_dot(p, vs_ref[h, krow, :]) / l)
            if h == NA_GROUP_HEADS - 1:
                o_ref[0, qrow, :] = jnp.concatenate(outs[u], axis=-1)
        return carry

    lax.fori_loop(0, rows // NA_ROW_UNROLL, row_group, 0)


def na_bias_table(rel_bias):
    qc = np.arange(GRID_W)
    kc = np.arange(GRID_W)
    wc0 = np.clip(qc - NA_KW // 2, 0, GRID_W - NA_KW)
    col_ok = (kc[None, :] >= wc0[:, None]) & (kc[None, :] < wc0[:, None] + NA_KW)
    dx_idx = np.clip(kc[None, :] - qc[:, None], 1 - NA_KW, NA_KW - 1) + NA_KW - 1
    tab = rel_bias[:, :, dx_idx]
    return jnp.where(jnp.asarray(col_ok)[None, None], tab, NEG_INF).astype(F32)


def neighbourhood(z3, q_blk0, k_blk0, v_blk0, bias_tab):
    B, T, ncols = z3.shape
    rows = T // GRID_W
    assert rows >= NA_KH
    W = NA_GROUP_WIDTH
    G = NA_HEADS // NA_GROUP_HEADS
    zspec = lambda blk0: pl.BlockSpec((1, T, W), lambda b, g: (b, 0, blk0 + g))
    return pl.pallas_call(
        functools.partial(_na_kernel, rows=rows),
        grid=(B, G),
        in_specs=[zspec(q_blk0), zspec(k_blk0), zspec(v_blk0),
                  pl.BlockSpec((NA_GROUP_HEADS, 2 * NA_KH - 1, GRID_W, GRID_W), lambda b, g: (g, 0, 0, 0))],
        out_specs=pl.BlockSpec((1, T, W), lambda b, g: (b, 0, g)),
        out_shape=jax.ShapeDtypeStruct((B, T, NA_WIDTH), F32),
        scratch_shapes=[pltpu.VMEM((NA_GROUP_HEADS, T, HEAD_DIM), BF16)] * 3
                       + [pltpu.VMEM((NA_KH, NA_GROUP_HEADS, GRID_W, NA_KH * GRID_W), F32)],
        compiler_params=_params("parallel", "parallel"),
        name="neighbourhood",
    )(z3, z3, z3, bias_tab)


def _merge_kernel(ya_ref, o1_ref, o2_ref, o3_ref, l1_ref, l2_ref, l3_ref, yc_ref, ga_ref, gb_ref, gc_ref,
                  wa_ref, wb_ref, wc_ref, out_ref, yb_ref):
    @pl.when(pl.program_id(1) == 0)
    def _():
        l1, l2, l3 = l1_ref[...], l2_ref[...], l3_ref[...]
        m = jnp.maximum(jnp.maximum(l1, l2), l3)
        e1, e2, e3 = jnp.exp(l1 - m), jnp.exp(l2 - m), jnp.exp(l3 - m)
        yb = (e1 * o1_ref[...] + e2 * o2_ref[...] + e3 * o3_ref[...]) / (e1 + e2 + e3)
        yb_ref[...] = yb.astype(BF16)

    pa = _dot(ya_ref[...].astype(BF16), wa_ref[...])
    pb = _dot(yb_ref[...], wb_ref[...])
    pc = _dot(yc_ref[...].astype(BF16), wc_ref[...])
    gate = lambda ref: _sigmoid(ref[...].astype(F32))
    out = gate(ga_ref) * pa + gate(gb_ref) * pb + gate(gc_ref) * pc
    out_ref[...] = out.astype(BF16)


def merge(ya, dil_outs, yc, z, wa, wb, wc, bm, bn):
    M = ya.shape[0]
    D = wa.shape[1]
    nj = D // bn
    tok = lambda w: pl.BlockSpec((bm, w), lambda i, j: (i, 0))
    gate = lambda g: pl.BlockSpec((bm, bn), lambda i, j: (i, g * nj + j))
    wsp = lambda k: pl.BlockSpec((k, bn), lambda i, j: (0, j))
    (o1, l1), (o2, l2), (o3, l3) = dil_outs
    W = DIL_GROUP_WIDTH
    return pl.pallas_call(
        _merge_kernel,
        grid=(M // bm, nj),
        in_specs=[tok(RW_WIDTH)] + [tok(W)] * 6 + [tok(NA_WIDTH), gate(0), gate(1), gate(2),
                                                  wsp(RW_WIDTH), wsp(W), wsp(NA_WIDTH)],
        out_specs=pl.BlockSpec((bm, bn), lambda i, j: (i, j)),
        out_shape=jax.ShapeDtypeStruct((M, D), BF16),
        scratch_shapes=[pltpu.VMEM((bm, W), BF16)],
        compiler_params=_params("parallel", "arbitrary"),
        name="merge",
    )(ya, o1, o2, o3, l1, l2, l3, yc, z, z, z, wa, wb, wc)


def _matmul_res_kernel(a_ref, w_ref, x_ref, o_ref):
    o_ref[...] = x_ref[...] + _dot(a_ref[...], w_ref[...])


def matmul_res(a, w, x, bm, bn):
    M, K = a.shape
    N = w.shape[1]
    return pl.pallas_call(
        _matmul_res_kernel,
        grid=(M // bm, N // bn),
        in_specs=[pl.BlockSpec((bm, K), lambda i, j: (i, 0)),
                  pl.BlockSpec((K, bn), lambda i, j: (0, j)),
                  pl.BlockSpec((bm, bn), lambda i, j: (i, j))],
        out_specs=pl.BlockSpec((bm, bn), lambda i, j: (i, j)),
        out_shape=jax.ShapeDtypeStruct((M, N), F32),
        compiler_params=_params("parallel", "parallel"),
        name="out_proj",
    )(a, w, x)


def _gelu(x):
    return 0.5 * x * (1.0 + jnp.tanh(np.sqrt(2.0 / np.pi).astype(np.float32) * (x + 0.044715 * (x * x * x))))


FFN_HALO = 16
FFN_SPLIT = 2


def _ffn_kernel(x_ref, xp_ref, xn_ref, g_ref, wg_ref, wu_ref, cw_ref, cb_ref, wd_ref, o_ref,
                h_ref, gs_ref, acc_ref, *, bm, bf, seq):
    i = pl.program_id(0)
    f = pl.program_id(1)
    HL = FFN_HALO

    @pl.when(f == 0)
    def _():
        g = g_ref[...]
        h_ref[0:HL, :] = _rms(xp_ref[...], g).astype(BF16)
        h_ref[HL:bm + HL, :] = _rms(x_ref[...], g).astype(BF16)
        h_ref[bm + HL:bm + 2 * HL, :] = _rms(xn_ref[...], g).astype(BF16)
        acc_ref[...] = jnp.zeros_like(acc_ref)

    pos = (i * bm) % seq
    prev_ok = (pos != 0).astype(F32)
    next_ok = (pos + bm != seq).astype(F32)
    hw = bf // FFN_SPLIT
    h_ext = h_ref[...]
    h = h_ref[HL:bm + HL, :]
    cw = cw_ref[...]
    cb = cb_ref[...]
    gates, ups = [], []
    for s in range(FFN_SPLIT):
        cs = slice(s * hw, (s + 1) * hw)
        gates.append(_dot(h_ext, wg_ref[:, cs]))
        ups.append(_dot(h, wu_ref[:, cs]))
    acts = []
    for s in range(FFN_SPLIT):
        cs = slice(s * hw, (s + 1) * hw)
        gs_ref[s] = gates[s]
        gs_ref[s, HL - 8:HL, :] = gs_ref[s, HL - 8:HL, :] * prev_ok
        gs_ref[s, bm + HL:bm + HL + 8, :] = gs_ref[s, bm + HL:bm + HL + 8, :] * next_ok
        gc = (gs_ref[s, HL - 1:bm + HL - 1, :] * cw[0:1, cs] + gs_ref[s, HL:bm + HL, :] * cw[1:2, cs]
              + gs_ref[s, HL + 1:bm + HL + 1, :] * cw[2:3, cs] + cb[:, cs])
        acts.append((_gelu(gc) * ups[s]).astype(BF16))
    upd = _dot(acts[0], wd_ref[0:hw, :])
    for s in range(1, FFN_SPLIT):
        upd = upd + _dot(acts[s], wd_ref[s * hw:(s + 1) * hw, :])
    acc_ref[...] += upd

    @pl.when(f == pl.num_programs(1) - 1)
    def _():
        o_ref[...] = x_ref[...] + acc_ref[...]


def ffn(x, g, wg, wu, cw, cb, wd, seq, bm, bf):
    M, D = x.shape
    F = wg.shape[1]
    HL = FFN_HALO
    nbh = M // HL
    return pl.pallas_call(
        functools.partial(_ffn_kernel, bm=bm, bf=bf, seq=seq),
        grid=(M // bm, F // bf),
        in_specs=[pl.BlockSpec((bm, D), lambda i, f: (i, 0)),
                  pl.BlockSpec((HL, D), lambda i, f: (jnp.maximum(i * (bm // HL) - 1, 0), 0)),
                  pl.BlockSpec((HL, D), lambda i, f: (jnp.minimum((i + 1) * (bm // HL), nbh - 1), 0)),
                  pl.BlockSpec((1, D), lambda i, f: (0, 0)),
                  pl.BlockSpec((D, bf), lambda i, f: (0, f)),
                  pl.BlockSpec((D, bf), lambda i, f: (0, f)),
                  pl.BlockSpec((3, bf), lambda i, f: (0, f)),
                  pl.BlockSpec((1, bf), lambda i, f: (0, f)),
                  pl.BlockSpec((bf, D), lambda i, f: (f, 0))],
        out_specs=pl.BlockSpec((bm, D), lambda i, f: (i, 0)),
        out_shape=jax.ShapeDtypeStruct((M, D), F32),
        scratch_shapes=[pltpu.VMEM((bm + 2 * HL, D), BF16),
                        pltpu.VMEM((FFN_SPLIT, bm + 2 * HL, bf // FFN_SPLIT), F32),
                        pltpu.VMEM((bm, D), F32)],
        compiler_params=_params("parallel", "arbitrary"),
        name="conv_ffn",
    )(x, x, x, g, wg, wu, cw, cb, wd)


def _ple_kernel(x_ref, p_ref, g_ref, wg_ref, wp_ref, gf_ref, o_ref, *, final_norm):
    x = x_ref[...]
    h = _rms(x, g_ref[...]).astype(BF16)
    gate = _sigmoid(_dot(h, wg_ref[...]))
    y = x + gate * _dot(p_ref[...].astype(BF16), wp_ref[...])
    if final_norm:
        y = _rms(y, gf_ref[...])
    o_ref[...] = y


def ple(x, p, layer, g, wg, wp, gf, bm, final_norm):
    M, D = x.shape
    P = p.shape[-1]
    return pl.pallas_call(
        functools.partial(_ple_kernel, final_norm=final_norm),
        grid=(M // bm,),
        in_specs=[pl.BlockSpec((bm, D), lambda i: (i, 0)),
                  pl.BlockSpec((None, bm, P), lambda i: (layer, i, 0)),
                  pl.BlockSpec((1, D), lambda i: (0, 0)),
                  pl.BlockSpec((D, D), lambda i: (0, 0)),
                  pl.BlockSpec((P, D), lambda i: (0, 0)),
                  pl.BlockSpec((1, D), lambda i: (0, 0))],
        out_specs=pl.BlockSpec((bm, D), lambda i: (i, 0)),
        out_shape=jax.ShapeDtypeStruct((M, D), F32),
        compiler_params=_params("parallel"),
        name="ple_final" if final_norm else "ple",
    )(x, p, g, wg, wp, gf)


def _to_scan_kernel(x0_ref, x1_ref, rev_ref, o_ref, a_ref, *, nb, tb, dup, pitch, apitch):
    N = HEAD_DIM
    NV = N // 2
    lanes = 2 * 2 * nb * RW_HEADS
    n_rows = N if dup else NV
    rev = rev_ref[...]
    for d, ref in enumerate((x0_ref, x1_ref)):
        for b in range(nb):
            x = ref[b]
            if d == 1:
                x = _dot_hi(rev, x)
            xt = x.T
            for h in range(RW_HEADS):
                slot = ((d * nb + b) * RW_HEADS + h) * 2
                for s in range(2):
                    blk = xt[h * N:(h + 1) * N] if dup else xt[h * N + s * NV:h * N + (s + 1) * NV]
                    a_ref[(slot + s) * apitch:(slot + s) * apitch + n_rows, :] = blk
    for j in range(n_rows):
        rows = a_ref[pl.ds(j, lanes, stride=apitch), :]
        o_ref[pl.ds(j, tb, stride=pitch), :] = rows.T
    for j in range(n_rows, pitch):
        o_ref[pl.ds(j, tb, stride=pitch), :] = jnp.zeros((tb, lanes), F32)


def to_scan(x0, x1, rev, B, T, tb, dup, pitch):
    C = RW_WIDTH
    nblk = T // tb
    lanes = 4 * B * RW_HEADS
    apitch = (HEAD_DIM if dup else HEAD_DIM // 2) + 4
    return pl.pallas_call(
        functools.partial(_to_scan_kernel, nb=B, tb=tb, dup=dup, pitch=pitch, apitch=apitch),
        grid=(nblk,),
        in_specs=[pl.BlockSpec((B, tb, C), lambda i: (0, i, 0)),
                  pl.BlockSpec((B, tb, C), lambda i: (0, nblk - 1 - i, 0)),
                  pl.BlockSpec((tb, tb), lambda i: (0, 0))],
        out_specs=pl.BlockSpec((tb * pitch, lanes), lambda i: (i, 0)),
        out_shape=jax.ShapeDtypeStruct((T * pitch, lanes), F32),
        scratch_shapes=[pltpu.VMEM((lanes * apitch, tb), F32)],
        compiler_params=_params("parallel"),
        name="to_scan_keys" if dup else "to_scan_values",
    )(x0.reshape(B, T, C), x1.reshape(B, T, C), rev)


def _from_scan_kernel(y_ref, rev_ref, yf_ref, yb_ref, a_ref, *, nb, tb, pitch, apitch):
    NV = HEAD_DIM // 2
    lanes = 2 * 2 * nb * RW_HEADS
    for ih in range(NV):
        rows = y_ref[pl.ds(ih, tb, stride=pitch), :]
        a_ref[pl.ds(ih, lanes, stride=apitch), :] = rows.T
    rev = rev_ref[...]
    for d, ref in enumerate((yf_ref, yb_ref)):
        for b in range(nb):
            slot0 = (d * nb + b) * RW_HEADS * 2
            pieces = [a_ref[(slot0 + s) * apitch:(slot0 + s) * apitch + NV, :] for s in range(2 * RW_HEADS)]
            y = jnp.concatenate(pieces, axis=0).T
            if d == 1:
                y = _dot_hi(rev, y)
            ref[b] = y


def from_scan(y, rev, B, T, tb, pitch):
    C = RW_WIDTH
    nblk = T // tb
    lanes = y.shape[-1]
    apitch = HEAD_DIM // 2 + 4
    out = jax.ShapeDtypeStruct((B, T, C), F32)
    yf, yb = pl.pallas_call(
        functools.partial(_from_scan_kernel, nb=B, tb=tb, pitch=pitch, apitch=apitch),
        grid=(nblk,),
        in_specs=[pl.BlockSpec((tb * pitch, lanes), lambda i: (i, 0)),
                  pl.BlockSpec((tb, tb), lambda i: (0, 0))],
        out_specs=[pl.BlockSpec((B, tb, C), lambda i: (0, i, 0)),
                   pl.BlockSpec((B, tb, C), lambda i: (0, nblk - 1 - i, 0))],
        out_shape=[out, out],
        scratch_shapes=[pltpu.VMEM((lanes * apitch, tb), F32)],
        compiler_params=_params("parallel"),
        name="from_scan",
    )(y, rev)
    return yf.reshape(B * T, C), yb.reshape(B * T, C)


def _rope_tables(T):
    inv = ROPE_THETA ** (-jnp.arange(0, HEAD_DIM, 2, dtype=jnp.float32) / HEAD_DIM)
    ang = jnp.arange(T, dtype=jnp.float32)[:, None] * inv[None, :]
    cos, sin = jnp.cos(ang), jnp.sin(ang)
    cos_t = jnp.tile(jnp.concatenate([cos, cos], axis=-1), (1, DIL_STEP_WIDTH // HEAD_DIM))
    sin_t = jnp.tile(jnp.concatenate([-sin, sin], axis=-1), (1, DIL_STEP_WIDTH // HEAD_DIM))
    return cos_t, sin_t


def _pick(n, prefs):
    for c in prefs:
        if n % c == 0:
            return c
    return n


def kernel(x, p, norm_mix, w_in, rw_mu, rw_w0, rw_w2, rw_a0, rw_a2, rw_g2, rw_k_k, rw_k_a, rw_r_k, rw_lnx_g,
           rw_lnx_b, na_bias, w_br_a, w_br_b, w_br_c, w_out, norm_ffn, w_ffn_gate, w_ffn_up, ffn_conv_w,
           ffn_conv_b, w_ffn_down, norm_ple, w_ple_gate, w_ple, norm_final):
    B, T, D = x.shape
    depth = w_in.shape[0]
    M = B * T
    F = w_ffn_gate.shape[-1]
    C = RW_WIDTH
    W = DIL_GROUP_WIDTH
    assert (N_BRANCH * D) % W == 0 and w_in.shape[-1] == RW_COLS + 2 * 3 * DIL_WIDTH + N_BRANCH * D

    gates_w = N_BRANCH * D
    dil_w = 3 * DIL_WIDTH
    na_w = 3 * NA_WIDTH
    nb_cols = gates_w + na_w
    nf_cols = RW_COLS_PAD + dil_w
    na_blk0 = gates_w // W
    dil_blk0 = RW_COLS_PAD // W
    assert RW_COLS_PAD % W == 0

    cos_t, sin_t = _rope_tables(T)
    seg = jnp.asarray(np.kron(np.eye(RW_HEADS), np.ones((HEAD_DIM, HEAD_DIM))), BF16)
    xf = x.reshape(M, D)
    p3 = p.reshape(depth, M, p.shape[-1])
    bm_in = _pick(M, (1024, 512, 256, 128))
    bt_rw = _pick(T, (256, 128))
    tb = _pick(T, (64, 32))
    tt = _pick(T, (128,))
    rev = jnp.asarray(np.eye(tt)[::-1], F32)

    for i in range(depth):
        w = w_in[i]
        o_dil, o_na, o_g = RW_COLS, RW_COLS + dil_w, RW_COLS + dil_w + na_w
        w_b = jnp.concatenate([w[:, o_g:], w[:, o_na:o_g]], axis=1).astype(BF16)
        w_f = jnp.concatenate([w[:, :RW_COLS], jnp.zeros((D, RW_COLS_PAD - RW_COLS), w.dtype), w[:, o_dil:o_na]],
                              axis=1).astype(BF16)
        g_mix = norm_mix[i][None]
        zb = norm_matmul(xf, g_mix, w_b, bm_in, _pick(nb_cols, (768, 256, 128)), BF16, "in_proj_bf16")
        zf = norm_matmul(xf, g_mix, w_f, bm_in, _pick(nf_cols, (1024, 256, 128)), F32, "in_proj_f32")
        zb3 = zb.reshape(B, T, nb_cols)
        zf3 = zf.reshape(B, T, nf_cols)

        mu = jnp.pad(rw_mu[i], (0, RW_COLS_PAD - RW_COLS))[None]
        r, v, a, dec0, dec1, kd0, kd1, b0, b1, gate = rw_prep(
            zf, 0, mu, rw_w0[i], rw_w2[i], rw_a0[i], rw_a2[i], rw_g2[i], rw_k_k[i][None], rw_k_a[i][None],
            seg, T, bt_rw)
        keys = lambda x0, x1: to_scan(x0, x1, rev, B, T, tt, dup=True, pitch=KEY_PITCH)
        vals = to_scan(v, v, rev, B, T, tt, dup=False, pitch=VAL_PITCH)
        y = rw_scan(keys(dec0, dec1), keys(kd0, kd1), keys(a, a), keys(b0, b1), keys(r, r), vals, T, tb,
                    KEY_PITCH, VAL_PITCH)
        yf, yb = from_scan(y, rev, B, T, tt, VAL_PITCH)
        ya = rw_post(yf, yb, r, v, kd0, kd1, gate, rw_r_k[i].reshape(1, C), rw_lnx_g[i][None], rw_lnx_b[i][None],
                     seg, bt_rw)

        dil_outs = []
        for g, (window, dil) in enumerate(DIL_PATTERNS):
            dil_outs.append(dilated_group(zf3, dil_blk0 + g, dil_blk0 + 3 + g, dil_blk0 + 6 + g, cos_t, sin_t,
                                          window, dil))

        yc = neighbourhood(zb3, na_blk0, na_blk0 + 3, na_blk0 + 6, na_bias_table(na_bias[i])).reshape(M, NA_WIDTH)

        merged = merge(ya, dil_outs, yc, zb, w_br_a[i].astype(BF16), w_br_b[i].astype(BF16),
                       w_br_c[i].astype(BF16), _pick(M, (512, 256, 128)), _pick(D, (512, 256, 128)))
        xf = matmul_res(merged, w_out[i].astype(BF16), xf, _pick(M, (1024, 512, 256, 128)),
                        _pick(D, (1024, 512, 256, 128)))

        xf = ffn(xf, norm_ffn[i][None], w_ffn_gate[i].astype(BF16), w_ffn_up[i].astype(BF16), ffn_conv_w[i],
                 ffn_conv_b[i][None], w_ffn_down[i].astype(BF16), T, _pick(T, (512, 256, 128)),
                 _pick(F, (512, 256, 128)))

        xf = ple(xf, p3, i, norm_ple[i][None], w_ple_gate[i].astype(BF16), w_ple[i].astype(BF16),
                 norm_final[None], _pick(M, (512, 256, 128)), final_norm=(i == depth - 1))
    return xf.reshape(B, T, D)
```

```python
import functools

import numpy as np
import jax
import jax.numpy as jnp
from jax import lax
from jax.experimental import pallas as pl
from jax.experimental.pallas import tpu as pltpu

HEAD_DIM = 64
HALF_DIM = HEAD_DIM // 2
SCALE = HEAD_DIM ** -0.5
NORM_EPS = 1e-6
ROPE_THETA = 10000.0
NEG_INF = -1e30

RW_HEADS = 8
RW_WIDTH = RW_HEADS * HEAD_DIM
RW_DECAY_LORA = 32
RW_AAA_LORA = 32
RW_GATE_LORA = 96
RW_LNX_EPS = 64e-5
RW_COLS = 3 * RW_WIDTH + 2 * RW_DECAY_LORA + 2 * RW_AAA_LORA + RW_GATE_LORA
RW_COLS_PAD = 1792
KEY_PITCH = HEAD_DIM + 4
RW_PREP_ROWS = 256
VAL_PITCH = HEAD_DIM // 2 + 8
SCAN_UNROLL = 32

DIL_PATTERNS = ((128, 1), (512, 4), (2048, 16))
DIL_HEADS_PER_GROUP = 4
DIL_GROUP_WIDTH = DIL_HEADS_PER_GROUP * HEAD_DIM
DIL_WIDTH = len(DIL_PATTERNS) * DIL_GROUP_WIDTH
DIL_STEP_WIDTH = 2 * HEAD_DIM

GRID_W = 64
NA_HEADS = 12
NA_WIDTH = NA_HEADS * HEAD_DIM
NA_KH = 8
NA_KW = 16
NA_GROUP_HEADS = 4
NA_GROUP_WIDTH = NA_GROUP_HEADS * HEAD_DIM
NA_ROW_UNROLL = 4

N_BRANCH = 3
VMEM_LIMIT = 56 * 1024 * 1024

BF16 = jnp.bfloat16
F32 = jnp.float32
HIGHEST = lax.Precision.HIGHEST


def _params(*sem):
    return pltpu.CompilerParams(dimension_semantics=sem, vmem_limit_bytes=VMEM_LIMIT)


def _rms(x, g):
    ms = jnp.mean(x * x, axis=-1, keepdims=True)
    return x * lax.rsqrt(ms + NORM_EPS) * g


def _sigmoid(x):
    return 0.5 * jnp.tanh(0.5 * x) + 0.5


def _dot(a, b):
    return jnp.dot(a, b, preferred_element_type=F32)


def _dot_nt(a, b):
    return lax.dot_general(a, b, (((1,), (1,)), ((), ())), preferred_element_type=F32)


def _dot_hi(a, b):
    return jnp.dot(a, b, preferred_element_type=F32, precision=HIGHEST)


def _split_hi_lo(w):
    hi = w.astype(BF16)
    return jnp.stack([hi, (w - hi.astype(F32)).astype(BF16)])


def _dot_split(a, w_hi, w_lo):
    a_hi = a.astype(BF16)
    a_lo = (a - a_hi.astype(F32)).astype(BF16)
    return _dot(a_hi, w_hi) + _dot(a_hi, w_lo) + _dot(a_lo, w_hi)


def _seg_sum(x, seg):
    hi = x.astype(BF16)
    rest = x - hi.astype(F32)
    mid = rest.astype(BF16)
    lo = (rest - mid.astype(F32)).astype(BF16)
    return _dot(hi, seg) + _dot(mid, seg) + _dot(lo, seg)


def _norm_matmul_kernel(x_ref, g_ref, w_ref, o_ref, h_ref):
    @pl.when(pl.program_id(1) == 0)
    def _():
        h_ref[...] = _rms(x_ref[...], g_ref[...]).astype(BF16)

    o_ref[...] = _dot(h_ref[...], w_ref[...]).astype(o_ref.dtype)


def norm_matmul(x, g, w, bm, bn, out_dtype, name):
    M, K = x.shape
    N = w.shape[1]
    return pl.pallas_call(
        _norm_matmul_kernel,
        grid=(M // bm, N // bn),
        in_specs=[pl.BlockSpec((bm, K), lambda i, j: (i, 0)),
                  pl.BlockSpec((1, K), lambda i, j: (0, 0)),
                  pl.BlockSpec((K, bn), lambda i, j: (0, j))],
        out_specs=pl.BlockSpec((bm, bn), lambda i, j: (i, j)),
        out_shape=jax.ShapeDtypeStruct((M, N), out_dtype),
        scratch_shapes=[pltpu.VMEM((bm, K), BF16)],
        compiler_params=_params("parallel", "arbitrary"),
        name=name,
    )(x, g, w)


DECAY_SCALE = float(np.exp(-0.5))


def _rw_prep_kernel(c_ref, cp_ref, cn_ref, mu_ref, w0_ref, w2_ref, a0_ref, a2_ref, g2_ref, kk_ref, ka_ref,
                    seg_ref, r_out, v_out, a_out, dec0_out, dec1_out, kd0_out, kd1_out, b0_out, b1_out,
                    gate_out, buf_ref, mix_ref, *, bt, seq):
    i = pl.program_id(0)
    pos = (i * bt) % seq
    prev_ok = (pos != 0).astype(F32)
    next_ok = (pos + bt != seq).astype(F32)
    buf_ref[0:8, :] = cp_ref[...] * prev_ok
    buf_ref[8:bt + 8, :] = c_ref[...]
    buf_ref[bt + 8:bt + 16, :] = cn_ref[...] * next_ok
    mu = mu_ref[...]
    mix_ref[...] = c_ref[...] * (1.0 - mu) + (buf_ref[7:bt + 7, :] + buf_ref[9:bt + 9, :]) * (0.5 * mu)
    C = RW_WIDTH
    rc = RW_PREP_ROWS

    def chunk(ci, carry):
        rows = pl.ds(pl.multiple_of(ci * rc, rc), rc)
        r = mix_ref[rows, 0:C]
        k = mix_ref[rows, C:2 * C]
        o = 3 * C
        wd = mix_ref[rows, o:o + 2 * RW_DECAY_LORA]
        o += 2 * RW_DECAY_LORA
        ad = mix_ref[rows, o:o + 2 * RW_AAA_LORA]
        o += 2 * RW_AAA_LORA
        gd = mix_ref[rows, o:o + RW_GATE_LORA]
        kk = k * kk_ref[...]
        kk = kk * lax.rsqrt(jnp.maximum(_seg_sum(kk * kk, seg_ref[...]), 1e-24))
        r_out[rows, :] = r
        v_out[rows, :] = mix_ref[rows, 2 * C:3 * C]
        a_out[rows, :] = -kk
        gate_out[rows, :] = _dot_split(_sigmoid(gd), g2_ref[0], g2_ref[1])
        twd = jnp.tanh(wd)
        for d, (dec_out, kd_out, b_out) in enumerate(((dec0_out, kd0_out, b0_out), (dec1_out, kd1_out, b1_out))):
            lw = _dot_split(twd[:, d * RW_DECAY_LORA:(d + 1) * RW_DECAY_LORA], w2_ref[0, d], w2_ref[1, d])
            dec_out[rows, :] = jnp.exp(-DECAY_SCALE * _sigmoid(w0_ref[d:d + 1, :] + lw))
            la = _dot_split(ad[:, d * RW_AAA_LORA:(d + 1) * RW_AAA_LORA], a2_ref[0, d], a2_ref[1, d])
            lr = _sigmoid(a0_ref[d:d + 1, :] + la)
            kd_out[rows, :] = k * (1.0 + (lr - 1.0) * ka_ref[...])
            b_out[rows, :] = kk * lr
        return carry

    lax.fori_loop(0, bt // rc, chunk, 0)


def rw_prep(z, rw_col_block, mu, w0, w2, a0, a2, g2, k_k, k_a, seg, seq, bt):
    M = z.shape[0]
    C = RW_WIDTH
    W = RW_COLS_PAD
    nb8 = M // 8
    full = lambda shape: pl.BlockSpec(shape, lambda i: (0,) * len(shape))
    out = jax.ShapeDtypeStruct((M, C), F32)
    return pl.pallas_call(
        functools.partial(_rw_prep_kernel, bt=bt, seq=seq),
        grid=(M // bt,),
        in_specs=[pl.BlockSpec((bt, W), lambda i: (i, rw_col_block)),
                  pl.BlockSpec((8, W), lambda i: (jnp.maximum(i * (bt // 8) - 1, 0), rw_col_block)),
                  pl.BlockSpec((8, W), lambda i: (jnp.minimum((i + 1) * (bt // 8), nb8 - 1), rw_col_block)),
                  full((1, W)), full((2, C)), full((2, 2, RW_DECAY_LORA, C)), full((2, C)),
                  full((2, 2, RW_AAA_LORA, C)), full((2, RW_GATE_LORA, C)), full((1, C)), full((1, C)),
                  full((C, C))],
        out_specs=[pl.BlockSpec((bt, C), lambda i: (i, 0))] * 10,
        out_shape=[out] * 10,
        scratch_shapes=[pltpu.VMEM((bt + 16, W), F32), pltpu.VMEM((bt, W), F32)],
        compiler_params=_params("parallel"),
        name="rw_prep",
    )(z, z, z, mu, w0, _split_hi_lo(w2), a0, _split_hi_lo(a2), _split_hi_lo(g2), k_k, k_a, seg)


def _rw_scan_kernel(w_ref, k_ref, a_ref, b_ref, r_ref, v_ref, y_ref, s_ref, *, tb, pitch, vpitch):
    N = HEAD_DIM
    NV = N // 2

    @pl.when(pl.program_id(0) == 0)
    def _():
        s_ref[...] = jnp.zeros_like(s_ref)

    def step(t, carry):
        base = t * pitch
        vbase = pl.multiple_of(t * vpitch, 8)
        row = lambda ref, j: ref[pl.ds(base + j, 1), :]
        vt = v_ref[pl.ds(vbase, NV), :]
        zero = jnp.zeros_like(vt)

        def dot_a(c, acc):
            acc0, acc1 = acc
            for u in range(SCAN_UNROLL):
                j = c * SCAN_UNROLL + u
                term = s_ref[j] * row(a_ref, j)
                if u % 2 == 0:
                    acc0 = acc0 + term
                else:
                    acc1 = acc1 + term
            return acc0, acc1

        sa0, sa1 = lax.fori_loop(0, N // SCAN_UNROLL, dot_a, (zero, zero))
        sa = sa0 + sa1

        def update(c, acc):
            acc0, acc1 = acc
            for u in range(SCAN_UNROLL):
                j = c * SCAN_UNROLL + u
                s = s_ref[j] * row(w_ref, j) + sa * row(b_ref, j) + vt * row(k_ref, j)
                s_ref[j] = s
                term = s * row(r_ref, j)
                if u % 2 == 0:
                    acc0 = acc0 + term
                else:
                    acc1 = acc1 + term
            return acc0, acc1

        y0, y1 = lax.fori_loop(0, N // SCAN_UNROLL, update, (zero, zero))
        y_ref[pl.ds(vbase, NV), :] = y0 + y1
        y_ref[pl.ds(vbase + NV, vpitch - NV), :] = jnp.zeros((vpitch - NV, zero.shape[1]), F32)
        return carry

    lax.fori_loop(0, tb, step, 0)


def rw_scan(w, k, a, b, r, v, T, tb, pitch, vpitch):
    L = v.shape[-1]
    kspec = pl.BlockSpec((tb * pitch, L), lambda i: (i, 0))
    vspec = pl.BlockSpec((tb * vpitch, L), lambda i: (i, 0))
    return pl.pallas_call(
        functools.partial(_rw_scan_kernel, tb=tb, pitch=pitch, vpitch=vpitch),
        grid=(T // tb,),
        in_specs=[kspec] * 5 + [vspec],
        out_specs=vspec,
        out_shape=jax.ShapeDtypeStruct((T * vpitch, L), F32),
        scratch_shapes=[pltpu.VMEM((HEAD_DIM, HEAD_DIM // 2, L), F32)],
        compiler_params=_params("arbitrary"),
        name="rw_scan",
    )(w, k, a, b, r, v)


def _rw_post_kernel(yf_ref, yb_ref, r_ref, v_ref, kd0_ref, kd1_ref, gate_ref, rk_ref, g_ref, b_ref, seg_ref,
                    o_ref):
    seg = seg_ref[...]
    y = yf_ref[...] + yb_ref[...]
    inv_n = 1.0 / HEAD_DIM
    mean = _seg_sum(y, seg) * inv_n
    d = y - mean
    var = _seg_sum(d * d, seg) * inv_n
    yn = d * lax.rsqrt(var + RW_LNX_EPS) * g_ref[...] + b_ref[...]
    r = r_ref[...]
    rk = rk_ref[...]
    bonus = _seg_sum(r * kd0_ref[...] * rk + r * kd1_ref[...] * rk, seg) * v_ref[...]
    o_ref[...] = (yn + bonus) * gate_ref[...]


def rw_post(yf, yb, r, v, kd0, kd1, gate, rk, g, b, seg, bt):
    M, C = yf.shape
    tok = pl.BlockSpec((bt, C), lambda i: (i, 0))
    row = pl.BlockSpec((1, C), lambda i: (0, 0))
    return pl.pallas_call(
        _rw_post_kernel,
        grid=(M // bt,),
        in_specs=[tok] * 7 + [row] * 3 + [pl.BlockSpec((C, C), lambda i: (0, 0))],
        out_specs=tok,
        out_shape=jax.ShapeDtypeStruct((M, C), F32),
        compiler_params=_params("parallel"),
        name="rw_post",
    )(yf, yb, r, v, kd0, kd1, gate, rk, g, b, seg)


def _dil_kernel(q_ref, k_ref, v_ref, cos_ref, sin_ref, o_ref, lse_ref, qs_ref, ks_ref, vs_ref, *, n_sub, dil, qb,
                kwin, half):
    W = DIL_STEP_WIDTH
    lane = lax.broadcasted_iota(jnp.int32, (1, W), 1)
    first_half = (lane % HEAD_DIM) < HALF_DIM

    def rope(z, c, s):
        swapped = jnp.where(first_half, pltpu.roll(z, W - HALF_DIM, 1), pltpu.roll(z, HALF_DIM, 1))
        return z * c + swapped * s

    rc = min(256, n_sub)

    def subsequence(r, carry):
        def rope_chunk(i, carry):
            rows = pl.ds(r + i * (rc * dil), rc, stride=dil)
            sl = pl.ds(pl.multiple_of(i * rc, rc), rc)
            c = cos_ref[rows, :]
            s = sin_ref[rows, :]
            qs_ref[sl, :] = (rope(q_ref[0, rows, :], c, s) * SCALE).astype(BF16)
            ks_ref[sl, :] = rope(k_ref[0, rows, :], c, s).astype(BF16)
            vs_ref[sl, :] = v_ref[0, rows, :].astype(BF16)
            return carry

        lax.fori_loop(0, n_sub // rc, rope_chunk, 0)

        nblk = n_sub // qb
        unroll = next(u for u in (4, 2, 1) if nblk % u == 0)
        heads = W // HEAD_DIM

        def block_group(ig, carry):
            chains = []
            for u in range(unroll):
                q0 = pl.multiple_of((ig * unroll + u) * qb, qb)
                k0 = pl.multiple_of(jnp.clip(q0 - (kwin - qb) // 2, 0, n_sub - kwin), 64)
                q = qs_ref[pl.ds(q0, qb), :]
                kk = ks_ref[pl.ds(k0, kwin), :]
                jq = q0 + lax.broadcasted_iota(jnp.int32, (qb, kwin), 0)
                jk = k0 + lax.broadcasted_iota(jnp.int32, (qb, kwin), 1)
                ok = jnp.abs(jk - jq) <= half
                for h in range(heads):
                    hs = slice(h * HEAD_DIM, (h + 1) * HEAD_DIM)
                    chains.append((q0, k0, hs, jnp.where(ok, _dot_nt(q[:, hs], kk[:, hs]), NEG_INF)))
            probs = []
            for q0, k0, hs, s in chains:
                m = jnp.max(s, axis=-1, keepdims=True)
                p = jnp.exp(s - m)
                l = jnp.sum(p, axis=-1, keepdims=True)
                probs.append((q0, k0, hs, p.astype(BF16), l, m + jnp.log(l)))
            outs, lses = [], []
            for q0, k0, hs, p, l, lse in probs:
                outs.append(_dot(p, vs_ref[pl.ds(k0, kwin), hs]) / l)
                lses.append(jnp.broadcast_to(lse, (qb, HEAD_DIM)))
                if len(outs) == heads:
                    rows = pl.ds(r + q0 * dil, qb, stride=dil)
                    o_ref[0, rows, :] = jnp.concatenate(outs, axis=-1)
                    lse_ref[0, rows, :] = jnp.concatenate(lses, axis=-1)
                    outs, lses = [], []
            return carry

        lax.fori_loop(0, nblk // unroll, block_group, 0)
        return carry

    lax.fori_loop(0, dil, subsequence, 0)


def dilated_group(z3, q_blk, k_blk, v_blk, cos_t, sin_t, window, dil):
    B, T, ncols = z3.shape
    W = DIL_STEP_WIDTH
    steps = DIL_GROUP_WIDTH // W
    n_sub = T // dil
    half = window // (2 * dil)
    qb = min(128, n_sub)
    kwin = min(qb + 2 * half, n_sub)
    zspec = lambda blk: pl.BlockSpec((1, T, W), lambda b, s: (b, 0, blk * steps + s))
    tspec = pl.BlockSpec((T, W), lambda b, s: (0, 0))
    ospec = pl.BlockSpec((1, T, W), lambda b, s: (b, 0, s))
    oshape = jax.ShapeDtypeStruct((B, T, DIL_GROUP_WIDTH), F32)
    o, lse = pl.pallas_call(
        functools.partial(_dil_kernel, n_sub=n_sub, dil=dil, qb=qb, kwin=kwin, half=half),
        grid=(B, steps),
        in_specs=[zspec(q_blk), zspec(k_blk), zspec(v_blk), tspec, tspec],
        out_specs=[ospec, ospec],
        out_shape=[oshape, oshape],
        scratch_shapes=[pltpu.VMEM((n_sub, W), BF16)] * 3,
        compiler_params=_params("parallel", "parallel"),
        name=f"dilated_d{dil}",
    )(z3, z3, z3, cos_t, sin_t)
    return o.reshape(B * T, DIL_GROUP_WIDTH), lse.reshape(B * T, DIL_GROUP_WIDTH)


def _na_kernel(q_ref, k_ref, v_ref, tab_ref, o_ref, qs_ref, ks_ref, vs_ref, bias_ref, *, rows):
    nkeys = NA_KH * GRID_W
    pc = NA_KH * GRID_W

    for off in range(NA_KH):
        for h in range(NA_GROUP_HEADS):
            for ky in range(NA_KH):
                bias_ref[off, h, :, ky * GRID_W:(ky + 1) * GRID_W] = tab_ref[h, ky - off + NA_KH - 1]

    def prep(c, carry):
        sl = pl.ds(pl.multiple_of(c * pc, pc), pc)
        q = q_ref[0, sl, :] * SCALE
        k = k_ref[0, sl, :]
        v = v_ref[0, sl, :]
        for h in range(NA_GROUP_HEADS):
            hs = slice(h * HEAD_DIM, (h + 1) * HEAD_DIM)
            qs_ref[h, sl, :] = q[:, hs].astype(BF16)
            ks_ref[h, sl, :] = k[:, hs].astype(BF16)
            vs_ref[h, sl, :] = v[:, hs].astype(BF16)
        return carry

    lax.fori_loop(0, rows * GRID_W // pc, prep, 0)

    def row_group(rg, carry):
        chains = []
        for u in range(NA_ROW_UNROLL):
            r = rg * NA_ROW_UNROLL + u
            r0 = jnp.clip(r - NA_KH // 2, 0, rows - NA_KH)
            qrow = pl.ds(pl.multiple_of(r * GRID_W, GRID_W), GRID_W)
            krow = pl.ds(pl.multiple_of(r0 * GRID_W, GRID_W), nkeys)
            for h in range(NA_GROUP_HEADS):
                s = _dot_nt(qs_ref[h, qrow, :], ks_ref[h, krow, :]) + bias_ref[r - r0, h]
                chains.append((u, h, qrow, krow, s))
        probs = []
        for u, h, qrow, krow, s in chains:
            m = jnp.max(s, axis=-1, keepdims=True)
            p = jnp.exp(s - m)
            l = jnp.sum(p, axis=-1, keepdims=True)
            probs.append((u, h, qrow, krow, p.astype(BF16), l))
        outs = {}
        for u, h, qrow, krow, p, l in probs:
            outs.setdefault(u, []).append(_dot(p, vs_ref[h, krow, :]) / l)
            if h == NA_GROUP_HEADS - 1:
                o_ref[0, qrow, :] = jnp.concatenate(outs[u], axis=-1)
        return carry

    lax.fori_loop(0, rows // NA_ROW_UNROLL, row_group, 0)


def na_bias_table(rel_bias):
    qc = np.arange(GRID_W)
    kc = np.arange(GRID_W)
    wc0 = np.clip(qc - NA_KW // 2, 0, GRID_W - NA_KW)
    col_ok = (kc[None, :] >= wc0[:, None]) & (kc[None, :] < wc0[:, None] + NA_KW)
    dx_idx = np.clip(kc[None, :] - qc[:, None], 1 - NA_KW, NA_KW - 1) + NA_KW - 1
    tab = rel_bias[:, :, dx_idx]
    return jnp.where(jnp.asarray(col_ok)[None, None], tab, NEG_INF).astype(F32)


def neighbourhood(z3, q_blk0, k_blk0, v_blk0, bias_tab):
    B, T, ncols = z3.shape
    rows = T // GRID_W
    assert rows >= NA_KH
    W = NA_GROUP_WIDTH
    G = NA_HEADS // NA_GROUP_HEADS
    zspec = lambda blk0: pl.BlockSpec((1, T, W), lambda b, g: (b, 0, blk0 + g))
    return pl.pallas_call(
        functools.partial(_na_kernel, rows=rows),
        grid=(B, G),
        in_specs=[zspec(q_blk0), zspec(k_blk0), zspec(v_blk0),
                  pl.BlockSpec((NA_GROUP_HEADS, 2 * NA_KH - 1, GRID_W, GRID_W), lambda b, g: (g, 0, 0, 0))],
        out_specs=pl.BlockSpec((1, T, W), lambda b, g: (b, 0, g)),
        out_shape=jax.ShapeDtypeStruct((B, T, NA_WIDTH), F32),
        scratch_shapes=[pltpu.VMEM((NA_GROUP_HEADS, T, HEAD_DIM), BF16)] * 3
                       + [pltpu.VMEM((NA_KH, NA_GROUP_HEADS, GRID_W, NA_KH * GRID_W), F32)],
        compiler_params=_params("parallel", "parallel"),
        name="neighbourhood",
    )(z3, z3, z3, bias_tab)


def _merge_kernel(ya_ref, o1_ref, o2_ref, o3_ref, l1_ref, l2_ref, l3_ref, yc_ref, ga_ref, gb_ref, gc_ref,
                  wa_ref, wb_ref, wc_ref, out_ref, yb_ref):
    @pl.when(pl.program_id(1) == 0)
    def _():
        l1, l2, l3 = l1_ref[...], l2_ref[...], l3_ref[...]
        m = jnp.maximum(jnp.maximum(l1, l2), l3)
        e1, e2, e3 = jnp.exp(l1 - m), jnp.exp(l2 - m), jnp.exp(l3 - m)
        yb = (e1 * o1_ref[...] + e2 * o2_ref[...] + e3 * o3_ref[...]) / (e1 + e2 + e3)
        yb_ref[...] = yb.astype(BF16)

    pa = _dot(ya_ref[...].astype(BF16), wa_ref[...])
    pb = _dot(yb_ref[...], wb_ref[...])
    pc = _dot(yc_ref[...].astype(BF16), wc_ref[...])
    gate = lambda ref: _sigmoid(ref[...].astype(F32))
    out = gate(ga_ref) * pa + gate(gb_ref) * pb + gate(gc_ref) * pc
    out_ref[...] = out.astype(BF16)


def merge(ya, dil_outs, yc, z, layer, wa, wb, wc, bm, bn):
    M = ya.shape[0]
    D = wa.shape[-1]
    nj = D // bn
    tok = lambda w: pl.BlockSpec((bm, w), lambda i, j: (i, 0))
    gate = lambda g: pl.BlockSpec((bm, bn), lambda i, j: (i, g * nj + j))
    wsp = lambda k: pl.BlockSpec((None, k, bn), lambda i, j: (layer, 0, j))
    (o1, l1), (o2, l2), (o3, l3) = dil_outs
    W = DIL_GROUP_WIDTH
    return pl.pallas_call(
        _merge_kernel,
        grid=(M // bm, nj),
        in_specs=[tok(RW_WIDTH)] + [tok(W)] * 6 + [tok(NA_WIDTH), gate(0), gate(1), gate(2),
                                                  wsp(RW_WIDTH), wsp(W), wsp(NA_WIDTH)],
        out_specs=pl.BlockSpec((bm, bn), lambda i, j: (i, j)),
        out_shape=jax.ShapeDtypeStruct((M, D), BF16),
        scratch_shapes=[pltpu.VMEM((bm, W), BF16)],
        compiler_params=_params("parallel", "arbitrary"),
        name="merge",
    )(ya, o1, o2, o3, l1, l2, l3, yc, z, z, z, wa, wb, wc)


def _matmul_res_kernel(a_ref, w_ref, x_ref, o_ref):
    o_ref[...] = x_ref[...] + _dot(a_ref[...], w_ref[...])


def matmul_res(a, w, layer, x, bm, bn):
    M, K = a.shape
    N = w.shape[-1]
    return pl.pallas_call(
        _matmul_res_kernel,
        grid=(M // bm, N // bn),
        in_specs=[pl.BlockSpec((bm, K), lambda i, j: (i, 0)),
                  pl.BlockSpec((None, K, bn), lambda i, j: (layer, 0, j)),
                  pl.BlockSpec((bm, bn), lambda i, j: (i, j))],
        out_specs=pl.BlockSpec((bm, bn), lambda i, j: (i, j)),
        out_shape=jax.ShapeDtypeStruct((M, N), F32),
        compiler_params=_params("parallel", "parallel"),
        name="out_proj",
    )(a, w, x)


def _gelu(x):
    return 0.5 * x * (1.0 + jnp.tanh(np.sqrt(2.0 / np.pi).astype(np.float32) * (x + 0.044715 * (x * x * x))))


FFN_HALO = 16
FFN_SPLIT = 2


def _ffn_kernel(x_ref, xp_ref, xn_ref, g_ref, wg_ref, wu_ref, cw_ref, cb_ref, wd_ref, o_ref,
                h_ref, gs_ref, acc_ref, *, bm, bf, seq):
    i = pl.program_id(0)
    f = pl.program_id(1)
    HL = FFN_HALO

    @pl.when(f == 0)
    def _():
        g = g_ref[...]
        h_ref[0:HL, :] = _rms(xp_ref[...], g).astype(BF16)
        h_ref[HL:bm + HL, :] = _rms(x_ref[...], g).astype(BF16)
        h_ref[bm + HL:bm + 2 * HL, :] = _rms(xn_ref[...], g).astype(BF16)
        acc_ref[...] = jnp.zeros_like(acc_ref)

    pos = (i * bm) % seq
    prev_ok = (pos != 0).astype(F32)
    next_ok = (pos + bm != seq).astype(F32)
    hw = bf // FFN_SPLIT
    h_ext = h_ref[...]
    h = h_ref[HL:bm + HL, :]
    cw = cw_ref[...]
    cb = cb_ref[...]
    gates, ups = [], []
    for s in range(FFN_SPLIT):
        cs = slice(s * hw, (s + 1) * hw)
        gates.append(_dot(h_ext, wg_ref[:, cs]))
        ups.append(_dot(h, wu_ref[:, cs]))
    acts = []
    for s in range(FFN_SPLIT):
        cs = slice(s * hw, (s + 1) * hw)
        gs_ref[s] = gates[s]
        gs_ref[s, HL - 8:HL, :] = gs_ref[s, HL - 8:HL, :] * prev_ok
        gs_ref[s, bm + HL:bm + HL + 8, :] = gs_ref[s, bm + HL:bm + HL + 8, :] * next_ok
        gc = (gs_ref[s, HL - 1:bm + HL - 1, :] * cw[0:1, cs] + gs_ref[s, HL:bm + HL, :] * cw[1:2, cs]
              + gs_ref[s, HL + 1:bm + HL + 1, :] * cw[2:3, cs] + cb[:, cs])
        acts.append((_gelu(gc) * ups[s]).astype(BF16))
    upd = _dot(acts[0], wd_ref[0:hw, :])
    for s in range(1, FFN_SPLIT):
        upd = upd + _dot(acts[s], wd_ref[s * hw:(s + 1) * hw, :])
    acc_ref[...] += upd

    @pl.when(f == pl.num_programs(1) - 1)
    def _():
        o_ref[...] = x_ref[...] + acc_ref[...]


def ffn(x, g, layer, wg, wu, cw, cb, wd, seq, bm, bf):
    M, D = x.shape
    F = wg.shape[-1]
    HL = FFN_HALO
    nbh = M // HL
    return pl.pallas_call(
        functools.partial(_ffn_kernel, bm=bm, bf=bf, seq=seq),
        grid=(M // bm, F // bf),
        in_specs=[pl.BlockSpec((bm, D), lambda i, f: (i, 0)),
                  pl.BlockSpec((HL, D), lambda i, f: (jnp.maximum(i * (bm // HL) - 1, 0), 0)),
                  pl.BlockSpec((HL, D), lambda i, f: (jnp.minimum((i + 1) * (bm // HL), nbh - 1), 0)),
                  pl.BlockSpec((1, D), lambda i, f: (0, 0)),
                  pl.BlockSpec((None, D, bf), lambda i, f: (layer, 0, f)),
                  pl.BlockSpec((None, D, bf), lambda i, f: (layer, 0, f)),
                  pl.BlockSpec((3, bf), lambda i, f: (0, f)),
                  pl.BlockSpec((1, bf), lambda i, f: (0, f)),
                  pl.BlockSpec((None, bf, D), lambda i, f: (layer, f, 0))],
        out_specs=pl.BlockSpec((bm, D), lambda i, f: (i, 0)),
        out_shape=jax.ShapeDtypeStruct((M, D), F32),
        scratch_shapes=[pltpu.VMEM((bm + 2 * HL, D), BF16),
                        pltpu.VMEM((FFN_SPLIT, bm + 2 * HL, bf // FFN_SPLIT), F32),
                        pltpu.VMEM((bm, D), F32)],
        compiler_params=_params("parallel", "arbitrary"),
        name="conv_ffn",
    )(x, x, x, g, wg, wu, cw, cb, wd)


def _ple_kernel(x_ref, p_ref, g_ref, wg_ref, wp_ref, gf_ref, o_ref, *, final_norm):
    x = x_ref[...]
    h = _rms(x, g_ref[...]).astype(BF16)
    gate = _sigmoid(_dot(h, wg_ref[...]))
    y = x + gate * _dot(p_ref[...].astype(BF16), wp_ref[...])
    if final_norm:
        y = _rms(y, gf_ref[...])
    o_ref[...] = y


def ple(x, p, layer, g, wg, wp, gf, bm, final_norm):
    M, D = x.shape
    P = p.shape[-1]
    return pl.pallas_call(
        functools.partial(_ple_kernel, final_norm=final_norm),
        grid=(M // bm,),
        in_specs=[pl.BlockSpec((bm, D), lambda i: (i, 0)),
                  pl.BlockSpec((None, bm, P), lambda i: (layer, i, 0)),
                  pl.BlockSpec((1, D), lambda i: (0, 0)),
                  pl.BlockSpec((None, D, D), lambda i: (layer, 0, 0)),
                  pl.BlockSpec((None, P, D), lambda i: (layer, 0, 0)),
                  pl.BlockSpec((1, D), lambda i: (0, 0))],
        out_specs=pl.BlockSpec((bm, D), lambda i: (i, 0)),
        out_shape=jax.ShapeDtypeStruct((M, D), F32),
        compiler_params=_params("parallel"),
        name="ple_final" if final_norm else "ple",
    )(x, p, g, wg, wp, gf)


def _to_scan_kernel(x0_ref, x1_ref, rev_ref, o_ref, a_ref, *, nb, tb, dup, pitch, apitch):
    N = HEAD_DIM
    NV = N // 2
    lanes = 2 * 2 * nb * RW_HEADS
    n_rows = N if dup else NV
    rev = rev_ref[...]
    for d, ref in enumerate((x0_ref, x1_ref)):
        for b in range(nb):
            x = ref[b]
            if d == 1:
                x = _dot_hi(rev, x)
            xt = x.T
            for h in range(RW_HEADS):
                slot = ((d * nb + b) * RW_HEADS + h) * 2
                for s in range(2):
                    blk = xt[h * N:(h + 1) * N] if dup else xt[h * N + s * NV:h * N + (s + 1) * NV]
                    a_ref[(slot + s) * apitch:(slot + s) * apitch + n_rows, :] = blk
    for j in range(n_rows):
        rows = a_ref[pl.ds(j, lanes, stride=apitch), :]
        o_ref[pl.ds(j, tb, stride=pitch), :] = rows.T
    for j in range(n_rows, pitch):
        o_ref[pl.ds(j, tb, stride=pitch), :] = jnp.zeros((tb, lanes), F32)


def to_scan(x0, x1, rev, B, T, tb, dup, pitch):
    C = RW_WIDTH
    nblk = T // tb
    lanes = 4 * B * RW_HEADS
    apitch = (HEAD_DIM if dup else HEAD_DIM // 2) + 4
    return pl.pallas_call(
        functools.partial(_to_scan_kernel, nb=B, tb=tb, dup=dup, pitch=pitch, apitch=apitch),
        grid=(nblk,),
        in_specs=[pl.BlockSpec((B, tb, C), lambda i: (0, i, 0)),
                  pl.BlockSpec((B, tb, C), lambda i: (0, nblk - 1 - i, 0)),
                  pl.BlockSpec((tb, tb), lambda i: (0, 0))],
        out_specs=pl.BlockSpec((tb * pitch, lanes), lambda i: (i, 0)),
        out_shape=jax.ShapeDtypeStruct((T * pitch, lanes), F32),
        scratch_shapes=[pltpu.VMEM((lanes * apitch, tb), F32)],
        compiler_params=_params("parallel"),
        name="to_scan_keys" if dup else "to_scan_values",
    )(x0.reshape(B, T, C), x1.reshape(B, T, C), rev)


def _from_scan_kernel(y_ref, rev_ref, yf_ref, yb_ref, a_ref, *, nb, tb, pitch, apitch):
    NV = HEAD_DIM // 2
    lanes = 2 * 2 * nb * RW_HEADS
    for ih in range(NV):
        rows = y_ref[pl.ds(ih, tb, stride=pitch), :]
        a_ref[pl.ds(ih, lanes, stride=apitch), :] = rows.T
    rev = rev_ref[...]
    for d, ref in enumerate((yf_ref, yb_ref)):
        for b in range(nb):
            slot0 = (d * nb + b) * RW_HEADS * 2
            pieces = [a_ref[(slot0 + s) * apitch:(slot0 + s) * apitch + NV, :] for s in range(2 * RW_HEADS)]
            y = jnp.concatenate(pieces, axis=0).T
            if d == 1:
                y = _dot_hi(rev, y)
            ref[b] = y


def from_scan(y, rev, B, T, tb, pitch):
    C = RW_WIDTH
    nblk = T // tb
    lanes = y.shape[-1]
    apitch = HEAD_DIM // 2 + 4
    out = jax.ShapeDtypeStruct((B, T, C), F32)
    yf, yb = pl.pallas_call(
        functools.partial(_from_scan_kernel, nb=B, tb=tb, pitch=pitch, apitch=apitch),
        grid=(nblk,),
        in_specs=[pl.BlockSpec((tb * pitch, lanes), lambda i: (i, 0)),
                  pl.BlockSpec((tb, tb), lambda i: (0, 0))],
        out_specs=[pl.BlockSpec((B, tb, C), lambda i: (0, i, 0)),
                   pl.BlockSpec((B, tb, C), lambda i: (0, nblk - 1 - i, 0))],
        out_shape=[out, out],
        scratch_shapes=[pltpu.VMEM((lanes * apitch, tb), F32)],
        compiler_params=_params("parallel"),
        name="from_scan",
    )(y, rev)
    return yf.reshape(B * T, C), yb.reshape(B * T, C)


def _rope_tables(T):
    inv = ROPE_THETA ** (-jnp.arange(0, HEAD_DIM, 2, dtype=jnp.float32) / HEAD_DIM)
    ang = jnp.arange(T, dtype=jnp.float32)[:, None] * inv[None, :]
    cos, sin = jnp.cos(ang), jnp.sin(ang)
    cos_t = jnp.tile(jnp.concatenate([cos, cos], axis=-1), (1, DIL_STEP_WIDTH // HEAD_DIM))
    sin_t = jnp.tile(jnp.concatenate([-sin, sin], axis=-1), (1, DIL_STEP_WIDTH // HEAD_DIM))
    return cos_t, sin_t


def _pick(n, prefs):
    for c in prefs:
        if n % c == 0:
            return c
    return n


def kernel(x, p, norm_mix, w_in, rw_mu, rw_w0, rw_w2, rw_a0, rw_a2, rw_g2, rw_k_k, rw_k_a, rw_r_k, rw_lnx_g,
           rw_lnx_b, na_bias, w_br_a, w_br_b, w_br_c, w_out, norm_ffn, w_ffn_gate, w_ffn_up, ffn_conv_w,
           ffn_conv_b, w_ffn_down, norm_ple, w_ple_gate, w_ple, norm_final):
    B, T, D = x.shape
    depth = w_in.shape[0]
    M = B * T
    F = w_ffn_gate.shape[-1]
    C = RW_WIDTH
    W = DIL_GROUP_WIDTH
    assert (N_BRANCH * D) % W == 0 and w_in.shape[-1] == RW_COLS + 2 * 3 * DIL_WIDTH + N_BRANCH * D

    gates_w = N_BRANCH * D
    dil_w = 3 * DIL_WIDTH
    na_w = 3 * NA_WIDTH
    nb_cols = gates_w + na_w
    nf_cols = RW_COLS_PAD + dil_w
    na_blk0 = gates_w // W
    dil_blk0 = RW_COLS_PAD // W
    assert RW_COLS_PAD % W == 0

    cos_t, sin_t = _rope_tables(T)
    seg = jnp.asarray(np.kron(np.eye(RW_HEADS), np.ones((HEAD_DIM, HEAD_DIM))), BF16)
    xf = x.reshape(M, D)
    p3 = p.reshape(depth, M, p.shape[-1])
    bm_in = _pick(M, (1024, 512, 256, 128))
    bt_rw = _pick(T, (256, 128))
    tb = _pick(T, (64, 32))
    tt = _pick(T, (128,))
    rev = jnp.asarray(np.eye(tt)[::-1], F32)

    wb16 = {name: wt.astype(BF16) for name, wt in (
        ("br_a", w_br_a), ("br_b", w_br_b), ("br_c", w_br_c), ("out", w_out), ("ffn_gate", w_ffn_gate),
        ("ffn_up", w_ffn_up), ("ffn_down", w_ffn_down), ("ple_gate", w_ple_gate), ("ple", w_ple))}
    w_in16 = w_in.astype(BF16)

    for i in range(depth):
        w = w_in16[i]
        o_dil, o_na, o_g = RW_COLS, RW_COLS + dil_w, RW_COLS + dil_w + na_w
        w_b = jnp.concatenate([w[:, o_g:], w[:, o_na:o_g]], axis=1)
        w_f = jnp.concatenate([w[:, :RW_COLS], jnp.zeros((D, RW_COLS_PAD - RW_COLS), w.dtype), w[:, o_dil:o_na]],
                              axis=1)
        g_mix = norm_mix[i][None]
        zb = norm_matmul(xf, g_mix, w_b, bm_in, _pick(nb_cols, (768, 256, 128)), BF16, "in_proj_bf16")
        zf = norm_matmul(xf, g_mix, w_f, bm_in, _pick(nf_cols, (1024, 256, 128)), F32, "in_proj_f32")
        zb3 = zb.reshape(B, T, nb_cols)
        zf3 = zf.reshape(B, T, nf_cols)

        mu = jnp.pad(rw_mu[i], (0, RW_COLS_PAD - RW_COLS))[None]
        r, v, a, dec0, dec1, kd0, kd1, b0, b1, gate = rw_prep(
            zf, 0, mu, rw_w0[i], rw_w2[i], rw_a0[i], rw_a2[i], rw_g2[i], rw_k_k[i][None], rw_k_a[i][None],
            seg, T, bt_rw)
        keys = lambda x0, x1: to_scan(x0, x1, rev, B, T, tt, dup=True, pitch=KEY_PITCH)
        vals = to_scan(v, v, rev, B, T, tt, dup=False, pitch=VAL_PITCH)
        y = rw_scan(keys(dec0, dec1), keys(kd0, kd1), keys(a, a), keys(b0, b1), keys(r, r), vals, T, tb,
                    KEY_PITCH, VAL_PITCH)
        yf, yb = from_scan(y, rev, B, T, tt, VAL_PITCH)
        ya = rw_post(yf, yb, r, v, kd0, kd1, gate, rw_r_k[i].reshape(1, C), rw_lnx_g[i][None], rw_lnx_b[i][None],
                     seg, bt_rw)

        dil_outs = []
        for g, (window, dil) in enumerate(DIL_PATTERNS):
            dil_outs.append(dilated_group(zf3, dil_blk0 + g, dil_blk0 + 3 + g, dil_blk0 + 6 + g, cos_t, sin_t,
                                          window, dil))

        yc = neighbourhood(zb3, na_blk0, na_blk0 + 3, na_blk0 + 6, na_bias_table(na_bias[i])).reshape(M, NA_WIDTH)

        merged = merge(ya, dil_outs, yc, zb, i, wb16["br_a"], wb16["br_b"], wb16["br_c"],
                       _pick(M, (512, 256, 128)), _pick(D, (512, 256, 128)))
        xf = matmul_res(merged, wb16["out"], i, xf, _pick(M, (1024, 512, 256, 128)),
                        _pick(D, (1024, 512, 256, 128)))

        xf = ffn(xf, norm_ffn[i][None], i, wb16["ffn_gate"], wb16["ffn_up"], ffn_conv_w[i],
                 ffn_conv_b[i][None], wb16["ffn_down"], T, _pick(T, (512, 256, 128)),
                 _pick(F, (512, 256, 128)))

        xf = ple(xf, p3, i, norm_ple[i][None], wb16["ple_gate"], wb16["ple"],
                 norm_final[None], _pick(M, (512, 256, 128)), final_norm=(i == depth - 1))
    return xf.reshape(B, T, D)
```

```python
import functools

import numpy as np
import jax
import jax.numpy as jnp
from jax import lax
from jax.experimental import pallas as pl
from jax.experimental.pallas import tpu as pltpu

HEAD_DIM = 64
HALF_DIM = HEAD_DIM // 2
SCALE = HEAD_DIM ** -0.5
NORM_EPS = 1e-6
ROPE_THETA = 10000.0
NEG_INF = -1e30

RW_HEADS = 8
RW_WIDTH = RW_HEADS * HEAD_DIM
RW_DECAY_LORA = 32
RW_AAA_LORA = 32
RW_GATE_LORA = 96
RW_LNX_EPS = 64e-5
RW_COLS = 3 * RW_WIDTH + 2 * RW_DECAY_LORA + 2 * RW_AAA_LORA + RW_GATE_LORA
RW_COLS_PAD = 1792
KEY_PITCH = HEAD_DIM + 4
RW_PREP_ROWS = 256
VAL_PITCH = HEAD_DIM // 2 + 8
SCAN_UNROLL = 32

DIL_PATTERNS = ((128, 1), (512, 4), (2048, 16))
DIL_HEADS_PER_GROUP = 4
DIL_GROUP_WIDTH = DIL_HEADS_PER_GROUP * HEAD_DIM
DIL_WIDTH = len(DIL_PATTERNS) * DIL_GROUP_WIDTH
DIL_STEP_WIDTH = 2 * HEAD_DIM

GRID_W = 64
NA_HEADS = 12
NA_WIDTH = NA_HEADS * HEAD_DIM
NA_KH = 8
NA_KW = 16
NA_GROUP_HEADS = 4
NA_GROUP_WIDTH = NA_GROUP_HEADS * HEAD_DIM
NA_ROW_UNROLL = 4

N_BRANCH = 3
VMEM_LIMIT = 56 * 1024 * 1024

BF16 = jnp.bfloat16
F32 = jnp.float32
HIGHEST = lax.Precision.HIGHEST


def _params(*sem):
    return pltpu.CompilerParams(dimension_semantics=sem, vmem_limit_bytes=VMEM_LIMIT)


def _rms(x, g):
    ms = jnp.mean(x * x, axis=-1, keepdims=True)
    return x * lax.rsqrt(ms + NORM_EPS) * g


def _sigmoid(x):
    return 0.5 * jnp.tanh(0.5 * x) + 0.5


def _dot(a, b):
    return jnp.dot(a, b, preferred_element_type=F32)


def _dot_nt(a, b):
    return lax.dot_general(a, b, (((1,), (1,)), ((), ())), preferred_element_type=F32)


def _dot_hi(a, b):
    return jnp.dot(a, b, preferred_element_type=F32, precision=HIGHEST)


def _split_hi_lo(w):
    hi = w.astype(BF16)
    return jnp.stack([hi, (w - hi.astype(F32)).astype(BF16)])


def _dot_split(a, w_hi, w_lo):
    a_hi = a.astype(BF16)
    a_lo = (a - a_hi.astype(F32)).astype(BF16)
    return _dot(a_hi, w_hi) + _dot(a_hi, w_lo) + _dot(a_lo, w_hi)


def _seg_sum(x, seg):
    hi = x.astype(BF16)
    rest = x - hi.astype(F32)
    mid = rest.astype(BF16)
    lo = (rest - mid.astype(F32)).astype(BF16)
    return _dot(hi, seg) + _dot(mid, seg) + _dot(lo, seg)


def _norm_matmul_kernel(x_ref, g_ref, w_ref, o_ref, h_ref):
    @pl.when(pl.program_id(1) == 0)
    def _():
        h_ref[...] = _rms(x_ref[...], g_ref[...]).astype(BF16)

    o_ref[...] = _dot(h_ref[...], w_ref[...]).astype(o_ref.dtype)


def norm_matmul(x, g, w, layer, bm, bn, out_dtype, name):
    M, K = x.shape
    N = w.shape[-1]
    return pl.pallas_call(
        _norm_matmul_kernel,
        grid=(M // bm, N // bn),
        in_specs=[pl.BlockSpec((bm, K), lambda i, j: (i, 0)),
                  pl.BlockSpec((1, K), lambda i, j: (0, 0)),
                  pl.BlockSpec((None, K, bn), lambda i, j: (layer, 0, j))],
        out_specs=pl.BlockSpec((bm, bn), lambda i, j: (i, j)),
        out_shape=jax.ShapeDtypeStruct((M, N), out_dtype),
        scratch_shapes=[pltpu.VMEM((bm, K), BF16)],
        compiler_params=_params("parallel", "arbitrary"),
        name=name,
    )(x, g, w)


DECAY_SCALE = float(np.exp(-0.5))


def _rw_prep_kernel(c_ref, cp_ref, cn_ref, mu_ref, w0_ref, w2_ref, a0_ref, a2_ref, g2_ref, kk_ref, ka_ref,
                    seg_ref, r_out, v_out, a_out, dec0_out, dec1_out, kd0_out, kd1_out, b0_out, b1_out,
                    gate_out, buf_ref, mix_ref, *, bt, seq):
    i = pl.program_id(0)
    pos = (i * bt) % seq
    prev_ok = (pos != 0).astype(F32)
    next_ok = (pos + bt != seq).astype(F32)
    buf_ref[0:8, :] = cp_ref[...] * prev_ok
    buf_ref[8:bt + 8, :] = c_ref[...]
    buf_ref[bt + 8:bt + 16, :] = cn_ref[...] * next_ok
    mu = mu_ref[...]
    mix_ref[...] = c_ref[...] * (1.0 - mu) + (buf_ref[7:bt + 7, :] + buf_ref[9:bt + 9, :]) * (0.5 * mu)
    C = RW_WIDTH
    rc = RW_PREP_ROWS

    def chunk(ci, carry):
        rows = pl.ds(pl.multiple_of(ci * rc, rc), rc)
        r = mix_ref[rows, 0:C]
        k = mix_ref[rows, C:2 * C]
        o = 3 * C
        wd = mix_ref[rows, o:o + 2 * RW_DECAY_LORA]
        o += 2 * RW_DECAY_LORA
        ad = mix_ref[rows, o:o + 2 * RW_AAA_LORA]
        o += 2 * RW_AAA_LORA
        gd = mix_ref[rows, o:o + RW_GATE_LORA]
        kk = k * kk_ref[...]
        kk = kk * lax.rsqrt(jnp.maximum(_seg_sum(kk * kk, seg_ref[...]), 1e-24))
        r_out[rows, :] = r
        v_out[rows, :] = mix_ref[rows, 2 * C:3 * C]
        a_out[rows, :] = -kk
        gate_out[rows, :] = _dot_split(_sigmoid(gd), g2_ref[0], g2_ref[1])
        twd = jnp.tanh(wd)
        for d, (dec_out, kd_out, b_out) in enumerate(((dec0_out, kd0_out, b0_out), (dec1_out, kd1_out, b1_out))):
            lw = _dot_split(twd[:, d * RW_DECAY_LORA:(d + 1) * RW_DECAY_LORA], w2_ref[0, d], w2_ref[1, d])
            dec_out[rows, :] = jnp.exp(-DECAY_SCALE * _sigmoid(w0_ref[d:d + 1, :] + lw))
            la = _dot_split(ad[:, d * RW_AAA_LORA:(d + 1) * RW_AAA_LORA], a2_ref[0, d], a2_ref[1, d])
            lr = _sigmoid(a0_ref[d:d + 1, :] + la)
            kd_out[rows, :] = k * (1.0 + (lr - 1.0) * ka_ref[...])
            b_out[rows, :] = kk * lr
        return carry

    lax.fori_loop(0, bt // rc, chunk, 0)


def rw_prep(z, rw_col_block, mu, w0, w2, a0, a2, g2, k_k, k_a, seg, seq, bt):
    M = z.shape[0]
    C = RW_WIDTH
    W = RW_COLS_PAD
    nb8 = M // 8
    full = lambda shape: pl.BlockSpec(shape, lambda i: (0,) * len(shape))
    out = jax.ShapeDtypeStruct((M, C), F32)
    return pl.pallas_call(
        functools.partial(_rw_prep_kernel, bt=bt, seq=seq),
        grid=(M // bt,),
        in_specs=[pl.BlockSpec((bt, W), lambda i: (i, rw_col_block)),
                  pl.BlockSpec((8, W), lambda i: (jnp.maximum(i * (bt // 8) - 1, 0), rw_col_block)),
                  pl.BlockSpec((8, W), lambda i: (jnp.minimum((i + 1) * (bt // 8), nb8 - 1), rw_col_block)),
                  full((1, W)), full((2, C)), full((2, 2, RW_DECAY_LORA, C)), full((2, C)),
                  full((2, 2, RW_AAA_LORA, C)), full((2, RW_GATE_LORA, C)), full((1, C)), full((1, C)),
                  full((C, C))],
        out_specs=[pl.BlockSpec((bt, C), lambda i: (i, 0))] * 10,
        out_shape=[out] * 10,
        scratch_shapes=[pltpu.VMEM((bt + 16, W), F32), pltpu.VMEM((bt, W), F32)],
        compiler_params=_params("parallel"),
        name="rw_prep",
    )(z, z, z, mu, w0, _split_hi_lo(w2), a0, _split_hi_lo(a2), _split_hi_lo(g2), k_k, k_a, seg)


def _rw_scan_kernel(w_ref, k_ref, a_ref, b_ref, r_ref, v_ref, y_ref, s_ref, *, tb, pitch, vpitch):
    N = HEAD_DIM
    NV = N // 2

    @pl.when(pl.program_id(0) == 0)
    def _():
        s_ref[...] = jnp.zeros_like(s_ref)

    zero = jnp.zeros(s_ref.shape[1:], F32)

    def dot_a(c, acc):
        for u in range(SCAN_UNROLL):
            j = c * SCAN_UNROLL + u
            acc = acc + s_ref[j] * a_ref[pl.ds(j, 1), :]
        return acc

    sa_first = lax.fori_loop(0, N // SCAN_UNROLL, dot_a, zero)

    def step(t, sa):
        base = t * pitch
        nbase = jnp.minimum(t + 1, tb - 1) * pitch
        vbase = pl.multiple_of(t * vpitch, 8)
        row = lambda ref, j: ref[pl.ds(base + j, 1), :]
        vt = v_ref[pl.ds(vbase, NV), :]

        def update(c, acc):
            y0, y1, n0, n1 = acc
            for u in range(SCAN_UNROLL):
                j = c * SCAN_UNROLL + u
                s = s_ref[j] * row(w_ref, j) + sa * row(b_ref, j) + vt * row(k_ref, j)
                s_ref[j] = s
                ty = s * row(r_ref, j)
                tn = s * a_ref[pl.ds(nbase + j, 1), :]
                if u % 2 == 0:
                    y0, n0 = y0 + ty, n0 + tn
                else:
                    y1, n1 = y1 + ty, n1 + tn
            return y0, y1, n0, n1

        y0, y1, n0, n1 = lax.fori_loop(0, N // SCAN_UNROLL, update, (zero, zero, zero, zero))
        y_ref[pl.ds(vbase, NV), :] = y0 + y1
        y_ref[pl.ds(vbase + NV, vpitch - NV), :] = jnp.zeros((vpitch - NV, zero.shape[1]), F32)
        return n0 + n1

    lax.fori_loop(0, tb, step, sa_first)


def rw_scan(w, k, a, b, r, v, T, tb, pitch, vpitch):
    L = v.shape[-1]
    kspec = pl.BlockSpec((tb * pitch, L), lambda i: (i, 0))
    vspec = pl.BlockSpec((tb * vpitch, L), lambda i: (i, 0))
    return pl.pallas_call(
        functools.partial(_rw_scan_kernel, tb=tb, pitch=pitch, vpitch=vpitch),
        grid=(T // tb,),
        in_specs=[kspec] * 5 + [vspec],
        out_specs=vspec,
        out_shape=jax.ShapeDtypeStruct((T * vpitch, L), F32),
        scratch_shapes=[pltpu.VMEM((HEAD_DIM, HEAD_DIM // 2, L), F32)],
        compiler_params=_params("arbitrary"),
        name="rw_scan",
    )(w, k, a, b, r, v)


def _rw_post_kernel(yf_ref, yb_ref, r_ref, v_ref, kd0_ref, kd1_ref, gate_ref, rk_ref, g_ref, b_ref, seg_ref,
                    o_ref):
    seg = seg_ref[...]
    y = yf_ref[...] + yb_ref[...]
    inv_n = 1.0 / HEAD_DIM
    mean = _seg_sum(y, seg) * inv_n
    d = y - mean
    var = _seg_sum(d * d, seg) * inv_n
    yn = d * lax.rsqrt(var + RW_LNX_EPS) * g_ref[...] + b_ref[...]
    r = r_ref[...]
    rk = rk_ref[...]
    bonus = _seg_sum(r * kd0_ref[...] * rk + r * kd1_ref[...] * rk, seg) * v_ref[...]
    o_ref[...] = (yn + bonus) * gate_ref[...]


def rw_post(yf, yb, r, v, kd0, kd1, gate, rk, g, b, seg, bt):
    M, C = yf.shape
    tok = pl.BlockSpec((bt, C), lambda i: (i, 0))
    row = pl.BlockSpec((1, C), lambda i: (0, 0))
    return pl.pallas_call(
        _rw_post_kernel,
        grid=(M // bt,),
        in_specs=[tok] * 7 + [row] * 3 + [pl.BlockSpec((C, C), lambda i: (0, 0))],
        out_specs=tok,
        out_shape=jax.ShapeDtypeStruct((M, C), F32),
        compiler_params=_params("parallel"),
        name="rw_post",
    )(yf, yb, r, v, kd0, kd1, gate, rk, g, b, seg)


def _dil_kernel(q_ref, k_ref, v_ref, cos_ref, sin_ref, o_ref, lse_ref, qs_ref, ks_ref, vs_ref, *, n_sub, dil, qb,
                kwin, half):
    W = DIL_STEP_WIDTH
    lane = lax.broadcasted_iota(jnp.int32, (1, W), 1)
    first_half = (lane % HEAD_DIM) < HALF_DIM

    def rope(z, c, s):
        swapped = jnp.where(first_half, pltpu.roll(z, W - HALF_DIM, 1), pltpu.roll(z, HALF_DIM, 1))
        return z * c + swapped * s

    rc = min(256, n_sub)

    def subsequence(r, carry):
        def rope_chunk(i, carry):
            rows = pl.ds(r + i * (rc * dil), rc, stride=dil)
            sl = pl.ds(pl.multiple_of(i * rc, rc), rc)
            c = cos_ref[rows, :]
            s = sin_ref[rows, :]
            qs_ref[sl, :] = (rope(q_ref[0, rows, :], c, s) * SCALE).astype(BF16)
            ks_ref[sl, :] = rope(k_ref[0, rows, :], c, s).astype(BF16)
            vs_ref[sl, :] = v_ref[0, rows, :].astype(BF16)
            return carry

        lax.fori_loop(0, n_sub // rc, rope_chunk, 0)

        nblk = n_sub // qb
        unroll = next(u for u in (4, 2, 1) if nblk % u == 0)
        heads = W // HEAD_DIM

        def block_group(ig, carry):
            chains = []
            for u in range(unroll):
                q0 = pl.multiple_of((ig * unroll + u) * qb, qb)
                k0 = pl.multiple_of(jnp.clip(q0 - (kwin - qb) // 2, 0, n_sub - kwin), 64)
                q = qs_ref[pl.ds(q0, qb), :]
                kk = ks_ref[pl.ds(k0, kwin), :]
                jq = q0 + lax.broadcasted_iota(jnp.int32, (qb, kwin), 0)
                jk = k0 + lax.broadcasted_iota(jnp.int32, (qb, kwin), 1)
                ok = jnp.abs(jk - jq) <= half
                for h in range(heads):
                    hs = slice(h * HEAD_DIM, (h + 1) * HEAD_DIM)
                    chains.append((q0, k0, hs, jnp.where(ok, _dot_nt(q[:, hs], kk[:, hs]), NEG_INF)))
            probs = []
            for q0, k0, hs, s in chains:
                m = jnp.max(s, axis=-1, keepdims=True)
                p = jnp.exp(s - m)
                l = jnp.sum(p, axis=-1, keepdims=True)
                probs.append((q0, k0, hs, p.astype(BF16), l, m + jnp.log(l)))
            outs, lses = [], []
            for q0, k0, hs, p, l, lse in probs:
                outs.append(_dot(p, vs_ref[pl.ds(k0, kwin), hs]) / l)
                lses.append(jnp.broadcast_to(lse, (qb, HEAD_DIM)))
                if len(outs) == heads:
                    rows = pl.ds(r + q0 * dil, qb, stride=dil)
                    o_ref[0, rows, :] = jnp.concatenate(outs, axis=-1)
                    lse_ref[0, rows, :] = jnp.concatenate(lses, axis=-1)
                    outs, lses = [], []
            return carry

        lax.fori_loop(0, nblk // unroll, block_group, 0)
        return carry

    lax.fori_loop(0, dil, subsequence, 0)


def dilated_group(z3, q_blk, k_blk, v_blk, cos_t, sin_t, window, dil):
    B, T, ncols = z3.shape
    W = DIL_STEP_WIDTH
    steps = DIL_GROUP_WIDTH // W
    n_sub = T // dil
    half = window // (2 * dil)
    qb = min(128, n_sub)
    kwin = min(qb + 2 * half, n_sub)
    zspec = lambda blk: pl.BlockSpec((1, T, W), lambda b, s: (b, 0, blk * steps + s))
    tspec = pl.BlockSpec((T, W), lambda b, s: (0, 0))
    ospec = pl.BlockSpec((1, T, W), lambda b, s: (b, 0, s))
    oshape = jax.ShapeDtypeStruct((B, T, DIL_GROUP_WIDTH), F32)
    o, lse = pl.pallas_call(
        functools.partial(_dil_kernel, n_sub=n_sub, dil=dil, qb=qb, kwin=kwin, half=half),
        grid=(B, steps),
        in_specs=[zspec(q_blk), zspec(k_blk), zspec(v_blk), tspec, tspec],
        out_specs=[ospec, ospec],
        out_shape=[oshape, oshape],
        scratch_shapes=[pltpu.VMEM((n_sub, W), BF16)] * 3,
        compiler_params=_params("parallel", "parallel"),
        name=f"dilated_d{dil}",
    )(z3, z3, z3, cos_t, sin_t)
    return o.reshape(B * T, DIL_GROUP_WIDTH), lse.reshape(B * T, DIL_GROUP_WIDTH)


def _na_kernel(q_ref, k_ref, v_ref, tab_ref, o_ref, qs_ref, ks_ref, vs_ref, bias_ref, *, rows):
    nkeys = NA_KH * GRID_W
    pc = NA_KH * GRID_W

    for off in range(NA_KH):
        for h in range(NA_GROUP_HEADS):
            for ky in range(NA_KH):
                bias_ref[off, h, :, ky * GRID_W:(ky + 1) * GRID_W] = tab_ref[h, ky - off + NA_KH - 1]

    def prep(c, carry):
        sl = pl.ds(pl.multiple_of(c * pc, pc), pc)
        q = q_ref[0, sl, :] * SCALE
        k = k_ref[0, sl, :]
        v = v_ref[0, sl, :]
        for h in range(NA_GROUP_HEADS):
            hs = slice(h * HEAD_DIM, (h + 1) * HEAD_DIM)
            qs_ref[h, sl, :] = q[:, hs].astype(BF16)
            ks_ref[h, sl, :] = k[:, hs].astype(BF16)
            vs_ref[h, sl, :] = v[:, hs].astype(BF16)
        return carry

    lax.fori_loop(0, rows * GRID_W // pc, prep, 0)

    def row_group(rg, carry):
        chains = []
        for u in range(NA_ROW_UNROLL):
            r = rg * NA_ROW_UNROLL + u
            r0 = jnp.clip(r - NA_KH // 2, 0, rows - NA_KH)
            qrow = pl.ds(pl.multiple_of(r * GRID_W, GRID_W), GRID_W)
            krow = pl.ds(pl.multiple_of(r0 * GRID_W, GRID_W), nkeys)
            for h in range(NA_GROUP_HEADS):
                s = _dot_nt(qs_ref[h, qrow, :], ks_ref[h, krow, :]) + bias_ref[r - r0, h]
                chains.append((u, h, qrow, krow, s))
        probs = []
        for u, h, qrow, krow, s in chains:
            m = jnp.max(s, axis=-1, keepdims=True)
            p = jnp.exp(s - m)
            l = jnp.sum(p, axis=-1, keepdims=True)
            probs.append((u, h, qrow, krow, p.astype(BF16), l))
        outs = {}
        for u, h, qrow, krow, p, l in probs:
            outs.setdefault(u, []).append(_dot(p, vs_ref[h, krow, :]) / l)
            if h == NA_GROUP_HEADS - 1:
                o_ref[0, qrow, :] = jnp.concatenate(outs[u], axis=-1)
        return carry

    lax.fori_loop(0, rows // NA_ROW_UNROLL, row_group, 0)


def na_bias_table(rel_bias):
    qc = np.arange(GRID_W)
    kc = np.arange(GRID_W)
    wc0 = np.clip(qc - NA_KW // 2, 0, GRID_W - NA_KW)
    col_ok = (kc[None, :] >= wc0[:, None]) & (kc[None, :] < wc0[:, None] + NA_KW)
    dx_idx = np.clip(kc[None, :] - qc[:, None], 1 - NA_KW, NA_KW - 1) + NA_KW - 1
    onehot = jnp.asarray(dx_idx[None] == np.arange(2 * NA_KW - 1)[:, None, None], F32)
    tab = jnp.einsum('hyd,dqk->hyqk', rel_bias.astype(F32), onehot, precision=HIGHEST)
    return jnp.where(jnp.asarray(col_ok)[None, None], tab, NEG_INF)


def neighbourhood(z3, q_blk0, k_blk0, v_blk0, bias_tab):
    B, T, ncols = z3.shape
    rows = T // GRID_W
    assert rows >= NA_KH
    W = NA_GROUP_WIDTH
    G = NA_HEADS // NA_GROUP_HEADS
    zspec = lambda blk0: pl.BlockSpec((1, T, W), lambda b, g: (b, 0, blk0 + g))
    return pl.pallas_call(
        functools.partial(_na_kernel, rows=rows),
        grid=(B, G),
        in_specs=[zspec(q_blk0), zspec(k_blk0), zspec(v_blk0),
                  pl.BlockSpec((NA_GROUP_HEADS, 2 * NA_KH - 1, GRID_W, GRID_W), lambda b, g: (g, 0, 0, 0))],
        out_specs=pl.BlockSpec((1, T, W), lambda b, g: (b, 0, g)),
        out_shape=jax.ShapeDtypeStruct((B, T, NA_WIDTH), F32),
        scratch_shapes=[pltpu.VMEM((NA_GROUP_HEADS, T, HEAD_DIM), BF16)] * 3
                       + [pltpu.VMEM((NA_KH, NA_GROUP_HEADS, GRID_W, NA_KH * GRID_W), F32)],
        compiler_params=_params("parallel", "parallel"),
        name="neighbourhood",
    )(z3, z3, z3, bias_tab)


def _merge_kernel(ya_ref, o1_ref, o2_ref, o3_ref, l1_ref, l2_ref, l3_ref, yc_ref, ga_ref, gb_ref, gc_ref,
                  wa_ref, wb_ref, wc_ref, out_ref, yb_ref):
    @pl.when(pl.program_id(1) == 0)
    def _():
        l1, l2, l3 = l1_ref[...], l2_ref[...], l3_ref[...]
        m = jnp.maximum(jnp.maximum(l1, l2), l3)
        e1, e2, e3 = jnp.exp(l1 - m), jnp.exp(l2 - m), jnp.exp(l3 - m)
        yb = (e1 * o1_ref[...] + e2 * o2_ref[...] + e3 * o3_ref[...]) / (e1 + e2 + e3)
        yb_ref[...] = yb.astype(BF16)

    pa = _dot(ya_ref[...].astype(BF16), wa_ref[...])
    pb = _dot(yb_ref[...], wb_ref[...])
    pc = _dot(yc_ref[...].astype(BF16), wc_ref[...])
    gate = lambda ref: _sigmoid(ref[...].astype(F32))
    out = gate(ga_ref) * pa + gate(gb_ref) * pb + gate(gc_ref) * pc
    out_ref[...] = out.astype(BF16)


def merge(ya, dil_outs, yc, z, layer, wa, wb, wc, bm, bn):
    M = ya.shape[0]
    D = wa.shape[-1]
    nj = D // bn
    tok = lambda w: pl.BlockSpec((bm, w), lambda i, j: (i, 0))
    gate = lambda g: pl.BlockSpec((bm, bn), lambda i, j: (i, g * nj + j))
    wsp = lambda k: pl.BlockSpec((None, k, bn), lambda i, j: (layer, 0, j))
    (o1, l1), (o2, l2), (o3, l3) = dil_outs
    W = DIL_GROUP_WIDTH
    return pl.pallas_call(
        _merge_kernel,
        grid=(M // bm, nj),
        in_specs=[tok(RW_WIDTH)] + [tok(W)] * 6 + [tok(NA_WIDTH), gate(0), gate(1), gate(2),
                                                  wsp(RW_WIDTH), wsp(W), wsp(NA_WIDTH)],
        out_specs=pl.BlockSpec((bm, bn), lambda i, j: (i, j)),
        out_shape=jax.ShapeDtypeStruct((M, D), BF16),
        scratch_shapes=[pltpu.VMEM((bm, W), BF16)],
        compiler_params=_params("parallel", "arbitrary"),
        name="merge",
    )(ya, o1, o2, o3, l1, l2, l3, yc, z, z, z, wa, wb, wc)


def _matmul_res_kernel(a_ref, w_ref, x_ref, o_ref):
    o_ref[...] = x_ref[...] + _dot(a_ref[...], w_ref[...])


def matmul_res(a, w, layer, x, bm, bn):
    M, K = a.shape
    N = w.shape[-1]
    return pl.pallas_call(
        _matmul_res_kernel,
        grid=(M // bm, N // bn),
        in_specs=[pl.BlockSpec((bm, K), lambda i, j: (i, 0)),
                  pl.BlockSpec((None, K, bn), lambda i, j: (layer, 0, j)),
                  pl.BlockSpec((bm, bn), lambda i, j: (i, j))],
        out_specs=pl.BlockSpec((bm, bn), lambda i, j: (i, j)),
        out_shape=jax.ShapeDtypeStruct((M, N), F32),
        compiler_params=_params("parallel", "parallel"),
        name="out_proj",
    )(a, w, x)


def _gelu(x):
    return 0.5 * x * (1.0 + jnp.tanh(np.sqrt(2.0 / np.pi).astype(np.float32) * (x + 0.044715 * (x * x * x))))


FFN_HALO = 16
FFN_SPLIT = 2


def _ffn_kernel(x_ref, xp_ref, xn_ref, g_ref, wg_ref, wu_ref, cw_ref, cb_ref, wd_ref, o_ref,
                h_ref, gs_ref, acc_ref, *, bm, bf, seq):
    i = pl.program_id(0)
    f = pl.program_id(1)
    HL = FFN_HALO

    @pl.when(f == 0)
    def _():
        g = g_ref[...]
        h_ref[0:HL, :] = _rms(xp_ref[...], g).astype(BF16)
        h_ref[HL:bm + HL, :] = _rms(x_ref[...], g).astype(BF16)
        h_ref[bm + HL:bm + 2 * HL, :] = _rms(xn_ref[...], g).astype(BF16)
        acc_ref[...] = jnp.zeros_like(acc_ref)

    pos = (i * bm) % seq
    prev_ok = (pos != 0).astype(F32)
    next_ok = (pos + bm != seq).astype(F32)
    hw = bf // FFN_SPLIT
    h_ext = h_ref[...]
    h = h_ref[HL:bm + HL, :]
    cw = cw_ref[...]
    cb = cb_ref[...]
    gates, ups = [], []
    for s in range(FFN_SPLIT):
        cs = slice(s * hw, (s + 1) * hw)
        gates.append(_dot(h_ext, wg_ref[:, cs]))
        ups.append(_dot(h, wu_ref[:, cs]))
    acts = []
    for s in range(FFN_SPLIT):
        cs = slice(s * hw, (s + 1) * hw)
        gs_ref[s] = gates[s]
        gs_ref[s, HL - 8:HL, :] = gs_ref[s, HL - 8:HL, :] * prev_ok
        gs_ref[s, bm + HL:bm + HL + 8, :] = gs_ref[s, bm + HL:bm + HL + 8, :] * next_ok
        gc = (gs_ref[s, HL - 1:bm + HL - 1, :] * cw[0:1, cs] + gs_ref[s, HL:bm + HL, :] * cw[1:2, cs]
              + gs_ref[s, HL + 1:bm + HL + 1, :] * cw[2:3, cs] + cb[:, cs])
        acts.append((_gelu(gc) * ups[s]).astype(BF16))
    upd = _dot(acts[0], wd_ref[0:hw, :])
    for s in range(1, FFN_SPLIT):
        upd = upd + _dot(acts[s], wd_ref[s * hw:(s + 1) * hw, :])
    acc_ref[...] += upd

    @pl.when(f == pl.num_programs(1) - 1)
    def _():
        o_ref[...] = x_ref[...] + acc_ref[...]


def ffn(x, g, layer, wg, wu, cw, cb, wd, seq, bm, bf):
    M, D = x.shape
    F = wg.shape[-1]
    HL = FFN_HALO
    nbh = M // HL
    return pl.pallas_call(
        functools.partial(_ffn_kernel, bm=bm, bf=bf, seq=seq),
        grid=(M // bm, F // bf),
        in_specs=[pl.BlockSpec((bm, D), lambda i, f: (i, 0)),
                  pl.BlockSpec((HL, D), lambda i, f: (jnp.maximum(i * (bm // HL) - 1, 0), 0)),
                  pl.BlockSpec((HL, D), lambda i, f: (jnp.minimum((i + 1) * (bm // HL), nbh - 1), 0)),
                  pl.BlockSpec((1, D), lambda i, f: (0, 0)),
                  pl.BlockSpec((None, D, bf), lambda i, f: (layer, 0, f)),
                  pl.BlockSpec((None, D, bf), lambda i, f: (layer, 0, f)),
                  pl.BlockSpec((3, bf), lambda i, f: (0, f)),
                  pl.BlockSpec((1, bf), lambda i, f: (0, f)),
                  pl.BlockSpec((None, bf, D), lambda i, f: (layer, f, 0))],
        out_specs=pl.BlockSpec((bm, D), lambda i, f: (i, 0)),
        out_shape=jax.ShapeDtypeStruct((M, D), F32),
        scratch_shapes=[pltpu.VMEM((bm + 2 * HL, D), BF16),
                        pltpu.VMEM((FFN_SPLIT, bm + 2 * HL, bf // FFN_SPLIT), F32),
                        pltpu.VMEM((bm, D), F32)],
        compiler_params=_params("parallel", "arbitrary"),
        name="conv_ffn",
    )(x, x, x, g, wg, wu, cw, cb, wd)


def _ple_kernel(x_ref, p_ref, g_ref, wg_ref, wp_ref, gf_ref, o_ref, *, final_norm):
    x = x_ref[...]
    h = _rms(x, g_ref[...]).astype(BF16)
    gate = _sigmoid(_dot(h, wg_ref[...]))
    y = x + gate * _dot(p_ref[...].astype(BF16), wp_ref[...])
    if final_norm:
        y = _rms(y, gf_ref[...])
    o_ref[...] = y


def ple(x, p, layer, g, wg, wp, gf, bm, final_norm):
    M, D = x.shape
    P = p.shape[-1]
    return pl.pallas_call(
        functools.partial(_ple_kernel, final_norm=final_norm),
        grid=(M // bm,),
        in_specs=[pl.BlockSpec((bm, D), lambda i: (i, 0)),
                  pl.BlockSpec((None, bm, P), lambda i: (layer, i, 0)),
                  pl.BlockSpec((1, D), lambda i: (0, 0)),
                  pl.BlockSpec((None, D, D), lambda i: (layer, 0, 0)),
                  pl.BlockSpec((None, P, D), lambda i: (layer, 0, 0)),
                  pl.BlockSpec((1, D), lambda i: (0, 0))],
        out_specs=pl.BlockSpec((bm, D), lambda i: (i, 0)),
        out_shape=jax.ShapeDtypeStruct((M, D), F32),
        compiler_params=_params("parallel"),
        name="ple_final" if final_norm else "ple",
    )(x, p, g, wg, wp, gf)


def _to_scan_kernel(x0_ref, x1_ref, rev_ref, o_ref, a_ref, *, nb, tb, dup, pitch, apitch):
    N = HEAD_DIM
    NV = N // 2
    lanes = 2 * 2 * nb * RW_HEADS
    n_rows = N if dup else NV
    rev = rev_ref[...]
    for d, ref in enumerate((x0_ref, x1_ref)):
        for b in range(nb):
            x = ref[b]
            if d == 1:
                x = _dot_hi(rev, x)
            xt = x.T
            for h in range(RW_HEADS):
                slot = ((d * nb + b) * RW_HEADS + h) * 2
                for s in range(2):
                    blk = xt[h * N:(h + 1) * N] if dup else xt[h * N + s * NV:h * N + (s + 1) * NV]
                    a_ref[(slot + s) * apitch:(slot + s) * apitch + n_rows, :] = blk
    for j in range(n_rows):
        rows = a_ref[pl.ds(j, lanes, stride=apitch), :]
        o_ref[pl.ds(j, tb, stride=pitch), :] = rows.T
    for j in range(n_rows, pitch):
        o_ref[pl.ds(j, tb, stride=pitch), :] = jnp.zeros((tb, lanes), F32)


def to_scan(x0, x1, rev, B, T, tb, dup, pitch):
    C = RW_WIDTH
    nblk = T // tb
    lanes = 4 * B * RW_HEADS
    apitch = (HEAD_DIM if dup else HEAD_DIM // 2) + 4
    return pl.pallas_call(
        functools.partial(_to_scan_kernel, nb=B, tb=tb, dup=dup, pitch=pitch, apitch=apitch),
        grid=(nblk,),
        in_specs=[pl.BlockSpec((B, tb, C), lambda i: (0, i, 0)),
                  pl.BlockSpec((B, tb, C), lambda i: (0, nblk - 1 - i, 0)),
                  pl.BlockSpec((tb, tb), lambda i: (0, 0))],
        out_specs=pl.BlockSpec((tb * pitch, lanes), lambda i: (i, 0)),
        out_shape=jax.ShapeDtypeStruct((T * pitch, lanes), F32),
        scratch_shapes=[pltpu.VMEM((lanes * apitch, tb), F32)],
        compiler_params=_params("parallel"),
        name="to_scan_keys" if dup else "to_scan_values",
    )(x0.reshape(B, T, C), x1.reshape(B, T, C), rev)


def _from_scan_kernel(y_ref, rev_ref, yf_ref, yb_ref, a_ref, *, nb, tb, pitch, apitch):
    NV = HEAD_DIM // 2
    lanes = 2 * 2 * nb * RW_HEADS
    for ih in range(NV):
        rows = y_ref[pl.ds(ih, tb, stride=pitch), :]
        a_ref[pl.ds(ih, lanes, stride=apitch), :] = rows.T
    rev = rev_ref[...]
    for d, ref in enumerate((yf_ref, yb_ref)):
        for b in range(nb):
            slot0 = (d * nb + b) * RW_HEADS * 2
            pieces = [a_ref[(slot0 + s) * apitch:(slot0 + s) * apitch + NV, :] for s in range(2 * RW_HEADS)]
            y = jnp.concatenate(pieces, axis=0).T
            if d == 1:
                y = _dot_hi(rev, y)
            ref[b] = y


def from_scan(y, rev, B, T, tb, pitch):
    C = RW_WIDTH
    nblk = T // tb
    lanes = y.shape[-1]
    apitch = HEAD_DIM // 2 + 4
    out = jax.ShapeDtypeStruct((B, T, C), F32)
    yf, yb = pl.pallas_call(
        functools.partial(_from_scan_kernel, nb=B, tb=tb, pitch=pitch, apitch=apitch),
        grid=(nblk,),
        in_specs=[pl.BlockSpec((tb * pitch, lanes), lambda i: (i, 0)),
                  pl.BlockSpec((tb, tb), lambda i: (0, 0))],
        out_specs=[pl.BlockSpec((B, tb, C), lambda i: (0, i, 0)),
                   pl.BlockSpec((B, tb, C), lambda i: (0, nblk - 1 - i, 0))],
        out_shape=[out, out],
        scratch_shapes=[pltpu.VMEM((lanes * apitch, tb), F32)],
        compiler_params=_params("parallel"),
        name="from_scan",
    )(y, rev)
    return yf.reshape(B * T, C), yb.reshape(B * T, C)


def _rope_tables(T):
    inv = ROPE_THETA ** (-jnp.arange(0, HEAD_DIM, 2, dtype=jnp.float32) / HEAD_DIM)
    ang = jnp.arange(T, dtype=jnp.float32)[:, None] * inv[None, :]
    cos, sin = jnp.cos(ang), jnp.sin(ang)
    cos_t = jnp.tile(jnp.concatenate([cos, cos], axis=-1), (1, DIL_STEP_WIDTH // HEAD_DIM))
    sin_t = jnp.tile(jnp.concatenate([-sin, sin], axis=-1), (1, DIL_STEP_WIDTH // HEAD_DIM))
    return cos_t, sin_t


def _pick(n, prefs):
    for c in prefs:
        if n % c == 0:
            return c
    return n


def kernel(x, p, norm_mix, w_in, rw_mu, rw_w0, rw_w2, rw_a0, rw_a2, rw_g2, rw_k_k, rw_k_a, rw_r_k, rw_lnx_g,
           rw_lnx_b, na_bias, w_br_a, w_br_b, w_br_c, w_out, norm_ffn, w_ffn_gate, w_ffn_up, ffn_conv_w,
           ffn_conv_b, w_ffn_down, norm_ple, w_ple_gate, w_ple, norm_final):
    B, T, D = x.shape
    depth = w_in.shape[0]
    M = B * T
    F = w_ffn_gate.shape[-1]
    C = RW_WIDTH
    W = DIL_GROUP_WIDTH
    assert (N_BRANCH * D) % W == 0 and w_in.shape[-1] == RW_COLS + 2 * 3 * DIL_WIDTH + N_BRANCH * D

    gates_w = N_BRANCH * D
    dil_w = 3 * DIL_WIDTH
    na_w = 3 * NA_WIDTH
    nb_cols = gates_w + na_w
    nf_cols = RW_COLS_PAD + dil_w
    na_blk0 = gates_w // W
    dil_blk0 = RW_COLS_PAD // W
    assert RW_COLS_PAD % W == 0

    cos_t, sin_t = _rope_tables(T)
    seg = jnp.asarray(np.kron(np.eye(RW_HEADS), np.ones((HEAD_DIM, HEAD_DIM))), BF16)
    xf = x.reshape(M, D)
    p3 = p.reshape(depth, M, p.shape[-1])
    bm_in = _pick(M, (1024, 512, 256, 128))
    bt_rw = _pick(T, (256, 128))
    tb = _pick(T, (64, 32))
    tt = _pick(T, (128,))
    rev = jnp.asarray(np.eye(tt)[::-1], F32)

    wb16 = {name: wt.astype(BF16) for name, wt in (
        ("br_a", w_br_a), ("br_b", w_br_b), ("br_c", w_br_c), ("out", w_out), ("ffn_gate", w_ffn_gate),
        ("ffn_up", w_ffn_up), ("ffn_down", w_ffn_down), ("ple_gate", w_ple_gate), ("ple", w_ple))}
    o_dil, o_na, o_g = RW_COLS, RW_COLS + dil_w, RW_COLS + dil_w + na_w
    w_b = jnp.concatenate([w_in[:, :, o_g:].astype(BF16), w_in[:, :, o_na:o_g].astype(BF16)], axis=2)
    w_f = jnp.concatenate([w_in[:, :, :RW_COLS].astype(BF16), jnp.zeros((depth, D, RW_COLS_PAD - RW_COLS), BF16),
                           w_in[:, :, o_dil:o_na].astype(BF16)], axis=2)

    for i in range(depth):
        g_mix = norm_mix[i][None]
        zb = norm_matmul(xf, g_mix, w_b, i, bm_in, _pick(nb_cols, (768, 256, 128)), BF16, "in_proj_bf16")
        zf = norm_matmul(xf, g_mix, w_f, i, bm_in, _pick(nf_cols, (1024, 256, 128)), F32, "in_proj_f32")
        zb3 = zb.reshape(B, T, nb_cols)
        zf3 = zf.reshape(B, T, nf_cols)

        mu = jnp.pad(rw_mu[i], (0, RW_COLS_PAD - RW_COLS))[None]
        r, v, a, dec0, dec1, kd0, kd1, b0, b1, gate = rw_prep(
            zf, 0, mu, rw_w0[i], rw_w2[i], rw_a0[i], rw_a2[i], rw_g2[i], rw_k_k[i][None], rw_k_a[i][None],
            seg, T, bt_rw)
        keys = lambda x0, x1: to_scan(x0, x1, rev, B, T, tt, dup=True, pitch=KEY_PITCH)
        vals = to_scan(v, v, rev, B, T, tt, dup=False, pitch=VAL_PITCH)
        y = rw_scan(keys(dec0, dec1), keys(kd0, kd1), keys(a, a), keys(b0, b1), keys(r, r), vals, T, tb,
                    KEY_PITCH, VAL_PITCH)
        yf, yb = from_scan(y, rev, B, T, tt, VAL_PITCH)
        ya = rw_post(yf, yb, r, v, kd0, kd1, gate, rw_r_k[i].reshape(1, C), rw_lnx_g[i][None], rw_lnx_b[i][None],
                     seg, bt_rw)

        dil_outs = []
        for g, (window, dil) in enumerate(DIL_PATTERNS):
            dil_outs.append(dilated_group(zf3, dil_blk0 + g, dil_blk0 + 3 + g, dil_blk0 + 6 + g, cos_t, sin_t,
                                          window, dil))

        yc = neighbourhood(zb3, na_blk0, na_blk0 + 3, na_blk0 + 6, na_bias_table(na_bias[i])).reshape(M, NA_WIDTH)

        merged = merge(ya, dil_outs, yc, zb, i, wb16["br_a"], wb16["br_b"], wb16["br_c"],
                       _pick(M, (512, 256, 128)), _pick(D, (512, 256, 128)))
        xf = matmul_res(merged, wb16["out"], i, xf, _pick(M, (1024, 512, 256, 128)),
                        _pick(D, (1024, 512, 256, 128)))

        xf = ffn(xf, norm_ffn[i][None], i, wb16["ffn_gate"], wb16["ffn_up"], ffn_conv_w[i],
                 ffn_conv_b[i][None], wb16["ffn_down"], T, _pick(T, (512, 256, 128)),
                 _pick(F, (512, 256, 128)))

        xf = ple(xf, p3, i, norm_ple[i][None], wb16["ple_gate"], wb16["ple"],
                 norm_final[None], _pick(M, (512, 256, 128)), final_norm=(i == depth - 1))
    return xf.reshape(B, T, D)
```

```python
import functools

import numpy as np
import jax
import jax.numpy as jnp
from jax import lax
from jax.experimental import pallas as pl
from jax.experimental.pallas import tpu as pltpu

HEAD_DIM = 64
HALF_DIM = HEAD_DIM // 2
SCALE = HEAD_DIM ** -0.5
NORM_EPS = 1e-6
ROPE_THETA = 10000.0
NEG_INF = -1e30

RW_HEADS = 8
RW_WIDTH = RW_HEADS * HEAD_DIM
RW_DECAY_LORA = 32
RW_AAA_LORA = 32
RW_GATE_LORA = 96
RW_LNX_EPS = 64e-5
RW_COLS = 3 * RW_WIDTH + 2 * RW_DECAY_LORA + 2 * RW_AAA_LORA + RW_GATE_LORA
RW_COLS_PAD = 1792
KEY_PITCH = HEAD_DIM + 4
VAL_PITCH = HEAD_DIM // 2 + 8
SCAN_UNROLL = 32

DIL_PATTERNS = ((128, 1), (512, 4), (2048, 16))
DIL_HEADS_PER_GROUP = 4
DIL_GROUP_WIDTH = DIL_HEADS_PER_GROUP * HEAD_DIM
DIL_WIDTH = len(DIL_PATTERNS) * DIL_GROUP_WIDTH
DIL_STEP_WIDTH = 2 * HEAD_DIM

GRID_W = 64
NA_HEADS = 12
NA_WIDTH = NA_HEADS * HEAD_DIM
NA_KH = 8
NA_KW = 16
NA_GROUP_HEADS = 4
NA_GROUP_WIDTH = NA_GROUP_HEADS * HEAD_DIM
NA_ROW_UNROLL = 4

N_BRANCH = 3
VMEM_LIMIT = 56 * 1024 * 1024

BF16 = jnp.bfloat16
F32 = jnp.float32
HIGHEST = lax.Precision.HIGHEST


def _params(*sem):
    return pltpu.CompilerParams(dimension_semantics=sem, vmem_limit_bytes=VMEM_LIMIT)


def _rms(x, g):
    ms = jnp.mean(x * x, axis=-1, keepdims=True)
    return x * lax.rsqrt(ms + NORM_EPS) * g


def _sigmoid(x):
    return 0.5 * jnp.tanh(0.5 * x) + 0.5


def _dot(a, b):
    return jnp.dot(a, b, preferred_element_type=F32)


def _dot_nt(a, b):
    return lax.dot_general(a, b, (((1,), (1,)), ((), ())), preferred_element_type=F32)


def _dot_hi(a, b):
    return jnp.dot(a, b, preferred_element_type=F32, precision=HIGHEST)


def _split_hi_lo(w):
    hi = w.astype(BF16)
    return jnp.stack([hi, (w - hi.astype(F32)).astype(BF16)])


def _dot_split(a, w_hi, w_lo):
    a_hi = a.astype(BF16)
    a_lo = (a - a_hi.astype(F32)).astype(BF16)
    return _dot(a_hi, w_hi) + _dot(a_hi, w_lo) + _dot(a_lo, w_hi)


def _seg_sum(x, seg):
    hi = x.astype(BF16)
    rest = x - hi.astype(F32)
    mid = rest.astype(BF16)
    lo = (rest - mid.astype(F32)).astype(BF16)
    return _dot(hi, seg) + _dot(mid, seg) + _dot(lo, seg)


def _norm_matmul_kernel(x_ref, g_ref, w_ref, o_ref, h_ref):
    @pl.when(pl.program_id(1) == 0)
    def _():
        h_ref[...] = _rms(x_ref[...], g_ref[...]).astype(BF16)

    o_ref[...] = _dot(h_ref[...], w_ref[...]).astype(o_ref.dtype)


def norm_matmul(x, g, w, layer, bm, bn, out_dtype, name):
    M, K = x.shape
    N = w.shape[-1]
    return pl.pallas_call(
        _norm_matmul_kernel,
        grid=(M // bm, N // bn),
        in_specs=[pl.BlockSpec((bm, K), lambda i, j: (i, 0)),
                  pl.BlockSpec((1, K), lambda i, j: (0, 0)),
                  pl.BlockSpec((None, K, bn), lambda i, j: (layer, 0, j))],
        out_specs=pl.BlockSpec((bm, bn), lambda i, j: (i, j)),
        out_shape=jax.ShapeDtypeStruct((M, N), out_dtype),
        scratch_shapes=[pltpu.VMEM((bm, K), BF16)],
        compiler_params=_params("parallel", "arbitrary"),
        name=name,
    )(x, g, w)


DECAY_SCALE = float(np.exp(-0.5))


def _rw_prep_kernel(c_ref, cp_ref, cn_ref, mu_ref, w0_ref, w2_ref, a0_ref, a2_ref, g2_ref, kk_ref, ka_ref,
                    seg_ref, r_out, v_out, a_out, dec0_out, dec1_out, kd0_out, kd1_out, b0_out, b1_out,
                    gate_out, buf_ref, *, bt, seq):
    i = pl.program_id(0)
    pos = (i * bt) % seq
    prev_ok = (pos != 0).astype(F32)
    next_ok = (pos + bt != seq).astype(F32)
    buf_ref[0:8, :] = cp_ref[...] * prev_ok
    buf_ref[8:bt + 8, :] = c_ref[...]
    buf_ref[bt + 8:bt + 16, :] = cn_ref[...] * next_ok
    mu = mu_ref[...]
    c = c_ref[...] * (1.0 - mu) + (buf_ref[7:bt + 7, :] + buf_ref[9:bt + 9, :]) * (0.5 * mu)
    C = RW_WIDTH
    r = c[:, 0:C]
    k = c[:, C:2 * C]
    v = c[:, 2 * C:3 * C]
    o = 3 * C
    wd = c[:, o:o + 2 * RW_DECAY_LORA]
    o += 2 * RW_DECAY_LORA
    ad = c[:, o:o + 2 * RW_AAA_LORA]
    o += 2 * RW_AAA_LORA
    gd = c[:, o:o + RW_GATE_LORA]
    seg = seg_ref[...]
    kk = k * kk_ref[...]
    kk = kk * lax.rsqrt(jnp.maximum(_seg_sum(kk * kk, seg), 1e-24))
    r_out[...] = r
    v_out[...] = v
    a_out[...] = -kk
    gate_out[...] = _dot_split(_sigmoid(gd), g2_ref[0], g2_ref[1])
    twd = jnp.tanh(wd)
    for d, (dec_out, kd_out, b_out) in enumerate(((dec0_out, kd0_out, b0_out), (dec1_out, kd1_out, b1_out))):
        lw = _dot_split(twd[:, d * RW_DECAY_LORA:(d + 1) * RW_DECAY_LORA], w2_ref[0, d], w2_ref[1, d])
        dec_out[...] = jnp.exp(-DECAY_SCALE * _sigmoid(w0_ref[d:d + 1, :] + lw))
        la = _dot_split(ad[:, d * RW_AAA_LORA:(d + 1) * RW_AAA_LORA], a2_ref[0, d], a2_ref[1, d])
        lr = _sigmoid(a0_ref[d:d + 1, :] + la)
        kd_out[...] = k * (1.0 + (lr - 1.0) * ka_ref[...])
        b_out[...] = kk * lr


def rw_prep(z, rw_col_block, mu, w0, w2, a0, a2, g2, k_k, k_a, seg, seq, bt):
    M = z.shape[0]
    C = RW_WIDTH
    W = RW_COLS_PAD
    nb8 = M // 8
    full = lambda shape: pl.BlockSpec(shape, lambda i: (0,) * len(shape))
    out = jax.ShapeDtypeStruct((M, C), F32)
    return pl.pallas_call(
        functools.partial(_rw_prep_kernel, bt=bt, seq=seq),
        grid=(M // bt,),
        in_specs=[pl.BlockSpec((bt, W), lambda i: (i, rw_col_block)),
                  pl.BlockSpec((8, W), lambda i: (jnp.maximum(i * (bt // 8) - 1, 0), rw_col_block)),
                  pl.BlockSpec((8, W), lambda i: (jnp.minimum((i + 1) * (bt // 8), nb8 - 1), rw_col_block)),
                  full((1, W)), full((2, C)), full((2, 2, RW_DECAY_LORA, C)), full((2, C)),
                  full((2, 2, RW_AAA_LORA, C)), full((2, RW_GATE_LORA, C)), full((1, C)), full((1, C)),
                  full((C, C))],
        out_specs=[pl.BlockSpec((bt, C), lambda i: (i, 0))] * 10,
        out_shape=[out] * 10,
        scratch_shapes=[pltpu.VMEM((bt + 16, W), F32)],
        compiler_params=_params("parallel"),
        name="rw_prep",
    )(z, z, z, mu, w0, _split_hi_lo(w2), a0, _split_hi_lo(a2), _split_hi_lo(g2), k_k, k_a, seg)


def _rw_scan_kernel(w_ref, k_ref, a_ref, b_ref, r_ref, v_ref, y_ref, s_ref, *, tb, pitch, vpitch):
    N = HEAD_DIM
    NV = N // 2

    @pl.when(pl.program_id(0) == 0)
    def _():
        s_ref[...] = jnp.zeros_like(s_ref)

    zero = jnp.zeros(s_ref.shape[1:], F32)

    def dot_a(c, acc):
        for u in range(SCAN_UNROLL):
            j = c * SCAN_UNROLL + u
            acc = acc + s_ref[j] * a_ref[pl.ds(j, 1), :]
        return acc

    sa_first = lax.fori_loop(0, N // SCAN_UNROLL, dot_a, zero)

    def step(t, sa):
        base = t * pitch
        nbase = jnp.minimum(t + 1, tb - 1) * pitch
        vbase = pl.multiple_of(t * vpitch, 8)
        row = lambda ref, j: ref[pl.ds(base + j, 1), :]
        vt = v_ref[pl.ds(vbase, NV), :]

        def update(c, acc):
            y0, y1, n0, n1 = acc
            for u in range(SCAN_UNROLL):
                j = c * SCAN_UNROLL + u
                s = s_ref[j] * row(w_ref, j) + sa * row(b_ref, j) + vt * row(k_ref, j)
                s_ref[j] = s
                ty = s * row(r_ref, j)
                tn = s * a_ref[pl.ds(nbase + j, 1), :]
                if u % 2 == 0:
                    y0, n0 = y0 + ty, n0 + tn
                else:
                    y1, n1 = y1 + ty, n1 + tn
            return y0, y1, n0, n1

        y0, y1, n0, n1 = lax.fori_loop(0, N // SCAN_UNROLL, update, (zero, zero, zero, zero))
        y_ref[pl.ds(vbase, NV), :] = y0 + y1
        y_ref[pl.ds(vbase + NV, vpitch - NV), :] = jnp.zeros((vpitch - NV, zero.shape[1]), F32)
        return n0 + n1

    lax.fori_loop(0, tb, step, sa_first)


def rw_scan(w, k, a, b, r, v, T, tb, pitch, vpitch):
    L = v.shape[-1]
    kspec = pl.BlockSpec((tb * pitch, L), lambda i: (i, 0))
    vspec = pl.BlockSpec((tb * vpitch, L), lambda i: (i, 0))
    return pl.pallas_call(
        functools.partial(_rw_scan_kernel, tb=tb, pitch=pitch, vpitch=vpitch),
        grid=(T // tb,),
        in_specs=[kspec] * 5 + [vspec],
        out_specs=vspec,
        out_shape=jax.ShapeDtypeStruct((T * vpitch, L), F32),
        scratch_shapes=[pltpu.VMEM((HEAD_DIM, HEAD_DIM // 2, L), F32)],
        compiler_params=_params("arbitrary"),
        name="rw_scan",
    )(w, k, a, b, r, v)


def _rw_post_kernel(yf_ref, yb_ref, r_ref, v_ref, kd0_ref, kd1_ref, gate_ref, rk_ref, g_ref, b_ref, seg_ref,
                    o_ref):
    seg = seg_ref[...]
    y = yf_ref[...] + yb_ref[...]
    inv_n = 1.0 / HEAD_DIM
    mean = _seg_sum(y, seg) * inv_n
    d = y - mean
    var = _seg_sum(d * d, seg) * inv_n
    yn = d * lax.rsqrt(var + RW_LNX_EPS) * g_ref[...] + b_ref[...]
    r = r_ref[...]
    rk = rk_ref[...]
    bonus = _seg_sum(r * kd0_ref[...] * rk + r * kd1_ref[...] * rk, seg) * v_ref[...]
    o_ref[...] = (yn + bonus) * gate_ref[...]


def rw_post(yf, yb, r, v, kd0, kd1, gate, rk, g, b, seg, bt):
    M, C = yf.shape
    tok = pl.BlockSpec((bt, C), lambda i: (i, 0))
    row = pl.BlockSpec((1, C), lambda i: (0, 0))
    return pl.pallas_call(
        _rw_post_kernel,
        grid=(M // bt,),
        in_specs=[tok] * 7 + [row] * 3 + [pl.BlockSpec((C, C), lambda i: (0, 0))],
        out_specs=tok,
        out_shape=jax.ShapeDtypeStruct((M, C), F32),
        compiler_params=_params("parallel"),
        name="rw_post",
    )(yf, yb, r, v, kd0, kd1, gate, rk, g, b, seg)


def _dil_kernel(q_ref, k_ref, v_ref, cos_ref, sin_ref, o_ref, lse_ref, qs_ref, ks_ref, vs_ref, *, n_sub, dil, qb,
                kwin, half):
    W = DIL_STEP_WIDTH
    lane = lax.broadcasted_iota(jnp.int32, (1, W), 1)
    first_half = (lane % HEAD_DIM) < HALF_DIM

    def rope(z, c, s):
        swapped = jnp.where(first_half, pltpu.roll(z, W - HALF_DIM, 1), pltpu.roll(z, HALF_DIM, 1))
        return z * c + swapped * s

    rc = min(256, n_sub)

    def subsequence(r, carry):
        def rope_chunk(i, carry):
            rows = pl.ds(r + i * (rc * dil), rc, stride=dil)
            sl = pl.ds(pl.multiple_of(i * rc, rc), rc)
            c = cos_ref[rows, :]
            s = sin_ref[rows, :]
            qs_ref[sl, :] = (rope(q_ref[0, rows, :], c, s) * SCALE).astype(BF16)
            ks_ref[sl, :] = rope(k_ref[0, rows, :], c, s).astype(BF16)
            vs_ref[sl, :] = v_ref[0, rows, :].astype(BF16)
            return carry

        lax.fori_loop(0, n_sub // rc, rope_chunk, 0)

        nblk = n_sub // qb
        unroll = next(u for u in (4, 2, 1) if nblk % u == 0)
        heads = W // HEAD_DIM

        def block_group(ig, carry):
            chains = []
            for u in range(unroll):
                q0 = pl.multiple_of((ig * unroll + u) * qb, qb)
                k0 = pl.multiple_of(jnp.clip(q0 - (kwin - qb) // 2, 0, n_sub - kwin), 64)
                q = qs_ref[pl.ds(q0, qb), :]
                kk = ks_ref[pl.ds(k0, kwin), :]
                jq = q0 + lax.broadcasted_iota(jnp.int32, (qb, kwin), 0)
                jk = k0 + lax.broadcasted_iota(jnp.int32, (qb, kwin), 1)
                ok = jnp.abs(jk - jq) <= half
                for h in range(heads):
                    hs = slice(h * HEAD_DIM, (h + 1) * HEAD_DIM)
                    chains.append((q0, k0, hs, jnp.where(ok, _dot_nt(q[:, hs], kk[:, hs]), NEG_INF)))
            probs = []
            for q0, k0, hs, s in chains:
                m = jnp.max(s, axis=-1, keepdims=True)
                p = jnp.exp(s - m)
                l = jnp.sum(p, axis=-1, keepdims=True)
                probs.append((q0, k0, hs, p.astype(BF16), l, m + jnp.log(l)))
            outs, lses = [], []
            for q0, k0, hs, p, l, lse in probs:
                outs.append(_dot(p, vs_ref[pl.ds(k0, kwin), hs]) / l)
                lses.append(jnp.broadcast_to(lse, (qb, HEAD_DIM)))
                if len(outs) == heads:
                    rows = pl.ds(r + q0 * dil, qb, stride=dil)
                    o_ref[0, rows, :] = jnp.concatenate(outs, axis=-1)
                    lse_ref[0, rows, :] = jnp.concatenate(lses, axis=-1)
                    outs, lses = [], []
            return carry

        lax.fori_loop(0, nblk // unroll, block_group, 0)
        return carry

    lax.fori_loop(0, dil, subsequence, 0)


def dilated_group(z3, q_blk, k_blk, v_blk, cos_t, sin_t, window, dil):
    B, T, ncols = z3.shape
    W = DIL_STEP_WIDTH
    steps = DIL_GROUP_WIDTH // W
    n_sub = T // dil
    half = window // (2 * dil)
    qb = min(128, n_sub)
    kwin = min(qb + 2 * half, n_sub)
    zspec = lambda blk: pl.BlockSpec((1, T, W), lambda b, s: (b, 0, blk * steps + s))
    tspec = pl.BlockSpec((T, W), lambda b, s: (0, 0))
    ospec = pl.BlockSpec((1, T, W), lambda b, s: (b, 0, s))
    oshape = jax.ShapeDtypeStruct((B, T, DIL_GROUP_WIDTH), F32)
    o, lse = pl.pallas_call(
        functools.partial(_dil_kernel, n_sub=n_sub, dil=dil, qb=qb, kwin=kwin, half=half),
        grid=(B, steps),
        in_specs=[zspec(q_blk), zspec(k_blk), zspec(v_blk), tspec, tspec],
        out_specs=[ospec, ospec],
        out_shape=[oshape, oshape],
        scratch_shapes=[pltpu.VMEM((n_sub, W), BF16)] * 3,
        compiler_params=_params("parallel", "parallel"),
        name=f"dilated_d{dil}",
    )(z3, z3, z3, cos_t, sin_t)
    return o.reshape(B * T, DIL_GROUP_WIDTH), lse.reshape(B * T, DIL_GROUP_WIDTH)


def _na_kernel(q_ref, k_ref, v_ref, tab_ref, o_ref, qs_ref, ks_ref, vs_ref, bias_ref, *, rows):
    nkeys = NA_KH * GRID_W
    pc = NA_KH * GRID_W

    for off in range(NA_KH):
        for h in range(NA_GROUP_HEADS):
            for ky in range(NA_KH):
                bias_ref[off, h, :, ky * GRID_W:(ky + 1) * GRID_W] = tab_ref[h, ky - off + NA_KH - 1]

    def prep(c, carry):
        sl = pl.ds(pl.multiple_of(c * pc, pc), pc)
        q = q_ref[0, sl, :] * SCALE
        k = k_ref[0, sl, :]
        v = v_ref[0, sl, :]
        for h in range(NA_GROUP_HEADS):
            hs = slice(h * HEAD_DIM, (h + 1) * HEAD_DIM)
            qs_ref[h, sl, :] = q[:, hs].astype(BF16)
            ks_ref[h, sl, :] = k[:, hs].astype(BF16)
            vs_ref[h, sl, :] = v[:, hs].astype(BF16)
        return carry

    lax.fori_loop(0, rows * GRID_W // pc, prep, 0)

    def row_group(rg, carry):
        chains = []
        for u in range(NA_ROW_UNROLL):
            r = rg * NA_ROW_UNROLL + u
            r0 = jnp.clip(r - NA_KH // 2, 0, rows - NA_KH)
            qrow = pl.ds(pl.multiple_of(r * GRID_W, GRID_W), GRID_W)
            krow = pl.ds(pl.multiple_of(r0 * GRID_W, GRID_W), nkeys)
            for h in range(NA_GROUP_HEADS):
                s = _dot_nt(qs_ref[h, qrow, :], ks_ref[h, krow, :]) + bias_ref[r - r0, h]
                chains.append((u, h, qrow, krow, s))
        probs = []
        for u, h, qrow, krow, s in chains:
            m = jnp.max(s, axis=-1, keepdims=True)
            p = jnp.exp(s - m)
            l = jnp.sum(p, axis=-1, keepdims=True)
            probs.append((u, h, qrow, krow, p.astype(BF16), l))
        outs = {}
        for u, h, qrow, krow, p, l in probs:
            outs.setdefault(u, []).append(_dot(p, vs_ref[h, krow, :]) / l)
            if h == NA_GROUP_HEADS - 1:
                o_ref[0, qrow, :] = jnp.concatenate(outs[u], axis=-1)
        return carry

    lax.fori_loop(0, rows // NA_ROW_UNROLL, row_group, 0)


def na_bias_table(rel_bias):
    qc = np.arange(GRID_W)
    kc = np.arange(GRID_W)
    wc0 = np.clip(qc - NA_KW // 2, 0, GRID_W - NA_KW)
    col_ok = (kc[None, :] >= wc0[:, None]) & (kc[None, :] < wc0[:, None] + NA_KW)
    dx_idx = np.clip(kc[None, :] - qc[:, None], 1 - NA_KW, NA_KW - 1) + NA_KW - 1
    onehot = jnp.asarray(dx_idx[None] == np.arange(2 * NA_KW - 1)[:, None, None], F32)
    tab = jnp.einsum('hyd,dqk->hyqk', rel_bias.astype(F32), onehot, precision=HIGHEST)
    return jnp.where(jnp.asarray(col_ok)[None, None], tab, NEG_INF)


def neighbourhood(z3, q_blk0, k_blk0, v_blk0, bias_tab):
    B, T, ncols = z3.shape
    rows = T // GRID_W
    assert rows >= NA_KH
    W = NA_GROUP_WIDTH
    G = NA_HEADS // NA_GROUP_HEADS
    zspec = lambda blk0: pl.BlockSpec((1, T, W), lambda b, g: (b, 0, blk0 + g))
    return pl.pallas_call(
        functools.partial(_na_kernel, rows=rows),
        grid=(B, G),
        in_specs=[zspec(q_blk0), zspec(k_blk0), zspec(v_blk0),
                  pl.BlockSpec((NA_GROUP_HEADS, 2 * NA_KH - 1, GRID_W, GRID_W), lambda b, g: (g, 0, 0, 0))],
        out_specs=pl.BlockSpec((1, T, W), lambda b, g: (b, 0, g)),
        out_shape=jax.ShapeDtypeStruct((B, T, NA_WIDTH), F32),
        scratch_shapes=[pltpu.VMEM((NA_GROUP_HEADS, T, HEAD_DIM), BF16)] * 3
                       + [pltpu.VMEM((NA_KH, NA_GROUP_HEADS, GRID_W, NA_KH * GRID_W), F32)],
        compiler_params=_params("parallel", "parallel"),
        name="neighbourhood",
    )(z3, z3, z3, bias_tab)


def _merge_kernel(ya_ref, o1_ref, o2_ref, o3_ref, l1_ref, l2_ref, l3_ref, yc_ref, ga_ref, gb_ref, gc_ref,
                  wa_ref, wb_ref, wc_ref, out_ref, ya16_ref, yb16_ref, yc16_ref, *, bm):
    @pl.when(pl.program_id(1) == 0)
    def _():
        l1, l2, l3 = l1_ref[...], l2_ref[...], l3_ref[...]
        m = jnp.maximum(jnp.maximum(l1, l2), l3)
        e1, e2, e3 = jnp.exp(l1 - m), jnp.exp(l2 - m), jnp.exp(l3 - m)
        yb = (e1 * o1_ref[...] + e2 * o2_ref[...] + e3 * o3_ref[...]) / (e1 + e2 + e3)
        yb16_ref[...] = yb.astype(BF16)
        ya16_ref[...] = ya_ref[...].astype(BF16)
        yc16_ref[...] = yc_ref[...].astype(BF16)

    hb = bm // 2
    halves = [slice(s * hb, (s + 1) * hb) for s in range(2)]
    prods = [(_dot(ya16_ref[rs, :], wa_ref[...]), _dot(yb16_ref[rs, :], wb_ref[...]),
              _dot(yc16_ref[rs, :], wc_ref[...])) for rs in halves]
    gate = lambda ref, rs: _sigmoid(ref[rs, :]).astype(F32)
    for rs, (pa, pb, pc) in zip(halves, prods):
        out = gate(ga_ref, rs) * pa + gate(gb_ref, rs) * pb + gate(gc_ref, rs) * pc
        out_ref[rs, :] = out.astype(BF16)


def merge(ya, dil_outs, yc, z, layer, wa, wb, wc, bm, bn):
    M = ya.shape[0]
    D = wa.shape[-1]
    nj = D // bn
    tok = lambda w: pl.BlockSpec((bm, w), lambda i, j: (i, 0))
    gate = lambda g: pl.BlockSpec((bm, bn), lambda i, j: (i, g * nj + j))
    wsp = lambda k: pl.BlockSpec((None, k, bn), lambda i, j: (layer, 0, j))
    (o1, l1), (o2, l2), (o3, l3) = dil_outs
    W = DIL_GROUP_WIDTH
    return pl.pallas_call(
        functools.partial(_merge_kernel, bm=bm),
        grid=(M // bm, nj),
        in_specs=[tok(RW_WIDTH)] + [tok(W)] * 6 + [tok(NA_WIDTH), gate(0), gate(1), gate(2),
                                                  wsp(RW_WIDTH), wsp(W), wsp(NA_WIDTH)],
        out_specs=pl.BlockSpec((bm, bn), lambda i, j: (i, j)),
        out_shape=jax.ShapeDtypeStruct((M, D), BF16),
        scratch_shapes=[pltpu.VMEM((bm, RW_WIDTH), BF16), pltpu.VMEM((bm, W), BF16),
                        pltpu.VMEM((bm, NA_WIDTH), BF16)],
        compiler_params=_params("parallel", "arbitrary"),
        name="merge",
    )(ya, o1, o2, o3, l1, l2, l3, yc, z, z, z, wa, wb, wc)


def _matmul_res_kernel(a_ref, w_ref, x_ref, o_ref):
    o_ref[...] = x_ref[...] + _dot(a_ref[...], w_ref[...])


def matmul_res(a, w, layer, x, bm, bn):
    M, K = a.shape
    N = w.shape[-1]
    return pl.pallas_call(
        _matmul_res_kernel,
        grid=(M // bm, N // bn),
        in_specs=[pl.BlockSpec((bm, K), lambda i, j: (i, 0)),
                  pl.BlockSpec((None, K, bn), lambda i, j: (layer, 0, j)),
                  pl.BlockSpec((bm, bn), lambda i, j: (i, j))],
        out_specs=pl.BlockSpec((bm, bn), lambda i, j: (i, j)),
        out_shape=jax.ShapeDtypeStruct((M, N), F32),
        compiler_params=_params("parallel", "parallel"),
        name="out_proj",
    )(a, w, x)


def _gelu(x):
    return 0.5 * x * (1.0 + jnp.tanh(np.sqrt(2.0 / np.pi).astype(np.float32) * (x + 0.044715 * (x * x * x))))


FFN_HALO = 16
FFN_SPLIT = 2


def _ffn_kernel(x_ref, xp_ref, xn_ref, g_ref, wg_ref, wu_ref, cw_ref, cb_ref, wd_ref, o_ref,
                h_ref, gs_ref, acc_ref, *, bm, bf, seq):
    i = pl.program_id(0)
    f = pl.program_id(1)
    HL = FFN_HALO

    @pl.when(f == 0)
    def _():
        g = g_ref[...]
        h_ref[0:HL, :] = _rms(xp_ref[...], g).astype(BF16)
        h_ref[HL:bm + HL, :] = _rms(x_ref[...], g).astype(BF16)
        h_ref[bm + HL:bm + 2 * HL, :] = _rms(xn_ref[...], g).astype(BF16)
        acc_ref[...] = jnp.zeros_like(acc_ref)

    pos = (i * bm) % seq
    prev_ok = (pos != 0).astype(F32)
    next_ok = (pos + bm != seq).astype(F32)
    hw = bf // FFN_SPLIT
    h_ext = h_ref[...]
    h = h_ref[HL:bm + HL, :]
    cw = cw_ref[...]
    cb = cb_ref[...]
    gates, ups = [], []
    for s in range(FFN_SPLIT):
        cs = slice(s * hw, (s + 1) * hw)
        gates.append(_dot(h_ext, wg_ref[:, cs]))
        ups.append(_dot(h, wu_ref[:, cs]))
    acts = []
    for s in range(FFN_SPLIT):
        cs = slice(s * hw, (s + 1) * hw)
        gs_ref[s] = gates[s]
        gs_ref[s, HL - 8:HL, :] = gs_ref[s, HL - 8:HL, :] * prev_ok
        gs_ref[s, bm + HL:bm + HL + 8, :] = gs_ref[s, bm + HL:bm + HL + 8, :] * next_ok
        gc = (gs_ref[s, HL - 1:bm + HL - 1, :] * cw[0:1, cs] + gs_ref[s, HL:bm + HL, :] * cw[1:2, cs]
              + gs_ref[s, HL + 1:bm + HL + 1, :] * cw[2:3, cs] + cb[:, cs])
        acts.append((_gelu(gc) * ups[s]).astype(BF16))
    upd = _dot(acts[0], wd_ref[0:hw, :])
    for s in range(1, FFN_SPLIT):
        upd = upd + _dot(acts[s], wd_ref[s * hw:(s + 1) * hw, :])
    acc_ref[...] += upd

    @pl.when(f == pl.num_programs(1) - 1)
    def _():
        o_ref[...] = x_ref[...] + acc_ref[...]


def ffn(x, g, layer, wg, wu, cw, cb, wd, seq, bm, bf):
    M, D = x.shape
    F = wg.shape[-1]
    HL = FFN_HALO
    nbh = M // HL
    return pl.pallas_call(
        functools.partial(_ffn_kernel, bm=bm, bf=bf, seq=seq),
        grid=(M // bm, F // bf),
        in_specs=[pl.BlockSpec((bm, D), lambda i, f: (i, 0)),
                  pl.BlockSpec((HL, D), lambda i, f: (jnp.maximum(i * (bm // HL) - 1, 0), 0)),
                  pl.BlockSpec((HL, D), lambda i, f: (jnp.minimum((i + 1) * (bm // HL), nbh - 1), 0)),
                  pl.BlockSpec((1, D), lambda i, f: (0, 0)),
                  pl.BlockSpec((None, D, bf), lambda i, f: (layer, 0, f)),
                  pl.BlockSpec((None, D, bf), lambda i, f: (layer, 0, f)),
                  pl.BlockSpec((3, bf), lambda i, f: (0, f)),
                  pl.BlockSpec((1, bf), lambda i, f: (0, f)),
                  pl.BlockSpec((None, bf, D), lambda i, f: (layer, f, 0))],
        out_specs=pl.BlockSpec((bm, D), lambda i, f: (i, 0)),
        out_shape=jax.ShapeDtypeStruct((M, D), F32),
        scratch_shapes=[pltpu.VMEM((bm + 2 * HL, D), BF16),
                        pltpu.VMEM((FFN_SPLIT, bm + 2 * HL, bf // FFN_SPLIT), F32),
                        pltpu.VMEM((bm, D), F32)],
        compiler_params=_params("parallel", "arbitrary"),
        name="conv_ffn",
    )(x, x, x, g, wg, wu, cw, cb, wd)


def _ple_kernel(x_ref, p_ref, g_ref, wg_ref, wp_ref, gf_ref, o_ref, *, final_norm):
    x = x_ref[...]
    h = _rms(x, g_ref[...]).astype(BF16)
    gate = _sigmoid(_dot(h, wg_ref[...]))
    y = x + gate * _dot(p_ref[...].astype(BF16), wp_ref[...])
    if final_norm:
        y = _rms(y, gf_ref[...])
    o_ref[...] = y


def ple(x, p, layer, g, wg, wp, gf, bm, final_norm):
    M, D = x.shape
    P = p.shape[-1]
    return pl.pallas_call(
        functools.partial(_ple_kernel, final_norm=final_norm),
        grid=(M // bm,),
        in_specs=[pl.BlockSpec((bm, D), lambda i: (i, 0)),
                  pl.BlockSpec((None, bm, P), lambda i: (layer, i, 0)),
                  pl.BlockSpec((1, D), lambda i: (0, 0)),
                  pl.BlockSpec((None, D, D), lambda i: (layer, 0, 0)),
                  pl.BlockSpec((None, P, D), lambda i: (layer, 0, 0)),
                  pl.BlockSpec((1, D), lambda i: (0, 0))],
        out_specs=pl.BlockSpec((bm, D), lambda i: (i, 0)),
        out_shape=jax.ShapeDtypeStruct((M, D), F32),
        compiler_params=_params("parallel"),
        name="ple_final" if final_norm else "ple",
    )(x, p, g, wg, wp, gf)


def _to_scan_kernel(x0_ref, x1_ref, rev_ref, o_ref, a_ref, *, nb, tb, dup, pitch, apitch):
    N = HEAD_DIM
    NV = N // 2
    lanes = 2 * 2 * nb * RW_HEADS
    n_rows = N if dup else NV
    rev = rev_ref[...]
    for d, ref in enumerate((x0_ref, x1_ref)):
        for b in range(nb):
            x = ref[b]
            if d == 1:
                x = _dot_hi(rev, x)
            xt = x.T
            for h in range(RW_HEADS):
                slot = ((d * nb + b) * RW_HEADS + h) * 2
                for s in range(2):
                    blk = xt[h * N:(h + 1) * N] if dup else xt[h * N + s * NV:h * N + (s + 1) * NV]
                    a_ref[(slot + s) * apitch:(slot + s) * apitch + n_rows, :] = blk
    for j in range(n_rows):
        rows = a_ref[pl.ds(j, lanes, stride=apitch), :]
        o_ref[pl.ds(j, tb, stride=pitch), :] = rows.T
    for j in range(n_rows, pitch):
        o_ref[pl.ds(j, tb, stride=pitch), :] = jnp.zeros((tb, lanes), F32)


def to_scan(x0, x1, rev, B, T, tb, dup, pitch):
    C = RW_WIDTH
    nblk = T // tb
    lanes = 4 * B * RW_HEADS
    apitch = (HEAD_DIM if dup else HEAD_DIM // 2) + 4
    return pl.pallas_call(
        functools.partial(_to_scan_kernel, nb=B, tb=tb, dup=dup, pitch=pitch, apitch=apitch),
        grid=(nblk,),
        in_specs=[pl.BlockSpec((B, tb, C), lambda i: (0, i, 0)),
                  pl.BlockSpec((B, tb, C), lambda i: (0, nblk - 1 - i, 0)),
                  pl.BlockSpec((tb, tb), lambda i: (0, 0))],
        out_specs=pl.BlockSpec((tb * pitch, lanes), lambda i: (i, 0)),
        out_shape=jax.ShapeDtypeStruct((T * pitch, lanes), F32),
        scratch_shapes=[pltpu.VMEM((lanes * apitch, tb), F32)],
        compiler_params=_params("parallel"),
        name="to_scan_keys" if dup else "to_scan_values",
    )(x0.reshape(B, T, C), x1.reshape(B, T, C), rev)


def _from_scan_kernel(y_ref, rev_ref, yf_ref, yb_ref, a_ref, *, nb, tb, pitch, apitch):
    NV = HEAD_DIM // 2
    lanes = 2 * 2 * nb * RW_HEADS
    for ih in range(NV):
        rows = y_ref[pl.ds(ih, tb, stride=pitch), :]
        a_ref[pl.ds(ih, lanes, stride=apitch), :] = rows.T
    rev = rev_ref[...]
    for d, ref in enumerate((yf_ref, yb_ref)):
        for b in range(nb):
            slot0 = (d * nb + b) * RW_HEADS * 2
            pieces = [a_ref[(slot0 + s) * apitch:(slot0 + s) * apitch + NV, :] for s in range(2 * RW_HEADS)]
            y = jnp.concatenate(pieces, axis=0).T
            if d == 1:
                y = _dot_hi(rev, y)
            ref[b] = y


def from_scan(y, rev, B, T, tb, pitch):
    C = RW_WIDTH
    nblk = T // tb
    lanes = y.shape[-1]
    apitch = HEAD_DIM // 2 + 4
    out = jax.ShapeDtypeStruct((B, T, C), F32)
    yf, yb = pl.pallas_call(
        functools.partial(_from_scan_kernel, nb=B, tb=tb, pitch=pitch, apitch=apitch),
        grid=(nblk,),
        in_specs=[pl.BlockSpec((tb * pitch, lanes), lambda i: (i, 0)),
                  pl.BlockSpec((tb, tb), lambda i: (0, 0))],
        out_specs=[pl.BlockSpec((B, tb, C), lambda i: (0, i, 0)),
                   pl.BlockSpec((B, tb, C), lambda i: (0, nblk - 1 - i, 0))],
        out_shape=[out, out],
        scratch_shapes=[pltpu.VMEM((lanes * apitch, tb), F32)],
        compiler_params=_params("parallel"),
        name="from_scan",
    )(y, rev)
    return yf.reshape(B * T, C), yb.reshape(B * T, C)


def _rope_tables(T):
    inv = ROPE_THETA ** (-jnp.arange(0, HEAD_DIM, 2, dtype=jnp.float32) / HEAD_DIM)
    ang = jnp.arange(T, dtype=jnp.float32)[:, None] * inv[None, :]
    cos, sin = jnp.cos(ang), jnp.sin(ang)
    cos_t = jnp.tile(jnp.concatenate([cos, cos], axis=-1), (1, DIL_STEP_WIDTH // HEAD_DIM))
    sin_t = jnp.tile(jnp.concatenate([-sin, sin], axis=-1), (1, DIL_STEP_WIDTH // HEAD_DIM))
    return cos_t, sin_t


def _pick(n, prefs):
    for c in prefs:
        if n % c == 0:
            return c
    return n


def kernel(x, p, norm_mix, w_in, rw_mu, rw_w0, rw_w2, rw_a0, rw_a2, rw_g2, rw_k_k, rw_k_a, rw_r_k, rw_lnx_g,
           rw_lnx_b, na_bias, w_br_a, w_br_b, w_br_c, w_out, norm_ffn, w_ffn_gate, w_ffn_up, ffn_conv_w,
           ffn_conv_b, w_ffn_down, norm_ple, w_ple_gate, w_ple, norm_final):
    B, T, D = x.shape
    depth = w_in.shape[0]
    M = B * T
    F = w_ffn_gate.shape[-1]
    C = RW_WIDTH
    W = DIL_GROUP_WIDTH
    assert (N_BRANCH * D) % W == 0 and w_in.shape[-1] == RW_COLS + 2 * 3 * DIL_WIDTH + N_BRANCH * D

    gates_w = N_BRANCH * D
    dil_w = 3 * DIL_WIDTH
    na_w = 3 * NA_WIDTH
    nb_cols = gates_w + na_w
    nf_cols = RW_COLS_PAD + dil_w
    na_blk0 = gates_w // W
    dil_blk0 = RW_COLS_PAD // W
    assert RW_COLS_PAD % W == 0

    cos_t, sin_t = _rope_tables(T)
    seg = jnp.asarray(np.kron(np.eye(RW_HEADS), np.ones((HEAD_DIM, HEAD_DIM))), BF16)
    xf = x.reshape(M, D)
    p3 = p.reshape(depth, M, p.shape[-1])
    bm_in = _pick(M, (1024, 512, 256, 128))
    bt_rw = _pick(T, (256, 128))
    tb = _pick(T, (64, 32))
    tt = _pick(T, (128,))
    rev = jnp.asarray(np.eye(tt)[::-1], F32)

    wb16 = {name: wt.astype(BF16) for name, wt in (
        ("br_a", w_br_a), ("br_b", w_br_b), ("br_c", w_br_c), ("out", w_out), ("ffn_gate", w_ffn_gate),
        ("ffn_up", w_ffn_up), ("ffn_down", w_ffn_down), ("ple_gate", w_ple_gate), ("ple", w_ple))}
    o_dil, o_na, o_g = RW_COLS, RW_COLS + dil_w, RW_COLS + dil_w + na_w
    w_b = jnp.concatenate([w_in[:, :, o_g:].astype(BF16), w_in[:, :, o_na:o_g].astype(BF16)], axis=2)
    w_f = jnp.concatenate([w_in[:, :, :RW_COLS].astype(BF16), jnp.zeros((depth, D, RW_COLS_PAD - RW_COLS), BF16),
                           w_in[:, :, o_dil:o_na].astype(BF16)], axis=2)

    for i in range(depth):
        g_mix = norm_mix[i][None]
        zb = norm_matmul(xf, g_mix, w_b, i, bm_in, _pick(nb_cols, (768, 256, 128)), BF16, "in_proj_bf16")
        zf = norm_matmul(xf, g_mix, w_f, i, bm_in, _pick(nf_cols, (1024, 256, 128)), F32, "in_proj_f32")
        zb3 = zb.reshape(B, T, nb_cols)
        zf3 = zf.reshape(B, T, nf_cols)

        mu = jnp.pad(rw_mu[i], (0, RW_COLS_PAD - RW_COLS))[None]
        r, v, a, dec0, dec1, kd0, kd1, b0, b1, gate = rw_prep(
            zf, 0, mu, rw_w0[i], rw_w2[i], rw_a0[i], rw_a2[i], rw_g2[i], rw_k_k[i][None], rw_k_a[i][None],
            seg, T, bt_rw)
        keys = lambda x0, x1: to_scan(x0, x1, rev, B, T, tt, dup=True, pitch=KEY_PITCH)
        vals = to_scan(v, v, rev, B, T, tt, dup=False, pitch=VAL_PITCH)
        y = rw_scan(keys(dec0, dec1), keys(kd0, kd1), keys(a, a), keys(b0, b1), keys(r, r), vals, T, tb,
                    KEY_PITCH, VAL_PITCH)
        yf, yb = from_scan(y, rev, B, T, tt, VAL_PITCH)
        ya = rw_post(yf, yb, r, v, kd0, kd1, gate, rw_r_k[i].reshape(1, C), rw_lnx_g[i][None], rw_lnx_b[i][None],
                     seg, bt_rw)

        dil_outs = []
        for g, (window, dil) in enumerate(DIL_PATTERNS):
            dil_outs.append(dilated_group(zf3, dil_blk0 + g, dil_blk0 + 3 + g, dil_blk0 + 6 + g, cos_t, sin_t,
                                          window, dil))

        yc = neighbourhood(zb3, na_blk0, na_blk0 + 3, na_blk0 + 6, na_bias_table(na_bias[i])).reshape(M, NA_WIDTH)

        merged = merge(ya, dil_outs, yc, zb, i, wb16["br_a"], wb16["br_b"], wb16["br_c"],
                       _pick(M, (512, 256, 128)), _pick(D, (512, 256, 128)))
        xf = matmul_res(merged, wb16["out"], i, xf, _pick(M, (512, 256, 128)), D)

        xf = ffn(xf, norm_ffn[i][None], i, wb16["ffn_gate"], wb16["ffn_up"], ffn_conv_w[i],
                 ffn_conv_b[i][None], wb16["ffn_down"], T, _pick(T, (512, 256, 128)),
                 _pick(F, (512, 256, 128)))

        xf = ple(xf, p3, i, norm_ple[i][None], wb16["ple_gate"], wb16["ple"],
                 norm_final[None], _pick(M, (512, 256, 128)), final_norm=(i == depth - 1))
    return xf.reshape(B, T, D)
```

```python
import functools

import numpy as np
import jax
import jax.numpy as jnp
from jax import lax
from jax.experimental import pallas as pl
from jax.experimental.pallas import tpu as pltpu

HEAD_DIM = 64
HALF_DIM = HEAD_DIM // 2
SCALE = HEAD_DIM ** -0.5
NORM_EPS = 1e-6
ROPE_THETA = 10000.0
NEG_INF = -1e30

RW_HEADS = 8
RW_WIDTH = RW_HEADS * HEAD_DIM
RW_DECAY_LORA = 32
RW_AAA_LORA = 32
RW_GATE_LORA = 96
RW_LNX_EPS = 64e-5
RW_COLS = 3 * RW_WIDTH + 2 * RW_DECAY_LORA + 2 * RW_AAA_LORA + RW_GATE_LORA
RW_COLS_PAD = 1792
KEY_PITCH = HEAD_DIM + 4
VAL_PITCH = HEAD_DIM // 2 + 8
SCAN_UNROLL = 32

DIL_PATTERNS = ((128, 1), (512, 4), (2048, 16))
DIL_HEADS_PER_GROUP = 4
DIL_GROUP_WIDTH = DIL_HEADS_PER_GROUP * HEAD_DIM
DIL_WIDTH = len(DIL_PATTERNS) * DIL_GROUP_WIDTH
DIL_STEP_WIDTH = 2 * HEAD_DIM

GRID_W = 64
NA_HEADS = 12
NA_WIDTH = NA_HEADS * HEAD_DIM
NA_KH = 8
NA_KW = 16
NA_GROUP_HEADS = 4
NA_GROUP_WIDTH = NA_GROUP_HEADS * HEAD_DIM
NA_ROW_UNROLL = 4

N_BRANCH = 3
VMEM_LIMIT = 56 * 1024 * 1024

BF16 = jnp.bfloat16
F32 = jnp.float32
HIGHEST = lax.Precision.HIGHEST


def _params(*sem):
    return pltpu.CompilerParams(dimension_semantics=sem, vmem_limit_bytes=VMEM_LIMIT)


def _rms(x, g):
    ms = jnp.mean(x * x, axis=-1, keepdims=True)
    return x * lax.rsqrt(ms + NORM_EPS) * g


def _sigmoid(x):
    return 0.5 * jnp.tanh(0.5 * x) + 0.5


def _dot(a, b):
    return jnp.dot(a, b, preferred_element_type=F32)


def _dot_nt(a, b):
    return lax.dot_general(a, b, (((1,), (1,)), ((), ())), preferred_element_type=F32)


def _dot_hi(a, b):
    return jnp.dot(a, b, preferred_element_type=F32, precision=HIGHEST)


def _split_hi_lo(w):
    hi = w.astype(BF16)
    return jnp.stack([hi, (w - hi.astype(F32)).astype(BF16)])


def _dot_split(a, w_hi, w_lo):
    a_hi = a.astype(BF16)
    a_lo = (a - a_hi.astype(F32)).astype(BF16)
    return _dot(a_hi, w_hi) + _dot(a_hi, w_lo) + _dot(a_lo, w_hi)


def _seg_sum(x, seg):
    hi = x.astype(BF16)
    rest = x - hi.astype(F32)
    mid = rest.astype(BF16)
    lo = (rest - mid.astype(F32)).astype(BF16)
    return _dot(hi, seg) + _dot(mid, seg) + _dot(lo, seg)


def _norm_matmul_kernel(x_ref, g_ref, w_ref, o_ref, h_ref):
    @pl.when(pl.program_id(1) == 0)
    def _():
        h_ref[...] = _rms(x_ref[...], g_ref[...]).astype(BF16)

    o_ref[...] = _dot_nt(h_ref[...], w_ref[...]).astype(o_ref.dtype)


def norm_matmul(x, g, w_t, layer, bm, bn, out_dtype, name):
    M, K = x.shape
    N = w_t.shape[1]
    return pl.pallas_call(
        _norm_matmul_kernel,
        grid=(M // bm, N // bn),
        in_specs=[pl.BlockSpec((bm, K), lambda i, j: (i, 0)),
                  pl.BlockSpec((1, K), lambda i, j: (0, 0)),
                  pl.BlockSpec((None, bn, K), lambda i, j: (layer, j, 0))],
        out_specs=pl.BlockSpec((bm, bn), lambda i, j: (i, j)),
        out_shape=jax.ShapeDtypeStruct((M, N), out_dtype),
        scratch_shapes=[pltpu.VMEM((bm, K), BF16)],
        compiler_params=_params("parallel", "arbitrary"),
        name=name,
    )(x, g, w_t)


DECAY_SCALE = float(np.exp(-0.5))


def _rw_prep_kernel(c_ref, cp_ref, cn_ref, mu_ref, w0_ref, w2_ref, a0_ref, a2_ref, g2_ref, kk_ref, ka_ref,
                    seg_ref, r_out, v_out, a_out, dec0_out, dec1_out, kd0_out, kd1_out, b0_out, b1_out,
                    gate_out, buf_ref, *, bt, seq):
    i = pl.program_id(0)
    pos = (i * bt) % seq
    prev_ok = (pos != 0).astype(F32)
    next_ok = (pos + bt != seq).astype(F32)
    buf_ref[0:8, :] = cp_ref[...] * prev_ok
    buf_ref[8:bt + 8, :] = c_ref[...]
    buf_ref[bt + 8:bt + 16, :] = cn_ref[...] * next_ok
    mu = mu_ref[...]
    c = c_ref[...] * (1.0 - mu) + (buf_ref[7:bt + 7, :] + buf_ref[9:bt + 9, :]) * (0.5 * mu)
    C = RW_WIDTH
    r = c[:, 0:C]
    k = c[:, C:2 * C]
    v = c[:, 2 * C:3 * C]
    o = 3 * C
    wd = c[:, o:o + 2 * RW_DECAY_LORA]
    o += 2 * RW_DECAY_LORA
    ad = c[:, o:o + 2 * RW_AAA_LORA]
    o += 2 * RW_AAA_LORA
    gd = c[:, o:o + RW_GATE_LORA]
    seg = seg_ref[...]
    kk = k * kk_ref[...]
    kk = kk * lax.rsqrt(jnp.maximum(_seg_sum(kk * kk, seg), 1e-24))
    r_out[...] = r
    v_out[...] = v
    a_out[...] = -kk
    gate_out[...] = _dot_split(_sigmoid(gd), g2_ref[0], g2_ref[1])
    twd = jnp.tanh(wd)
    for d, (dec_out, kd_out, b_out) in enumerate(((dec0_out, kd0_out, b0_out), (dec1_out, kd1_out, b1_out))):
        lw = _dot_split(twd[:, d * RW_DECAY_LORA:(d + 1) * RW_DECAY_LORA], w2_ref[0, d], w2_ref[1, d])
        dec_out[...] = jnp.exp(-DECAY_SCALE * _sigmoid(w0_ref[d:d + 1, :] + lw))
        la = _dot_split(ad[:, d * RW_AAA_LORA:(d + 1) * RW_AAA_LORA], a2_ref[0, d], a2_ref[1, d])
        lr = _sigmoid(a0_ref[d:d + 1, :] + la)
        kd_out[...] = k * (1.0 + (lr - 1.0) * ka_ref[...])
        b_out[...] = kk * lr


def rw_prep(z, rw_col_block, mu, w0, w2, a0, a2, g2, k_k, k_a, seg, seq, bt):
    M = z.shape[0]
    C = RW_WIDTH
    W = RW_COLS_PAD
    nb8 = M // 8
    full = lambda shape: pl.BlockSpec(shape, lambda i: (0,) * len(shape))
    out = jax.ShapeDtypeStruct((M, C), F32)
    return pl.pallas_call(
        functools.partial(_rw_prep_kernel, bt=bt, seq=seq),
        grid=(M // bt,),
        in_specs=[pl.BlockSpec((bt, W), lambda i: (i, rw_col_block)),
                  pl.BlockSpec((8, W), lambda i: (jnp.maximum(i * (bt // 8) - 1, 0), rw_col_block)),
                  pl.BlockSpec((8, W), lambda i: (jnp.minimum((i + 1) * (bt // 8), nb8 - 1), rw_col_block)),
                  full((1, W)), full((2, C)), full((2, 2, RW_DECAY_LORA, C)), full((2, C)),
                  full((2, 2, RW_AAA_LORA, C)), full((2, RW_GATE_LORA, C)), full((1, C)), full((1, C)),
                  full((C, C))],
        out_specs=[pl.BlockSpec((bt, C), lambda i: (i, 0))] * 10,
        out_shape=[out] * 10,
        scratch_shapes=[pltpu.VMEM((bt + 16, W), F32)],
        compiler_params=_params("parallel"),
        name="rw_prep",
    )(z, z, z, mu, w0, _split_hi_lo(w2), a0, _split_hi_lo(a2), _split_hi_lo(g2), k_k, k_a, seg)


def _rw_scan_kernel(w_ref, k_ref, a_ref, b_ref, r_ref, v_ref, y_ref, s_ref, *, tb, pitch, vpitch):
    N = HEAD_DIM
    NV = N // 2

    @pl.when(pl.program_id(0) == 0)
    def _():
        s_ref[...] = jnp.zeros_like(s_ref)

    zero = jnp.zeros(s_ref.shape[1:], F32)

    def dot_a(c, acc):
        for u in range(SCAN_UNROLL):
            j = c * SCAN_UNROLL + u
            acc = acc + s_ref[j] * a_ref[pl.ds(j, 1), :]
        return acc

    sa_first = lax.fori_loop(0, N // SCAN_UNROLL, dot_a, zero)

    def step(t, sa):
        base = t * pitch
        nbase = jnp.minimum(t + 1, tb - 1) * pitch
        vbase = pl.multiple_of(t * vpitch, 8)
        row = lambda ref, j: ref[pl.ds(base + j, 1), :]
        vt = v_ref[pl.ds(vbase, NV), :]

        def update(c, acc):
            y0, y1, n0, n1 = acc
            for u in range(SCAN_UNROLL):
                j = c * SCAN_UNROLL + u
                s = s_ref[j] * row(w_ref, j) + sa * row(b_ref, j) + vt * row(k_ref, j)
                s_ref[j] = s
                ty = s * row(r_ref, j)
                tn = s * a_ref[pl.ds(nbase + j, 1), :]
                if u % 2 == 0:
                    y0, n0 = y0 + ty, n0 + tn
                else:
                    y1, n1 = y1 + ty, n1 + tn
            return y0, y1, n0, n1

        y0, y1, n0, n1 = lax.fori_loop(0, N // SCAN_UNROLL, update, (zero, zero, zero, zero))
        y_ref[pl.ds(vbase, NV), :] = y0 + y1
        y_ref[pl.ds(vbase + NV, vpitch - NV), :] = jnp.zeros((vpitch - NV, zero.shape[1]), F32)
        return n0 + n1

    lax.fori_loop(0, tb, step, sa_first)


def rw_scan(w, k, a, b, r, v, T, tb, pitch, vpitch):
    L = v.shape[-1]
    kspec = pl.BlockSpec((tb * pitch, L), lambda i: (i, 0))
    vspec = pl.BlockSpec((tb * vpitch, L), lambda i: (i, 0))
    return pl.pallas_call(
        functools.partial(_rw_scan_kernel, tb=tb, pitch=pitch, vpitch=vpitch),
        grid=(T // tb,),
        in_specs=[kspec] * 5 + [vspec],
        out_specs=vspec,
        out_shape=jax.ShapeDtypeStruct((T * vpitch, L), F32),
        scratch_shapes=[pltpu.VMEM((HEAD_DIM, HEAD_DIM // 2, L), F32)],
        compiler_params=_params("arbitrary"),
        name="rw_scan",
    )(w, k, a, b, r, v)


def _rw_post_kernel(yf_ref, yb_ref, r_ref, v_ref, kd0_ref, kd1_ref, gate_ref, rk_ref, g_ref, b_ref, seg_ref,
                    o_ref):
    seg = seg_ref[...]
    y = yf_ref[...] + yb_ref[...]
    inv_n = 1.0 / HEAD_DIM
    mean = _seg_sum(y, seg) * inv_n
    d = y - mean
    var = _seg_sum(d * d, seg) * inv_n
    yn = d * lax.rsqrt(var + RW_LNX_EPS) * g_ref[...] + b_ref[...]
    r = r_ref[...]
    rk = rk_ref[...]
    bonus = _seg_sum(r * kd0_ref[...] * rk + r * kd1_ref[...] * rk, seg) * v_ref[...]
    o_ref[...] = (yn + bonus) * gate_ref[...]


def rw_post(yf, yb, r, v, kd0, kd1, gate, rk, g, b, seg, bt):
    M, C = yf.shape
    tok = pl.BlockSpec((bt, C), lambda i: (i, 0))
    row = pl.BlockSpec((1, C), lambda i: (0, 0))
    return pl.pallas_call(
        _rw_post_kernel,
        grid=(M // bt,),
        in_specs=[tok] * 7 + [row] * 3 + [pl.BlockSpec((C, C), lambda i: (0, 0))],
        out_specs=tok,
        out_shape=jax.ShapeDtypeStruct((M, C), F32),
        compiler_params=_params("parallel"),
        name="rw_post",
    )(yf, yb, r, v, kd0, kd1, gate, rk, g, b, seg)


def _dil_kernel(q_ref, k_ref, v_ref, cos_ref, sin_ref, o_ref, lse_ref, qs_ref, ks_ref, vs_ref, *, n_sub, dil, qb,
                kwin, half):
    W = DIL_STEP_WIDTH
    lane = lax.broadcasted_iota(jnp.int32, (1, W), 1)
    first_half = (lane % HEAD_DIM) < HALF_DIM

    def rope(z, c, s):
        swapped = jnp.where(first_half, pltpu.roll(z, W - HALF_DIM, 1), pltpu.roll(z, HALF_DIM, 1))
        return z * c + swapped * s

    rc = min(256, n_sub)

    def subsequence(r, carry):
        def rope_chunk(i, carry):
            rows = pl.ds(r + i * (rc * dil), rc, stride=dil)
            sl = pl.ds(pl.multiple_of(i * rc, rc), rc)
            c = cos_ref[rows, :]
            s = sin_ref[rows, :]
            qs_ref[sl, :] = (rope(q_ref[0, rows, :], c, s) * SCALE).astype(BF16)
            ks_ref[sl, :] = rope(k_ref[0, rows, :], c, s).astype(BF16)
            vs_ref[sl, :] = v_ref[0, rows, :].astype(BF16)
            return carry

        lax.fori_loop(0, n_sub // rc, rope_chunk, 0)

        nblk = n_sub // qb
        unroll = next(u for u in (4, 2, 1) if nblk % u == 0)
        low_head = lane < HEAD_DIM

        def block_group(ig, carry):
            chains = []
            for u in range(unroll):
                q0 = pl.multiple_of((ig * unroll + u) * qb, qb)
                k0 = pl.multiple_of(jnp.clip(q0 - (kwin - qb) // 2, 0, n_sub - kwin), 64)
                q = qs_ref[pl.ds(q0, qb), :]
                kk = ks_ref[pl.ds(k0, kwin), :]
                jq = q0 + lax.broadcasted_iota(jnp.int32, (qb, kwin), 0)
                jk = k0 + lax.broadcasted_iota(jnp.int32, (qb, kwin), 1)
                ok = jnp.abs(jk - jq) <= half
                for h in range(2):
                    qh = jnp.where(low_head == (h == 0), q, jnp.zeros_like(q))
                    chains.append((q0, k0, jnp.where(ok, _dot_nt(qh, kk), NEG_INF)))
            probs = []
            for q0, k0, s in chains:
                m = jnp.max(s, axis=-1, keepdims=True)
                p = jnp.exp(s - m)
                l = jnp.sum(p, axis=-1, keepdims=True)
                probs.append((q0, k0, p.astype(BF16), l, m + jnp.log(l)))
            for c in range(0, len(probs), 2):
                (q0, k0, p0, l0, lse0), (_, _, p1, l1, lse1) = probs[c], probs[c + 1]
                vv = vs_ref[pl.ds(k0, kwin), :]
                rows = pl.ds(r + q0 * dil, qb, stride=dil)
                o_ref[0, rows, :] = jnp.where(low_head, _dot(p0, vv) / l0, _dot(p1, vv) / l1)
                lse_ref[0, rows, :] = jnp.where(low_head, lse0, lse1)
            return carry

        lax.fori_loop(0, nblk // unroll, block_group, 0)
        return carry

    lax.fori_loop(0, dil, subsequence, 0)


def dilated_group(z3, q_blk, k_blk, v_blk, cos_t, sin_t, window, dil):
    B, T, ncols = z3.shape
    W = DIL_STEP_WIDTH
    steps = DIL_GROUP_WIDTH // W
    n_sub = T // dil
    half = window // (2 * dil)
    qb = min(128, n_sub)
    kwin = min(qb + 2 * half, n_sub)
    zspec = lambda blk: pl.BlockSpec((1, T, W), lambda b, s: (b, 0, blk * steps + s))
    tspec = pl.BlockSpec((T, W), lambda b, s: (0, 0))
    ospec = pl.BlockSpec((1, T, W), lambda b, s: (b, 0, s))
    oshape = jax.ShapeDtypeStruct((B, T, DIL_GROUP_WIDTH), F32)
    o, lse = pl.pallas_call(
        functools.partial(_dil_kernel, n_sub=n_sub, dil=dil, qb=qb, kwin=kwin, half=half),
        grid=(B, steps),
        in_specs=[zspec(q_blk), zspec(k_blk), zspec(v_blk), tspec, tspec],
        out_specs=[ospec, ospec],
        out_shape=[oshape, oshape],
        scratch_shapes=[pltpu.VMEM((n_sub, W), BF16)] * 3,
        compiler_params=_params("parallel", "parallel"),
        name=f"dilated_d{dil}",
    )(z3, z3, z3, cos_t, sin_t)
    return o.reshape(B * T, DIL_GROUP_WIDTH), lse.reshape(B * T, DIL_GROUP_WIDTH)


def _na_kernel(q_ref, k_ref, v_ref, tab_ref, o_ref, qs_ref, ks_ref, vs_ref, bias_ref, *, rows):
    nkeys = NA_KH * GRID_W
    pc = NA_KH * GRID_W

    for off in range(NA_KH):
        for h in range(NA_GROUP_HEADS):
            for ky in range(NA_KH):
                bias_ref[off, h, :, ky * GRID_W:(ky + 1) * GRID_W] = tab_ref[h, ky - off + NA_KH - 1]

    PW = 2 * HEAD_DIM
    pairs = NA_GROUP_HEADS // 2
    low_head = lax.broadcasted_iota(jnp.int32, (1, PW), 1) < HEAD_DIM

    def prep(c, carry):
        sl = pl.ds(pl.multiple_of(c * pc, pc), pc)
        q = q_ref[0, sl, :] * SCALE
        k = k_ref[0, sl, :]
        v = v_ref[0, sl, :]
        for pp in range(pairs):
            ps = slice(pp * PW, (pp + 1) * PW)
            qs_ref[pp, sl, :] = q[:, ps].astype(BF16)
            ks_ref[pp, sl, :] = k[:, ps].astype(BF16)
            vs_ref[pp, sl, :] = v[:, ps].astype(BF16)
        return carry

    lax.fori_loop(0, rows * GRID_W // pc, prep, 0)

    def row_group(rg, carry):
        chains = []
        for u in range(NA_ROW_UNROLL):
            r = rg * NA_ROW_UNROLL + u
            r0 = jnp.clip(r - NA_KH // 2, 0, rows - NA_KH)
            qrow = pl.ds(pl.multiple_of(r * GRID_W, GRID_W), GRID_W)
            krow = pl.ds(pl.multiple_of(r0 * GRID_W, GRID_W), nkeys)
            for pp in range(pairs):
                q = qs_ref[pp, qrow, :]
                kk = ks_ref[pp, krow, :]
                for h in range(2):
                    qh = jnp.where(low_head == (h == 0), q, jnp.zeros_like(q))
                    chains.append((qrow, krow, pp, _dot_nt(qh, kk) + bias_ref[r - r0, 2 * pp + h]))
        probs = []
        for qrow, krow, pp, s in chains:
            m = jnp.max(s, axis=-1, keepdims=True)
            p = jnp.exp(s - m)
            l = jnp.sum(p, axis=-1, keepdims=True)
            probs.append((qrow, krow, pp, p.astype(BF16), l))
        outs = []
        for c in range(0, len(probs), 2):
            (qrow, krow, pp, p0, l0), (_, _, _, p1, l1) = probs[c], probs[c + 1]
            vv = vs_ref[pp, krow, :]
            outs.append(jnp.where(low_head, _dot(p0, vv) / l0, _dot(p1, vv) / l1))
            if len(outs) == pairs:
                o_ref[0, qrow, :] = jnp.concatenate(outs, axis=-1)
                outs = []
        return carry

    lax.fori_loop(0, rows // NA_ROW_UNROLL, row_group, 0)


def na_bias_table(rel_bias):
    qc = np.arange(GRID_W)
    kc = np.arange(GRID_W)
    wc0 = np.clip(qc - NA_KW // 2, 0, GRID_W - NA_KW)
    col_ok = (kc[None, :] >= wc0[:, None]) & (kc[None, :] < wc0[:, None] + NA_KW)
    dx_idx = np.clip(kc[None, :] - qc[:, None], 1 - NA_KW, NA_KW - 1) + NA_KW - 1
    onehot = jnp.asarray(dx_idx[None] == np.arange(2 * NA_KW - 1)[:, None, None], F32)
    tab = jnp.einsum('hyd,dqk->hyqk', rel_bias.astype(F32), onehot, precision=HIGHEST)
    return jnp.where(jnp.asarray(col_ok)[None, None], tab, NEG_INF)


def neighbourhood(z3, q_blk0, k_blk0, v_blk0, bias_tab):
    B, T, ncols = z3.shape
    rows = T // GRID_W
    assert rows >= NA_KH
    W = NA_GROUP_WIDTH
    G = NA_HEADS // NA_GROUP_HEADS
    zspec = lambda blk0: pl.BlockSpec((1, T, W), lambda b, g: (b, 0, blk0 + g))
    return pl.pallas_call(
        functools.partial(_na_kernel, rows=rows),
        grid=(B, G),
        in_specs=[zspec(q_blk0), zspec(k_blk0), zspec(v_blk0),
                  pl.BlockSpec((NA_GROUP_HEADS, 2 * NA_KH - 1, GRID_W, GRID_W), lambda b, g: (g, 0, 0, 0))],
        out_specs=pl.BlockSpec((1, T, W), lambda b, g: (b, 0, g)),
        out_shape=jax.ShapeDtypeStruct((B, T, NA_WIDTH), F32),
        scratch_shapes=[pltpu.VMEM((NA_GROUP_HEADS // 2, T, 2 * HEAD_DIM), BF16)] * 3
                       + [pltpu.VMEM((NA_KH, NA_GROUP_HEADS, GRID_W, NA_KH * GRID_W), F32)],
        compiler_params=_params("parallel", "parallel"),
        name="neighbourhood",
    )(z3, z3, z3, bias_tab)


def _merge_kernel(ya_ref, o1_ref, o2_ref, o3_ref, l1_ref, l2_ref, l3_ref, yc_ref, ga_ref, gb_ref, gc_ref,
                  wa_ref, wb_ref, wc_ref, out_ref, ya16_ref, yb16_ref, yc16_ref, *, bm):
    @pl.when(pl.program_id(1) == 0)
    def _():
        l1, l2, l3 = l1_ref[...], l2_ref[...], l3_ref[...]
        m = jnp.maximum(jnp.maximum(l1, l2), l3)
        e1, e2, e3 = jnp.exp(l1 - m), jnp.exp(l2 - m), jnp.exp(l3 - m)
        yb = (e1 * o1_ref[...] + e2 * o2_ref[...] + e3 * o3_ref[...]) / (e1 + e2 + e3)
        yb16_ref[...] = yb.astype(BF16)
        ya16_ref[...] = ya_ref[...].astype(BF16)
        yc16_ref[...] = yc_ref[...].astype(BF16)

    hb = bm // 2
    halves = [slice(s * hb, (s + 1) * hb) for s in range(2)]
    prods = [(_dot(ya16_ref[rs, :], wa_ref[...]), _dot(yb16_ref[rs, :], wb_ref[...]),
              _dot(yc16_ref[rs, :], wc_ref[...])) for rs in halves]
    gate = lambda ref, rs: _sigmoid(ref[rs, :]).astype(F32)
    for rs, (pa, pb, pc) in zip(halves, prods):
        out = gate(ga_ref, rs) * pa + gate(gb_ref, rs) * pb + gate(gc_ref, rs) * pc
        out_ref[rs, :] = out.astype(BF16)


def merge(ya, dil_outs, yc, z, layer, wa, wb, wc, bm, bn):
    M = ya.shape[0]
    D = wa.shape[-1]
    nj = D // bn
    tok = lambda w: pl.BlockSpec((bm, w), lambda i, j: (i, 0))
    gate = lambda g: pl.BlockSpec((bm, bn), lambda i, j: (i, g * nj + j))
    wsp = lambda k: pl.BlockSpec((None, k, bn), lambda i, j: (layer, 0, j))
    (o1, l1), (o2, l2), (o3, l3) = dil_outs
    W = DIL_GROUP_WIDTH
    return pl.pallas_call(
        functools.partial(_merge_kernel, bm=bm),
        grid=(M // bm, nj),
        in_specs=[tok(RW_WIDTH)] + [tok(W)] * 6 + [tok(NA_WIDTH), gate(0), gate(1), gate(2),
                                                  wsp(RW_WIDTH), wsp(W), wsp(NA_WIDTH)],
        out_specs=pl.BlockSpec((bm, bn), lambda i, j: (i, j)),
        out_shape=jax.ShapeDtypeStruct((M, D), BF16),
        scratch_shapes=[pltpu.VMEM((bm, RW_WIDTH), BF16), pltpu.VMEM((bm, W), BF16),
                        pltpu.VMEM((bm, NA_WIDTH), BF16)],
        compiler_params=_params("parallel", "arbitrary"),
        name="merge",
    )(ya, o1, o2, o3, l1, l2, l3, yc, z, z, z, wa, wb, wc)


def _matmul_res_kernel(a_ref, w_ref, x_ref, o_ref):
    o_ref[...] = x_ref[...] + _dot(a_ref[...], w_ref[...])


def matmul_res(a, w, layer, x, bm, bn):
    M, K = a.shape
    N = w.shape[-1]
    return pl.pallas_call(
        _matmul_res_kernel,
        grid=(M // bm, N // bn),
        in_specs=[pl.BlockSpec((bm, K), lambda i, j: (i, 0)),
                  pl.BlockSpec((None, K, bn), lambda i, j: (layer, 0, j)),
                  pl.BlockSpec((bm, bn), lambda i, j: (i, j))],
        out_specs=pl.BlockSpec((bm, bn), lambda i, j: (i, j)),
        out_shape=jax.ShapeDtypeStruct((M, N), F32),
        compiler_params=_params("parallel", "parallel"),
        name="out_proj",
    )(a, w, x)


def _gelu(x):
    return 0.5 * x * (1.0 + jnp.tanh(np.sqrt(2.0 / np.pi).astype(np.float32) * (x + 0.044715 * (x * x * x))))


FFN_HALO = 16
FFN_SPLIT = 2


def _ffn_kernel(x_ref, xp_ref, xn_ref, g_ref, wg_ref, wu_ref, cw_ref, cb_ref, wd_ref, o_ref,
                h_ref, gs_ref, acc_ref, *, bm, bf, seq):
    i = pl.program_id(0)
    f = pl.program_id(1)
    HL = FFN_HALO

    @pl.when(f == 0)
    def _():
        g = g_ref[...]
        h_ref[0:HL, :] = _rms(xp_ref[...], g).astype(BF16)
        h_ref[HL:bm + HL, :] = _rms(x_ref[...], g).astype(BF16)
        h_ref[bm + HL:bm + 2 * HL, :] = _rms(xn_ref[...], g).astype(BF16)
        acc_ref[...] = jnp.zeros_like(acc_ref)

    pos = (i * bm) % seq
    prev_ok = (pos != 0).astype(F32)
    next_ok = (pos + bm != seq).astype(F32)
    hw = bf // FFN_SPLIT
    h_ext = h_ref[...]
    h = h_ref[HL:bm + HL, :]
    cw = cw_ref[...]
    cb = cb_ref[...]
    gates, ups = [], []
    for s in range(FFN_SPLIT):
        cs = slice(s * hw, (s + 1) * hw)
        gates.append(_dot(h_ext, wg_ref[:, cs]))
        ups.append(_dot(h, wu_ref[:, cs]))
    acts = []
    for s in range(FFN_SPLIT):
        cs = slice(s * hw, (s + 1) * hw)
        gs_ref[s] = gates[s]
        gs_ref[s, HL - 8:HL, :] = gs_ref[s, HL - 8:HL, :] * prev_ok
        gs_ref[s, bm + HL:bm + HL + 8, :] = gs_ref[s, bm + HL:bm + HL + 8, :] * next_ok
        gc = (gs_ref[s, HL - 1:bm + HL - 1, :] * cw[0:1, cs] + gs_ref[s, HL:bm + HL, :] * cw[1:2, cs]
              + gs_ref[s, HL + 1:bm + HL + 1, :] * cw[2:3, cs] + cb[:, cs])
        acts.append((_gelu(gc) * ups[s]).astype(BF16))
    upd = _dot(acts[0], wd_ref[0:hw, :])
    for s in range(1, FFN_SPLIT):
        upd = upd + _dot(acts[s], wd_ref[s * hw:(s + 1) * hw, :])
    acc_ref[...] += upd

    @pl.when(f == pl.num_programs(1) - 1)
    def _():
        o_ref[...] = x_ref[...] + acc_ref[...]


def ffn(x, g, layer, wg, wu, cw, cb, wd, seq, bm, bf):
    M, D = x.shape
    F = wg.shape[-1]
    HL = FFN_HALO
    nbh = M // HL
    return pl.pallas_call(
        functools.partial(_ffn_kernel, bm=bm, bf=bf, seq=seq),
        grid=(M // bm, F // bf),
        in_specs=[pl.BlockSpec((bm, D), lambda i, f: (i, 0)),
                  pl.BlockSpec((HL, D), lambda i, f: (jnp.maximum(i * (bm // HL) - 1, 0), 0)),
                  pl.BlockSpec((HL, D), lambda i, f: (jnp.minimum((i + 1) * (bm // HL), nbh - 1), 0)),
                  pl.BlockSpec((1, D), lambda i, f: (0, 0)),
                  pl.BlockSpec((None, D, bf), lambda i, f: (layer, 0, f)),
                  pl.BlockSpec((None, D, bf), lambda i, f: (layer, 0, f)),
                  pl.BlockSpec((3, bf), lambda i, f: (0, f)),
                  pl.BlockSpec((1, bf), lambda i, f: (0, f)),
                  pl.BlockSpec((None, bf, D), lambda i, f: (layer, f, 0))],
        out_specs=pl.BlockSpec((bm, D), lambda i, f: (i, 0)),
        out_shape=jax.ShapeDtypeStruct((M, D), F32),
        scratch_shapes=[pltpu.VMEM((bm + 2 * HL, D), BF16),
                        pltpu.VMEM((FFN_SPLIT, bm + 2 * HL, bf // FFN_SPLIT), F32),
                        pltpu.VMEM((bm, D), F32)],
        compiler_params=_params("parallel", "arbitrary"),
        name="conv_ffn",
    )(x, x, x, g, wg, wu, cw, cb, wd)


def _ple_kernel(x_ref, p_ref, g_ref, wg_ref, wp_ref, gf_ref, o_ref, *, final_norm):
    x = x_ref[...]
    h = _rms(x, g_ref[...]).astype(BF16)
    gate = _sigmoid(_dot(h, wg_ref[...]))
    y = x + gate * _dot(p_ref[...].astype(BF16), wp_ref[...])
    if final_norm:
        y = _rms(y, gf_ref[...])
    o_ref[...] = y


def ple(x, p, layer, g, wg, wp, gf, bm, final_norm):
    M, D = x.shape
    P = p.shape[-1]
    return pl.pallas_call(
        functools.partial(_ple_kernel, final_norm=final_norm),
        grid=(M // bm,),
        in_specs=[pl.BlockSpec((bm, D), lambda i: (i, 0)),
                  pl.BlockSpec((None, bm, P), lambda i: (layer, i, 0)),
                  pl.BlockSpec((1, D), lambda i: (0, 0)),
                  pl.BlockSpec((None, D, D), lambda i: (layer, 0, 0)),
                  pl.BlockSpec((None, P, D), lambda i: (layer, 0, 0)),
                  pl.BlockSpec((1, D), lambda i: (0, 0))],
        out_specs=pl.BlockSpec((bm, D), lambda i: (i, 0)),
        out_shape=jax.ShapeDtypeStruct((M, D), F32),
        compiler_params=_params("parallel"),
        name="ple_final" if final_norm else "ple",
    )(x, p, g, wg, wp, gf)


def _to_scan_kernel(x0_ref, x1_ref, rev_ref, o_ref, a_ref, *, nb, tb, dup, pitch, apitch):
    N = HEAD_DIM
    NV = N // 2
    lanes = 2 * 2 * nb * RW_HEADS
    n_rows = N if dup else NV
    rev = rev_ref[...]
    for d, ref in enumerate((x0_ref, x1_ref)):
        for b in range(nb):
            x = ref[b]
            if d == 1:
                x = _dot_hi(rev, x)
            xt = x.T
            for h in range(RW_HEADS):
                slot = ((d * nb + b) * RW_HEADS + h) * 2
                for s in range(2):
                    blk = xt[h * N:(h + 1) * N] if dup else xt[h * N + s * NV:h * N + (s + 1) * NV]
                    a_ref[(slot + s) * apitch:(slot + s) * apitch + n_rows, :] = blk
    for j in range(n_rows):
        rows = a_ref[pl.ds(j, lanes, stride=apitch), :]
        o_ref[pl.ds(j, tb, stride=pitch), :] = rows.T
    for j in range(n_rows, pitch):
        o_ref[pl.ds(j, tb, stride=pitch), :] = jnp.zeros((tb, lanes), F32)


def to_scan(x0, x1, rev, B, T, tb, dup, pitch):
    C = RW_WIDTH
    nblk = T // tb
    lanes = 4 * B * RW_HEADS
    apitch = (HEAD_DIM if dup else HEAD_DIM // 2) + 4
    return pl.pallas_call(
        functools.partial(_to_scan_kernel, nb=B, tb=tb, dup=dup, pitch=pitch, apitch=apitch),
        grid=(nblk,),
        in_specs=[pl.BlockSpec((B, tb, C), lambda i: (0, i, 0)),
                  pl.BlockSpec((B, tb, C), lambda i: (0, nblk - 1 - i, 0)),
                  pl.BlockSpec((tb, tb), lambda i: (0, 0))],
        out_specs=pl.BlockSpec((tb * pitch, lanes), lambda i: (i, 0)),
        out_shape=jax.ShapeDtypeStruct((T * pitch, lanes), F32),
        scratch_shapes=[pltpu.VMEM((lanes * apitch, tb), F32)],
        compiler_params=_params("parallel"),
        name="to_scan_keys" if dup else "to_scan_values",
    )(x0.reshape(B, T, C), x1.reshape(B, T, C), rev)


def _from_scan_kernel(y_ref, rev_ref, yf_ref, yb_ref, a_ref, *, nb, tb, pitch, apitch):
    NV = HEAD_DIM // 2
    lanes = 2 * 2 * nb * RW_HEADS
    for ih in range(NV):
        rows = y_ref[pl.ds(ih, tb, stride=pitch), :]
        a_ref[pl.ds(ih, lanes, stride=apitch), :] = rows.T
    rev = rev_ref[...]
    for d, ref in enumerate((yf_ref, yb_ref)):
        for b in range(nb):
            slot0 = (d * nb + b) * RW_HEADS * 2
            pieces = [a_ref[(slot0 + s) * apitch:(slot0 + s) * apitch + NV, :] for s in range(2 * RW_HEADS)]
            y = jnp.concatenate(pieces, axis=0).T
            if d == 1:
                y = _dot_hi(rev, y)
            ref[b] = y


def from_scan(y, rev, B, T, tb, pitch):
    C = RW_WIDTH
    nblk = T // tb
    lanes = y.shape[-1]
    apitch = HEAD_DIM // 2 + 4
    out = jax.ShapeDtypeStruct((B, T, C), F32)
    yf, yb = pl.pallas_call(
        functools.partial(_from_scan_kernel, nb=B, tb=tb, pitch=pitch, apitch=apitch),
        grid=(nblk,),
        in_specs=[pl.BlockSpec((tb * pitch, lanes), lambda i: (i, 0)),
                  pl.BlockSpec((tb, tb), lambda i: (0, 0))],
        out_specs=[pl.BlockSpec((B, tb, C), lambda i: (0, i, 0)),
                   pl.BlockSpec((B, tb, C), lambda i: (0, nblk - 1 - i, 0))],
        out_shape=[out, out],
        scratch_shapes=[pltpu.VMEM((lanes * apitch, tb), F32)],
        compiler_params=_params("parallel"),
        name="from_scan",
    )(y, rev)
    return yf.reshape(B * T, C), yb.reshape(B * T, C)


def split_w_in(w_in, o_dil, o_na, o_g):
    depth, K, _ = w_in.shape
    w_t = jnp.swapaxes(w_in, 1, 2)
    w_b = jnp.concatenate([w_t[:, o_g:].astype(BF16), w_t[:, o_na:o_g].astype(BF16)], axis=1)
    w_f = jnp.concatenate([w_t[:, :o_dil].astype(BF16), jnp.zeros((depth, RW_COLS_PAD - o_dil, K), BF16),
                           w_t[:, o_dil:o_na].astype(BF16)], axis=1)
    return w_b, w_f


def _rope_tables(T):
    inv = ROPE_THETA ** (-jnp.arange(0, HEAD_DIM, 2, dtype=jnp.float32) / HEAD_DIM)
    ang = jnp.arange(T, dtype=jnp.float32)[:, None] * inv[None, :]
    cos, sin = jnp.cos(ang), jnp.sin(ang)
    cos_t = jnp.tile(jnp.concatenate([cos, cos], axis=-1), (1, DIL_STEP_WIDTH // HEAD_DIM))
    sin_t = jnp.tile(jnp.concatenate([-sin, sin], axis=-1), (1, DIL_STEP_WIDTH // HEAD_DIM))
    return cos_t, sin_t


def _pick(n, prefs):
    for c in prefs:
        if n % c == 0:
            return c
    return n


def kernel(x, p, norm_mix, w_in, rw_mu, rw_w0, rw_w2, rw_a0, rw_a2, rw_g2, rw_k_k, rw_k_a, rw_r_k, rw_lnx_g,
           rw_lnx_b, na_bias, w_br_a, w_br_b, w_br_c, w_out, norm_ffn, w_ffn_gate, w_ffn_up, ffn_conv_w,
           ffn_conv_b, w_ffn_down, norm_ple, w_ple_gate, w_ple, norm_final):
    B, T, D = x.shape
    depth = w_in.shape[0]
    M = B * T
    F = w_ffn_gate.shape[-1]
    C = RW_WIDTH
    W = DIL_GROUP_WIDTH
    assert (N_BRANCH * D) % W == 0 and w_in.shape[-1] == RW_COLS + 2 * 3 * DIL_WIDTH + N_BRANCH * D

    gates_w = N_BRANCH * D
    dil_w = 3 * DIL_WIDTH
    na_w = 3 * NA_WIDTH
    nb_cols = gates_w + na_w
    nf_cols = RW_COLS_PAD + dil_w
    na_blk0 = gates_w // W
    dil_blk0 = RW_COLS_PAD // W
    assert RW_COLS_PAD % W == 0

    cos_t, sin_t = _rope_tables(T)
    seg = jnp.asarray(np.kron(np.eye(RW_HEADS), np.ones((HEAD_DIM, HEAD_DIM))), BF16)
    xf = x.reshape(M, D)
    p3 = p.reshape(depth, M, p.shape[-1])
    bm_in = _pick(M, (1024, 512, 256, 128))
    bt_rw = _pick(T, (256, 128))
    tb = _pick(T, (64, 32))
    tt = _pick(T, (128,))
    rev = jnp.asarray(np.eye(tt)[::-1], F32)

    wb16 = {name: wt.astype(BF16) for name, wt in (
        ("br_a", w_br_a), ("br_b", w_br_b), ("br_c", w_br_c), ("out", w_out), ("ffn_gate", w_ffn_gate),
        ("ffn_up", w_ffn_up), ("ffn_down", w_ffn_down), ("ple_gate", w_ple_gate), ("ple", w_ple))}
    o_dil, o_na, o_g = RW_COLS, RW_COLS + dil_w, RW_COLS + dil_w + na_w
    w_b, w_f = split_w_in(w_in, o_dil, o_na, o_g)

    for i in range(depth):
        g_mix = norm_mix[i][None]
        zb = norm_matmul(xf, g_mix, w_b, i, bm_in, _pick(nb_cols, (768, 256, 128)), BF16, "in_proj_bf16")
        zf = norm_matmul(xf, g_mix, w_f, i, bm_in, _pick(nf_cols, (1024, 256, 128)), F32, "in_proj_f32")
        zb3 = zb.reshape(B, T, nb_cols)
        zf3 = zf.reshape(B, T, nf_cols)

        mu = jnp.pad(rw_mu[i], (0, RW_COLS_PAD - RW_COLS))[None]
        r, v, a, dec0, dec1, kd0, kd1, b0, b1, gate = rw_prep(
            zf, 0, mu, rw_w0[i], rw_w2[i], rw_a0[i], rw_a2[i], rw_g2[i], rw_k_k[i][None], rw_k_a[i][None],
            seg, T, bt_rw)
        keys = lambda x0, x1: to_scan(x0, x1, rev, B, T, tt, dup=True, pitch=KEY_PITCH)
        vals = to_scan(v, v, rev, B, T, tt, dup=False, pitch=VAL_PITCH)
        y = rw_scan(keys(dec0, dec1), keys(kd0, kd1), keys(a, a), keys(b0, b1), keys(r, r), vals, T, tb,
                    KEY_PITCH, VAL_PITCH)
        yf, yb = from_scan(y, rev, B, T, tt, VAL_PITCH)
        ya = rw_post(yf, yb, r, v, kd0, kd1, gate, rw_r_k[i].reshape(1, C), rw_lnx_g[i][None], rw_lnx_b[i][None],
                     seg, _pick(T, (512, 256, 128)))

        dil_outs = []
        for g, (window, dil) in enumerate(DIL_PATTERNS):
            dil_outs.append(dilated_group(zf3, dil_blk0 + g, dil_blk0 + 3 + g, dil_blk0 + 6 + g, cos_t, sin_t,
                                          window, dil))

        yc = neighbourhood(zb3, na_blk0, na_blk0 + 3, na_blk0 + 6, na_bias_table(na_bias[i])).reshape(M, NA_WIDTH)

        merged = merge(ya, dil_outs, yc, zb, i, wb16["br_a"], wb16["br_b"], wb16["br_c"],
                       _pick(M, (512, 256, 128)), _pick(D, (512, 256, 128)))
        xf = matmul_res(merged, wb16["out"], i, xf, _pick(M, (512, 256, 128)), D)

        xf = ffn(xf, norm_ffn[i][None], i, wb16["ffn_gate"], wb16["ffn_up"], ffn_conv_w[i],
                 ffn_conv_b[i][None], wb16["ffn_down"], T, _pick(T, (512, 256, 128)),
                 _pick(F, (512, 256, 128)))

        xf = ple(xf, p3, i, norm_ple[i][None], wb16["ple_gate"], wb16["ple"],
                 norm_final[None], _pick(M, (512, 256, 128)), final_norm=(i == depth - 1))
    return xf.reshape(B, T, D)
```

```python
import functools

import numpy as np
import jax
import jax.numpy as jnp
from jax import lax
from jax.experimental import pallas as pl
from jax.experimental.pallas import tpu as pltpu

HEAD_DIM = 64
HALF_DIM = HEAD_DIM // 2
SCALE = HEAD_DIM ** -0.5
NORM_EPS = 1e-6
ROPE_THETA = 10000.0
NEG_INF = -1e30

RW_HEADS = 8
RW_WIDTH = RW_HEADS * HEAD_DIM
RW_DECAY_LORA = 32
RW_AAA_LORA = 32
RW_GATE_LORA = 96
RW_LNX_EPS = 64e-5
RW_COLS = 3 * RW_WIDTH + 2 * RW_DECAY_LORA + 2 * RW_AAA_LORA + RW_GATE_LORA
RW_COLS_PAD = 1792
KEY_PITCH = HEAD_DIM + 4
VAL_PITCH = HEAD_DIM // 2 + 8
SCAN_UNROLL = 32

DIL_PATTERNS = ((128, 1), (512, 4), (2048, 16))
DIL_HEADS_PER_GROUP = 4
DIL_GROUP_WIDTH = DIL_HEADS_PER_GROUP * HEAD_DIM
DIL_WIDTH = len(DIL_PATTERNS) * DIL_GROUP_WIDTH
DIL_STEP_WIDTH = 2 * HEAD_DIM

GRID_W = 64
NA_HEADS = 12
NA_WIDTH = NA_HEADS * HEAD_DIM
NA_KH = 8
NA_KW = 16
NA_GROUP_HEADS = 4
NA_GROUP_WIDTH = NA_GROUP_HEADS * HEAD_DIM
NA_ROW_UNROLL = 4

N_BRANCH = 3
VMEM_LIMIT = 56 * 1024 * 1024

BF16 = jnp.bfloat16
F32 = jnp.float32
HIGHEST = lax.Precision.HIGHEST


def _params(*sem):
    return pltpu.CompilerParams(dimension_semantics=sem, vmem_limit_bytes=VMEM_LIMIT)


def _rms(x, g):
    ms = jnp.mean(x * x, axis=-1, keepdims=True)
    return x * lax.rsqrt(ms + NORM_EPS) * g


def _sigmoid(x):
    return 0.5 * jnp.tanh(0.5 * x) + 0.5


def _dot(a, b):
    return jnp.dot(a, b, preferred_element_type=F32)


def _dot_nt(a, b):
    return lax.dot_general(a, b, (((1,), (1,)), ((), ())), preferred_element_type=F32)


def _dot_hi(a, b):
    return jnp.dot(a, b, preferred_element_type=F32, precision=HIGHEST)


def _split_hi_lo(w):
    hi = w.astype(BF16)
    return jnp.stack([hi, (w - hi.astype(F32)).astype(BF16)])


def _dot_split(a, w_hi, w_lo):
    a_hi = a.astype(BF16)
    a_lo = (a - a_hi.astype(F32)).astype(BF16)
    return _dot(a_hi, w_hi) + _dot(a_hi, w_lo) + _dot(a_lo, w_hi)


def _seg_sum(x, seg):
    hi = x.astype(BF16)
    rest = x - hi.astype(F32)
    mid = rest.astype(BF16)
    lo = (rest - mid.astype(F32)).astype(BF16)
    return _dot(hi, seg) + _dot(mid, seg) + _dot(lo, seg)


def _norm_matmul_kernel(x_ref, g_ref, w_ref, o_ref, h_ref):
    @pl.when(pl.program_id(1) == 0)
    def _():
        h_ref[...] = _rms(x_ref[...], g_ref[...]).astype(BF16)

    o_ref[...] = _dot_nt(h_ref[...], w_ref[...]).astype(o_ref.dtype)


def norm_matmul(x, g, w_t, layer, bm, bn, out_dtype, name):
    M, K = x.shape
    N = w_t.shape[1]
    return pl.pallas_call(
        _norm_matmul_kernel,
        grid=(M // bm, N // bn),
        in_specs=[pl.BlockSpec((bm, K), lambda i, j: (i, 0)),
                  pl.BlockSpec((1, K), lambda i, j: (0, 0)),
                  pl.BlockSpec((None, bn, K), lambda i, j: (layer, j, 0))],
        out_specs=pl.BlockSpec((bm, bn), lambda i, j: (i, j)),
        out_shape=jax.ShapeDtypeStruct((M, N), out_dtype),
        scratch_shapes=[pltpu.VMEM((bm, K), BF16)],
        compiler_params=_params("parallel", "arbitrary"),
        name=name,
    )(x, g, w_t)


DECAY_SCALE = float(np.exp(-0.5))


def _rw_prep_kernel(c_ref, cp_ref, cn_ref, mu_ref, w0_ref, w2_ref, a0_ref, a2_ref, g2_ref, kk_ref, ka_ref,
                    seg_ref, r_out, v_out, a_out, dec0_out, dec1_out, kd0_out, kd1_out, b0_out, b1_out,
                    gate_out, buf_ref, *, bt, seq):
    i = pl.program_id(0)
    pos = (i * bt) % seq
    prev_ok = (pos != 0).astype(F32)
    next_ok = (pos + bt != seq).astype(F32)
    buf_ref[0:8, :] = cp_ref[...] * prev_ok
    buf_ref[8:bt + 8, :] = c_ref[...]
    buf_ref[bt + 8:bt + 16, :] = cn_ref[...] * next_ok
    mu = mu_ref[...]
    c = c_ref[...] * (1.0 - mu) + (buf_ref[7:bt + 7, :] + buf_ref[9:bt + 9, :]) * (0.5 * mu)
    C = RW_WIDTH
    r = c[:, 0:C]
    k = c[:, C:2 * C]
    v = c[:, 2 * C:3 * C]
    o = 3 * C
    wd = c[:, o:o + 2 * RW_DECAY_LORA]
    o += 2 * RW_DECAY_LORA
    ad = c[:, o:o + 2 * RW_AAA_LORA]
    o += 2 * RW_AAA_LORA
    gd = c[:, o:o + RW_GATE_LORA]
    seg = seg_ref[...]
    kk = k * kk_ref[...]
    kk = kk * lax.rsqrt(jnp.maximum(_seg_sum(kk * kk, seg), 1e-24))
    r_out[...] = r
    v_out[...] = v
    a_out[...] = -kk
    gate_out[...] = _dot_split(_sigmoid(gd), g2_ref[0], g2_ref[1])
    twd = jnp.tanh(wd)
    for d, (dec_out, kd_out, b_out) in enumerate(((dec0_out, kd0_out, b0_out), (dec1_out, kd1_out, b1_out))):
        lw = _dot_split(twd[:, d * RW_DECAY_LORA:(d + 1) * RW_DECAY_LORA], w2_ref[0, d], w2_ref[1, d])
        dec_out[...] = jnp.exp(-DECAY_SCALE * _sigmoid(w0_ref[d:d + 1, :] + lw))
        la = _dot_split(ad[:, d * RW_AAA_LORA:(d + 1) * RW_AAA_LORA], a2_ref[0, d], a2_ref[1, d])
        lr = _sigmoid(a0_ref[d:d + 1, :] + la)
        kd_out[...] = k * (1.0 + (lr - 1.0) * ka_ref[...])
        b_out[...] = kk * lr


def rw_prep(z, rw_col_block, mu, w0, w2, a0, a2, g2, k_k, k_a, seg, seq, bt):
    M = z.shape[0]
    C = RW_WIDTH
    W = RW_COLS_PAD
    nb8 = M // 8
    full = lambda shape: pl.BlockSpec(shape, lambda i: (0,) * len(shape))
    out = jax.ShapeDtypeStruct((M, C), F32)
    return pl.pallas_call(
        functools.partial(_rw_prep_kernel, bt=bt, seq=seq),
        grid=(M // bt,),
        in_specs=[pl.BlockSpec((bt, W), lambda i: (i, rw_col_block)),
                  pl.BlockSpec((8, W), lambda i: (jnp.maximum(i * (bt // 8) - 1, 0), rw_col_block)),
                  pl.BlockSpec((8, W), lambda i: (jnp.minimum((i + 1) * (bt // 8), nb8 - 1), rw_col_block)),
                  full((1, W)), full((2, C)), full((2, 2, RW_DECAY_LORA, C)), full((2, C)),
                  full((2, 2, RW_AAA_LORA, C)), full((2, RW_GATE_LORA, C)), full((1, C)), full((1, C)),
                  full((C, C))],
        out_specs=[pl.BlockSpec((bt, C), lambda i: (i, 0))] * 10,
        out_shape=[out] * 10,
        scratch_shapes=[pltpu.VMEM((bt + 16, W), F32)],
        compiler_params=_params("parallel"),
        name="rw_prep",
    )(z, z, z, mu, w0, _split_hi_lo(w2), a0, _split_hi_lo(a2), _split_hi_lo(g2), k_k, k_a, seg)


def _rw_scan_kernel(w_ref, k_ref, a_ref, b_ref, r_ref, v_ref, y_ref, s_ref, *, tb, pitch, vpitch):
    N = HEAD_DIM
    NV = N // 2

    @pl.when(pl.program_id(0) == 0)
    def _():
        s_ref[...] = jnp.zeros_like(s_ref)

    zero = jnp.zeros(s_ref.shape[1:], F32)

    def dot_a(c, acc):
        for u in range(SCAN_UNROLL):
            j = c * SCAN_UNROLL + u
            acc = acc + s_ref[j] * a_ref[pl.ds(j, 1), :]
        return acc

    sa_first = lax.fori_loop(0, N // SCAN_UNROLL, dot_a, zero)

    def step(t, sa):
        base = t * pitch
        nbase = jnp.minimum(t + 1, tb - 1) * pitch
        vbase = pl.multiple_of(t * vpitch, 8)
        row = lambda ref, j: ref[pl.ds(base + j, 1), :]
        vt = v_ref[pl.ds(vbase, NV), :]

        def update(c, acc):
            y0, y1, n0, n1 = acc
            for u in range(SCAN_UNROLL):
                j = c * SCAN_UNROLL + u
                s = s_ref[j] * row(w_ref, j) + sa * row(b_ref, j) + vt * row(k_ref, j)
                s_ref[j] = s
                ty = s * row(r_ref, j)
                tn = s * a_ref[pl.ds(nbase + j, 1), :]
                if u % 2 == 0:
                    y0, n0 = y0 + ty, n0 + tn
                else:
                    y1, n1 = y1 + ty, n1 + tn
            return y0, y1, n0, n1

        y0, y1, n0, n1 = lax.fori_loop(0, N // SCAN_UNROLL, update, (zero, zero, zero, zero))
        y_ref[pl.ds(vbase, NV), :] = y0 + y1
        y_ref[pl.ds(vbase + NV, vpitch - NV), :] = jnp.zeros((vpitch - NV, zero.shape[1]), F32)
        return n0 + n1

    lax.fori_loop(0, tb, step, sa_first)


def rw_scan(w, k, a, b, r, v, T, tb, pitch, vpitch):
    L = v.shape[-1]
    kspec = pl.BlockSpec((tb * pitch, L), lambda i: (i, 0))
    vspec = pl.BlockSpec((tb * vpitch, L), lambda i: (i, 0))
    return pl.pallas_call(
        functools.partial(_rw_scan_kernel, tb=tb, pitch=pitch, vpitch=vpitch),
        grid=(T // tb,),
        in_specs=[kspec] * 5 + [vspec],
        out_specs=vspec,
        out_shape=jax.ShapeDtypeStruct((T * vpitch, L), F32),
        scratch_shapes=[pltpu.VMEM((HEAD_DIM, HEAD_DIM // 2, L), F32)],
        compiler_params=_params("arbitrary"),
        name="rw_scan",
    )(w, k, a, b, r, v)


def _rw_post_kernel(yf_ref, yb_ref, r_ref, v_ref, kd0_ref, kd1_ref, gate_ref, rk_ref, g_ref, b_ref, seg_ref,
                    o_ref):
    seg = seg_ref[...]
    y = yf_ref[...] + yb_ref[...]
    inv_n = 1.0 / HEAD_DIM
    mean = _seg_sum(y, seg) * inv_n
    d = y - mean
    var = _seg_sum(d * d, seg) * inv_n
    yn = d * lax.rsqrt(var + RW_LNX_EPS) * g_ref[...] + b_ref[...]
    r = r_ref[...]
    rk = rk_ref[...]
    bonus = _seg_sum(r * kd0_ref[...] * rk + r * kd1_ref[...] * rk, seg) * v_ref[...]
    o_ref[...] = (yn + bonus) * gate_ref[...]


def rw_post(yf, yb, r, v, kd0, kd1, gate, rk, g, b, seg, bt):
    M, C = yf.shape
    tok = pl.BlockSpec((bt, C), lambda i: (i, 0))
    row = pl.BlockSpec((1, C), lambda i: (0, 0))
    return pl.pallas_call(
        _rw_post_kernel,
        grid=(M // bt,),
        in_specs=[tok] * 7 + [row] * 3 + [pl.BlockSpec((C, C), lambda i: (0, 0))],
        out_specs=tok,
        out_shape=jax.ShapeDtypeStruct((M, C), F32),
        compiler_params=_params("parallel"),
        name="rw_post",
    )(yf, yb, r, v, kd0, kd1, gate, rk, g, b, seg)


def _dil_kernel(q_ref, k_ref, v_ref, cos_ref, sin_ref, o_ref, lse_ref, qs_ref, ks_ref, vs_ref, *, n_sub, dil, qb,
                kwin, half):
    W = DIL_STEP_WIDTH
    lane = lax.broadcasted_iota(jnp.int32, (1, W), 1)
    first_half = (lane % HEAD_DIM) < HALF_DIM

    def rope(z, c, s):
        swapped = jnp.where(first_half, pltpu.roll(z, W - HALF_DIM, 1), pltpu.roll(z, HALF_DIM, 1))
        return z * c + swapped * s

    rc = min(256, n_sub)

    def subsequence(r, carry):
        def rope_chunk(i, carry):
            rows = pl.ds(r + i * (rc * dil), rc, stride=dil)
            sl = pl.ds(pl.multiple_of(i * rc, rc), rc)
            c = cos_ref[rows, :]
            s = sin_ref[rows, :]
            qs_ref[sl, :] = (rope(q_ref[0, rows, :], c, s) * SCALE).astype(BF16)
            ks_ref[sl, :] = rope(k_ref[0, rows, :], c, s).astype(BF16)
            vs_ref[sl, :] = v_ref[0, rows, :].astype(BF16)
            return carry

        lax.fori_loop(0, n_sub // rc, rope_chunk, 0)

        nblk = n_sub // qb
        unroll = next(u for u in (4, 2, 1) if nblk % u == 0)
        low_head = lane < HEAD_DIM

        def block_group(ig, carry):
            chains = []
            for u in range(unroll):
                q0 = pl.multiple_of((ig * unroll + u) * qb, qb)
                k0 = pl.multiple_of(jnp.clip(q0 - (kwin - qb) // 2, 0, n_sub - kwin), 64)
                q = qs_ref[pl.ds(q0, qb), :]
                kk = ks_ref[pl.ds(k0, kwin), :]
                jq = q0 + lax.broadcasted_iota(jnp.int32, (qb, kwin), 0)
                jk = k0 + lax.broadcasted_iota(jnp.int32, (qb, kwin), 1)
                ok = jnp.abs(jk - jq) <= half
                for h in range(2):
                    qh = jnp.where(low_head == (h == 0), q, jnp.zeros_like(q))
                    chains.append((q0, k0, jnp.where(ok, _dot_nt(qh, kk), NEG_INF)))
            probs = []
            for q0, k0, s in chains:
                m = jnp.max(s, axis=-1, keepdims=True)
                p = jnp.exp(s - m)
                l = jnp.sum(p, axis=-1, keepdims=True)
                probs.append((q0, k0, p.astype(BF16), l, m + jnp.log(l)))
            for c in range(0, len(probs), 2):
                (q0, k0, p0, l0, lse0), (_, _, p1, l1, lse1) = probs[c], probs[c + 1]
                vv = vs_ref[pl.ds(k0, kwin), :]
                rows = pl.ds(r + q0 * dil, qb, stride=dil)
                o_ref[0, rows, :] = jnp.where(low_head, _dot(p0, vv) / l0, _dot(p1, vv) / l1)
                lse_ref[0, rows, :] = jnp.where(low_head, lse0, lse1)
            return carry

        lax.fori_loop(0, nblk // unroll, block_group, 0)
        return carry

    lax.fori_loop(0, dil, subsequence, 0)


def dilated_group(z3, q_blk, k_blk, v_blk, cos_t, sin_t, window, dil):
    B, T, ncols = z3.shape
    W = DIL_STEP_WIDTH
    steps = DIL_GROUP_WIDTH // W
    n_sub = T // dil
    half = window // (2 * dil)
    qb = min(128, n_sub)
    kwin = min(qb + 2 * half, n_sub)
    zspec = lambda blk: pl.BlockSpec((1, T, W), lambda b, s: (b, 0, blk * steps + s))
    tspec = pl.BlockSpec((T, W), lambda b, s: (0, 0))
    ospec = pl.BlockSpec((1, T, W), lambda b, s: (b, 0, s))
    oshape = jax.ShapeDtypeStruct((B, T, DIL_GROUP_WIDTH), F32)
    o, lse = pl.pallas_call(
        functools.partial(_dil_kernel, n_sub=n_sub, dil=dil, qb=qb, kwin=kwin, half=half),
        grid=(B, steps),
        in_specs=[zspec(q_blk), zspec(k_blk), zspec(v_blk), tspec, tspec],
        out_specs=[ospec, ospec],
        out_shape=[oshape, oshape],
        scratch_shapes=[pltpu.VMEM((n_sub, W), BF16)] * 3,
        compiler_params=_params("parallel", "parallel"),
        name=f"dilated_d{dil}",
    )(z3, z3, z3, cos_t, sin_t)
    return o.reshape(B * T, DIL_GROUP_WIDTH), lse.reshape(B * T, DIL_GROUP_WIDTH)


def _na_kernel(q_ref, k_ref, v_ref, tab_ref, o_ref, qs_ref, ks_ref, vs_ref, bias_ref, *, rows):
    nkeys = NA_KH * GRID_W
    pc = NA_KH * GRID_W

    for off in range(NA_KH):
        for h in range(NA_GROUP_HEADS):
            for ky in range(NA_KH):
                bias_ref[off, h, :, ky * GRID_W:(ky + 1) * GRID_W] = tab_ref[h, ky - off + NA_KH - 1]

    PW = 2 * HEAD_DIM
    pairs = NA_GROUP_HEADS // 2
    low_head = lax.broadcasted_iota(jnp.int32, (1, PW), 1) < HEAD_DIM

    def prep(c, carry):
        sl = pl.ds(pl.multiple_of(c * pc, pc), pc)
        q = q_ref[0, sl, :] * SCALE
        k = k_ref[0, sl, :]
        v = v_ref[0, sl, :]
        for pp in range(pairs):
            ps = slice(pp * PW, (pp + 1) * PW)
            qs_ref[pp, sl, :] = q[:, ps].astype(BF16)
            ks_ref[pp, sl, :] = k[:, ps].astype(BF16)
            vs_ref[pp, sl, :] = v[:, ps].astype(BF16)
        return carry

    lax.fori_loop(0, rows * GRID_W // pc, prep, 0)

    def row_group(rg, carry):
        chains = []
        for u in range(NA_ROW_UNROLL):
            r = rg * NA_ROW_UNROLL + u
            r0 = jnp.clip(r - NA_KH // 2, 0, rows - NA_KH)
            qrow = pl.ds(pl.multiple_of(r * GRID_W, GRID_W), GRID_W)
            krow = pl.ds(pl.multiple_of(r0 * GRID_W, GRID_W), nkeys)
            for pp in range(pairs):
                q = qs_ref[pp, qrow, :]
                kk = ks_ref[pp, krow, :]
                for h in range(2):
                    qh = jnp.where(low_head == (h == 0), q, jnp.zeros_like(q))
                    chains.append((qrow, krow, pp, _dot_nt(qh, kk) + bias_ref[r - r0, 2 * pp + h]))
        probs = []
        for qrow, krow, pp, s in chains:
            m = jnp.max(s, axis=-1, keepdims=True)
            p = jnp.exp(s - m)
            l = jnp.sum(p, axis=-1, keepdims=True)
            probs.append((qrow, krow, pp, p.astype(BF16), l))
        outs = []
        for c in range(0, len(probs), 2):
            (qrow, krow, pp, p0, l0), (_, _, _, p1, l1) = probs[c], probs[c + 1]
            vv = vs_ref[pp, krow, :]
            outs.append(jnp.where(low_head, _dot(p0, vv) / l0, _dot(p1, vv) / l1))
            if len(outs) == pairs:
                o_ref[0, qrow, :] = jnp.concatenate(outs, axis=-1)
                outs = []
        return carry

    lax.fori_loop(0, rows // NA_ROW_UNROLL, row_group, 0)


def na_bias_table(rel_bias):
    qc = np.arange(GRID_W)
    kc = np.arange(GRID_W)
    wc0 = np.clip(qc - NA_KW // 2, 0, GRID_W - NA_KW)
    col_ok = (kc[None, :] >= wc0[:, None]) & (kc[None, :] < wc0[:, None] + NA_KW)
    dx_idx = np.clip(kc[None, :] - qc[:, None], 1 - NA_KW, NA_KW - 1) + NA_KW - 1
    onehot = jnp.asarray(dx_idx[None] == np.arange(2 * NA_KW - 1)[:, None, None], F32)
    tab = jnp.einsum('hyd,dqk->hyqk', rel_bias.astype(F32), onehot, precision=HIGHEST)
    return jnp.where(jnp.asarray(col_ok)[None, None], tab, NEG_INF)


def neighbourhood(z3, q_blk0, k_blk0, v_blk0, bias_tab):
    B, T, ncols = z3.shape
    rows = T // GRID_W
    assert rows >= NA_KH
    W = NA_GROUP_WIDTH
    G = NA_HEADS // NA_GROUP_HEADS
    zspec = lambda blk0: pl.BlockSpec((1, T, W), lambda b, g: (b, 0, blk0 + g))
    return pl.pallas_call(
        functools.partial(_na_kernel, rows=rows),
        grid=(B, G),
        in_specs=[zspec(q_blk0), zspec(k_blk0), zspec(v_blk0),
                  pl.BlockSpec((NA_GROUP_HEADS, 2 * NA_KH - 1, GRID_W, GRID_W), lambda b, g: (g, 0, 0, 0))],
        out_specs=pl.BlockSpec((1, T, W), lambda b, g: (b, 0, g)),
        out_shape=jax.ShapeDtypeStruct((B, T, NA_WIDTH), F32),
        scratch_shapes=[pltpu.VMEM((NA_GROUP_HEADS // 2, T, 2 * HEAD_DIM), BF16)] * 3
                       + [pltpu.VMEM((NA_KH, NA_GROUP_HEADS, GRID_W, NA_KH * GRID_W), F32)],
        compiler_params=_params("parallel", "parallel"),
        name="neighbourhood",
    )(z3, z3, z3, bias_tab)


def _merge_kernel(ya_ref, o1_ref, o2_ref, o3_ref, l1_ref, l2_ref, l3_ref, yc_ref, ga_ref, gb_ref, gc_ref,
                  wa_ref, wb_ref, wc_ref, out_ref, ya16_ref, yb16_ref, yc16_ref, *, bm):
    @pl.when(pl.program_id(1) == 0)
    def _():
        l1, l2, l3 = l1_ref[...], l2_ref[...], l3_ref[...]
        m = jnp.maximum(jnp.maximum(l1, l2), l3)
        e1, e2, e3 = jnp.exp(l1 - m), jnp.exp(l2 - m), jnp.exp(l3 - m)
        yb = (e1 * o1_ref[...] + e2 * o2_ref[...] + e3 * o3_ref[...]) / (e1 + e2 + e3)
        yb16_ref[...] = yb.astype(BF16)
        ya16_ref[...] = ya_ref[...].astype(BF16)
        yc16_ref[...] = yc_ref[...].astype(BF16)

    hb = bm // 2
    halves = [slice(s * hb, (s + 1) * hb) for s in range(2)]
    prods = [(_dot(ya16_ref[rs, :], wa_ref[...]), _dot(yb16_ref[rs, :], wb_ref[...]),
              _dot(yc16_ref[rs, :], wc_ref[...])) for rs in halves]
    gate = lambda ref, rs: _sigmoid(ref[rs, :]).astype(F32)
    for rs, (pa, pb, pc) in zip(halves, prods):
        out = gate(ga_ref, rs) * pa + gate(gb_ref, rs) * pb + gate(gc_ref, rs) * pc
        out_ref[rs, :] = out.astype(BF16)


def merge(ya, dil_outs, yc, z, layer, wa, wb, wc, bm, bn):
    M = ya.shape[0]
    D = wa.shape[-1]
    nj = D // bn
    tok = lambda w: pl.BlockSpec((bm, w), lambda i, j: (i, 0))
    gate = lambda g: pl.BlockSpec((bm, bn), lambda i, j: (i, g * nj + j))
    wsp = lambda k: pl.BlockSpec((None, k, bn), lambda i, j: (layer, 0, j))
    (o1, l1), (o2, l2), (o3, l3) = dil_outs
    W = DIL_GROUP_WIDTH
    return pl.pallas_call(
        functools.partial(_merge_kernel, bm=bm),
        grid=(M // bm, nj),
        in_specs=[tok(RW_WIDTH)] + [tok(W)] * 6 + [tok(NA_WIDTH), gate(0), gate(1), gate(2),
                                                  wsp(RW_WIDTH), wsp(W), wsp(NA_WIDTH)],
        out_specs=pl.BlockSpec((bm, bn), lambda i, j: (i, j)),
        out_shape=jax.ShapeDtypeStruct((M, D), BF16),
        scratch_shapes=[pltpu.VMEM((bm, RW_WIDTH), BF16), pltpu.VMEM((bm, W), BF16),
                        pltpu.VMEM((bm, NA_WIDTH), BF16)],
        compiler_params=_params("parallel", "arbitrary"),
        name="merge",
    )(ya, o1, o2, o3, l1, l2, l3, yc, z, z, z, wa, wb, wc)


def _matmul_res_kernel(a_ref, w_ref, x_ref, o_ref):
    o_ref[...] = x_ref[...] + _dot(a_ref[...], w_ref[...])


def matmul_res(a, w, layer, x, bm, bn):
    M, K = a.shape
    N = w.shape[-1]
    return pl.pallas_call(
        _matmul_res_kernel,
        grid=(M // bm, N // bn),
        in_specs=[pl.BlockSpec((bm, K), lambda i, j: (i, 0)),
                  pl.BlockSpec((None, K, bn), lambda i, j: (layer, 0, j)),
                  pl.BlockSpec((bm, bn), lambda i, j: (i, j))],
        out_specs=pl.BlockSpec((bm, bn), lambda i, j: (i, j)),
        out_shape=jax.ShapeDtypeStruct((M, N), F32),
        compiler_params=_params("parallel", "parallel"),
        name="out_proj",
    )(a, w, x)


def _gelu(x):
    return 0.5 * x * (1.0 + jnp.tanh(np.sqrt(2.0 / np.pi).astype(np.float32) * (x + 0.044715 * (x * x * x))))


FFN_HALO = 16
FFN_SPLIT = 2


def _ffn_kernel(x_ref, xp_ref, xn_ref, g_ref, wg_ref, wu_ref, cw_ref, cb_ref, wd_ref, o_ref,
                h_ref, gs_ref, *, bm, bf, seq):
    i = pl.program_id(0)
    f = pl.program_id(1)
    HL = FFN_HALO

    @pl.when(f == 0)
    def _():
        g = g_ref[...]
        h_ref[0:HL, :] = _rms(xp_ref[...], g).astype(BF16)
        h_ref[HL:bm + HL, :] = _rms(x_ref[...], g).astype(BF16)
        h_ref[bm + HL:bm + 2 * HL, :] = _rms(xn_ref[...], g).astype(BF16)
        o_ref[...] = x_ref[...]

    pos = (i * bm) % seq
    prev_ok = (pos != 0).astype(F32)
    next_ok = (pos + bm != seq).astype(F32)
    hw = bf // FFN_SPLIT
    h_ext = h_ref[...]
    h = h_ref[HL:bm + HL, :]
    cw = cw_ref[...]
    cb = cb_ref[...]
    gates, ups = [], []
    for s in range(FFN_SPLIT):
        cs = slice(s * hw, (s + 1) * hw)
        gates.append(_dot(h_ext, wg_ref[:, cs]))
        ups.append(_dot(h, wu_ref[:, cs]))
    acts = []
    for s in range(FFN_SPLIT):
        cs = slice(s * hw, (s + 1) * hw)
        gs_ref[s] = gates[s]
        gs_ref[s, HL - 8:HL, :] = gs_ref[s, HL - 8:HL, :] * prev_ok
        gs_ref[s, bm + HL:bm + HL + 8, :] = gs_ref[s, bm + HL:bm + HL + 8, :] * next_ok
        gc = (gs_ref[s, HL - 1:bm + HL - 1, :] * cw[0:1, cs] + gs_ref[s, HL:bm + HL, :] * cw[1:2, cs]
              + gs_ref[s, HL + 1:bm + HL + 1, :] * cw[2:3, cs] + cb[:, cs])
        acts.append((_gelu(gc) * ups[s]).astype(BF16))
    upd = _dot(acts[0], wd_ref[0:hw, :])
    for s in range(1, FFN_SPLIT):
        upd = upd + _dot(acts[s], wd_ref[s * hw:(s + 1) * hw, :])
    o_ref[...] += upd


def ffn(x, g, layer, wg, wu, cw, cb, wd, seq, bm, bf):
    M, D = x.shape
    F = wg.shape[-1]
    HL = FFN_HALO
    nbh = M // HL
    return pl.pallas_call(
        functools.partial(_ffn_kernel, bm=bm, bf=bf, seq=seq),
        grid=(M // bm, F // bf),
        in_specs=[pl.BlockSpec((bm, D), lambda i, f: (i, 0)),
                  pl.BlockSpec((HL, D), lambda i, f: (jnp.maximum(i * (bm // HL) - 1, 0), 0)),
                  pl.BlockSpec((HL, D), lambda i, f: (jnp.minimum((i + 1) * (bm // HL), nbh - 1), 0)),
                  pl.BlockSpec((1, D), lambda i, f: (0, 0)),
                  pl.BlockSpec((None, D, bf), lambda i, f: (layer, 0, f)),
                  pl.BlockSpec((None, D, bf), lambda i, f: (layer, 0, f)),
                  pl.BlockSpec((3, bf), lambda i, f: (0, f)),
                  pl.BlockSpec((1, bf), lambda i, f: (0, f)),
                  pl.BlockSpec((None, bf, D), lambda i, f: (layer, f, 0))],
        out_specs=pl.BlockSpec((bm, D), lambda i, f: (i, 0)),
        out_shape=jax.ShapeDtypeStruct((M, D), F32),
        scratch_shapes=[pltpu.VMEM((bm + 2 * HL, D), BF16),
                        pltpu.VMEM((FFN_SPLIT, bm + 2 * HL, bf // FFN_SPLIT), F32)],
        compiler_params=_params("parallel", "arbitrary"),
        name="conv_ffn",
    )(x, x, x, g, wg, wu, cw, cb, wd)


def _ple_kernel(x_ref, p_ref, g_ref, wg_ref, wp_ref, gf_ref, o_ref, *, final_norm):
    x = x_ref[...]
    h = _rms(x, g_ref[...]).astype(BF16)
    gate = _sigmoid(_dot(h, wg_ref[...]))
    y = x + gate * _dot(p_ref[...].astype(BF16), wp_ref[...])
    if final_norm:
        y = _rms(y, gf_ref[...])
    o_ref[...] = y


def ple(x, p, layer, g, wg, wp, gf, bm, final_norm):
    M, D = x.shape
    P = p.shape[-1]
    return pl.pallas_call(
        functools.partial(_ple_kernel, final_norm=final_norm),
        grid=(M // bm,),
        in_specs=[pl.BlockSpec((bm, D), lambda i: (i, 0)),
                  pl.BlockSpec((None, bm, P), lambda i: (layer, i, 0)),
                  pl.BlockSpec((1, D), lambda i: (0, 0)),
                  pl.BlockSpec((None, D, D), lambda i: (layer, 0, 0)),
                  pl.BlockSpec((None, P, D), lambda i: (layer, 0, 0)),
                  pl.BlockSpec((1, D), lambda i: (0, 0))],
        out_specs=pl.BlockSpec((bm, D), lambda i: (i, 0)),
        out_shape=jax.ShapeDtypeStruct((M, D), F32),
        compiler_params=_params("parallel"),
        name="ple_final" if final_norm else "ple",
    )(x, p, g, wg, wp, gf)


def _to_scan_kernel(x0_ref, x1_ref, rev_ref, o_ref, a_ref, *, nb, tb, dup, pitch, apitch):
    N = HEAD_DIM
    NV = N // 2
    lanes = 2 * 2 * nb * RW_HEADS
    n_rows = N if dup else NV
    rev = rev_ref[...]
    for d, ref in enumerate((x0_ref, x1_ref)):
        for b in range(nb):
            x = ref[b]
            if d == 1:
                x = _dot_hi(rev, x)
            xt = x.T
            for h in range(RW_HEADS):
                slot = ((d * nb + b) * RW_HEADS + h) * 2
                for s in range(2):
                    blk = xt[h * N:(h + 1) * N] if dup else xt[h * N + s * NV:h * N + (s + 1) * NV]
                    a_ref[(slot + s) * apitch:(slot + s) * apitch + n_rows, :] = blk
    for j in range(n_rows):
        rows = a_ref[pl.ds(j, lanes, stride=apitch), :]
        o_ref[pl.ds(j, tb, stride=pitch), :] = rows.T
    for j in range(n_rows, pitch):
        o_ref[pl.ds(j, tb, stride=pitch), :] = jnp.zeros((tb, lanes), F32)


def to_scan(x0, x1, rev, B, T, tb, dup, pitch):
    C = RW_WIDTH
    nblk = T // tb
    lanes = 4 * B * RW_HEADS
    apitch = (HEAD_DIM if dup else HEAD_DIM // 2) + 4
    return pl.pallas_call(
        functools.partial(_to_scan_kernel, nb=B, tb=tb, dup=dup, pitch=pitch, apitch=apitch),
        grid=(nblk,),
        in_specs=[pl.BlockSpec((B, tb, C), lambda i: (0, i, 0)),
                  pl.BlockSpec((B, tb, C), lambda i: (0, nblk - 1 - i, 0)),
                  pl.BlockSpec((tb, tb), lambda i: (0, 0))],
        out_specs=pl.BlockSpec((tb * pitch, lanes), lambda i: (i, 0)),
        out_shape=jax.ShapeDtypeStruct((T * pitch, lanes), F32),
        scratch_shapes=[pltpu.VMEM((lanes * apitch, tb), F32)],
        compiler_params=_params("parallel"),
        name="to_scan_keys" if dup else "to_scan_values",
    )(x0.reshape(B, T, C), x1.reshape(B, T, C), rev)


def _from_scan_kernel(y_ref, rev_ref, yf_ref, yb_ref, a_ref, *, nb, tb, pitch, apitch):
    NV = HEAD_DIM // 2
    lanes = 2 * 2 * nb * RW_HEADS
    for ih in range(NV):
        rows = y_ref[pl.ds(ih, tb, stride=pitch), :]
        a_ref[pl.ds(ih, lanes, stride=apitch), :] = rows.T
    rev = rev_ref[...]
    for d, ref in enumerate((yf_ref, yb_ref)):
        for b in range(nb):
            slot0 = (d * nb + b) * RW_HEADS * 2
            pieces = [a_ref[(slot0 + s) * apitch:(slot0 + s) * apitch + NV, :] for s in range(2 * RW_HEADS)]
            y = jnp.concatenate(pieces, axis=0).T
            if d == 1:
                y = _dot_hi(rev, y)
            ref[b] = y


def from_scan(y, rev, B, T, tb, pitch):
    C = RW_WIDTH
    nblk = T // tb
    lanes = y.shape[-1]
    apitch = HEAD_DIM // 2 + 4
    out = jax.ShapeDtypeStruct((B, T, C), F32)
    yf, yb = pl.pallas_call(
        functools.partial(_from_scan_kernel, nb=B, tb=tb, pitch=pitch, apitch=apitch),
        grid=(nblk,),
        in_specs=[pl.BlockSpec((tb * pitch, lanes), lambda i: (i, 0)),
                  pl.BlockSpec((tb, tb), lambda i: (0, 0))],
        out_specs=[pl.BlockSpec((B, tb, C), lambda i: (0, i, 0)),
                   pl.BlockSpec((B, tb, C), lambda i: (0, nblk - 1 - i, 0))],
        out_shape=[out, out],
        scratch_shapes=[pltpu.VMEM((lanes * apitch, tb), F32)],
        compiler_params=_params("parallel"),
        name="from_scan",
    )(y, rev)
    return yf.reshape(B * T, C), yb.reshape(B * T, C)


def split_w_in(w_in, o_dil, o_na, o_g):
    depth, K, _ = w_in.shape
    w_t = jnp.swapaxes(w_in, 1, 2)
    w_b = jnp.concatenate([w_t[:, o_g:].astype(BF16), w_t[:, o_na:o_g].astype(BF16)], axis=1)
    w_f = jnp.concatenate([w_t[:, :o_dil].astype(BF16), jnp.zeros((depth, RW_COLS_PAD - o_dil, K), BF16),
                           w_t[:, o_dil:o_na].astype(BF16)], axis=1)
    return w_b, w_f


def _rope_tables(T):
    inv = ROPE_THETA ** (-jnp.arange(0, HEAD_DIM, 2, dtype=jnp.float32) / HEAD_DIM)
    ang = jnp.arange(T, dtype=jnp.float32)[:, None] * inv[None, :]
    cos, sin = jnp.cos(ang), jnp.sin(ang)
    cos_t = jnp.tile(jnp.concatenate([cos, cos], axis=-1), (1, DIL_STEP_WIDTH // HEAD_DIM))
    sin_t = jnp.tile(jnp.concatenate([-sin, sin], axis=-1), (1, DIL_STEP_WIDTH // HEAD_DIM))
    return cos_t, sin_t


def _pick(n, prefs):
    for c in prefs:
        if n % c == 0:
            return c
    return n


def kernel(x, p, norm_mix, w_in, rw_mu, rw_w0, rw_w2, rw_a0, rw_a2, rw_g2, rw_k_k, rw_k_a, rw_r_k, rw_lnx_g,
           rw_lnx_b, na_bias, w_br_a, w_br_b, w_br_c, w_out, norm_ffn, w_ffn_gate, w_ffn_up, ffn_conv_w,
           ffn_conv_b, w_ffn_down, norm_ple, w_ple_gate, w_ple, norm_final):
    B, T, D = x.shape
    depth = w_in.shape[0]
    M = B * T
    F = w_ffn_gate.shape[-1]
    C = RW_WIDTH
    W = DIL_GROUP_WIDTH
    assert (N_BRANCH * D) % W == 0 and w_in.shape[-1] == RW_COLS + 2 * 3 * DIL_WIDTH + N_BRANCH * D

    gates_w = N_BRANCH * D
    dil_w = 3 * DIL_WIDTH
    na_w = 3 * NA_WIDTH
    nb_cols = gates_w + na_w
    nf_cols = RW_COLS_PAD + dil_w
    na_blk0 = gates_w // W
    dil_blk0 = RW_COLS_PAD // W
    assert RW_COLS_PAD % W == 0

    cos_t, sin_t = _rope_tables(T)
    seg = jnp.asarray(np.kron(np.eye(RW_HEADS), np.ones((HEAD_DIM, HEAD_DIM))), BF16)
    xf = x.reshape(M, D)
    p3 = p.reshape(depth, M, p.shape[-1])
    bm_in = _pick(M, (1024, 512, 256, 128))
    bt_rw = _pick(T, (256, 128))
    tb = _pick(T, (64, 32))
    tt = _pick(T, (128,))
    rev = jnp.asarray(np.eye(tt)[::-1], F32)

    wb16 = {name: wt.astype(BF16) for name, wt in (
        ("br_a", w_br_a), ("br_b", w_br_b), ("br_c", w_br_c), ("out", w_out), ("ffn_gate", w_ffn_gate),
        ("ffn_up", w_ffn_up), ("ffn_down", w_ffn_down), ("ple_gate", w_ple_gate), ("ple", w_ple))}
    o_dil, o_na, o_g = RW_COLS, RW_COLS + dil_w, RW_COLS + dil_w + na_w
    w_b, w_f = split_w_in(w_in, o_dil, o_na, o_g)

    for i in range(depth):
        g_mix = norm_mix[i][None]
        zb = norm_matmul(xf, g_mix, w_b, i, _pick(M, (512, 256, 128)), _pick(nb_cols, (2816, 768, 256, 128)), BF16,
                         "in_proj_bf16")
        zf = norm_matmul(xf, g_mix, w_f, i, _pick(M, (512, 256, 128)), _pick(nf_cols, (2048, 1024, 256, 128)), F32,
                         "in_proj_f32")
        zb3 = zb.reshape(B, T, nb_cols)
        zf3 = zf.reshape(B, T, nf_cols)

        mu = jnp.pad(rw_mu[i], (0, RW_COLS_PAD - RW_COLS))[None]
        r, v, a, dec0, dec1, kd0, kd1, b0, b1, gate = rw_prep(
            zf, 0, mu, rw_w0[i], rw_w2[i], rw_a0[i], rw_a2[i], rw_g2[i], rw_k_k[i][None], rw_k_a[i][None],
            seg, T, bt_rw)
        keys = lambda x0, x1: to_scan(x0, x1, rev, B, T, tt, dup=True, pitch=KEY_PITCH)
        vals = to_scan(v, v, rev, B, T, tt, dup=False, pitch=VAL_PITCH)
        y = rw_scan(keys(dec0, dec1), keys(kd0, kd1), keys(a, a), keys(b0, b1), keys(r, r), vals, T, tb,
                    KEY_PITCH, VAL_PITCH)
        yf, yb = from_scan(y, rev, B, T, tt, VAL_PITCH)
        ya = rw_post(yf, yb, r, v, kd0, kd1, gate, rw_r_k[i].reshape(1, C), rw_lnx_g[i][None], rw_lnx_b[i][None],
                     seg, _pick(T, (512, 256, 128)))

        dil_outs = []
        for g, (window, dil) in enumerate(DIL_PATTERNS):
            dil_outs.append(dilated_group(zf3, dil_blk0 + g, dil_blk0 + 3 + g, dil_blk0 + 6 + g, cos_t, sin_t,
                                          window, dil))

        yc = neighbourhood(zb3, na_blk0, na_blk0 + 3, na_blk0 + 6, na_bias_table(na_bias[i])).reshape(M, NA_WIDTH)

        merged = merge(ya, dil_outs, yc, zb, i, wb16["br_a"], wb16["br_b"], wb16["br_c"],
                       _pick(M, (1024, 512, 256, 128)), _pick(D, (512, 256, 128)))
        xf = matmul_res(merged, wb16["out"], i, xf, _pick(M, (512, 256, 128)), D)

        xf = ffn(xf, norm_ffn[i][None], i, wb16["ffn_gate"], wb16["ffn_up"], ffn_conv_w[i],
                 ffn_conv_b[i][None], wb16["ffn_down"], T, _pick(T, (512, 256, 128)),
                 _pick(F, (512, 256, 128)))

        xf = ple(xf, p3, i, norm_ple[i][None], wb16["ple_gate"], wb16["ple"],
                 norm_final[None], _pick(M, (512, 256, 128)), final_norm=(i == depth - 1))
    return xf.reshape(B, T, D)
```

```python
import functools

import numpy as np
import jax
import jax.numpy as jnp
from jax import lax
from jax.experimental import pallas as pl
from jax.experimental.pallas import tpu as pltpu

HEAD_DIM = 64
HALF_DIM = HEAD_DIM // 2
SCALE = HEAD_DIM ** -0.5
NORM_EPS = 1e-6
ROPE_THETA = 10000.0
NEG_INF = -1e30

RW_HEADS = 8
RW_WIDTH = RW_HEADS * HEAD_DIM
RW_DECAY_LORA = 32
RW_AAA_LORA = 32
RW_GATE_LORA = 96
RW_LNX_EPS = 64e-5
RW_COLS = 3 * RW_WIDTH + 2 * RW_DECAY_LORA + 2 * RW_AAA_LORA + RW_GATE_LORA
RW_COLS_PAD = 1792
KEY_PITCH = HEAD_DIM + 4
VAL_PITCH = HEAD_DIM // 2 + 8
SCAN_UNROLL = 32

DIL_PATTERNS = ((128, 1), (512, 4), (2048, 16))
DIL_HEADS_PER_GROUP = 4
DIL_GROUP_WIDTH = DIL_HEADS_PER_GROUP * HEAD_DIM
DIL_WIDTH = len(DIL_PATTERNS) * DIL_GROUP_WIDTH
DIL_STEP_WIDTH = 2 * HEAD_DIM

GRID_W = 64
NA_HEADS = 12
NA_WIDTH = NA_HEADS * HEAD_DIM
NA_KH = 8
NA_KW = 16
NA_GROUP_HEADS = 4
NA_GROUP_WIDTH = NA_GROUP_HEADS * HEAD_DIM
NA_ROW_UNROLL = 4

N_BRANCH = 3
VMEM_LIMIT = 56 * 1024 * 1024

BF16 = jnp.bfloat16
F32 = jnp.float32
HIGHEST = lax.Precision.HIGHEST


def _params(*sem):
    return pltpu.CompilerParams(dimension_semantics=sem, vmem_limit_bytes=VMEM_LIMIT)


def _rms(x, g):
    ms = jnp.mean(x * x, axis=-1, keepdims=True)
    return x * lax.rsqrt(ms + NORM_EPS) * g


def _sigmoid(x):
    return 0.5 * jnp.tanh(0.5 * x) + 0.5


def _dot(a, b):
    return jnp.dot(a, b, preferred_element_type=F32)


def _dot_nt(a, b):
    return lax.dot_general(a, b, (((1,), (1,)), ((), ())), preferred_element_type=F32)


def _dot_hi(a, b):
    return jnp.dot(a, b, preferred_element_type=F32, precision=HIGHEST)


def _split_hi_lo(w):
    hi = w.astype(BF16)
    return jnp.stack([hi, (w - hi.astype(F32)).astype(BF16)])


def _dot_split(a, w_hi, w_lo):
    a_hi = a.astype(BF16)
    a_lo = (a - a_hi.astype(F32)).astype(BF16)
    return _dot(a_hi, w_hi) + _dot(a_hi, w_lo) + _dot(a_lo, w_hi)


def _seg_sum(x, seg):
    hi = x.astype(BF16)
    rest = x - hi.astype(F32)
    mid = rest.astype(BF16)
    lo = (rest - mid.astype(F32)).astype(BF16)
    return _dot(hi, seg) + _dot(mid, seg) + _dot(lo, seg)


def _norm_matmul_kernel(x_ref, g_ref, w_ref, o_ref, h_ref):
    @pl.when(pl.program_id(1) == 0)
    def _():
        h_ref[...] = _rms(x_ref[...], g_ref[...]).astype(BF16)

    o_ref[...] = _dot_nt(h_ref[...], w_ref[...]).astype(o_ref.dtype)


def norm_matmul(x, g, w_t, layer, bm, bn, out_dtype, name):
    M, K = x.shape
    N = w_t.shape[1]
    return pl.pallas_call(
        _norm_matmul_kernel,
        grid=(M // bm, N // bn),
        in_specs=[pl.BlockSpec((bm, K), lambda i, j: (i, 0)),
                  pl.BlockSpec((1, K), lambda i, j: (0, 0)),
                  pl.BlockSpec((None, bn, K), lambda i, j: (layer, j, 0))],
        out_specs=pl.BlockSpec((bm, bn), lambda i, j: (i, j)),
        out_shape=jax.ShapeDtypeStruct((M, N), out_dtype),
        scratch_shapes=[pltpu.VMEM((bm, K), BF16)],
        compiler_params=_params("parallel", "arbitrary"),
        name=name,
    )(x, g, w_t)


DECAY_SCALE = float(np.exp(-0.5))


def _rw_prep_kernel(c_ref, cp_ref, cn_ref, mu_ref, w0_ref, w2_ref, a0_ref, a2_ref, g2_ref, kk_ref, ka_ref,
                    seg_ref, r_out, v_out, a_out, dec0_out, dec1_out, kd0_out, kd1_out, b0_out, b1_out,
                    gate_out, buf_ref, *, bt, seq):
    i = pl.program_id(0)
    pos = (i * bt) % seq
    prev_ok = (pos != 0).astype(F32)
    next_ok = (pos + bt != seq).astype(F32)
    buf_ref[0:8, :] = cp_ref[...] * prev_ok
    buf_ref[8:bt + 8, :] = c_ref[...]
    buf_ref[bt + 8:bt + 16, :] = cn_ref[...] * next_ok
    mu = mu_ref[...]
    c = c_ref[...] * (1.0 - mu) + (buf_ref[7:bt + 7, :] + buf_ref[9:bt + 9, :]) * (0.5 * mu)
    C = RW_WIDTH
    r = c[:, 0:C]
    k = c[:, C:2 * C]
    v = c[:, 2 * C:3 * C]
    o = 3 * C
    wd = c[:, o:o + 2 * RW_DECAY_LORA]
    o += 2 * RW_DECAY_LORA
    ad = c[:, o:o + 2 * RW_AAA_LORA]
    o += 2 * RW_AAA_LORA
    gd = c[:, o:o + RW_GATE_LORA]
    seg = seg_ref[...]
    kk = k * kk_ref[...]
    kk = kk * lax.rsqrt(jnp.maximum(_seg_sum(kk * kk, seg), 1e-24))
    r_out[...] = r
    v_out[...] = v
    a_out[...] = -kk
    gate_out[...] = _dot_split(_sigmoid(gd), g2_ref[0], g2_ref[1])
    twd = jnp.tanh(wd)
    for d, (dec_out, kd_out, b_out) in enumerate(((dec0_out, kd0_out, b0_out), (dec1_out, kd1_out, b1_out))):
        lw = _dot_split(twd[:, d * RW_DECAY_LORA:(d + 1) * RW_DECAY_LORA], w2_ref[0, d], w2_ref[1, d])
        dec_out[...] = jnp.exp(-DECAY_SCALE * _sigmoid(w0_ref[d:d + 1, :] + lw))
        la = _dot_split(ad[:, d * RW_AAA_LORA:(d + 1) * RW_AAA_LORA], a2_ref[0, d], a2_ref[1, d])
        lr = _sigmoid(a0_ref[d:d + 1, :] + la)
        kd_out[...] = k * (1.0 + (lr - 1.0) * ka_ref[...])
        b_out[...] = kk * lr


def rw_prep(z, rw_col_block, mu, w0, w2, a0, a2, g2, k_k, k_a, seg, seq, bt):
    M = z.shape[0]
    C = RW_WIDTH
    W = RW_COLS_PAD
    nb8 = M // 8
    full = lambda shape: pl.BlockSpec(shape, lambda i: (0,) * len(shape))
    out = jax.ShapeDtypeStruct((M, C), F32)
    return pl.pallas_call(
        functools.partial(_rw_prep_kernel, bt=bt, seq=seq),
        grid=(M // bt,),
        in_specs=[pl.BlockSpec((bt, W), lambda i: (i, rw_col_block)),
                  pl.BlockSpec((8, W), lambda i: (jnp.maximum(i * (bt // 8) - 1, 0), rw_col_block)),
                  pl.BlockSpec((8, W), lambda i: (jnp.minimum((i + 1) * (bt // 8), nb8 - 1), rw_col_block)),
                  full((1, W)), full((2, C)), full((2, 2, RW_DECAY_LORA, C)), full((2, C)),
                  full((2, 2, RW_AAA_LORA, C)), full((2, RW_GATE_LORA, C)), full((1, C)), full((1, C)),
                  full((C, C))],
        out_specs=[pl.BlockSpec((bt, C), lambda i: (i, 0))] * 10,
        out_shape=[out] * 10,
        scratch_shapes=[pltpu.VMEM((bt + 16, W), F32)],
        compiler_params=_params("parallel"),
        name="rw_prep",
    )(z, z, z, mu, w0, _split_hi_lo(w2), a0, _split_hi_lo(a2), _split_hi_lo(g2), k_k, k_a, seg)


def _rw_scan_kernel(w_ref, k_ref, a_ref, b_ref, r_ref, v_ref, y_ref, s_ref, *, tb, pitch, vpitch):
    N = HEAD_DIM
    NV = N // 2

    @pl.when(pl.program_id(0) == 0)
    def _():
        s_ref[...] = jnp.zeros_like(s_ref)

    zero = jnp.zeros(s_ref.shape[1:], F32)

    def dot_a(c, acc):
        for u in range(SCAN_UNROLL):
            j = c * SCAN_UNROLL + u
            acc = acc + s_ref[j] * a_ref[pl.ds(j, 1), :]
        return acc

    sa_first = lax.fori_loop(0, N // SCAN_UNROLL, dot_a, zero)

    def step(t, sa):
        base = t * pitch
        nbase = jnp.minimum(t + 1, tb - 1) * pitch
        vbase = pl.multiple_of(t * vpitch, 8)
        row = lambda ref, j: ref[pl.ds(base + j, 1), :]
        vt = v_ref[pl.ds(vbase, NV), :]

        def update(c, acc):
            y0, y1, n0, n1 = acc
            for u in range(SCAN_UNROLL):
                j = c * SCAN_UNROLL + u
                s = s_ref[j] * row(w_ref, j) + sa * row(b_ref, j) + vt * row(k_ref, j)
                s_ref[j] = s
                ty = s * row(r_ref, j)
                tn = s * a_ref[pl.ds(nbase + j, 1), :]
                if u % 2 == 0:
                    y0, n0 = y0 + ty, n0 + tn
                else:
                    y1, n1 = y1 + ty, n1 + tn
            return y0, y1, n0, n1

        y0, y1, n0, n1 = lax.fori_loop(0, N // SCAN_UNROLL, update, (zero, zero, zero, zero))
        y_ref[pl.ds(vbase, NV), :] = y0 + y1
        y_ref[pl.ds(vbase + NV, vpitch - NV), :] = jnp.zeros((vpitch - NV, zero.shape[1]), F32)
        return n0 + n1

    lax.fori_loop(0, tb, step, sa_first)


def rw_scan(w, k, a, b, r, v, T, tb, pitch, vpitch):
    L = v.shape[-1]
    kspec = pl.BlockSpec((tb * pitch, L), lambda i: (i, 0))
    vspec = pl.BlockSpec((tb * vpitch, L), lambda i: (i, 0))
    return pl.pallas_call(
        functools.partial(_rw_scan_kernel, tb=tb, pitch=pitch, vpitch=vpitch),
        grid=(T // tb,),
        in_specs=[kspec] * 5 + [vspec],
        out_specs=vspec,
        out_shape=jax.ShapeDtypeStruct((T * vpitch, L), F32),
        scratch_shapes=[pltpu.VMEM((HEAD_DIM, HEAD_DIM // 2, L), F32)],
        compiler_params=_params("arbitrary"),
        name="rw_scan",
    )(w, k, a, b, r, v)


def _rw_post_kernel(yf_ref, yb_ref, r_ref, v_ref, kd0_ref, kd1_ref, gate_ref, rk_ref, g_ref, b_ref, seg_ref,
                    o_ref):
    seg = seg_ref[...]
    y = yf_ref[...] + yb_ref[...]
    inv_n = 1.0 / HEAD_DIM
    mean = _seg_sum(y, seg) * inv_n
    d = y - mean
    var = _seg_sum(d * d, seg) * inv_n
    yn = d * lax.rsqrt(var + RW_LNX_EPS) * g_ref[...] + b_ref[...]
    r = r_ref[...]
    rk = rk_ref[...]
    bonus = _seg_sum(r * kd0_ref[...] * rk + r * kd1_ref[...] * rk, seg) * v_ref[...]
    o_ref[...] = (yn + bonus) * gate_ref[...]


def rw_post(yf, yb, r, v, kd0, kd1, gate, rk, g, b, seg, bt):
    M, C = yf.shape
    tok = pl.BlockSpec((bt, C), lambda i: (i, 0))
    row = pl.BlockSpec((1, C), lambda i: (0, 0))
    return pl.pallas_call(
        _rw_post_kernel,
        grid=(M // bt,),
        in_specs=[tok] * 7 + [row] * 3 + [pl.BlockSpec((C, C), lambda i: (0, 0))],
        out_specs=tok,
        out_shape=jax.ShapeDtypeStruct((M, C), F32),
        compiler_params=_params("parallel"),
        name="rw_post",
    )(yf, yb, r, v, kd0, kd1, gate, rk, g, b, seg)


def _dil_kernel(q_ref, k_ref, v_ref, cos_ref, sin_ref, o_ref, lse_ref, qs_ref, ks_ref, vs_ref, *, n_sub, dil, qb,
                kwin, half):
    W = DIL_STEP_WIDTH
    lane = lax.broadcasted_iota(jnp.int32, (1, W), 1)
    first_half = (lane % HEAD_DIM) < HALF_DIM

    def rope(z, c, s):
        swapped = jnp.where(first_half, pltpu.roll(z, W - HALF_DIM, 1), pltpu.roll(z, HALF_DIM, 1))
        return z * c + swapped * s

    rc = min(256, n_sub)

    def subsequence(r, carry):
        def rope_chunk(i, carry):
            rows = pl.ds(r + i * (rc * dil), rc, stride=dil)
            sl = pl.ds(pl.multiple_of(i * rc, rc), rc)
            c = cos_ref[rows, :]
            s = sin_ref[rows, :]
            qs_ref[sl, :] = (rope(q_ref[0, rows, :], c, s) * SCALE).astype(BF16)
            ks_ref[sl, :] = rope(k_ref[0, rows, :], c, s).astype(BF16)
            vs_ref[sl, :] = v_ref[0, rows, :].astype(BF16)
            return carry

        lax.fori_loop(0, n_sub // rc, rope_chunk, 0)

        nblk = n_sub // qb
        unroll = next(u for u in (4, 2, 1) if nblk % u == 0)
        low_head = lane < HEAD_DIM

        def block_group(ig, carry):
            chains = []
            for u in range(unroll):
                q0 = pl.multiple_of((ig * unroll + u) * qb, qb)
                k0 = pl.multiple_of(jnp.clip(q0 - (kwin - qb) // 2, 0, n_sub - kwin), 64)
                q = qs_ref[pl.ds(q0, qb), :]
                kk = ks_ref[pl.ds(k0, kwin), :]
                jq = q0 + lax.broadcasted_iota(jnp.int32, (qb, kwin), 0)
                jk = k0 + lax.broadcasted_iota(jnp.int32, (qb, kwin), 1)
                ok = jnp.abs(jk - jq) <= half
                for h in range(2):
                    qh = jnp.where(low_head == (h == 0), q, jnp.zeros_like(q))
                    chains.append((q0, k0, jnp.where(ok, _dot_nt(qh, kk), NEG_INF)))
            probs = []
            for q0, k0, s in chains:
                m = jnp.max(s, axis=-1, keepdims=True)
                p = jnp.exp(s - m)
                l = jnp.sum(p, axis=-1, keepdims=True)
                probs.append((q0, k0, p.astype(BF16), l, m + jnp.log(l)))
            for c in range(0, len(probs), 2):
                (q0, k0, p0, l0, lse0), (_, _, p1, l1, lse1) = probs[c], probs[c + 1]
                vv = vs_ref[pl.ds(k0, kwin), :]
                rows = pl.ds(r + q0 * dil, qb, stride=dil)
                o_ref[0, rows, :] = jnp.where(low_head, _dot(p0, vv) / l0, _dot(p1, vv) / l1)
                lse_ref[0, rows, :] = jnp.where(low_head, lse0, lse1)
            return carry

        lax.fori_loop(0, nblk // unroll, block_group, 0)
        return carry

    lax.fori_loop(0, dil, subsequence, 0)


def dilated_group(z3, q_blk, k_blk, v_blk, cos_t, sin_t, window, dil):
    B, T, ncols = z3.shape
    W = DIL_STEP_WIDTH
    steps = DIL_GROUP_WIDTH // W
    n_sub = T // dil
    half = window // (2 * dil)
    qb = min(128, n_sub)
    kwin = min(qb + 2 * half, n_sub)
    zspec = lambda blk: pl.BlockSpec((1, T, W), lambda b, s: (b, 0, blk * steps + s))
    tspec = pl.BlockSpec((T, W), lambda b, s: (0, 0))
    ospec = pl.BlockSpec((1, T, W), lambda b, s: (b, 0, s))
    oshape = jax.ShapeDtypeStruct((B, T, DIL_GROUP_WIDTH), F32)
    o, lse = pl.pallas_call(
        functools.partial(_dil_kernel, n_sub=n_sub, dil=dil, qb=qb, kwin=kwin, half=half),
        grid=(B, steps),
        in_specs=[zspec(q_blk), zspec(k_blk), zspec(v_blk), tspec, tspec],
        out_specs=[ospec, ospec],
        out_shape=[oshape, oshape],
        scratch_shapes=[pltpu.VMEM((n_sub, W), BF16)] * 3,
        compiler_params=_params("parallel", "parallel"),
        name=f"dilated_d{dil}",
    )(z3, z3, z3, cos_t, sin_t)
    return o.reshape(B * T, DIL_GROUP_WIDTH), lse.reshape(B * T, DIL_GROUP_WIDTH)


def _na_kernel(q_ref, k_ref, v_ref, tab_ref, o_ref, qs_ref, ks_ref, vs_ref, bias_ref, *, rows):
    nkeys = NA_KH * GRID_W
    pc = NA_KH * GRID_W

    for off in range(NA_KH):
        for h in range(NA_GROUP_HEADS):
            for ky in range(NA_KH):
                bias_ref[off, h, :, ky * GRID_W:(ky + 1) * GRID_W] = tab_ref[h, ky - off + NA_KH - 1]

    PW = 2 * HEAD_DIM
    pairs = NA_GROUP_HEADS // 2
    low_head = lax.broadcasted_iota(jnp.int32, (1, PW), 1) < HEAD_DIM

    def prep(c, carry):
        sl = pl.ds(pl.multiple_of(c * pc, pc), pc)
        q = q_ref[0, sl, :] * SCALE
        k = k_ref[0, sl, :]
        v = v_ref[0, sl, :]
        for pp in range(pairs):
            ps = slice(pp * PW, (pp + 1) * PW)
            qs_ref[pp, sl, :] = q[:, ps].astype(BF16)
            ks_ref[pp, sl, :] = k[:, ps].astype(BF16)
            vs_ref[pp, sl, :] = v[:, ps].astype(BF16)
        return carry

    lax.fori_loop(0, rows * GRID_W // pc, prep, 0)

    def row_group(rg, carry):
        chains = []
        for u in range(NA_ROW_UNROLL):
            r = rg * NA_ROW_UNROLL + u
            r0 = jnp.clip(r - NA_KH // 2, 0, rows - NA_KH)
            qrow = pl.ds(pl.multiple_of(r * GRID_W, GRID_W), GRID_W)
            krow = pl.ds(pl.multiple_of(r0 * GRID_W, GRID_W), nkeys)
            for pp in range(pairs):
                q = qs_ref[pp, qrow, :]
                kk = ks_ref[pp, krow, :]
                for h in range(2):
                    qh = jnp.where(low_head == (h == 0), q, jnp.zeros_like(q))
                    chains.append((qrow, krow, pp, _dot_nt(qh, kk) + bias_ref[r - r0, 2 * pp + h]))
        probs = []
        for qrow, krow, pp, s in chains:
            m = jnp.max(s, axis=-1, keepdims=True)
            p = jnp.exp(s - m)
            l = jnp.sum(p, axis=-1, keepdims=True)
            probs.append((qrow, krow, pp, p.astype(BF16), l))
        outs = []
        for c in range(0, len(probs), 2):
            (qrow, krow, pp, p0, l0), (_, _, _, p1, l1) = probs[c], probs[c + 1]
            vv = vs_ref[pp, krow, :]
            outs.append(jnp.where(low_head, _dot(p0, vv) / l0, _dot(p1, vv) / l1))
            if len(outs) == pairs:
                o_ref[0, qrow, :] = jnp.concatenate(outs, axis=-1)
                outs = []
        return carry

    lax.fori_loop(0, rows // NA_ROW_UNROLL, row_group, 0)


def na_bias_table(rel_bias):
    qc = np.arange(GRID_W)
    kc = np.arange(GRID_W)
    wc0 = np.clip(qc - NA_KW // 2, 0, GRID_W - NA_KW)
    col_ok = (kc[None, :] >= wc0[:, None]) & (kc[None, :] < wc0[:, None] + NA_KW)
    dx_idx = np.clip(kc[None, :] - qc[:, None], 1 - NA_KW, NA_KW - 1) + NA_KW - 1
    onehot = jnp.asarray(dx_idx[None] == np.arange(2 * NA_KW - 1)[:, None, None], F32)
    tab = jnp.einsum('hyd,dqk->hyqk', rel_bias.astype(F32), onehot, precision=HIGHEST)
    return jnp.where(jnp.asarray(col_ok)[None, None], tab, NEG_INF)


def neighbourhood(z3, q_blk0, k_blk0, v_blk0, bias_tab):
    B, T, ncols = z3.shape
    rows = T // GRID_W
    assert rows >= NA_KH
    W = NA_GROUP_WIDTH
    G = NA_HEADS // NA_GROUP_HEADS
    zspec = lambda blk0: pl.BlockSpec((1, T, W), lambda b, g: (b, 0, blk0 + g))
    return pl.pallas_call(
        functools.partial(_na_kernel, rows=rows),
        grid=(B, G),
        in_specs=[zspec(q_blk0), zspec(k_blk0), zspec(v_blk0),
                  pl.BlockSpec((NA_GROUP_HEADS, 2 * NA_KH - 1, GRID_W, GRID_W), lambda b, g: (g, 0, 0, 0))],
        out_specs=pl.BlockSpec((1, T, W), lambda b, g: (b, 0, g)),
        out_shape=jax.ShapeDtypeStruct((B, T, NA_WIDTH), F32),
        scratch_shapes=[pltpu.VMEM((NA_GROUP_HEADS // 2, T, 2 * HEAD_DIM), BF16)] * 3
                       + [pltpu.VMEM((NA_KH, NA_GROUP_HEADS, GRID_W, NA_KH * GRID_W), F32)],
        compiler_params=_params("parallel", "parallel"),
        name="neighbourhood",
    )(z3, z3, z3, bias_tab)


def _merge_kernel(ya_ref, o1_ref, o2_ref, o3_ref, l1_ref, l2_ref, l3_ref, yc_ref, ga_ref, gb_ref, gc_ref,
                  wa_ref, wb_ref, wc_ref, out_ref, ya16_ref, yb16_ref, yc16_ref, *, bm):
    @pl.when(pl.program_id(1) == 0)
    def _():
        l1, l2, l3 = l1_ref[...], l2_ref[...], l3_ref[...]
        m = jnp.maximum(jnp.maximum(l1, l2), l3)
        e1, e2, e3 = jnp.exp(l1 - m), jnp.exp(l2 - m), jnp.exp(l3 - m)
        yb = (e1 * o1_ref[...] + e2 * o2_ref[...] + e3 * o3_ref[...]) / (e1 + e2 + e3)
        yb16_ref[...] = yb.astype(BF16)
        ya16_ref[...] = ya_ref[...].astype(BF16)
        yc16_ref[...] = yc_ref[...].astype(BF16)

    hb = bm // 2
    halves = [slice(s * hb, (s + 1) * hb) for s in range(2)]
    prods = [(_dot(ya16_ref[rs, :], wa_ref[...]), _dot(yb16_ref[rs, :], wb_ref[...]),
              _dot(yc16_ref[rs, :], wc_ref[...])) for rs in halves]
    gate = lambda ref, rs: _sigmoid(ref[rs, :]).astype(F32)
    for rs, (pa, pb, pc) in zip(halves, prods):
        out = gate(ga_ref, rs) * pa + gate(gb_ref, rs) * pb + gate(gc_ref, rs) * pc
        out_ref[rs, :] = out.astype(BF16)


def merge(ya, dil_outs, yc, z, layer, wa, wb, wc, bm, bn):
    M = ya.shape[0]
    D = wa.shape[-1]
    nj = D // bn
    tok = lambda w: pl.BlockSpec((bm, w), lambda i, j: (i, 0))
    gate = lambda g: pl.BlockSpec((bm, bn), lambda i, j: (i, g * nj + j))
    wsp = lambda k: pl.BlockSpec((None, k, bn), lambda i, j: (layer, 0, j))
    (o1, l1), (o2, l2), (o3, l3) = dil_outs
    W = DIL_GROUP_WIDTH
    return pl.pallas_call(
        functools.partial(_merge_kernel, bm=bm),
        grid=(M // bm, nj),
        in_specs=[tok(RW_WIDTH)] + [tok(W)] * 6 + [tok(NA_WIDTH), gate(0), gate(1), gate(2),
                                                  wsp(RW_WIDTH), wsp(W), wsp(NA_WIDTH)],
        out_specs=pl.BlockSpec((bm, bn), lambda i, j: (i, j)),
        out_shape=jax.ShapeDtypeStruct((M, D), BF16),
        scratch_shapes=[pltpu.VMEM((bm, RW_WIDTH), BF16), pltpu.VMEM((bm, W), BF16),
                        pltpu.VMEM((bm, NA_WIDTH), BF16)],
        compiler_params=_params("parallel", "arbitrary"),
        name="merge",
    )(ya, o1, o2, o3, l1, l2, l3, yc, z, z, z, wa, wb, wc)


def _matmul_res_kernel(a_ref, w_ref, x_ref, o_ref):
    o_ref[...] = x_ref[...] + _dot(a_ref[...], w_ref[...])


def matmul_res(a, w, layer, x, bm, bn):
    M, K = a.shape
    N = w.shape[-1]
    return pl.pallas_call(
        _matmul_res_kernel,
        grid=(M // bm, N // bn),
        in_specs=[pl.BlockSpec((bm, K), lambda i, j: (i, 0)),
                  pl.BlockSpec((None, K, bn), lambda i, j: (layer, 0, j)),
                  pl.BlockSpec((bm, bn), lambda i, j: (i, j))],
        out_specs=pl.BlockSpec((bm, bn), lambda i, j: (i, j)),
        out_shape=jax.ShapeDtypeStruct((M, N), F32),
        compiler_params=_params("parallel", "parallel"),
        name="out_proj",
    )(a, w, x)


def _gelu(x):
    return 0.5 * x * (1.0 + jnp.tanh(np.sqrt(2.0 / np.pi).astype(np.float32) * (x + 0.044715 * (x * x * x))))


FFN_HALO = 16
FFN_SPLIT = 2


def _ffn_kernel(x_ref, xp_ref, xn_ref, g_ref, wg_ref, wu_ref, cw_ref, cb_ref, wd_ref, o_ref,
                h_ref, gs_ref, *, bm, bf, seq):
    i = pl.program_id(0)
    f = pl.program_id(1)
    HL = FFN_HALO

    @pl.when(f == 0)
    def _():
        g = g_ref[...]
        h_ref[0:HL, :] = _rms(xp_ref[...], g).astype(BF16)
        h_ref[HL:bm + HL, :] = _rms(x_ref[...], g).astype(BF16)
        h_ref[bm + HL:bm + 2 * HL, :] = _rms(xn_ref[...], g).astype(BF16)
        o_ref[...] = x_ref[...]

    pos = (i * bm) % seq
    prev_ok = (pos != 0).astype(F32)
    next_ok = (pos + bm != seq).astype(F32)
    hw = bf // FFN_SPLIT
    h_ext = h_ref[...]
    h = h_ref[HL:bm + HL, :]
    cw = cw_ref[...]
    cb = cb_ref[...]
    gates, ups = [], []
    for s in range(FFN_SPLIT):
        cs = slice(s * hw, (s + 1) * hw)
        gates.append(_dot(h_ext, wg_ref[:, cs]))
        ups.append(_dot(h, wu_ref[:, cs]))
    acts = []
    for s in range(FFN_SPLIT):
        cs = slice(s * hw, (s + 1) * hw)
        gs_ref[s] = gates[s]
        gs_ref[s, HL - 8:HL, :] = gs_ref[s, HL - 8:HL, :] * prev_ok
        gs_ref[s, bm + HL:bm + HL + 8, :] = gs_ref[s, bm + HL:bm + HL + 8, :] * next_ok
        gc = (gs_ref[s, HL - 1:bm + HL - 1, :] * cw[0:1, cs] + gs_ref[s, HL:bm + HL, :] * cw[1:2, cs]
              + gs_ref[s, HL + 1:bm + HL + 1, :] * cw[2:3, cs] + cb[:, cs])
        acts.append((_gelu(gc) * ups[s]).astype(BF16))
    upd = _dot(acts[0], wd_ref[0:hw, :])
    for s in range(1, FFN_SPLIT):
        upd = upd + _dot(acts[s], wd_ref[s * hw:(s + 1) * hw, :])
    o_ref[...] += upd


def ffn(x, g, layer, wg, wu, cw, cb, wd, seq, bm, bf):
    M, D = x.shape
    F = wg.shape[-1]
    HL = FFN_HALO
    nbh = M // HL
    return pl.pallas_call(
        functools.partial(_ffn_kernel, bm=bm, bf=bf, seq=seq),
        grid=(M // bm, F // bf),
        in_specs=[pl.BlockSpec((bm, D), lambda i, f: (i, 0)),
                  pl.BlockSpec((HL, D), lambda i, f: (jnp.maximum(i * (bm // HL) - 1, 0), 0)),
                  pl.BlockSpec((HL, D), lambda i, f: (jnp.minimum((i + 1) * (bm // HL), nbh - 1), 0)),
                  pl.BlockSpec((1, D), lambda i, f: (0, 0)),
                  pl.BlockSpec((None, D, bf), lambda i, f: (layer, 0, f)),
                  pl.BlockSpec((None, D, bf), lambda i, f: (layer, 0, f)),
                  pl.BlockSpec((3, bf), lambda i, f: (0, f)),
                  pl.BlockSpec((1, bf), lambda i, f: (0, f)),
                  pl.BlockSpec((None, bf, D), lambda i, f: (layer, f, 0))],
        out_specs=pl.BlockSpec((bm, D), lambda i, f: (i, 0)),
        out_shape=jax.ShapeDtypeStruct((M, D), F32),
        scratch_shapes=[pltpu.VMEM((bm + 2 * HL, D), BF16),
                        pltpu.VMEM((FFN_SPLIT, bm + 2 * HL, bf // FFN_SPLIT), F32)],
        compiler_params=_params("parallel", "arbitrary"),
        name="conv_ffn",
    )(x, x, x, g, wg, wu, cw, cb, wd)


def _ple_kernel(x_ref, p_ref, g_ref, wg_ref, wp_ref, gf_ref, o_ref, *, final_norm):
    x = x_ref[...]
    h = _rms(x, g_ref[...]).astype(BF16)
    gate = _sigmoid(_dot(h, wg_ref[...]))
    y = x + gate * _dot(p_ref[...].astype(BF16), wp_ref[...])
    if final_norm:
        y = _rms(y, gf_ref[...])
    o_ref[...] = y


def ple(x, p, layer, g, wg, wp, gf, bm, final_norm):
    M, D = x.shape
    P = p.shape[-1]
    return pl.pallas_call(
        functools.partial(_ple_kernel, final_norm=final_norm),
        grid=(M // bm,),
        in_specs=[pl.BlockSpec((bm, D), lambda i: (i, 0)),
                  pl.BlockSpec((None, bm, P), lambda i: (layer, i, 0)),
                  pl.BlockSpec((1, D), lambda i: (0, 0)),
                  pl.BlockSpec((None, D, D), lambda i: (layer, 0, 0)),
                  pl.BlockSpec((None, P, D), lambda i: (layer, 0, 0)),
                  pl.BlockSpec((1, D), lambda i: (0, 0))],
        out_specs=pl.BlockSpec((bm, D), lambda i: (i, 0)),
        out_shape=jax.ShapeDtypeStruct((M, D), F32),
        compiler_params=_params("parallel"),
        name="ple_final" if final_norm else "ple",
    )(x, p, g, wg, wp, gf)


def _to_scan_kernel(x0_ref, x1_ref, rev_ref, o_ref, a_ref, *, nb, tb, dup, pitch, apitch):
    N = HEAD_DIM
    NV = N // 2
    lanes = 2 * 2 * nb * RW_HEADS
    n_rows = N if dup else NV
    rev = rev_ref[...]
    for d, ref in enumerate((x0_ref, x1_ref)):
        for b in range(nb):
            x = ref[b]
            if d == 1:
                x = _dot_hi(rev, x)
            xt = x.T
            for h in range(RW_HEADS):
                slot = ((d * nb + b) * RW_HEADS + h) * 2
                for s in range(2):
                    blk = xt[h * N:(h + 1) * N] if dup else xt[h * N + s * NV:h * N + (s + 1) * NV]
                    a_ref[(slot + s) * apitch:(slot + s) * apitch + n_rows, :] = blk
    for j in range(n_rows):
        rows = a_ref[pl.ds(j, lanes, stride=apitch), :]
        o_ref[pl.ds(j, tb, stride=pitch), :] = rows.T
    for j in range(n_rows, pitch):
        o_ref[pl.ds(j, tb, stride=pitch), :] = jnp.zeros((tb, lanes), F32)


def to_scan(x0, x1, rev, B, T, tb, dup, pitch):
    C = RW_WIDTH
    nblk = T // tb
    lanes = 4 * B * RW_HEADS
    apitch = (HEAD_DIM if dup else HEAD_DIM // 2) + 4
    return pl.pallas_call(
        functools.partial(_to_scan_kernel, nb=B, tb=tb, dup=dup, pitch=pitch, apitch=apitch),
        grid=(nblk,),
        in_specs=[pl.BlockSpec((B, tb, C), lambda i: (0, i, 0)),
                  pl.BlockSpec((B, tb, C), lambda i: (0, nblk - 1 - i, 0)),
                  pl.BlockSpec((tb, tb), lambda i: (0, 0))],
        out_specs=pl.BlockSpec((tb * pitch, lanes), lambda i: (i, 0)),
        out_shape=jax.ShapeDtypeStruct((T * pitch, lanes), F32),
        scratch_shapes=[pltpu.VMEM((lanes * apitch, tb), F32)],
        compiler_params=_params("parallel"),
        name="to_scan_keys" if dup else "to_scan_values",
    )(x0.reshape(B, T, C), x1.reshape(B, T, C), rev)


def _from_scan_kernel(y_ref, rev_ref, yf_ref, yb_ref, a_ref, *, nb, tb, pitch, apitch):
    NV = HEAD_DIM // 2
    lanes = 2 * 2 * nb * RW_HEADS
    for ih in range(NV):
        rows = y_ref[pl.ds(ih, tb, stride=pitch), :]
        a_ref[pl.ds(ih, lanes, stride=apitch), :] = rows.T
    rev = rev_ref[...]
    for d, ref in enumerate((yf_ref, yb_ref)):
        for b in range(nb):
            slot0 = (d * nb + b) * RW_HEADS * 2
            pieces = [a_ref[(slot0 + s) * apitch:(slot0 + s) * apitch + NV, :] for s in range(2 * RW_HEADS)]
            y = jnp.concatenate(pieces, axis=0).T
            if d == 1:
                y = _dot_hi(rev, y)
            ref[b] = y


def from_scan(y, rev, B, T, tb, pitch):
    C = RW_WIDTH
    nblk = T // tb
    lanes = y.shape[-1]
    apitch = HEAD_DIM // 2 + 4
    out = jax.ShapeDtypeStruct((B, T, C), F32)
    yf, yb = pl.pallas_call(
        functools.partial(_from_scan_kernel, nb=B, tb=tb, pitch=pitch, apitch=apitch),
        grid=(nblk,),
        in_specs=[pl.BlockSpec((tb * pitch, lanes), lambda i: (i, 0)),
                  pl.BlockSpec((tb, tb), lambda i: (0, 0))],
        out_specs=[pl.BlockSpec((B, tb, C), lambda i: (0, i, 0)),
                   pl.BlockSpec((B, tb, C), lambda i: (0, nblk - 1 - i, 0))],
        out_shape=[out, out],
        scratch_shapes=[pltpu.VMEM((lanes * apitch, tb), F32)],
        compiler_params=_params("parallel"),
        name="from_scan",
    )(y, rev)
    return yf.reshape(B * T, C), yb.reshape(B * T, C)


def split_w_in(w_in, o_dil, o_na, o_g):
    depth, K, _ = w_in.shape
    w_t = jnp.swapaxes(w_in, 1, 2)
    w_b = jnp.concatenate([w_t[:, o_g:].astype(BF16), w_t[:, o_na:o_g].astype(BF16)], axis=1)
    w_f = jnp.concatenate([w_t[:, :o_dil].astype(BF16), jnp.zeros((depth, RW_COLS_PAD - o_dil, K), BF16),
                           w_t[:, o_dil:o_na].astype(BF16)], axis=1)
    return w_b, w_f


def _rope_tables(T):
    inv = ROPE_THETA ** (-jnp.arange(0, HEAD_DIM, 2, dtype=jnp.float32) / HEAD_DIM)
    ang = jnp.arange(T, dtype=jnp.float32)[:, None] * inv[None, :]
    cos, sin = jnp.cos(ang), jnp.sin(ang)
    cos_t = jnp.tile(jnp.concatenate([cos, cos], axis=-1), (1, DIL_STEP_WIDTH // HEAD_DIM))
    sin_t = jnp.tile(jnp.concatenate([-sin, sin], axis=-1), (1, DIL_STEP_WIDTH // HEAD_DIM))
    return cos_t, sin_t


def _pick(n, prefs):
    for c in prefs:
        if n % c == 0:
            return c
    return n


def kernel(x, p, norm_mix, w_in, rw_mu, rw_w0, rw_w2, rw_a0, rw_a2, rw_g2, rw_k_k, rw_k_a, rw_r_k, rw_lnx_g,
           rw_lnx_b, na_bias, w_br_a, w_br_b, w_br_c, w_out, norm_ffn, w_ffn_gate, w_ffn_up, ffn_conv_w,
           ffn_conv_b, w_ffn_down, norm_ple, w_ple_gate, w_ple, norm_final):
    B, T, D = x.shape
    depth = w_in.shape[0]
    M = B * T
    F = w_ffn_gate.shape[-1]
    C = RW_WIDTH
    W = DIL_GROUP_WIDTH
    assert (N_BRANCH * D) % W == 0 and w_in.shape[-1] == RW_COLS + 2 * 3 * DIL_WIDTH + N_BRANCH * D

    gates_w = N_BRANCH * D
    dil_w = 3 * DIL_WIDTH
    na_w = 3 * NA_WIDTH
    nb_cols = gates_w + na_w
    nf_cols = RW_COLS_PAD + dil_w
    na_blk0 = gates_w // W
    dil_blk0 = RW_COLS_PAD // W
    assert RW_COLS_PAD % W == 0

    cos_t, sin_t = _rope_tables(T)
    seg = jnp.asarray(np.kron(np.eye(RW_HEADS), np.ones((HEAD_DIM, HEAD_DIM))), BF16)
    xf = x.reshape(M, D)
    p3 = p.reshape(depth, M, p.shape[-1])
    bm_in = _pick(M, (1024, 512, 256, 128))
    bt_rw = _pick(T, (256, 128))
    tb = _pick(T, (64, 32))
    tt = _pick(T, (128,))
    rev = jnp.asarray(np.eye(tt)[::-1], F32)

    wb16 = {name: wt.astype(BF16) for name, wt in (
        ("br_a", w_br_a), ("br_b", w_br_b), ("br_c", w_br_c), ("out", w_out), ("ffn_gate", w_ffn_gate),
        ("ffn_up", w_ffn_up), ("ffn_down", w_ffn_down), ("ple_gate", w_ple_gate), ("ple", w_ple))}
    o_dil, o_na, o_g = RW_COLS, RW_COLS + dil_w, RW_COLS + dil_w + na_w
    w_b, w_f = split_w_in(w_in, o_dil, o_na, o_g)

    for i in range(depth):
        g_mix = norm_mix[i][None]
        zb = norm_matmul(xf, g_mix, w_b, i, _pick(M, (512, 256, 128)), _pick(nb_cols, (2816, 768, 256, 128)), BF16,
                         "in_proj_bf16")
        zf = norm_matmul(xf, g_mix, w_f, i, bm_in, _pick(nf_cols, (1024, 256, 128)), F32, "in_proj_f32")
        zb3 = zb.reshape(B, T, nb_cols)
        zf3 = zf.reshape(B, T, nf_cols)

        mu = jnp.pad(rw_mu[i], (0, RW_COLS_PAD - RW_COLS))[None]
        r, v, a, dec0, dec1, kd0, kd1, b0, b1, gate = rw_prep(
            zf, 0, mu, rw_w0[i], rw_w2[i], rw_a0[i], rw_a2[i], rw_g2[i], rw_k_k[i][None], rw_k_a[i][None],
            seg, T, bt_rw)
        keys = lambda x0, x1: to_scan(x0, x1, rev, B, T, tt, dup=True, pitch=KEY_PITCH)
        vals = to_scan(v, v, rev, B, T, tt, dup=False, pitch=VAL_PITCH)
        y = rw_scan(keys(dec0, dec1), keys(kd0, kd1), keys(a, a), keys(b0, b1), keys(r, r), vals, T, tb,
                    KEY_PITCH, VAL_PITCH)
        yf, yb = from_scan(y, rev, B, T, tt, VAL_PITCH)
        ya = rw_post(yf, yb, r, v, kd0, kd1, gate, rw_r_k[i].reshape(1, C), rw_lnx_g[i][None], rw_lnx_b[i][None],
                     seg, _pick(T, (512, 256, 128)))

        dil_outs = []
        for g, (window, dil) in enumerate(DIL_PATTERNS):
            dil_outs.append(dilated_group(zf3, dil_blk0 + g, dil_blk0 + 3 + g, dil_blk0 + 6 + g, cos_t, sin_t,
                                          window, dil))

        yc = neighbourhood(zb3, na_blk0, na_blk0 + 3, na_blk0 + 6, na_bias_table(na_bias[i])).reshape(M, NA_WIDTH)

        merged = merge(ya, dil_outs, yc, zb, i, wb16["br_a"], wb16["br_b"], wb16["br_c"],
                       _pick(M, (1024, 512, 256, 128)), _pick(D, (512, 256, 128)))
        xf = matmul_res(merged, wb16["out"], i, xf, _pick(M, (512, 256, 128)), D)

        xf = ffn(xf, norm_ffn[i][None], i, wb16["ffn_gate"], wb16["ffn_up"], ffn_conv_w[i],
                 ffn_conv_b[i][None], wb16["ffn_down"], T, _pick(T, (512, 256, 128)),
                 _pick(F, (512, 256, 128)))

        xf = ple(xf, p3, i, norm_ple[i][None], wb16["ple_gate"], wb16["ple"],
                 norm_final[None], _pick(M, (512, 256, 128)), final_norm=(i == depth - 1))
    return xf.reshape(B, T, D)
```

```python
import functools

import numpy as np
import jax
import jax.numpy as jnp
from jax import lax
from jax.experimental import pallas as pl
from jax.experimental.pallas import tpu as pltpu

HEAD_DIM = 64
HALF_DIM = HEAD_DIM // 2
SCALE = HEAD_DIM ** -0.5
NORM_EPS = 1e-6
ROPE_THETA = 10000.0
NEG_INF = -1e30

RW_HEADS = 8
RW_WIDTH = RW_HEADS * HEAD_DIM
RW_DECAY_LORA = 32
RW_AAA_LORA = 32
RW_GATE_LORA = 96
RW_LNX_EPS = 64e-5
RW_COLS = 3 * RW_WIDTH + 2 * RW_DECAY_LORA + 2 * RW_AAA_LORA + RW_GATE_LORA
RW_COLS_PAD = 1792
KEY_PITCH = HEAD_DIM + 4
VAL_PITCH = HEAD_DIM // 2 + 8
SCAN_UNROLL = 32

DIL_PATTERNS = ((128, 1), (512, 4), (2048, 16))
DIL_HEADS_PER_GROUP = 4
DIL_GROUP_WIDTH = DIL_HEADS_PER_GROUP * HEAD_DIM
DIL_WIDTH = len(DIL_PATTERNS) * DIL_GROUP_WIDTH
DIL_STEP_WIDTH = 2 * HEAD_DIM

GRID_W = 64
NA_HEADS = 12
NA_WIDTH = NA_HEADS * HEAD_DIM
NA_KH = 8
NA_KW = 16
NA_GROUP_HEADS = 4
NA_GROUP_WIDTH = NA_GROUP_HEADS * HEAD_DIM
NA_ROW_UNROLL = 4

N_BRANCH = 3
VMEM_LIMIT = 56 * 1024 * 1024

BF16 = jnp.bfloat16
F32 = jnp.float32
HIGHEST = lax.Precision.HIGHEST


def _params(*sem):
    return pltpu.CompilerParams(dimension_semantics=sem, vmem_limit_bytes=VMEM_LIMIT)


def _rms(x, g):
    ms = jnp.mean(x * x, axis=-1, keepdims=True)
    return x * lax.rsqrt(ms + NORM_EPS) * g


def _sigmoid(x):
    return 0.5 * jnp.tanh(0.5 * x) + 0.5


def _dot(a, b):
    return jnp.dot(a, b, preferred_element_type=F32)


def _dot_nt(a, b):
    return lax.dot_general(a, b, (((1,), (1,)), ((), ())), preferred_element_type=F32)


def _dot_hi(a, b):
    return jnp.dot(a, b, preferred_element_type=F32, precision=HIGHEST)


def _split_hi_lo(w):
    hi = w.astype(BF16)
    return jnp.stack([hi, (w - hi.astype(F32)).astype(BF16)])


def _dot_split(a, w_hi, w_lo):
    a_hi = a.astype(BF16)
    a_lo = (a - a_hi.astype(F32)).astype(BF16)
    return _dot(a_hi, w_hi) + _dot(a_hi, w_lo) + _dot(a_lo, w_hi)


def _seg_sum(x, seg):
    hi = x.astype(BF16)
    rest = x - hi.astype(F32)
    mid = rest.astype(BF16)
    lo = (rest - mid.astype(F32)).astype(BF16)
    return _dot(hi, seg) + _dot(mid, seg) + _dot(lo, seg)


def _norm_matmul_kernel(x_ref, g_ref, w_ref, o_ref, h_ref):
    @pl.when(pl.program_id(1) == 0)
    def _():
        h_ref[...] = _rms(x_ref[...], g_ref[...]).astype(BF16)

    o_ref[...] = _dot_nt(h_ref[...], w_ref[...]).astype(o_ref.dtype)


def norm_matmul(x, g, w_t, layer, bm, bn, out_dtype, name):
    M, K = x.shape
    N = w_t.shape[1]
    return pl.pallas_call(
        _norm_matmul_kernel,
        grid=(M // bm, N // bn),
        in_specs=[pl.BlockSpec((bm, K), lambda i, j: (i, 0)),
                  pl.BlockSpec((1, K), lambda i, j: (0, 0)),
                  pl.BlockSpec((None, bn, K), lambda i, j: (layer, j, 0))],
        out_specs=pl.BlockSpec((bm, bn), lambda i, j: (i, j)),
        out_shape=jax.ShapeDtypeStruct((M, N), out_dtype),
        scratch_shapes=[pltpu.VMEM((bm, K), BF16)],
        compiler_params=_params("parallel", "arbitrary"),
        name=name,
    )(x, g, w_t)


DECAY_SCALE = float(np.exp(-0.5))


def _rw_prep_kernel(c_ref, cp_ref, cn_ref, mu_ref, w0_ref, w2_ref, a0_ref, a2_ref, g2_ref, kk_ref, ka_ref,
                    seg_ref, r_out, v_out, a_out, dec0_out, dec1_out, kd0_out, kd1_out, b0_out, b1_out,
                    gate_out, buf_ref, *, bt, seq):
    i = pl.program_id(0)
    pos = (i * bt) % seq
    prev_ok = (pos != 0).astype(F32)
    next_ok = (pos + bt != seq).astype(F32)
    buf_ref[0:8, :] = cp_ref[...] * prev_ok
    buf_ref[8:bt + 8, :] = c_ref[...]
    buf_ref[bt + 8:bt + 16, :] = cn_ref[...] * next_ok
    mu = mu_ref[...]
    c = c_ref[...] * (1.0 - mu) + (buf_ref[7:bt + 7, :] + buf_ref[9:bt + 9, :]) * (0.5 * mu)
    C = RW_WIDTH
    r = c[:, 0:C]
    k = c[:, C:2 * C]
    v = c[:, 2 * C:3 * C]
    o = 3 * C
    wd = c[:, o:o + 2 * RW_DECAY_LORA]
    o += 2 * RW_DECAY_LORA
    ad = c[:, o:o + 2 * RW_AAA_LORA]
    o += 2 * RW_AAA_LORA
    gd = c[:, o:o + RW_GATE_LORA]
    seg = seg_ref[...]
    kk = k * kk_ref[...]
    kk = kk * lax.rsqrt(jnp.maximum(_seg_sum(kk * kk, seg), 1e-24))
    r_out[...] = r
    v_out[...] = v
    a_out[...] = -kk
    gate_out[...] = _dot_split(_sigmoid(gd), g2_ref[0], g2_ref[1])
    twd = jnp.tanh(wd)
    for d, (dec_out, kd_out, b_out) in enumerate(((dec0_out, kd0_out, b0_out), (dec1_out, kd1_out, b1_out))):
        lw = _dot_split(twd[:, d * RW_DECAY_LORA:(d + 1) * RW_DECAY_LORA], w2_ref[0, d], w2_ref[1, d])
        dec_out[...] = jnp.exp(-DECAY_SCALE * _sigmoid(w0_ref[d:d + 1, :] + lw))
        la = _dot_split(ad[:, d * RW_AAA_LORA:(d + 1) * RW_AAA_LORA], a2_ref[0, d], a2_ref[1, d])
        lr = _sigmoid(a0_ref[d:d + 1, :] + la)
        kd_out[...] = k * (1.0 + (lr - 1.0) * ka_ref[...])
        b_out[...] = kk * lr


def rw_prep(z, rw_col_block, mu, w0, w2, a0, a2, g2, k_k, k_a, seg, seq, bt):
    M = z.shape[0]
    C = RW_WIDTH
    W = RW_COLS_PAD
    nb8 = M // 8
    full = lambda shape: pl.BlockSpec(shape, lambda i: (0,) * len(shape))
    out = jax.ShapeDtypeStruct((M, C), F32)
    return pl.pallas_call(
        functools.partial(_rw_prep_kernel, bt=bt, seq=seq),
        grid=(M // bt,),
        in_specs=[pl.BlockSpec((bt, W), lambda i: (i, rw_col_block)),
                  pl.BlockSpec((8, W), lambda i: (jnp.maximum(i * (bt // 8) - 1, 0), rw_col_block)),
                  pl.BlockSpec((8, W), lambda i: (jnp.minimum((i + 1) * (bt // 8), nb8 - 1), rw_col_block)),
                  full((1, W)), full((2, C)), full((2, 2, RW_DECAY_LORA, C)), full((2, C)),
                  full((2, 2, RW_AAA_LORA, C)), full((2, RW_GATE_LORA, C)), full((1, C)), full((1, C)),
                  full((C, C))],
        out_specs=[pl.BlockSpec((bt, C), lambda i: (i, 0))] * 10,
        out_shape=[out] * 10,
        scratch_shapes=[pltpu.VMEM((bt + 16, W), F32)],
        compiler_params=_params("parallel"),
        name="rw_prep",
    )(z, z, z, mu, w0, _split_hi_lo(w2), a0, _split_hi_lo(a2), _split_hi_lo(g2), k_k, k_a, seg)


def _rw_scan_kernel(w_ref, k_ref, a_ref, b_ref, r_ref, v_ref, y_ref, s_ref, *, tb, pitch, vpitch):
    N = HEAD_DIM
    NV = N // 2

    @pl.when(pl.program_id(0) == 0)
    def _():
        s_ref[...] = jnp.zeros_like(s_ref)

    zero = jnp.zeros(s_ref.shape[1:], F32)

    def dot_a(c, acc):
        for u in range(SCAN_UNROLL):
            j = c * SCAN_UNROLL + u
            acc = acc + s_ref[j] * a_ref[pl.ds(j, 1), :]
        return acc

    sa_first = lax.fori_loop(0, N // SCAN_UNROLL, dot_a, zero)

    def step(t, sa):
        base = t * pitch
        nbase = jnp.minimum(t + 1, tb - 1) * pitch
        vbase = pl.multiple_of(t * vpitch, 8)
        row = lambda ref, j: ref[pl.ds(base + j, 1), :]
        vt = v_ref[pl.ds(vbase, NV), :]

        def update(c, acc):
            y0, y1, n0, n1 = acc
            for u in range(SCAN_UNROLL):
                j = c * SCAN_UNROLL + u
                s = s_ref[j] * row(w_ref, j) + sa * row(b_ref, j) + vt * row(k_ref, j)
                s_ref[j] = s
                ty = s * row(r_ref, j)
                tn = s * a_ref[pl.ds(nbase + j, 1), :]
                if u % 2 == 0:
                    y0, n0 = y0 + ty, n0 + tn
                else:
                    y1, n1 = y1 + ty, n1 + tn
            return y0, y1, n0, n1

        y0, y1, n0, n1 = lax.fori_loop(0, N // SCAN_UNROLL, update, (zero, zero, zero, zero))
        y_ref[pl.ds(vbase, NV), :] = y0 + y1
        y_ref[pl.ds(vbase + NV, vpitch - NV), :] = jnp.zeros((vpitch - NV, zero.shape[1]), F32)
        return n0 + n1

    lax.fori_loop(0, tb, step, sa_first)


def rw_scan(w, k, a, b, r, v, T, tb, pitch, vpitch):
    L = v.shape[-1]
    kspec = pl.BlockSpec((tb * pitch, L), lambda i: (i, 0))
    vspec = pl.BlockSpec((tb * vpitch, L), lambda i: (i, 0))
    return pl.pallas_call(
        functools.partial(_rw_scan_kernel, tb=tb, pitch=pitch, vpitch=vpitch),
        grid=(T // tb,),
        in_specs=[kspec] * 5 + [vspec],
        out_specs=vspec,
        out_shape=jax.ShapeDtypeStruct((T * vpitch, L), F32),
        scratch_shapes=[pltpu.VMEM((HEAD_DIM, HEAD_DIM // 2, L), F32)],
        compiler_params=_params("arbitrary"),
        name="rw_scan",
    )(w, k, a, b, r, v)


def _rw_post_kernel(yf_ref, yb_ref, r_ref, v_ref, kd0_ref, kd1_ref, gate_ref, rk_ref, g_ref, b_ref, seg_ref,
                    o_ref):
    seg = seg_ref[...]
    y = yf_ref[...] + yb_ref[...]
    inv_n = 1.0 / HEAD_DIM
    mean = _seg_sum(y, seg) * inv_n
    d = y - mean
    var = _seg_sum(d * d, seg) * inv_n
    yn = d * lax.rsqrt(var + RW_LNX_EPS) * g_ref[...] + b_ref[...]
    r = r_ref[...]
    rk = rk_ref[...]
    bonus = _seg_sum(r * kd0_ref[...] * rk + r * kd1_ref[...] * rk, seg) * v_ref[...]
    o_ref[...] = (yn + bonus) * gate_ref[...]


def rw_post(yf, yb, r, v, kd0, kd1, gate, rk, g, b, seg, bt):
    M, C = yf.shape
    tok = pl.BlockSpec((bt, C), lambda i: (i, 0))
    row = pl.BlockSpec((1, C), lambda i: (0, 0))
    return pl.pallas_call(
        _rw_post_kernel,
        grid=(M // bt,),
        in_specs=[tok] * 7 + [row] * 3 + [pl.BlockSpec((C, C), lambda i: (0, 0))],
        out_specs=tok,
        out_shape=jax.ShapeDtypeStruct((M, C), F32),
        compiler_params=_params("parallel"),
        name="rw_post",
    )(yf, yb, r, v, kd0, kd1, gate, rk, g, b, seg)


def _dil_kernel(q_ref, k_ref, v_ref, cos_ref, sin_ref, o_ref, lse_ref, qs_ref, ks_ref, vs_ref, *, n_sub, dil, qb,
                kwin, half):
    W = DIL_STEP_WIDTH
    lane = lax.broadcasted_iota(jnp.int32, (1, W), 1)
    first_half = (lane % HEAD_DIM) < HALF_DIM

    def rope(z, c, s):
        swapped = jnp.where(first_half, pltpu.roll(z, W - HALF_DIM, 1), pltpu.roll(z, HALF_DIM, 1))
        return z * c + swapped * s

    rc = min(256, n_sub)

    def subsequence(r, carry):
        def rope_chunk(i, carry):
            rows = pl.ds(r + i * (rc * dil), rc, stride=dil)
            sl = pl.ds(pl.multiple_of(i * rc, rc), rc)
            c = cos_ref[rows, :]
            s = sin_ref[rows, :]
            qs_ref[sl, :] = (rope(q_ref[0, rows, :], c, s) * SCALE).astype(BF16)
            ks_ref[sl, :] = rope(k_ref[0, rows, :], c, s).astype(BF16)
            vs_ref[sl, :] = v_ref[0, rows, :].astype(BF16)
            return carry

        lax.fori_loop(0, n_sub // rc, rope_chunk, 0)

        nblk = n_sub // qb
        unroll = next(u for u in (4, 2, 1) if nblk % u == 0)
        low_head = lane < HEAD_DIM

        def block_group(ig, carry):
            chains = []
            for u in range(unroll):
                q0 = pl.multiple_of((ig * unroll + u) * qb, qb)
                k0 = pl.multiple_of(jnp.clip(q0 - (kwin - qb) // 2, 0, n_sub - kwin), 64)
                q = qs_ref[pl.ds(q0, qb), :]
                kk = ks_ref[pl.ds(k0, kwin), :]
                jq = q0 + lax.broadcasted_iota(jnp.int32, (qb, kwin), 0)
                jk = k0 + lax.broadcasted_iota(jnp.int32, (qb, kwin), 1)
                ok = jnp.abs(jk - jq) <= half
                for h in range(2):
                    qh = jnp.where(low_head == (h == 0), q, jnp.zeros_like(q))
                    chains.append((q0, k0, jnp.where(ok, _dot_nt(qh, kk), NEG_INF)))
            probs = []
            for q0, k0, s in chains:
                m = jnp.max(s, axis=-1, keepdims=True)
                p = jnp.exp(s - m)
                l = jnp.sum(p, axis=-1, keepdims=True)
                probs.append((q0, k0, p.astype(BF16), l, m + jnp.log(l)))
            for c in range(0, len(probs), 2):
                (q0, k0, p0, l0, lse0), (_, _, p1, l1, lse1) = probs[c], probs[c + 1]
                vv = vs_ref[pl.ds(k0, kwin), :]
                rows = pl.ds(r + q0 * dil, qb, stride=dil)
                o_ref[0, rows, :] = jnp.where(low_head, _dot(p0, vv) / l0, _dot(p1, vv) / l1)
                lse_ref[0, rows, :] = jnp.where(low_head, lse0, lse1)
            return carry

        lax.fori_loop(0, nblk // unroll, block_group, 0)
        return carry

    lax.fori_loop(0, dil, subsequence, 0)


def dilated_group(z3, q_blk, k_blk, v_blk, cos_t, sin_t, window, dil):
    B, T, ncols = z3.shape
    W = DIL_STEP_WIDTH
    steps = DIL_GROUP_WIDTH // W
    n_sub = T // dil
    half = window // (2 * dil)
    qb = min(128, n_sub)
    kwin = min(qb + 2 * half, n_sub)
    zspec = lambda blk: pl.BlockSpec((1, T, W), lambda b, s: (b, 0, blk * steps + s))
    tspec = pl.BlockSpec((T, W), lambda b, s: (0, 0))
    ospec = pl.BlockSpec((1, T, W), lambda b, s: (b, 0, s))
    oshape = jax.ShapeDtypeStruct((B, T, DIL_GROUP_WIDTH), F32)
    o, lse = pl.pallas_call(
        functools.partial(_dil_kernel, n_sub=n_sub, dil=dil, qb=qb, kwin=kwin, half=half),
        grid=(B, steps),
        in_specs=[zspec(q_blk), zspec(k_blk), zspec(v_blk), tspec, tspec],
        out_specs=[ospec, ospec],
        out_shape=[oshape, oshape],
        scratch_shapes=[pltpu.VMEM((n_sub, W), BF16)] * 3,
        compiler_params=_params("parallel", "parallel"),
        name=f"dilated_d{dil}",
    )(z3, z3, z3, cos_t, sin_t)
    return o.reshape(B * T, DIL_GROUP_WIDTH), lse.reshape(B * T, DIL_GROUP_WIDTH)


def _na_kernel(q_ref, k_ref, v_ref, tab_ref, o_ref, qs_ref, ks_ref, vs_ref, bias_ref, *, rows):
    nkeys = NA_KH * GRID_W
    pc = NA_KH * GRID_W

    for off in range(NA_KH):
        for h in range(NA_GROUP_HEADS):
            for ky in range(NA_KH):
                bias_ref[off, h, :, ky * GRID_W:(ky + 1) * GRID_W] = tab_ref[h, ky - off + NA_KH - 1]

    PW = 2 * HEAD_DIM
    pairs = NA_GROUP_HEADS // 2
    low_head = lax.broadcasted_iota(jnp.int32, (1, PW), 1) < HEAD_DIM

    def prep(c, carry):
        sl = pl.ds(pl.multiple_of(c * pc, pc), pc)
        q = q_ref[0, sl, :] * SCALE
        k = k_ref[0, sl, :]
        v = v_ref[0, sl, :]
        for pp in range(pairs):
            ps = slice(pp * PW, (pp + 1) * PW)
            qs_ref[pp, sl, :] = q[:, ps].astype(BF16)
            ks_ref[pp, sl, :] = k[:, ps].astype(BF16)
            vs_ref[pp, sl, :] = v[:, ps].astype(BF16)
        return carry

    lax.fori_loop(0, rows * GRID_W // pc, prep, 0)

    def row_group(rg, carry):
        chains = []
        for u in range(NA_ROW_UNROLL):
            r = rg * NA_ROW_UNROLL + u
            r0 = jnp.clip(r - NA_KH // 2, 0, rows - NA_KH)
            qrow = pl.ds(pl.multiple_of(r * GRID_W, GRID_W), GRID_W)
            krow = pl.ds(pl.multiple_of(r0 * GRID_W, GRID_W), nkeys)
            for pp in range(pairs):
                q = qs_ref[pp, qrow, :]
                kk = ks_ref[pp, krow, :]
                for h in range(2):
                    qh = jnp.where(low_head == (h == 0), q, jnp.zeros_like(q))
                    chains.append((qrow, krow, pp, _dot_nt(qh, kk) + bias_ref[r - r0, 2 * pp + h]))
        probs = []
        for qrow, krow, pp, s in chains:
            m = jnp.max(s, axis=-1, keepdims=True)
            p = jnp.exp(s - m)
            l = jnp.sum(p, axis=-1, keepdims=True)
            probs.append((qrow, krow, pp, p.astype(BF16), l))
        outs = []
        for c in range(0, len(probs), 2):
            (qrow, krow, pp, p0, l0), (_, _, _, p1, l1) = probs[c], probs[c + 1]
            vv = vs_ref[pp, krow, :]
            outs.append(jnp.where(low_head, _dot(p0, vv) / l0, _dot(p1, vv) / l1))
            if len(outs) == pairs:
                o_ref[0, qrow, :] = jnp.concatenate(outs, axis=-1)
                outs = []
        return carry

    lax.fori_loop(0, rows // NA_ROW_UNROLL, row_group, 0)


def na_bias_table(rel_bias):
    qc = np.arange(GRID_W)
    kc = np.arange(GRID_W)
    wc0 = np.clip(qc - NA_KW // 2, 0, GRID_W - NA_KW)
    col_ok = (kc[None, :] >= wc0[:, None]) & (kc[None, :] < wc0[:, None] + NA_KW)
    dx_idx = np.clip(kc[None, :] - qc[:, None], 1 - NA_KW, NA_KW - 1) + NA_KW - 1
    onehot = jnp.asarray(dx_idx[None] == np.arange(2 * NA_KW - 1)[:, None, None], F32)
    tab = jnp.einsum('hyd,dqk->hyqk', rel_bias.astype(F32), onehot, precision=HIGHEST)
    return jnp.where(jnp.asarray(col_ok)[None, None], tab, NEG_INF)


def neighbourhood(z3, q_blk0, k_blk0, v_blk0, bias_tab):
    B, T, ncols = z3.shape
    rows = T // GRID_W
    assert rows >= NA_KH
    W = NA_GROUP_WIDTH
    G = NA_HEADS // NA_GROUP_HEADS
    zspec = lambda blk0: pl.BlockSpec((1, T, W), lambda b, g: (b, 0, blk0 + g))
    return pl.pallas_call(
        functools.partial(_na_kernel, rows=rows),
        grid=(B, G),
        in_specs=[zspec(q_blk0), zspec(k_blk0), zspec(v_blk0),
                  pl.BlockSpec((NA_GROUP_HEADS, 2 * NA_KH - 1, GRID_W, GRID_W), lambda b, g: (g, 0, 0, 0))],
        out_specs=pl.BlockSpec((1, T, W), lambda b, g: (b, 0, g)),
        out_shape=jax.ShapeDtypeStruct((B, T, NA_WIDTH), F32),
        scratch_shapes=[pltpu.VMEM((NA_GROUP_HEADS // 2, T, 2 * HEAD_DIM), BF16)] * 3
                       + [pltpu.VMEM((NA_KH, NA_GROUP_HEADS, GRID_W, NA_KH * GRID_W), F32)],
        compiler_params=_params("parallel", "parallel"),
        name="neighbourhood",
    )(z3, z3, z3, bias_tab)


def _merge_kernel(ya_ref, o1_ref, o2_ref, o3_ref, l1_ref, l2_ref, l3_ref, yc_ref, ga_ref, gb_ref, gc_ref,
                  wa_ref, wb_ref, wc_ref, out_ref, ya16_ref, yb16_ref, yc16_ref, *, bm):
    @pl.when(pl.program_id(1) == 0)
    def _():
        l1, l2, l3 = l1_ref[...], l2_ref[...], l3_ref[...]
        m = jnp.maximum(jnp.maximum(l1, l2), l3)
        e1, e2, e3 = jnp.exp(l1 - m), jnp.exp(l2 - m), jnp.exp(l3 - m)
        yb = (e1 * o1_ref[...] + e2 * o2_ref[...] + e3 * o3_ref[...]) / (e1 + e2 + e3)
        yb16_ref[...] = yb.astype(BF16)
        ya16_ref[...] = ya_ref[...].astype(BF16)
        yc16_ref[...] = yc_ref[...].astype(BF16)

    hb = bm // 2
    halves = [slice(s * hb, (s + 1) * hb) for s in range(2)]
    prods = [(_dot(ya16_ref[rs, :], wa_ref[...]), _dot(yb16_ref[rs, :], wb_ref[...]),
              _dot(yc16_ref[rs, :], wc_ref[...])) for rs in halves]
    gate = lambda ref, rs: _sigmoid(ref[rs, :]).astype(F32)
    for rs, (pa, pb, pc) in zip(halves, prods):
        out = gate(ga_ref, rs) * pa + gate(gb_ref, rs) * pb + gate(gc_ref, rs) * pc
        out_ref[rs, :] = out.astype(BF16)


def merge(ya, dil_outs, yc, z, layer, wa, wb, wc, bm, bn):
    M = ya.shape[0]
    D = wa.shape[-1]
    nj = D // bn
    tok = lambda w: pl.BlockSpec((bm, w), lambda i, j: (i, 0))
    gate = lambda g: pl.BlockSpec((bm, bn), lambda i, j: (i, g * nj + j))
    wsp = lambda k: pl.BlockSpec((None, k, bn), lambda i, j: (layer, 0, j))
    (o1, l1), (o2, l2), (o3, l3) = dil_outs
    W = DIL_GROUP_WIDTH
    return pl.pallas_call(
        functools.partial(_merge_kernel, bm=bm),
        grid=(M // bm, nj),
        in_specs=[tok(RW_WIDTH)] + [tok(W)] * 6 + [tok(NA_WIDTH), gate(0), gate(1), gate(2),
                                                  wsp(RW_WIDTH), wsp(W), wsp(NA_WIDTH)],
        out_specs=pl.BlockSpec((bm, bn), lambda i, j: (i, j)),
        out_shape=jax.ShapeDtypeStruct((M, D), BF16),
        scratch_shapes=[pltpu.VMEM((bm, RW_WIDTH), BF16), pltpu.VMEM((bm, W), BF16),
                        pltpu.VMEM((bm, NA_WIDTH), BF16)],
        compiler_params=_params("parallel", "arbitrary"),
        name="merge",
    )(ya, o1, o2, o3, l1, l2, l3, yc, z, z, z, wa, wb, wc)


def _matmul_res_kernel(a_ref, w_ref, x_ref, o_ref):
    o_ref[...] = x_ref[...] + _dot(a_ref[...], w_ref[...])


def matmul_res(a, w, layer, x, bm, bn):
    M, K = a.shape
    N = w.shape[-1]
    return pl.pallas_call(
        _matmul_res_kernel,
        grid=(M // bm, N // bn),
        in_specs=[pl.BlockSpec((bm, K), lambda i, j: (i, 0)),
                  pl.BlockSpec((None, K, bn), lambda i, j: (layer, 0, j)),
                  pl.BlockSpec((bm, bn), lambda i, j: (i, j))],
        out_specs=pl.BlockSpec((bm, bn), lambda i, j: (i, j)),
        out_shape=jax.ShapeDtypeStruct((M, N), F32),
        compiler_params=_params("parallel", "parallel"),
        name="out_proj",
    )(a, w, x)


def _gelu(x):
    return 0.5 * x * (1.0 + jnp.tanh(np.sqrt(2.0 / np.pi).astype(np.float32) * (x + 0.044715 * (x * x * x))))


FFN_HALO = 16
FFN_SPLIT = 2


def _ffn_kernel(x_ref, xp_ref, xn_ref, g_ref, wg_ref, wu_ref, cw_ref, cb_ref, wd_ref, o_ref,
                h_ref, gs_ref, *, bm, bf, seq):
    i = pl.program_id(0)
    f = pl.program_id(1)
    HL = FFN_HALO

    @pl.when(f == 0)
    def _():
        g = g_ref[...]
        h_ref[0:HL, :] = _rms(xp_ref[...], g).astype(BF16)
        h_ref[HL:bm + HL, :] = _rms(x_ref[...], g).astype(BF16)
        h_ref[bm + HL:bm + 2 * HL, :] = _rms(xn_ref[...], g).astype(BF16)
        o_ref[...] = x_ref[...]

    pos = (i * bm) % seq
    prev_ok = (pos != 0).astype(F32)
    next_ok = (pos + bm != seq).astype(F32)
    hw = bf // FFN_SPLIT
    h_ext = h_ref[...]
    h = h_ref[HL:bm + HL, :]
    cw = cw_ref[...]
    cb = cb_ref[...]
    gates, ups = [], []
    for s in range(FFN_SPLIT):
        cs = slice(s * hw, (s + 1) * hw)
        gates.append(_dot(h_ext, wg_ref[:, cs]))
        ups.append(_dot(h, wu_ref[:, cs]))
    acts = []
    for s in range(FFN_SPLIT):
        cs = slice(s * hw, (s + 1) * hw)
        gs_ref[s] = gates[s]
        gs_ref[s, HL - 8:HL, :] = gs_ref[s, HL - 8:HL, :] * prev_ok
        gs_ref[s, bm + HL:bm + HL + 8, :] = gs_ref[s, bm + HL:bm + HL + 8, :] * next_ok
        gc = (gs_ref[s, HL - 1:bm + HL - 1, :] * cw[0:1, cs] + gs_ref[s, HL:bm + HL, :] * cw[1:2, cs]
              + gs_ref[s, HL + 1:bm + HL + 1, :] * cw[2:3, cs] + cb[:, cs])
        acts.append((_gelu(gc) * ups[s]).astype(BF16))
    upd = _dot(acts[0], wd_ref[0:hw, :])
    for s in range(1, FFN_SPLIT):
        upd = upd + _dot(acts[s], wd_ref[s * hw:(s + 1) * hw, :])
    o_ref[...] += upd


def ffn(x, g, layer, wg, wu, cw, cb, wd, seq, bm, bf):
    M, D = x.shape
    F = wg.shape[-1]
    HL = FFN_HALO
    nbh = M // HL
    return pl.pallas_call(
        functools.partial(_ffn_kernel, bm=bm, bf=bf, seq=seq),
        grid=(M // bm, F // bf),
        in_specs=[pl.BlockSpec((bm, D), lambda i, f: (i, 0)),
                  pl.BlockSpec((HL, D), lambda i, f: (jnp.maximum(i * (bm // HL) - 1, 0), 0)),
                  pl.BlockSpec((HL, D), lambda i, f: (jnp.minimum((i + 1) * (bm // HL), nbh - 1), 0)),
                  pl.BlockSpec((1, D), lambda i, f: (0, 0)),
                  pl.BlockSpec((None, D, bf), lambda i, f: (layer, 0, f)),
                  pl.BlockSpec((None, D, bf), lambda i, f: (layer, 0, f)),
                  pl.BlockSpec((3, bf), lambda i, f: (0, f)),
                  pl.BlockSpec((1, bf), lambda i, f: (0, f)),
                  pl.BlockSpec((None, bf, D), lambda i, f: (layer, f, 0))],
        out_specs=pl.BlockSpec((bm, D), lambda i, f: (i, 0)),
        out_shape=jax.ShapeDtypeStruct((M, D), F32),
        scratch_shapes=[pltpu.VMEM((bm + 2 * HL, D), BF16),
                        pltpu.VMEM((FFN_SPLIT, bm + 2 * HL, bf // FFN_SPLIT), F32)],
        compiler_params=_params("parallel", "arbitrary"),
        name="conv_ffn",
    )(x, x, x, g, wg, wu, cw, cb, wd)


def _ple_kernel(x_ref, p_ref, g_ref, wg_ref, wp_ref, gf_ref, o_ref, *, final_norm):
    x = x_ref[...]
    h = _rms(x, g_ref[...]).astype(BF16)
    gate = _sigmoid(_dot(h, wg_ref[...]))
    y = x + gate * _dot(p_ref[...].astype(BF16), wp_ref[...])
    if final_norm:
        y = _rms(y, gf_ref[...])
    o_ref[...] = y


def ple(x, p, layer, g, wg, wp, gf, bm, final_norm):
    M, D = x.shape
    P = p.shape[-1]
    return pl.pallas_call(
        functools.partial(_ple_kernel, final_norm=final_norm),
        grid=(M // bm,),
        in_specs=[pl.BlockSpec((bm, D), lambda i: (i, 0)),
                  pl.BlockSpec((None, bm, P), lambda i: (layer, i, 0)),
                  pl.BlockSpec((1, D), lambda i: (0, 0)),
                  pl.BlockSpec((None, D, D), lambda i: (layer, 0, 0)),
                  pl.BlockSpec((None, P, D), lambda i: (layer, 0, 0)),
                  pl.BlockSpec((1, D), lambda i: (0, 0))],
        out_specs=pl.BlockSpec((bm, D), lambda i: (i, 0)),
        out_shape=jax.ShapeDtypeStruct((M, D), F32),
        compiler_params=_params("parallel"),
        name="ple_final" if final_norm else "ple",
    )(x, p, g, wg, wp, gf)


def _to_scan_kernel(x0_ref, x1_ref, rev_ref, o_ref, a_ref, *, nb, tb, dup, pitch, apitch):
    N = HEAD_DIM
    NV = N // 2
    lanes = 2 * 2 * nb * RW_HEADS
    n_rows = N if dup else NV
    rev = rev_ref[...]
    for d, ref in enumerate((x0_ref, x1_ref)):
        for b in range(nb):
            x = ref[b]
            if d == 1:
                x = _dot_hi(rev, x)
            xt = x.T
            for h in range(RW_HEADS):
                slot = ((d * nb + b) * RW_HEADS + h) * 2
                for s in range(2):
                    blk = xt[h * N:(h + 1) * N] if dup else xt[h * N + s * NV:h * N + (s + 1) * NV]
                    a_ref[(slot + s) * apitch:(slot + s) * apitch + n_rows, :] = blk
    for j in range(n_rows):
        rows = a_ref[pl.ds(j, lanes, stride=apitch), :]
        o_ref[pl.ds(j, tb, stride=pitch), :] = rows.T
    for j in range(n_rows, pitch):
        o_ref[pl.ds(j, tb, stride=pitch), :] = jnp.zeros((tb, lanes), F32)


def to_scan(x0, x1, rev, B, T, tb, dup, pitch):
    C = RW_WIDTH
    nblk = T // tb
    lanes = 4 * B * RW_HEADS
    apitch = (HEAD_DIM if dup else HEAD_DIM // 2) + 4
    return pl.pallas_call(
        functools.partial(_to_scan_kernel, nb=B, tb=tb, dup=dup, pitch=pitch, apitch=apitch),
        grid=(nblk,),
        in_specs=[pl.BlockSpec((B, tb, C), lambda i: (0, i, 0)),
                  pl.BlockSpec((B, tb, C), lambda i: (0, nblk - 1 - i, 0)),
                  pl.BlockSpec((tb, tb), lambda i: (0, 0))],
        out_specs=pl.BlockSpec((tb * pitch, lanes), lambda i: (i, 0)),
        out_shape=jax.ShapeDtypeStruct((T * pitch, lanes), F32),
        scratch_shapes=[pltpu.VMEM((lanes * apitch, tb), F32)],
        compiler_params=_params("parallel"),
        name="to_scan_keys" if dup else "to_scan_values",
    )(x0.reshape(B, T, C), x1.reshape(B, T, C), rev)


def _from_scan_kernel(y_ref, rev_ref, yf_ref, yb_ref, a_ref, *, nb, tb, pitch, apitch):
    NV = HEAD_DIM // 2
    lanes = 2 * 2 * nb * RW_HEADS
    for ih in range(NV):
        rows = y_ref[pl.ds(ih, tb, stride=pitch), :]
        a_ref[pl.ds(ih, lanes, stride=apitch), :] = rows.T
    rev = rev_ref[...]
    for d, ref in enumerate((yf_ref, yb_ref)):
        for b in range(nb):
            slot0 = (d * nb + b) * RW_HEADS * 2
            pieces = [a_ref[(slot0 + s) * apitch:(slot0 + s) * apitch + NV, :] for s in range(2 * RW_HEADS)]
            y = jnp.concatenate(pieces, axis=0).T
            if d == 1:
                y = _dot_hi(rev, y)
            ref[b] = y


def from_scan(y, rev, B, T, tb, pitch):
    C = RW_WIDTH
    nblk = T // tb
    lanes = y.shape[-1]
    apitch = HEAD_DIM // 2 + 4
    out = jax.ShapeDtypeStruct((B, T, C), F32)
    yf, yb = pl.pallas_call(
        functools.partial(_from_scan_kernel, nb=B, tb=tb, pitch=pitch, apitch=apitch),
        grid=(nblk,),
        in_specs=[pl.BlockSpec((tb * pitch, lanes), lambda i: (i, 0)),
                  pl.BlockSpec((tb, tb), lambda i: (0, 0))],
        out_specs=[pl.BlockSpec((B, tb, C), lambda i: (0, i, 0)),
                   pl.BlockSpec((B, tb, C), lambda i: (0, nblk - 1 - i, 0))],
        out_shape=[out, out],
        scratch_shapes=[pltpu.VMEM((lanes * apitch, tb), F32)],
        compiler_params=_params("parallel"),
        name="from_scan",
    )(y, rev)
    return yf.reshape(B * T, C), yb.reshape(B * T, C)


def split_w_in(w_in, o_dil, o_na, o_g):
    depth, K, _ = w_in.shape
    w_t = jnp.swapaxes(w_in, 1, 2)
    w_b = jnp.concatenate([w_t[:, o_g:].astype(BF16), w_t[:, o_na:o_g].astype(BF16)], axis=1)
    w_f = jnp.concatenate([w_t[:, :o_dil].astype(BF16), jnp.zeros((depth, RW_COLS_PAD - o_dil, K), BF16),
                           w_t[:, o_dil:o_na].astype(BF16)], axis=1)
    return w_b, w_f


def _rope_tables(T):
    inv = ROPE_THETA ** (-jnp.arange(0, HEAD_DIM, 2, dtype=jnp.float32) / HEAD_DIM)
    ang = jnp.arange(T, dtype=jnp.float32)[:, None] * inv[None, :]
    cos, sin = jnp.cos(ang), jnp.sin(ang)
    cos_t = jnp.tile(jnp.concatenate([cos, cos], axis=-1), (1, DIL_STEP_WIDTH // HEAD_DIM))
    sin_t = jnp.tile(jnp.concatenate([-sin, sin], axis=-1), (1, DIL_STEP_WIDTH // HEAD_DIM))
    return cos_t, sin_t


def _pick(n, prefs):
    for c in prefs:
        if n % c == 0:
            return c
    return n


def kernel(x, p, norm_mix, w_in, rw_mu, rw_w0, rw_w2, rw_a0, rw_a2, rw_g2, rw_k_k, rw_k_a, rw_r_k, rw_lnx_g,
           rw_lnx_b, na_bias, w_br_a, w_br_b, w_br_c, w_out, norm_ffn, w_ffn_gate, w_ffn_up, ffn_conv_w,
           ffn_conv_b, w_ffn_down, norm_ple, w_ple_gate, w_ple, norm_final):
    B, T, D = x.shape
    depth = w_in.shape[0]
    M = B * T
    F = w_ffn_gate.shape[-1]
    C = RW_WIDTH
    W = DIL_GROUP_WIDTH
    assert (N_BRANCH * D) % W == 0 and w_in.shape[-1] == RW_COLS + 2 * 3 * DIL_WIDTH + N_BRANCH * D

    gates_w = N_BRANCH * D
    dil_w = 3 * DIL_WIDTH
    na_w = 3 * NA_WIDTH
    nb_cols = gates_w + na_w
    nf_cols = RW_COLS_PAD + dil_w
    na_blk0 = gates_w // W
    dil_blk0 = RW_COLS_PAD // W
    assert RW_COLS_PAD % W == 0

    cos_t, sin_t = _rope_tables(T)
    seg = jnp.asarray(np.kron(np.eye(RW_HEADS), np.ones((HEAD_DIM, HEAD_DIM))), BF16)
    xf = x.reshape(M, D)
    p3 = p.reshape(depth, M, p.shape[-1])
    bm_in = _pick(M, (1024, 512, 256, 128))
    bt_rw = _pick(T, (512, 256, 128))
    tb = _pick(T, (64, 32))
    tt = _pick(T, (128,))
    rev = jnp.asarray(np.eye(tt)[::-1], F32)

    wb16 = {name: wt.astype(BF16) for name, wt in (
        ("br_a", w_br_a), ("br_b", w_br_b), ("br_c", w_br_c), ("out", w_out), ("ffn_gate", w_ffn_gate),
        ("ffn_up", w_ffn_up), ("ffn_down", w_ffn_down), ("ple_gate", w_ple_gate), ("ple", w_ple))}
    o_dil, o_na, o_g = RW_COLS, RW_COLS + dil_w, RW_COLS + dil_w + na_w
    w_b, w_f = split_w_in(w_in, o_dil, o_na, o_g)

    for i in range(depth):
        g_mix = norm_mix[i][None]
        zb = norm_matmul(xf, g_mix, w_b, i, _pick(M, (512, 256, 128)), _pick(nb_cols, (2816, 768, 256, 128)), BF16,
                         "in_proj_bf16")
        zf = norm_matmul(xf, g_mix, w_f, i, bm_in, _pick(nf_cols, (1024, 256, 128)), F32, "in_proj_f32")
        zb3 = zb.reshape(B, T, nb_cols)
        zf3 = zf.reshape(B, T, nf_cols)

        mu = jnp.pad(rw_mu[i], (0, RW_COLS_PAD - RW_COLS))[None]
        r, v, a, dec0, dec1, kd0, kd1, b0, b1, gate = rw_prep(
            zf, 0, mu, rw_w0[i], rw_w2[i], rw_a0[i], rw_a2[i], rw_g2[i], rw_k_k[i][None], rw_k_a[i][None],
            seg, T, bt_rw)
        keys = lambda x0, x1: to_scan(x0, x1, rev, B, T, tt, dup=True, pitch=KEY_PITCH)
        vals = to_scan(v, v, rev, B, T, tt, dup=False, pitch=VAL_PITCH)
        y = rw_scan(keys(dec0, dec1), keys(kd0, kd1), keys(a, a), keys(b0, b1), keys(r, r), vals, T, tb,
                    KEY_PITCH, VAL_PITCH)
        yf, yb = from_scan(y, rev, B, T, tt, VAL_PITCH)
        ya = rw_post(yf, yb, r, v, kd0, kd1, gate, rw_r_k[i].reshape(1, C), rw_lnx_g[i][None], rw_lnx_b[i][None],
                     seg, _pick(T, (1024, 512, 256, 128)))

        dil_outs = []
        for g, (window, dil) in enumerate(DIL_PATTERNS):
            dil_outs.append(dilated_group(zf3, dil_blk0 + g, dil_blk0 + 3 + g, dil_blk0 + 6 + g, cos_t, sin_t,
                                          window, dil))

        yc = neighbourhood(zb3, na_blk0, na_blk0 + 3, na_blk0 + 6, na_bias_table(na_bias[i])).reshape(M, NA_WIDTH)

        merged = merge(ya, dil_outs, yc, zb, i, wb16["br_a"], wb16["br_b"], wb16["br_c"],
                       _pick(M, (1024, 512, 256, 128)), _pick(D, (1024, 512, 256, 128)))
        xf = matmul_res(merged, wb16["out"], i, xf, _pick(M, (512, 256, 128)), D)

        xf = ffn(xf, norm_ffn[i][None], i, wb16["ffn_gate"], wb16["ffn_up"], ffn_conv_w[i],
                 ffn_conv_b[i][None], wb16["ffn_down"], T, _pick(T, (512, 256, 128)),
                 _pick(F, (512, 256, 128)))

        xf = ple(xf, p3, i, norm_ple[i][None], wb16["ple_gate"], wb16["ple"],
                 norm_final[None], _pick(M, (512, 256, 128)), final_norm=(i == depth - 1))
    return xf.reshape(B, T, D)
```

```python
import functools

import numpy as np
import jax
import jax.numpy as jnp
from jax import lax
from jax.experimental import pallas as pl
from jax.experimental.pallas import tpu as pltpu

HEAD_DIM = 64
HALF_DIM = HEAD_DIM // 2
SCALE = HEAD_DIM ** -0.5
NORM_EPS = 1e-6
ROPE_THETA = 10000.0
NEG_INF = -1e30

RW_HEADS = 8
RW_WIDTH = RW_HEADS * HEAD_DIM
RW_DECAY_LORA = 32
RW_AAA_LORA = 32
RW_GATE_LORA = 96
RW_LNX_EPS = 64e-5
RW_COLS = 3 * RW_WIDTH + 2 * RW_DECAY_LORA + 2 * RW_AAA_LORA + RW_GATE_LORA
RW_COLS_PAD = 1792
KEY_PITCH = HEAD_DIM + 4
VAL_PITCH = HEAD_DIM // 2 + 8
SCAN_UNROLL = 32

DIL_PATTERNS = ((128, 1), (512, 4), (2048, 16))
DIL_HEADS_PER_GROUP = 4
DIL_GROUP_WIDTH = DIL_HEADS_PER_GROUP * HEAD_DIM
DIL_WIDTH = len(DIL_PATTERNS) * DIL_GROUP_WIDTH
DIL_STEP_WIDTH = 2 * HEAD_DIM

GRID_W = 64
NA_HEADS = 12
NA_WIDTH = NA_HEADS * HEAD_DIM
NA_KH = 8
NA_KW = 16
NA_GROUP_HEADS = 4
NA_GROUP_WIDTH = NA_GROUP_HEADS * HEAD_DIM
NA_ROW_UNROLL = 4

N_BRANCH = 3
VMEM_LIMIT = 56 * 1024 * 1024

BF16 = jnp.bfloat16
F32 = jnp.float32
HIGHEST = lax.Precision.HIGHEST


def _params(*sem):
    return pltpu.CompilerParams(dimension_semantics=sem, vmem_limit_bytes=VMEM_LIMIT)


def _rms(x, g):
    ms = jnp.mean(x * x, axis=-1, keepdims=True)
    return x * lax.rsqrt(ms + NORM_EPS) * g


def _sigmoid(x):
    return 0.5 * jnp.tanh(0.5 * x) + 0.5


def _dot(a, b):
    return jnp.dot(a, b, preferred_element_type=F32)


def _dot_nt(a, b):
    return lax.dot_general(a, b, (((1,), (1,)), ((), ())), preferred_element_type=F32)


def _dot_hi(a, b):
    return jnp.dot(a, b, preferred_element_type=F32, precision=HIGHEST)


def _split_hi_lo(w):
    hi = w.astype(BF16)
    return jnp.stack([hi, (w - hi.astype(F32)).astype(BF16)])


def _dot_split(a, w_hi, w_lo):
    a_hi = a.astype(BF16)
    a_lo = (a - a_hi.astype(F32)).astype(BF16)
    return _dot(a_hi, w_hi) + _dot(a_hi, w_lo) + _dot(a_lo, w_hi)


def _seg_sum(x, seg):
    hi = x.astype(BF16)
    rest = x - hi.astype(F32)
    mid = rest.astype(BF16)
    lo = (rest - mid.astype(F32)).astype(BF16)
    return _dot(hi, seg) + _dot(mid, seg) + _dot(lo, seg)


def _norm_matmul_kernel(x_ref, g_ref, w_ref, o_ref, h_ref):
    @pl.when(pl.program_id(1) == 0)
    def _():
        h_ref[...] = _rms(x_ref[...], g_ref[...]).astype(BF16)

    o_ref[...] = _dot_nt(h_ref[...], w_ref[...]).astype(o_ref.dtype)


def norm_matmul(x, g, w_t, layer, bm, bn, out_dtype, name):
    M, K = x.shape
    N = w_t.shape[1]
    return pl.pallas_call(
        _norm_matmul_kernel,
        grid=(M // bm, N // bn),
        in_specs=[pl.BlockSpec((bm, K), lambda i, j: (i, 0)),
                  pl.BlockSpec((1, K), lambda i, j: (0, 0)),
                  pl.BlockSpec((None, bn, K), lambda i, j: (layer, j, 0))],
        out_specs=pl.BlockSpec((bm, bn), lambda i, j: (i, j)),
        out_shape=jax.ShapeDtypeStruct((M, N), out_dtype),
        scratch_shapes=[pltpu.VMEM((bm, K), BF16)],
        compiler_params=_params("parallel", "arbitrary"),
        name=name,
    )(x, g, w_t)


DECAY_SCALE = float(np.exp(-0.5))


def _rw_prep_kernel(c_ref, cp_ref, cn_ref, mu_ref, w0_ref, w2_ref, a0_ref, a2_ref, g2_ref, kk_ref, ka_ref,
                    seg_ref, r_out, v_out, a_out, dec0_out, dec1_out, kd0_out, kd1_out, b0_out, b1_out,
                    gate_out, buf_ref, *, bt, seq):
    i = pl.program_id(0)
    pos = (i * bt) % seq
    prev_ok = (pos != 0).astype(F32)
    next_ok = (pos + bt != seq).astype(F32)
    buf_ref[0:8, :] = cp_ref[...] * prev_ok
    buf_ref[8:bt + 8, :] = c_ref[...]
    buf_ref[bt + 8:bt + 16, :] = cn_ref[...] * next_ok
    mu = mu_ref[...]
    c = c_ref[...] * (1.0 - mu) + (buf_ref[7:bt + 7, :] + buf_ref[9:bt + 9, :]) * (0.5 * mu)
    C = RW_WIDTH
    r = c[:, 0:C]
    k = c[:, C:2 * C]
    v = c[:, 2 * C:3 * C]
    o = 3 * C
    wd = c[:, o:o + 2 * RW_DECAY_LORA]
    o += 2 * RW_DECAY_LORA
    ad = c[:, o:o + 2 * RW_AAA_LORA]
    o += 2 * RW_AAA_LORA
    gd = c[:, o:o + RW_GATE_LORA]
    seg = seg_ref[...]
    kk = k * kk_ref[...]
    kk = kk * lax.rsqrt(jnp.maximum(_seg_sum(kk * kk, seg), 1e-24))
    r_out[...] = r
    v_out[...] = v
    a_out[...] = -kk
    gate_out[...] = _dot_split(_sigmoid(gd), g2_ref[0], g2_ref[1])
    twd = jnp.tanh(wd)
    for d, (dec_out, kd_out, b_out) in enumerate(((dec0_out, kd0_out, b0_out), (dec1_out, kd1_out, b1_out))):
        lw = _dot_split(twd[:, d * RW_DECAY_LORA:(d + 1) * RW_DECAY_LORA], w2_ref[0, d], w2_ref[1, d])
        dec_out[...] = jnp.exp(-DECAY_SCALE * _sigmoid(w0_ref[d:d + 1, :] + lw))
        la = _dot_split(ad[:, d * RW_AAA_LORA:(d + 1) * RW_AAA_LORA], a2_ref[0, d], a2_ref[1, d])
        lr = _sigmoid(a0_ref[d:d + 1, :] + la)
        kd_out[...] = k * (1.0 + (lr - 1.0) * ka_ref[...])
        b_out[...] = kk * lr


def rw_prep(z, rw_col_block, mu, w0, w2, a0, a2, g2, k_k, k_a, seg, seq, bt):
    M = z.shape[0]
    C = RW_WIDTH
    W = RW_COLS_PAD
    nb8 = M // 8
    full = lambda shape: pl.BlockSpec(shape, lambda i: (0,) * len(shape))
    out = jax.ShapeDtypeStruct((M, C), F32)
    return pl.pallas_call(
        functools.partial(_rw_prep_kernel, bt=bt, seq=seq),
        grid=(M // bt,),
        in_specs=[pl.BlockSpec((bt, W), lambda i: (i, rw_col_block)),
                  pl.BlockSpec((8, W), lambda i: (jnp.maximum(i * (bt // 8) - 1, 0), rw_col_block)),
                  pl.BlockSpec((8, W), lambda i: (jnp.minimum((i + 1) * (bt // 8), nb8 - 1), rw_col_block)),
                  full((1, W)), full((2, C)), full((2, 2, RW_DECAY_LORA, C)), full((2, C)),
                  full((2, 2, RW_AAA_LORA, C)), full((2, RW_GATE_LORA, C)), full((1, C)), full((1, C)),
                  full((C, C))],
        out_specs=[pl.BlockSpec((bt, C), lambda i: (i, 0))] * 10,
        out_shape=[out] * 10,
        scratch_shapes=[pltpu.VMEM((bt + 16, W), F32)],
        compiler_params=_params("parallel"),
        name="rw_prep",
    )(z, z, z, mu, w0, _split_hi_lo(w2), a0, _split_hi_lo(a2), _split_hi_lo(g2), k_k, k_a, seg)


def _rw_scan_kernel(w_ref, k_ref, a_ref, b_ref, r_ref, v_ref, y_ref, s_ref, *, tb, pitch, vpitch):
    N = HEAD_DIM
    NV = N // 2

    @pl.when(pl.program_id(0) == 0)
    def _():
        s_ref[...] = jnp.zeros_like(s_ref)

    zero = jnp.zeros(s_ref.shape[1:], F32)

    def dot_a(c, acc):
        for u in range(SCAN_UNROLL):
            j = c * SCAN_UNROLL + u
            acc = acc + s_ref[j] * a_ref[pl.ds(j, 1), :]
        return acc

    sa_first = lax.fori_loop(0, N // SCAN_UNROLL, dot_a, zero)

    def step(t, sa):
        base = t * pitch
        nbase = jnp.minimum(t + 1, tb - 1) * pitch
        vbase = pl.multiple_of(t * vpitch, 8)
        row = lambda ref, j: ref[pl.ds(base + j, 1), :]
        vt = v_ref[pl.ds(vbase, NV), :]

        def update(c, acc):
            y0, y1, n0, n1 = acc
            for u in range(SCAN_UNROLL):
                j = c * SCAN_UNROLL + u
                s = s_ref[j] * row(w_ref, j) + sa * row(b_ref, j) + vt * row(k_ref, j)
                s_ref[j] = s
                ty = s * row(r_ref, j)
                tn = s * a_ref[pl.ds(nbase + j, 1), :]
                if u % 2 == 0:
                    y0, n0 = y0 + ty, n0 + tn
                else:
                    y1, n1 = y1 + ty, n1 + tn
            return y0, y1, n0, n1

        y0, y1, n0, n1 = lax.fori_loop(0, N // SCAN_UNROLL, update, (zero, zero, zero, zero))
        y_ref[pl.ds(vbase, NV), :] = y0 + y1
        y_ref[pl.ds(vbase + NV, vpitch - NV), :] = jnp.zeros((vpitch - NV, zero.shape[1]), F32)
        return n0 + n1

    lax.fori_loop(0, tb, step, sa_first)


def rw_scan(w, k, a, b, r, v, T, tb, pitch, vpitch):
    L = v.shape[-1]
    kspec = pl.BlockSpec((tb * pitch, L), lambda i: (i, 0))
    vspec = pl.BlockSpec((tb * vpitch, L), lambda i: (i, 0))
    return pl.pallas_call(
        functools.partial(_rw_scan_kernel, tb=tb, pitch=pitch, vpitch=vpitch),
        grid=(T // tb,),
        in_specs=[kspec] * 5 + [vspec],
        out_specs=vspec,
        out_shape=jax.ShapeDtypeStruct((T * vpitch, L), F32),
        scratch_shapes=[pltpu.VMEM((HEAD_DIM, HEAD_DIM // 2, L), F32)],
        compiler_params=_params("arbitrary"),
        name="rw_scan",
    )(w, k, a, b, r, v)


def _rw_post_kernel(yf_ref, yb_ref, r_ref, v_ref, kd0_ref, kd1_ref, gate_ref, rk_ref, g_ref, b_ref, seg_ref,
                    o_ref):
    seg = seg_ref[...]
    y = yf_ref[...] + yb_ref[...]
    inv_n = 1.0 / HEAD_DIM
    mean = _seg_sum(y, seg) * inv_n
    d = y - mean
    var = _seg_sum(d * d, seg) * inv_n
    yn = d * lax.rsqrt(var + RW_LNX_EPS) * g_ref[...] + b_ref[...]
    r = r_ref[...]
    rk = rk_ref[...]
    bonus = _seg_sum(r * kd0_ref[...] * rk + r * kd1_ref[...] * rk, seg) * v_ref[...]
    o_ref[...] = (yn + bonus) * gate_ref[...]


def rw_post(yf, yb, r, v, kd0, kd1, gate, rk, g, b, seg, bt):
    M, C = yf.shape
    tok = pl.BlockSpec((bt, C), lambda i: (i, 0))
    row = pl.BlockSpec((1, C), lambda i: (0, 0))
    return pl.pallas_call(
        _rw_post_kernel,
        grid=(M // bt,),
        in_specs=[tok] * 7 + [row] * 3 + [pl.BlockSpec((C, C), lambda i: (0, 0))],
        out_specs=tok,
        out_shape=jax.ShapeDtypeStruct((M, C), F32),
        compiler_params=_params("parallel"),
        name="rw_post",
    )(yf, yb, r, v, kd0, kd1, gate, rk, g, b, seg)


def _dil_kernel(q_ref, k_ref, v_ref, cos_ref, sin_ref, o_ref, lse_ref, qs_ref, ks_ref, vs_ref, *, n_sub, dil, qb,
                kwin, half):
    W = DIL_STEP_WIDTH
    lane = lax.broadcasted_iota(jnp.int32, (1, W), 1)
    first_half = (lane % HEAD_DIM) < HALF_DIM

    def rope(z, c, s):
        swapped = jnp.where(first_half, pltpu.roll(z, W - HALF_DIM, 1), pltpu.roll(z, HALF_DIM, 1))
        return z * c + swapped * s

    rc = min(256, n_sub)

    def subsequence(r, carry):
        def rope_chunk(i, carry):
            rows = pl.ds(r + i * (rc * dil), rc, stride=dil)
            sl = pl.ds(pl.multiple_of(i * rc, rc), rc)
            c = cos_ref[rows, :]
            s = sin_ref[rows, :]
            qs_ref[sl, :] = (rope(q_ref[0, rows, :], c, s) * SCALE).astype(BF16)
            ks_ref[sl, :] = rope(k_ref[0, rows, :], c, s).astype(BF16)
            vs_ref[sl, :] = v_ref[0, rows, :].astype(BF16)
            return carry

        lax.fori_loop(0, n_sub // rc, rope_chunk, 0)

        nblk = n_sub // qb
        unroll = next(u for u in (4, 2, 1) if nblk % u == 0)
        low_head = lane < HEAD_DIM

        def block_group(ig, carry):
            chains = []
            for u in range(unroll):
                q0 = pl.multiple_of((ig * unroll + u) * qb, qb)
                k0 = pl.multiple_of(jnp.clip(q0 - (kwin - qb) // 2, 0, n_sub - kwin), 64)
                q = qs_ref[pl.ds(q0, qb), :]
                kk = ks_ref[pl.ds(k0, kwin), :]
                jq = q0 + lax.broadcasted_iota(jnp.int32, (qb, kwin), 0)
                jk = k0 + lax.broadcasted_iota(jnp.int32, (qb, kwin), 1)
                ok = jnp.abs(jk - jq) <= half
                for h in range(2):
                    qh = jnp.where(low_head == (h == 0), q, jnp.zeros_like(q))
                    chains.append((q0, k0, jnp.where(ok, _dot_nt(qh, kk), NEG_INF)))
            probs = []
            for q0, k0, s in chains:
                m = jnp.max(s, axis=-1, keepdims=True)
                p = jnp.exp(s - m)
                l = jnp.sum(p, axis=-1, keepdims=True)
                probs.append((q0, k0, p.astype(BF16), l, m + jnp.log(l)))
            for c in range(0, len(probs), 2):
                (q0, k0, p0, l0, lse0), (_, _, p1, l1, lse1) = probs[c], probs[c + 1]
                vv = vs_ref[pl.ds(k0, kwin), :]
                rows = pl.ds(r + q0 * dil, qb, stride=dil)
                o_ref[0, rows, :] = jnp.where(low_head, _dot(p0, vv) / l0, _dot(p1, vv) / l1)
                lse_ref[0, rows, :] = jnp.where(low_head, lse0, lse1)
            return carry

        lax.fori_loop(0, nblk // unroll, block_group, 0)
        return carry

    lax.fori_loop(0, dil, subsequence, 0)


def dilated_group(z3, q_blk, k_blk, v_blk, cos_t, sin_t, window, dil):
    B, T, ncols = z3.shape
    W = DIL_STEP_WIDTH
    steps = DIL_GROUP_WIDTH // W
    n_sub = T // dil
    half = window // (2 * dil)
    qb = min(128, n_sub)
    kwin = min(qb + 2 * half, n_sub)
    zspec = lambda blk: pl.BlockSpec((1, T, W), lambda b, s: (b, 0, blk * steps + s))
    tspec = pl.BlockSpec((T, W), lambda b, s: (0, 0))
    ospec = pl.BlockSpec((1, T, W), lambda b, s: (b, 0, s))
    oshape = jax.ShapeDtypeStruct((B, T, DIL_GROUP_WIDTH), F32)
    o, lse = pl.pallas_call(
        functools.partial(_dil_kernel, n_sub=n_sub, dil=dil, qb=qb, kwin=kwin, half=half),
        grid=(B, steps),
        in_specs=[zspec(q_blk), zspec(k_blk), zspec(v_blk), tspec, tspec],
        out_specs=[ospec, ospec],
        out_shape=[oshape, oshape],
        scratch_shapes=[pltpu.VMEM((n_sub, W), BF16)] * 3,
        compiler_params=_params("parallel", "parallel"),
        name=f"dilated_d{dil}",
    )(z3, z3, z3, cos_t, sin_t)
    return o.reshape(B * T, DIL_GROUP_WIDTH), lse.reshape(B * T, DIL_GROUP_WIDTH)


def _na_kernel(q_ref, k_ref, v_ref, tab_ref, o_ref, qs_ref, ks_ref, vs_ref, bias_ref, *, rows):
    nkeys = NA_KH * GRID_W
    pc = NA_KH * GRID_W

    for off in range(NA_KH):
        for h in range(NA_GROUP_HEADS):
            for ky in range(NA_KH):
                bias_ref[off, h, :, ky * GRID_W:(ky + 1) * GRID_W] = tab_ref[h, ky - off + NA_KH - 1]

    PW = 2 * HEAD_DIM
    pairs = NA_GROUP_HEADS // 2
    low_head = lax.broadcasted_iota(jnp.int32, (1, PW), 1) < HEAD_DIM

    def prep(c, carry):
        sl = pl.ds(pl.multiple_of(c * pc, pc), pc)
        q = q_ref[0, sl, :] * SCALE
        k = k_ref[0, sl, :]
        v = v_ref[0, sl, :]
        for pp in range(pairs):
            ps = slice(pp * PW, (pp + 1) * PW)
            qs_ref[pp, sl, :] = q[:, ps].astype(BF16)
            ks_ref[pp, sl, :] = k[:, ps].astype(BF16)
            vs_ref[pp, sl, :] = v[:, ps].astype(BF16)
        return carry

    lax.fori_loop(0, rows * GRID_W // pc, prep, 0)

    def row_group(rg, carry):
        chains = []
        for u in range(NA_ROW_UNROLL):
            r = rg * NA_ROW_UNROLL + u
            r0 = jnp.clip(r - NA_KH // 2, 0, rows - NA_KH)
            qrow = pl.ds(pl.multiple_of(r * GRID_W, GRID_W), GRID_W)
            krow = pl.ds(pl.multiple_of(r0 * GRID_W, GRID_W), nkeys)
            for pp in range(pairs):
                q = qs_ref[pp, qrow, :]
                zero = jnp.zeros_like(q)
                q2 = jnp.concatenate([jnp.where(low_head, q, zero), jnp.where(low_head, zero, q)], axis=0)
                bias2 = jnp.concatenate([bias_ref[r - r0, 2 * pp], bias_ref[r - r0, 2 * pp + 1]], axis=0)
                chains.append((qrow, krow, pp, _dot_nt(q2, ks_ref[pp, krow, :]) + bias2))
        probs = []
        for qrow, krow, pp, s in chains:
            m = jnp.max(s, axis=-1, keepdims=True)
            p = jnp.exp(s - m)
            l = jnp.sum(p, axis=-1, keepdims=True)
            probs.append((qrow, krow, pp, p.astype(BF16), l))
        outs = []
        for qrow, krow, pp, p, l in probs:
            pv = _dot(p, vs_ref[pp, krow, :]) / l
            outs.append(jnp.where(low_head, pv[:GRID_W], pv[GRID_W:]))
            if len(outs) == pairs:
                o_ref[0, qrow, :] = jnp.concatenate(outs, axis=-1)
                outs = []
        return carry

    lax.fori_loop(0, rows // NA_ROW_UNROLL, row_group, 0)


def na_bias_table(rel_bias):
    qc = np.arange(GRID_W)
    kc = np.arange(GRID_W)
    wc0 = np.clip(qc - NA_KW // 2, 0, GRID_W - NA_KW)
    col_ok = (kc[None, :] >= wc0[:, None]) & (kc[None, :] < wc0[:, None] + NA_KW)
    dx_idx = np.clip(kc[None, :] - qc[:, None], 1 - NA_KW, NA_KW - 1) + NA_KW - 1
    onehot = jnp.asarray(dx_idx[None] == np.arange(2 * NA_KW - 1)[:, None, None], F32)
    tab = jnp.einsum('hyd,dqk->hyqk', rel_bias.astype(F32), onehot, precision=HIGHEST)
    return jnp.where(jnp.asarray(col_ok)[None, None], tab, NEG_INF)


def neighbourhood(z3, q_blk0, k_blk0, v_blk0, bias_tab):
    B, T, ncols = z3.shape
    rows = T // GRID_W
    assert rows >= NA_KH
    W = NA_GROUP_WIDTH
    G = NA_HEADS // NA_GROUP_HEADS
    zspec = lambda blk0: pl.BlockSpec((1, T, W), lambda b, g: (b, 0, blk0 + g))
    return pl.pallas_call(
        functools.partial(_na_kernel, rows=rows),
        grid=(B, G),
        in_specs=[zspec(q_blk0), zspec(k_blk0), zspec(v_blk0),
                  pl.BlockSpec((NA_GROUP_HEADS, 2 * NA_KH - 1, GRID_W, GRID_W), lambda b, g: (g, 0, 0, 0))],
        out_specs=pl.BlockSpec((1, T, W), lambda b, g: (b, 0, g)),
        out_shape=jax.ShapeDtypeStruct((B, T, NA_WIDTH), F32),
        scratch_shapes=[pltpu.VMEM((NA_GROUP_HEADS // 2, T, 2 * HEAD_DIM), BF16)] * 3
                       + [pltpu.VMEM((NA_KH, NA_GROUP_HEADS, GRID_W, NA_KH * GRID_W), F32)],
        compiler_params=_params("parallel", "parallel"),
        name="neighbourhood",
    )(z3, z3, z3, bias_tab)


def _merge_kernel(ya_ref, o1_ref, o2_ref, o3_ref, l1_ref, l2_ref, l3_ref, yc_ref, ga_ref, gb_ref, gc_ref,
                  wa_ref, wb_ref, wc_ref, out_ref, ya16_ref, yb16_ref, yc16_ref, *, bm):
    @pl.when(pl.program_id(1) == 0)
    def _():
        l1, l2, l3 = l1_ref[...], l2_ref[...], l3_ref[...]
        m = jnp.maximum(jnp.maximum(l1, l2), l3)
        e1, e2, e3 = jnp.exp(l1 - m), jnp.exp(l2 - m), jnp.exp(l3 - m)
        yb = (e1 * o1_ref[...] + e2 * o2_ref[...] + e3 * o3_ref[...]) / (e1 + e2 + e3)
        yb16_ref[...] = yb.astype(BF16)
        ya16_ref[...] = ya_ref[...].astype(BF16)
        yc16_ref[...] = yc_ref[...].astype(BF16)

    hb = bm // 2
    halves = [slice(s * hb, (s + 1) * hb) for s in range(2)]
    prods = [(_dot(ya16_ref[rs, :], wa_ref[...]), _dot(yb16_ref[rs, :], wb_ref[...]),
              _dot(yc16_ref[rs, :], wc_ref[...])) for rs in halves]
    gate = lambda ref, rs: _sigmoid(ref[rs, :]).astype(F32)
    for rs, (pa, pb, pc) in zip(halves, prods):
        out = gate(ga_ref, rs) * pa + gate(gb_ref, rs) * pb + gate(gc_ref, rs) * pc
        out_ref[rs, :] = out.astype(BF16)


def merge(ya, dil_outs, yc, z, layer, wa, wb, wc, bm, bn):
    M = ya.shape[0]
    D = wa.shape[-1]
    nj = D // bn
    tok = lambda w: pl.BlockSpec((bm, w), lambda i, j: (i, 0))
    gate = lambda g: pl.BlockSpec((bm, bn), lambda i, j: (i, g * nj + j))
    wsp = lambda k: pl.BlockSpec((None, k, bn), lambda i, j: (layer, 0, j))
    (o1, l1), (o2, l2), (o3, l3) = dil_outs
    W = DIL_GROUP_WIDTH
    return pl.pallas_call(
        functools.partial(_merge_kernel, bm=bm),
        grid=(M // bm, nj),
        in_specs=[tok(RW_WIDTH)] + [tok(W)] * 6 + [tok(NA_WIDTH), gate(0), gate(1), gate(2),
                                                  wsp(RW_WIDTH), wsp(W), wsp(NA_WIDTH)],
        out_specs=pl.BlockSpec((bm, bn), lambda i, j: (i, j)),
        out_shape=jax.ShapeDtypeStruct((M, D), BF16),
        scratch_shapes=[pltpu.VMEM((bm, RW_WIDTH), BF16), pltpu.VMEM((bm, W), BF16),
                        pltpu.VMEM((bm, NA_WIDTH), BF16)],
        compiler_params=_params("parallel", "arbitrary"),
        name="merge",
    )(ya, o1, o2, o3, l1, l2, l3, yc, z, z, z, wa, wb, wc)


def _matmul_res_kernel(a_ref, w_ref, x_ref, o_ref):
    o_ref[...] = x_ref[...] + _dot(a_ref[...], w_ref[...])


def matmul_res(a, w, layer, x, bm, bn):
    M, K = a.shape
    N = w.shape[-1]
    return pl.pallas_call(
        _matmul_res_kernel,
        grid=(M // bm, N // bn),
        in_specs=[pl.BlockSpec((bm, K), lambda i, j: (i, 0)),
                  pl.BlockSpec((None, K, bn), lambda i, j: (layer, 0, j)),
                  pl.BlockSpec((bm, bn), lambda i, j: (i, j))],
        out_specs=pl.BlockSpec((bm, bn), lambda i, j: (i, j)),
        out_shape=jax.ShapeDtypeStruct((M, N), F32),
        compiler_params=_params("parallel", "parallel"),
        name="out_proj",
    )(a, w, x)


def _gelu(x):
    return 0.5 * x * (1.0 + jnp.tanh(np.sqrt(2.0 / np.pi).astype(np.float32) * (x + 0.044715 * (x * x * x))))


FFN_HALO = 16
FFN_SPLIT = 2


def _ffn_kernel(x_ref, xp_ref, xn_ref, g_ref, wg_ref, wu_ref, cw_ref, cb_ref, wd_ref, o_ref,
                h_ref, gs_ref, *, bm, bf, seq):
    i = pl.program_id(0)
    f = pl.program_id(1)
    HL = FFN_HALO

    @pl.when(f == 0)
    def _():
        g = g_ref[...]
        h_ref[0:HL, :] = _rms(xp_ref[...], g).astype(BF16)
        h_ref[HL:bm + HL, :] = _rms(x_ref[...], g).astype(BF16)
        h_ref[bm + HL:bm + 2 * HL, :] = _rms(xn_ref[...], g).astype(BF16)
        o_ref[...] = x_ref[...]

    pos = (i * bm) % seq
    prev_ok = (pos != 0).astype(F32)
    next_ok = (pos + bm != seq).astype(F32)
    hw = bf // FFN_SPLIT
    h_ext = h_ref[...]
    h = h_ref[HL:bm + HL, :]
    cw = cw_ref[...]
    cb = cb_ref[...]
    gates, ups = [], []
    for s in range(FFN_SPLIT):
        cs = slice(s * hw, (s + 1) * hw)
        gates.append(_dot(h_ext, wg_ref[:, cs]))
        ups.append(_dot(h, wu_ref[:, cs]))
    acts = []
    for s in range(FFN_SPLIT):
        cs = slice(s * hw, (s + 1) * hw)
        gs_ref[s] = gates[s]
        gs_ref[s, HL - 8:HL, :] = gs_ref[s, HL - 8:HL, :] * prev_ok
        gs_ref[s, bm + HL:bm + HL + 8, :] = gs_ref[s, bm + HL:bm + HL + 8, :] * next_ok
        gc = (gs_ref[s, HL - 1:bm + HL - 1, :] * cw[0:1, cs] + gs_ref[s, HL:bm + HL, :] * cw[1:2, cs]
              + gs_ref[s, HL + 1:bm + HL + 1, :] * cw[2:3, cs] + cb[:, cs])
        acts.append((_gelu(gc) * ups[s]).astype(BF16))
    upd = _dot(acts[0], wd_ref[0:hw, :])
    for s in range(1, FFN_SPLIT):
        upd = upd + _dot(acts[s], wd_ref[s * hw:(s + 1) * hw, :])
    o_ref[...] += upd


def ffn(x, g, layer, wg, wu, cw, cb, wd, seq, bm, bf):
    M, D = x.shape
    F = wg.shape[-1]
    HL = FFN_HALO
    nbh = M // HL
    return pl.pallas_call(
        functools.partial(_ffn_kernel, bm=bm, bf=bf, seq=seq),
        grid=(M // bm, F // bf),
        in_specs=[pl.BlockSpec((bm, D), lambda i, f: (i, 0)),
                  pl.BlockSpec((HL, D), lambda i, f: (jnp.maximum(i * (bm // HL) - 1, 0), 0)),
                  pl.BlockSpec((HL, D), lambda i, f: (jnp.minimum((i + 1) * (bm // HL), nbh - 1), 0)),
                  pl.BlockSpec((1, D), lambda i, f: (0, 0)),
                  pl.BlockSpec((None, D, bf), lambda i, f: (layer, 0, f)),
                  pl.BlockSpec((None, D, bf), lambda i, f: (layer, 0, f)),
                  pl.BlockSpec((3, bf), lambda i, f: (0, f)),
                  pl.BlockSpec((1, bf), lambda i, f: (0, f)),
                  pl.BlockSpec((None, bf, D), lambda i, f: (layer, f, 0))],
        out_specs=pl.BlockSpec((bm, D), lambda i, f: (i, 0)),
        out_shape=jax.ShapeDtypeStruct((M, D), F32),
        scratch_shapes=[pltpu.VMEM((bm + 2 * HL, D), BF16),
                        pltpu.VMEM((FFN_SPLIT, bm + 2 * HL, bf // FFN_SPLIT), F32)],
        compiler_params=_params("parallel", "arbitrary"),
        name="conv_ffn",
    )(x, x, x, g, wg, wu, cw, cb, wd)


def _ple_kernel(x_ref, p_ref, g_ref, wg_ref, wp_ref, gf_ref, o_ref, *, final_norm):
    x = x_ref[...]
    h = _rms(x, g_ref[...]).astype(BF16)
    gate = _sigmoid(_dot(h, wg_ref[...]))
    y = x + gate * _dot(p_ref[...].astype(BF16), wp_ref[...])
    if final_norm:
        y = _rms(y, gf_ref[...])
    o_ref[...] = y


def ple(x, p, layer, g, wg, wp, gf, bm, final_norm):
    M, D = x.shape
    P = p.shape[-1]
    return pl.pallas_call(
        functools.partial(_ple_kernel, final_norm=final_norm),
        grid=(M // bm,),
        in_specs=[pl.BlockSpec((bm, D), lambda i: (i, 0)),
                  pl.BlockSpec((None, bm, P), lambda i: (layer, i, 0)),
                  pl.BlockSpec((1, D), lambda i: (0, 0)),
                  pl.BlockSpec((None, D, D), lambda i: (layer, 0, 0)),
                  pl.BlockSpec((None, P, D), lambda i: (layer, 0, 0)),
                  pl.BlockSpec((1, D), lambda i: (0, 0))],
        out_specs=pl.BlockSpec((bm, D), lambda i: (i, 0)),
        out_shape=jax.ShapeDtypeStruct((M, D), F32),
        compiler_params=_params("parallel"),
        name="ple_final" if final_norm else "ple",
    )(x, p, g, wg, wp, gf)


def _to_scan_kernel(x0_ref, x1_ref, rev_ref, o_ref, a_ref, *, nb, tb, dup, pitch, apitch):
    N = HEAD_DIM
    NV = N // 2
    lanes = 2 * 2 * nb * RW_HEADS
    n_rows = N if dup else NV
    rev = rev_ref[...]
    for d, ref in enumerate((x0_ref, x1_ref)):
        for b in range(nb):
            x = ref[b]
            if d == 1:
                x = _dot_hi(rev, x)
            xt = x.T
            for h in range(RW_HEADS):
                slot = ((d * nb + b) * RW_HEADS + h) * 2
                for s in range(2):
                    blk = xt[h * N:(h + 1) * N] if dup else xt[h * N + s * NV:h * N + (s + 1) * NV]
                    a_ref[(slot + s) * apitch:(slot + s) * apitch + n_rows, :] = blk
    for j in range(n_rows):
        rows = a_ref[pl.ds(j, lanes, stride=apitch), :]
        o_ref[pl.ds(j, tb, stride=pitch), :] = rows.T
    for j in range(n_rows, pitch):
        o_ref[pl.ds(j, tb, stride=pitch), :] = jnp.zeros((tb, lanes), F32)


def to_scan(x0, x1, rev, B, T, tb, dup, pitch):
    C = RW_WIDTH
    nblk = T // tb
    lanes = 4 * B * RW_HEADS
    apitch = (HEAD_DIM if dup else HEAD_DIM // 2) + 4
    return pl.pallas_call(
        functools.partial(_to_scan_kernel, nb=B, tb=tb, dup=dup, pitch=pitch, apitch=apitch),
        grid=(nblk,),
        in_specs=[pl.BlockSpec((B, tb, C), lambda i: (0, i, 0)),
                  pl.BlockSpec((B, tb, C), lambda i: (0, nblk - 1 - i, 0)),
                  pl.BlockSpec((tb, tb), lambda i: (0, 0))],
        out_specs=pl.BlockSpec((tb * pitch, lanes), lambda i: (i, 0)),
        out_shape=jax.ShapeDtypeStruct((T * pitch, lanes), F32),
        scratch_shapes=[pltpu.VMEM((lanes * apitch, tb), F32)],
        compiler_params=_params("parallel"),
        name="to_scan_keys" if dup else "to_scan_values",
    )(x0.reshape(B, T, C), x1.reshape(B, T, C), rev)


def _from_scan_kernel(y_ref, rev_ref, yf_ref, yb_ref, a_ref, *, nb, tb, pitch, apitch):
    NV = HEAD_DIM // 2
    lanes = 2 * 2 * nb * RW_HEADS
    for ih in range(NV):
        rows = y_ref[pl.ds(ih, tb, stride=pitch), :]
        a_ref[pl.ds(ih, lanes, stride=apitch), :] = rows.T
    rev = rev_ref[...]
    for d, ref in enumerate((yf_ref, yb_ref)):
        for b in range(nb):
            slot0 = (d * nb + b) * RW_HEADS * 2
            pieces = [a_ref[(slot0 + s) * apitch:(slot0 + s) * apitch + NV, :] for s in range(2 * RW_HEADS)]
            y = jnp.concatenate(pieces, axis=0).T
            if d == 1:
                y = _dot_hi(rev, y)
            ref[b] = y


def from_scan(y, rev, B, T, tb, pitch):
    C = RW_WIDTH
    nblk = T // tb
    lanes = y.shape[-1]
    apitch = HEAD_DIM // 2 + 4
    out = jax.ShapeDtypeStruct((B, T, C), F32)
    yf, yb = pl.pallas_call(
        functools.partial(_from_scan_kernel, nb=B, tb=tb, pitch=pitch, apitch=apitch),
        grid=(nblk,),
        in_specs=[pl.BlockSpec((tb * pitch, lanes), lambda i: (i, 0)),
                  pl.BlockSpec((tb, tb), lambda i: (0, 0))],
        out_specs=[pl.BlockSpec((B, tb, C), lambda i: (0, i, 0)),
                   pl.BlockSpec((B, tb, C), lambda i: (0, nblk - 1 - i, 0))],
        out_shape=[out, out],
        scratch_shapes=[pltpu.VMEM((lanes * apitch, tb), F32)],
        compiler_params=_params("parallel"),
        name="from_scan",
    )(y, rev)
    return yf.reshape(B * T, C), yb.reshape(B * T, C)


def split_w_in(w_in, o_dil, o_na, o_g):
    depth, K, _ = w_in.shape
    w_t = jnp.swapaxes(w_in, 1, 2)
    w_b = jnp.concatenate([w_t[:, o_g:].astype(BF16), w_t[:, o_na:o_g].astype(BF16)], axis=1)
    w_f = jnp.concatenate([w_t[:, :o_dil].astype(BF16), jnp.zeros((depth, RW_COLS_PAD - o_dil, K), BF16),
                           w_t[:, o_dil:o_na].astype(BF16)], axis=1)
    return w_b, w_f


def _rope_tables(T):
    inv = ROPE_THETA ** (-jnp.arange(0, HEAD_DIM, 2, dtype=jnp.float32) / HEAD_DIM)
    ang = jnp.arange(T, dtype=jnp.float32)[:, None] * inv[None, :]
    cos, sin = jnp.cos(ang), jnp.sin(ang)
    cos_t = jnp.tile(jnp.concatenate([cos, cos], axis=-1), (1, DIL_STEP_WIDTH // HEAD_DIM))
    sin_t = jnp.tile(jnp.concatenate([-sin, sin], axis=-1), (1, DIL_STEP_WIDTH // HEAD_DIM))
    return cos_t, sin_t


def _pick(n, prefs):
    for c in prefs:
        if n % c == 0:
            return c
    return n


def kernel(x, p, norm_mix, w_in, rw_mu, rw_w0, rw_w2, rw_a0, rw_a2, rw_g2, rw_k_k, rw_k_a, rw_r_k, rw_lnx_g,
           rw_lnx_b, na_bias, w_br_a, w_br_b, w_br_c, w_out, norm_ffn, w_ffn_gate, w_ffn_up, ffn_conv_w,
           ffn_conv_b, w_ffn_down, norm_ple, w_ple_gate, w_ple, norm_final):
    B, T, D = x.shape
    depth = w_in.shape[0]
    M = B * T
    F = w_ffn_gate.shape[-1]
    C = RW_WIDTH
    W = DIL_GROUP_WIDTH
    assert (N_BRANCH * D) % W == 0 and w_in.shape[-1] == RW_COLS + 2 * 3 * DIL_WIDTH + N_BRANCH * D

    gates_w = N_BRANCH * D
    dil_w = 3 * DIL_WIDTH
    na_w = 3 * NA_WIDTH
    nb_cols = gates_w + na_w
    nf_cols = RW_COLS_PAD + dil_w
    na_blk0 = gates_w // W
    dil_blk0 = RW_COLS_PAD // W
    assert RW_COLS_PAD % W == 0

    cos_t, sin_t = _rope_tables(T)
    seg = jnp.asarray(np.kron(np.eye(RW_HEADS), np.ones((HEAD_DIM, HEAD_DIM))), BF16)
    xf = x.reshape(M, D)
    p3 = p.reshape(depth, M, p.shape[-1])
    bm_in = _pick(M, (1024, 512, 256, 128))
    bt_rw = _pick(T, (512, 256, 128))
    tb = _pick(T, (64, 32))
    tt = _pick(T, (128,))
    rev = jnp.asarray(np.eye(tt)[::-1], F32)

    wb16 = {name: wt.astype(BF16) for name, wt in (
        ("br_a", w_br_a), ("br_b", w_br_b), ("br_c", w_br_c), ("out", w_out), ("ffn_gate", w_ffn_gate),
        ("ffn_up", w_ffn_up), ("ffn_down", w_ffn_down), ("ple_gate", w_ple_gate), ("ple", w_ple))}
    o_dil, o_na, o_g = RW_COLS, RW_COLS + dil_w, RW_COLS + dil_w + na_w
    w_b, w_f = split_w_in(w_in, o_dil, o_na, o_g)

    for i in range(depth):
        g_mix = norm_mix[i][None]
        zb = norm_matmul(xf, g_mix, w_b, i, _pick(M, (512, 256, 128)), _pick(nb_cols, (2816, 768, 256, 128)), BF16,
                         "in_proj_bf16")
        zf = norm_matmul(xf, g_mix, w_f, i, bm_in, _pick(nf_cols, (1024, 256, 128)), F32, "in_proj_f32")
        zb3 = zb.reshape(B, T, nb_cols)
        zf3 = zf.reshape(B, T, nf_cols)

        mu = jnp.pad(rw_mu[i], (0, RW_COLS_PAD - RW_COLS))[None]
        r, v, a, dec0, dec1, kd0, kd1, b0, b1, gate = rw_prep(
            zf, 0, mu, rw_w0[i], rw_w2[i], rw_a0[i], rw_a2[i], rw_g2[i], rw_k_k[i][None], rw_k_a[i][None],
            seg, T, bt_rw)
        keys = lambda x0, x1: to_scan(x0, x1, rev, B, T, tt, dup=True, pitch=KEY_PITCH)
        vals = to_scan(v, v, rev, B, T, tt, dup=False, pitch=VAL_PITCH)
        y = rw_scan(keys(dec0, dec1), keys(kd0, kd1), keys(a, a), keys(b0, b1), keys(r, r), vals, T, tb,
                    KEY_PITCH, VAL_PITCH)
        yf, yb = from_scan(y, rev, B, T, tt, VAL_PITCH)
        ya = rw_post(yf, yb, r, v, kd0, kd1, gate, rw_r_k[i].reshape(1, C), rw_lnx_g[i][None], rw_lnx_b[i][None],
                     seg, _pick(T, (1024, 512, 256, 128)))

        dil_outs = []
        for g, (window, dil) in enumerate(DIL_PATTERNS):
            dil_outs.append(dilated_group(zf3, dil_blk0 + g, dil_blk0 + 3 + g, dil_blk0 + 6 + g, cos_t, sin_t,
                                          window, dil))

        yc = neighbourhood(zb3, na_blk0, na_blk0 + 3, na_blk0 + 6, na_bias_table(na_bias[i])).reshape(M, NA_WIDTH)

        merged = merge(ya, dil_outs, yc, zb, i, wb16["br_a"], wb16["br_b"], wb16["br_c"],
                       _pick(M, (1024, 512, 256, 128)), _pick(D, (1024, 512, 256, 128)))
        xf = matmul_res(merged, wb16["out"], i, xf, _pick(M, (512, 256, 128)), D)

        xf = ffn(xf, norm_ffn[i][None], i, wb16["ffn_gate"], wb16["ffn_up"], ffn_conv_w[i],
                 ffn_conv_b[i][None], wb16["ffn_down"], T, _pick(T, (512, 256, 128)),
                 _pick(F, (512, 256, 128)))

        xf = ple(xf, p3, i, norm_ple[i][None], wb16["ple_gate"], wb16["ple"],
                 norm_final[None], _pick(M, (512, 256, 128)), final_norm=(i == depth - 1))
    return xf.reshape(B, T, D)
```
